```python
import math
import numpy as np
import jax
import jax.numpy as jnp
from jax import lax

D_MODEL = 2048
BATCH = 2
SEQ = 4096
DEPTH = 2

HEAD_DIM = 128
N_MIXERS = 4
HEADS_PER_MIXER = D_MODEL // (N_MIXERS * HEAD_DIM)
MIX_W = HEADS_PER_MIXER * HEAD_DIM
IDX_HEADS = 16
IDX_DIM = 64
DSA_TOPK_MAX = 256
DIFF_DIM = HEAD_DIM // 2
MOBA_BLOCK = 256
MOBA_TOPK = 3
MOBA_Q_CHUNK = 64
DILATED_PATTERNS = ((128, 1), (512, 4), (2048, 16))
Q_BLOCK = 128
ROPE_THETA = 10000.0
N_EXPERTS = 16
N_GROUPS = 4
EXPERTS_PER_GROUP = N_EXPERTS // N_GROUPS
TOP_K = 2
D_FF_EXPERT = D_MODEL // 2
EXPERT_BLOCK = 128
LN_EPS = 1e-5
DEEPNORM_ALPHA = (2 * DEPTH) ** 0.25
DEEPNORM_BETA = (8 * DEPTH) ** -0.25
COL_SIZES = (
    MIX_W, HEAD_DIM, HEAD_DIM,
    IDX_HEADS * IDX_DIM, IDX_DIM, IDX_HEADS,
    MIX_W, MIX_W, MIX_W,
    MIX_W, MIX_W, MIX_W,
    MIX_W, MIX_W, MIX_W,
)
VALUE_COLS = (2, 8, 11, 14)
D_IN = sum(COL_SIZES)

kernel_name = 'hybrid_dsa_diff_moba_dilated_moe'


def _layer_norm(x, g, b):
    xf = x.astype(jnp.float32)
    mu = jnp.mean(xf, axis=-1, keepdims=True)
    var = jnp.mean(jnp.square(xf - mu), axis=-1, keepdims=True)
    return ((xf - mu) * lax.rsqrt(var + LN_EPS) * g + b).astype(x.dtype)


def _rms_norm(x, g):
    xf = x.astype(jnp.float32)
    return (xf * lax.rsqrt(jnp.mean(jnp.square(xf), axis=-1, keepdims=True) + LN_EPS) * g).astype(x.dtype)


def _rope(x, pos):
    d = x.shape[-1]
    inv = jnp.power(ROPE_THETA, -jnp.arange(0, d, 2, dtype=jnp.float32) / d)
    ang = pos.astype(jnp.float32)[..., None] * inv
    cos = jnp.cos(ang)[:, :, None, :]
    sin = jnp.sin(ang)[:, :, None, :]
    xf = x.astype(jnp.float32)
    x1, x2 = xf[..., : d // 2], xf[..., d // 2:]
    return jnp.concatenate([x1 * cos - x2 * sin, x1 * sin + x2 * cos], axis=-1).astype(x.dtype)


def _heads(t, d):
    return t.reshape(t.shape[0], t.shape[1], -1, d)


def _unblock(o):
    o = jnp.moveaxis(o, 0, 1)
    return o.reshape((o.shape[0], -1) + o.shape[3:])


def _dsa_attention(q, k, v, iq, ik, iw):
    B, S, H, Dh = q.shape
    n_keep = min(DSA_TOPK_MAX, S // 4)
    key_pos = jnp.arange(S)
    b_idx = jnp.arange(B)[:, None, None]

    def block(i):
        t0 = i * Q_BLOCK
        qb = lax.dynamic_slice_in_dim(q, t0, Q_BLOCK, axis=1)
        iqb = lax.dynamic_slice_in_dim(iq, t0, Q_BLOCK, axis=1)
        iwb = lax.dynamic_slice_in_dim(iw, t0, Q_BLOCK, axis=1).astype(jnp.float32) * IDX_HEADS ** -0.5
        q_pos = t0 + jnp.arange(Q_BLOCK)
        rel = jax.nn.relu(jnp.einsum('bqhd,bsd->bqhs', iqb, ik).astype(jnp.float32) * IDX_DIM ** -0.5)
        score = jnp.einsum('bqh,bqhs->bqs', iwb, rel)
        score = jnp.where(key_pos[None, None, :] <= q_pos[None, :, None], score, -jnp.inf)
        _, idx = lax.top_k(score, n_keep)
        valid = idx <= q_pos[None, :, None]
        k_sel = k[b_idx, idx]
        v_sel = v[b_idx, idx]
        logits = jnp.einsum('bqhd,bqkd->bhqk', qb, k_sel).astype(jnp.float32) * Dh ** -0.5
        p = jax.nn.softmax(jnp.where(valid[:, None], logits, -jnp.inf), axis=-1).astype(v.dtype)
        return jnp.einsum('bhqk,bqkd->bqhd', p, v_sel)

    return _unblock(lax.map(block, jnp.arange(S // Q_BLOCK)))


def _diff_attention(q, k, v, lam):
    B, S, H, _, dd = q.shape
    key_pos = jnp.arange(S)

    def block(i):
        t0 = i * Q_BLOCK
        qb = lax.dynamic_slice_in_dim(q, t0, Q_BLOCK, axis=1)
        q_pos = t0 + jnp.arange(Q_BLOCK)
        causal = key_pos[None, :] <= q_pos[:, None]
        s = jnp.einsum('bqhcd,bshcd->bhcqs', qb, k).astype(jnp.float32) * dd ** -0.5
        p = jax.nn.softmax(jnp.where(causal, s, -jnp.inf), axis=-1)
        w = p[:, :, 0] - lam * p[:, :, 1]
        return jnp.einsum('bhqs,bshd->bqhd', w.astype(v.dtype), v)

    return _unblock(lax.map(block, jnp.arange(S // Q_BLOCK)))


def _moba_attention(q, k, v):
    B, S, H, Dh = q.shape
    sp = -(-S // MOBA_BLOCK) * MOBA_BLOCK
    n_blk = sp // MOBA_BLOCK
    n_sel = min(MOBA_TOPK, n_blk)
    pad = ((0, 0), (0, sp - S), (0, 0), (0, 0))
    q, k, v = jnp.pad(q, pad), jnp.pad(k, pad), jnp.pad(v, pad)
    kbh = k.reshape(B, n_blk, MOBA_BLOCK, H, Dh).transpose(0, 3, 1, 2, 4)
    vbh = v.reshape(B, n_blk, MOBA_BLOCK, H, Dh).transpose(0, 3, 1, 2, 4)
    k_mean = jnp.mean(kbh.astype(jnp.float32), axis=3).astype(k.dtype)
    b_idx = jnp.arange(B)[:, None, None, None]
    h_idx = jnp.arange(H)[None, None, :, None]
    blk_ids = jnp.arange(n_blk)
    scale = Dh ** -0.5
    n_sel_keys = n_sel * MOBA_BLOCK

    def block(i):
        t0 = i * MOBA_Q_CHUNK
        qb = lax.dynamic_slice_in_dim(q, t0, MOBA_Q_CHUNK, axis=1)
        q_pos = t0 + jnp.arange(MOBA_Q_CHUNK)
        own = t0 // MOBA_BLOCK
        gate = jnp.einsum('bqhd,bhnd->bqhn', qb, k_mean).astype(jnp.float32)
        gate = jnp.where(blk_ids < own, gate, -jnp.inf)
        _, sel = lax.top_k(gate, n_sel)
        sel_ok = sel < own
        k_sel = kbh[b_idx, h_idx, sel]
        v_sel = vbh[b_idx, h_idx, sel]
        k_own = lax.dynamic_index_in_dim(kbh, own, axis=2, keepdims=False)
        v_own = lax.dynamic_index_in_dim(vbh, own, axis=2, keepdims=False)
        own_pos = own * MOBA_BLOCK + jnp.arange(MOBA_BLOCK)
        causal = own_pos[None, :] <= q_pos[:, None]
        s_sel = jnp.einsum('bqhd,bqhnkd->bqhnk', qb, k_sel).astype(jnp.float32) * scale
        s_sel = jnp.where(sel_ok[..., None], s_sel, -jnp.inf).reshape(B, MOBA_Q_CHUNK, H, n_sel_keys)
        s_own = jnp.einsum('bqhd,bhkd->bqhk', qb, k_own).astype(jnp.float32) * scale
        s_own = jnp.where(causal[None, :, None, :], s_own, -jnp.inf)
        p = jax.nn.softmax(jnp.concatenate([s_sel, s_own], axis=-1), axis=-1).astype(v.dtype)
        p_sel = p[..., :n_sel_keys].reshape(B, MOBA_Q_CHUNK, H, n_sel, MOBA_BLOCK)
        p_own = p[..., n_sel_keys:]
        return (jnp.einsum('bqhnk,bqhnkd->bqhd', p_sel, v_sel)
                + jnp.einsum('bqhk,bhkd->bqhd', p_own, v_own))

    return _unblock(lax.map(block, jnp.arange(sp // MOBA_Q_CHUNK)))[:, :S]


def _dilated_branch(q, k, v, window, dilation):
    B, S, H, Dh = q.shape
    band = window // dilation
    n_sub = -(-S // dilation)
    nb = -(-n_sub // band)
    sp = dilation * nb * band

    def to_blocks(a):
        a = jnp.pad(a, ((0, 0), (0, sp - S), (0, 0), (0, 0)))
        a = a.reshape(B, nb * band, dilation, H, Dh).transpose(0, 2, 1, 3, 4)
        return a.reshape(B, dilation, nb, band, H, Dh)

    def with_prev(a):
        prev = jnp.pad(a[:, :, :-1], ((0, 0), (0, 0), (1, 0), (0, 0), (0, 0), (0, 0)))
        return jnp.concatenate([prev, a], axis=3)

    qb = to_blocks(q)
    kk = with_prev(to_blocks(k))
    vv = with_prev(to_blocks(v))
    qi = jnp.arange(band)[:, None]
    ki = jnp.arange(2 * band)[None, :]
    dist = qi + band - ki
    band_ok = (dist >= 0) & (dist <= band)
    mask = band_ok[None] & ((jnp.arange(nb)[:, None, None] > 0) | (ki >= band)[None])
    logits = jnp.einsum('brjqhd,brjkhd->brjhqk', qb, kk).astype(jnp.float32) * Dh ** -0.5
    logits = jnp.where(mask[:, None], logits, -jnp.inf)
    lse = jax.nn.logsumexp(logits, axis=-1, keepdims=True)
    p = jnp.exp(logits - lse).astype(v.dtype)
    o = jnp.einsum('brjhqk,brjkhd->brjqhd', p, vv)
    o = o.reshape(B, dilation, nb * band, H, Dh).transpose(0, 2, 1, 3, 4).reshape(B, sp, H, Dh)[:, :S]
    lse = lse[..., 0].transpose(0, 1, 2, 4, 3).reshape(B, dilation, nb * band, H)
    lse = lse.transpose(0, 2, 1, 3).reshape(B, sp, H)[:, :S]
    return o, lse


def _dilated_attention(q, k, v):
    outs, lses = [], []
    for window, dilation in DILATED_PATTERNS:
        o, lse = _dilated_branch(q, k, v, window, dilation)
        outs.append(o)
        lses.append(lse)
    wts = jax.nn.softmax(jnp.stack(lses, axis=0), axis=0)
    return jnp.einsum('pbsh,pbshd->bshd', wts.astype(v.dtype), jnp.stack(outs, axis=0))


def _moe(x, router_w, router_bias, w_gate, w_up, w_down):
    B, S, D = x.shape
    T = B * S
    xt = x.reshape(T, D)
    aff = jax.nn.sigmoid(jnp.dot(xt, router_w).astype(jnp.float32))
    grouped = (aff + router_bias.astype(jnp.float32)).reshape(T, N_GROUPS, EXPERTS_PER_GROUP)
    group_score = jnp.sum(lax.top_k(grouped, TOP_K)[0], axis=-1)
    g = jnp.argmax(group_score, axis=-1)
    in_group = jnp.take_along_axis(grouped, g[:, None, None], axis=1)[:, 0]
    _, local = lax.top_k(in_group, TOP_K)
    e_idx = g[:, None] * EXPERTS_PER_GROUP + local
    gate = jnp.take_along_axis(aff, e_idx, axis=1)
    gate = gate / jnp.sum(gate, axis=-1, keepdims=True)
    A = T * TOP_K
    e_flat = e_idx.reshape(A)
    tok_flat = jnp.repeat(jnp.arange(T, dtype=jnp.int32), TOP_K)
    order = jnp.argsort(e_flat)
    e_sorted = e_flat[order]
    counts = jnp.bincount(e_flat, length=N_EXPERTS)
    padded = (counts + EXPERT_BLOCK - 1) // EXPERT_BLOCK * EXPERT_BLOCK
    start = jnp.cumsum(counts) - counts
    pend = jnp.cumsum(padded)
    pstart = pend - padded
    dest = pstart[e_sorted] + (jnp.arange(A) - start[e_sorted])
    n_slots = A + N_EXPERTS * EXPERT_BLOCK
    n_blocks = n_slots // EXPERT_BLOCK
    slot_tok = jnp.full((n_slots,), T, jnp.int32).at[dest].set(tok_flat[order])
    slot_gate = jnp.zeros((n_slots,), jnp.float32).at[dest].set(gate.reshape(A)[order])
    block_expert = jnp.minimum(jnp.searchsorted(pend, jnp.arange(n_blocks) * EXPERT_BLOCK, side='right'), N_EXPERTS - 1)
    x_pad = jnp.concatenate([xt, jnp.zeros((1, D), xt.dtype)], axis=0)
    xs = x_pad[slot_tok].reshape(n_blocks, EXPERT_BLOCK, D)

    def expert_block(args):
        xb, e = args
        h = jax.nn.silu(xb @ w_gate[e]) * (xb @ w_up[e])
        return h @ w_down[e]

    ys = lax.map(expert_block, (xs, block_expert)).reshape(n_slots, D)
    ys = (ys * slot_gate[:, None]).astype(xt.dtype)
    out = jnp.zeros((T + 1, D), xt.dtype).at[slot_tok].add(ys)[:T]
    return out.reshape(B, S, D)


def setup_inputs(seed: int = 0) -> dict:
    key = jax.random.key(seed)
    ks = jax.random.split(key, 16)
    f32 = jnp.float32
    col_scale = jnp.asarray(np.concatenate([
        np.full((n,), DEEPNORM_BETA if i in VALUE_COLS else 1.0, np.float32)
        for i, n in enumerate(COL_SIZES)]))
    x = jax.random.normal(ks[0], (BATCH, SEQ, D_MODEL), f32)
    positions = jnp.broadcast_to(jnp.arange(SEQ, dtype=jnp.int32), (BATCH, SEQ))
    w_in = jax.random.normal(ks[1], (DEPTH, D_MODEL, D_IN), f32) * (D_MODEL ** -0.5) * col_scale
    w_out = jax.random.normal(ks[2], (DEPTH, D_MODEL, D_MODEL), f32) * (D_MODEL ** -0.5 * DEEPNORM_BETA)
    diff_lambda = jax.random.normal(ks[3], (DEPTH, 4, DIFF_DIM), f32) * 0.1
    diff_norm_g = 1.0 + 0.02 * jax.random.normal(ks[4], (DEPTH, HEAD_DIM), f32)
    ln_mix_g = 1.0 + 0.02 * jax.random.normal(ks[5], (DEPTH, D_MODEL), f32)
    ln_mix_b = 0.02 * jax.random.normal(ks[6], (DEPTH, D_MODEL), f32)
    router_w = jax.random.normal(ks[7], (D_MODEL, N_EXPERTS), f32) * D_MODEL ** -0.5
    router_bias = 0.01 * jax.random.normal(ks[8], (N_EXPERTS,), f32)
    w_gate = jax.random.normal(ks[9], (DEPTH, N_EXPERTS, D_MODEL, D_FF_EXPERT), f32) * (D_MODEL ** -0.5 * DEEPNORM_BETA)
    w_up = jax.random.normal(ks[10], (DEPTH, N_EXPERTS, D_MODEL, D_FF_EXPERT), f32) * (D_MODEL ** -0.5 * DEEPNORM_BETA)
    w_down = jax.random.normal(ks[11], (DEPTH, N_EXPERTS, D_FF_EXPERT, D_MODEL), f32) * (D_FF_EXPERT ** -0.5 * DEEPNORM_BETA)
    ln_ffn_g = 1.0 + 0.02 * jax.random.normal(ks[12], (DEPTH, D_MODEL), f32)
    ln_ffn_b = 0.02 * jax.random.normal(ks[13], (DEPTH, D_MODEL), f32)
    return {'x': x, 'positions': positions, 'w_in': w_in, 'w_out': w_out,
            'diff_lambda': diff_lambda, 'diff_norm_g': diff_norm_g,
            'ln_mix_g': ln_mix_g, 'ln_mix_b': ln_mix_b,
            'router_w': router_w, 'router_bias': router_bias,
            'w_gate': w_gate, 'w_up': w_up, 'w_down': w_down,
            'ln_ffn_g': ln_ffn_g, 'ln_ffn_b': ln_ffn_b}


def reference(x, positions, w_in, w_out, diff_lambda, diff_norm_g, ln_mix_g, ln_mix_b,
              router_w, router_bias, w_gate, w_up, w_down, ln_ffn_g, ln_ffn_b):
    B, S, _ = x.shape
    H = HEADS_PER_MIXER
    split_at = np.cumsum(COL_SIZES)[:-1].tolist()
    for layer in range(DEPTH):
        proj = jnp.einsum('bsd,dc->bsc', x, w_in[layer])
        (a_q, a_k, a_v, i_q, i_k, i_w, b_q, b_k, b_v,
         c_q, c_k, c_v, d_q, d_k, d_v) = jnp.split(proj, split_at, axis=-1)
        o_a = _dsa_attention(_rope(_heads(a_q, HEAD_DIM), positions),
                             _rope(a_k[:, :, None], positions)[:, :, 0], a_v,
                             _rope(_heads(i_q, IDX_DIM), positions),
                             _rope(i_k[:, :, None], positions)[:, :, 0], i_w)
        lam_init = 0.8 - 0.6 * math.exp(-0.3 * layer)
        lq1, lk1, lq2, lk2 = diff_lambda[layer].astype(jnp.float32)
        lam = jnp.exp(jnp.sum(lq1 * lk1)) - jnp.exp(jnp.sum(lq2 * lk2)) + lam_init
        bq = _rope(_heads(b_q, DIFF_DIM), positions).reshape(B, S, H, 2, DIFF_DIM)
        bk = _rope(_heads(b_k, DIFF_DIM), positions).reshape(B, S, H, 2, DIFF_DIM)
        o_b = _diff_attention(bq, bk, _heads(b_v, HEAD_DIM), lam)
        o_b = (_rms_norm(o_b, diff_norm_g[layer]) * (1.0 - lam_init)).astype(x.dtype)
        o_c = _moba_attention(_rope(_heads(c_q, HEAD_DIM), positions),
                              _rope(_heads(c_k, HEAD_DIM), positions), _heads(c_v, HEAD_DIM))
        o_d = _dilated_attention(_rope(_heads(d_q, HEAD_DIM), positions),
                                 _rope(_heads(d_k, HEAD_DIM), positions), _heads(d_v, HEAD_DIM))
        mix = jnp.concatenate([o_a.reshape(B, S, MIX_W), o_b.reshape(B, S, MIX_W),
                               o_c.reshape(B, S, MIX_W), o_d.reshape(B, S, MIX_W)], axis=-1)
        x = _layer_norm(DEEPNORM_ALPHA * x + mix @ w_out[layer], ln_mix_g[layer], ln_mix_b[layer])
        moe_out = _moe(x, router_w, router_bias, w_gate[layer], w_up[layer], w_down[layer])
        x = _layer_norm(DEEPNORM_ALPHA * x + moe_out, ln_ffn_g[layer], ln_ffn_b[layer])
    return x
```

```python
import functools
import math

import numpy as np
import jax
import jax.numpy as jnp
from jax import lax
from jax.experimental import pallas as pl
from jax.experimental.pallas import tpu as pltpu

F32 = jnp.float32
BF16 = jnp.bfloat16
I32 = jnp.int32

D_MODEL = 2048
HEAD_DIM = 128
HEADS = 4
MIX_W = HEADS * HEAD_DIM
IDX_HEADS = 16
IDX_DIM = 64
DSA_TOPK_MAX = 256
DIFF_DIM = 64
MOBA_BLOCK = 256
MOBA_TOPK = 3
DILATED_PATTERNS = ((128, 1), (512, 4), (2048, 16))
ROPE_THETA = 10000.0
N_EXPERTS = 16
N_GROUPS = 4
EXPERTS_PER_GROUP = 4
D_FF = 1024
LN_EPS = 1e-5
DEPTH = 2
ALPHA = (2 * DEPTH) ** 0.25

LANES = 128
NEG = -1e30
INT_MIN = -(2 ** 31)
VMEM_LIMIT = 56 * 1024 * 1024

PROJ_TM = 512
OUT_TM = 256
DSA_TQ = 128
DSA_CK = 512
DIFF_T = 256
MOBA_T = MOBA_BLOCK
DIL_T = 256
ROUTER_TM = 256
DISPATCH_TM = 256
EXPERT_TB = 256
COMBINE_TM = 256

_OFF = {}
_o = 0
for _name, _n in (("a_q", 512), ("a_k", 128), ("a_v", 128), ("i_q", 1024), ("i_k", 64), ("i_w", 16),
                  ("b_q", 512), ("b_k", 512), ("b_v", 512), ("c_q", 512), ("c_k", 512), ("c_v", 512),
                  ("d_q", 512), ("d_k", 512), ("d_v", 512)):
    _OFF[_name] = (_o, _o + _n)
    _o += _n


def _params(sem=None):
    return pltpu.CompilerParams(dimension_semantics=sem, vmem_limit_bytes=VMEM_LIMIT)


def _dot_nt(a, b):
    return lax.dot_general(a, b, (((1,), (1,)), ((), ())), preferred_element_type=F32)


def _proj_kernel(x_ref, w_ref, *rest, mode):
    o_ref = rest[-1]
    acc = jnp.dot(x_ref[...], w_ref[...], preferred_element_type=F32)
    if mode == "none":
        o_ref[...] = acc.astype(o_ref.dtype)
        return
    n_groups = acc.shape[1] // LANES
    for g in range(n_groups):
        y = acc[:, g * LANES:(g + 1) * LANES]
        if mode == "rope128":
            cos_ref, sin_ref = rest[0], rest[1]
            r = y * cos_ref[...] + pltpu.roll(y, 64, 1) * sin_ref[...]
        else:
            cos_ref, sina_ref, sinb_ref = rest[0], rest[1], rest[2]
            r = (y * cos_ref[...] + pltpu.roll(y, 96, 1) * sina_ref[...]
                 + pltpu.roll(y, 32, 1) * sinb_ref[...])
        o_ref[:, g * LANES:(g + 1) * LANES] = r.astype(o_ref.dtype)


def _proj(x_bf, w_bf, tables, mode, tn, out_dtype):
    t, d = x_bf.shape
    n = w_bf.shape[1]
    tm = min(PROJ_TM, t)
    in_specs = [pl.BlockSpec((tm, d), lambda j, i: (i, 0)),
                pl.BlockSpec((d, tn), lambda j, i: (0, j))]
    in_specs += [pl.BlockSpec((tm, LANES), lambda j, i: (i, 0)) for _ in tables]
    return pl.pallas_call(
        functools.partial(_proj_kernel, mode=mode),
        grid=(n // tn, t // tm),
        in_specs=in_specs,
        out_specs=pl.BlockSpec((tm, tn), lambda j, i: (i, j)),
        out_shape=jax.ShapeDtypeStruct((t, n), out_dtype),
        compiler_params=_params(("arbitrary", "arbitrary")),
    )(x_bf, w_bf, *tables)


def _online_update(s, mask, v, m_ref, l_ref, acc_ref, idx, weight=None):
    if mask is not None:
        s = jnp.where(mask, s, NEG)
    m_prev = m_ref[idx]
    m_new = jnp.maximum(m_prev, jnp.max(s, axis=1, keepdims=True))
    p = jnp.exp(s - m_new)
    if weight is not None:
        p = p * weight
    elif mask is not None:
        p = jnp.where(mask, p, 0.0)
    alpha = jnp.exp(m_prev - m_new)
    l_ref[idx] = alpha * l_ref[idx] + jnp.sum(p, axis=1, keepdims=True)
    acc_ref[idx] = alpha * acc_ref[idx] + jnp.dot(p.astype(BF16), v, preferred_element_type=F32)
    m_ref[idx] = m_new


def _init_softmax(m_ref, l_ref, acc_ref):
    m_ref[...] = jnp.full(m_ref.shape, NEG, F32)
    l_ref[...] = jnp.zeros(l_ref.shape, F32)
    acc_ref[...] = jnp.zeros(acc_ref.shape, F32)


def _dsa_kernel(aq_ref, ak_ref, av_ref, iq_ref, ik_ref, iw_ref, o_ref,
                key_ref, m_ref, l_ref, acc_ref, jlim_ref, *, n_keep):
    tq = aq_ref.shape[0]
    ck = key_ref.shape[2]
    i = pl.program_id(1)
    n_chunks = ((i + 1) * tq + ck - 1) // ck
    lane = lax.broadcasted_iota(I32, (1, LANES), 1)
    col = lax.broadcasted_iota(I32, (1, ck), 1)
    q_pos = i * tq + lax.broadcasted_iota(I32, (tq, 1), 0)
    iw = iw_ref[...]

    def score_chunk(c, carry):
        ik_c = ik_ref[pl.ds(pl.multiple_of(c * ck, ck), ck), :]
        ik_lo = jnp.where(lane < IDX_DIM, ik_c, jnp.zeros_like(ik_c))
        ik_hi = jnp.where(lane >= IDX_DIM, ik_c, jnp.zeros_like(ik_c))
        score = jnp.zeros((tq, ck), F32)
        for j in range(IDX_HEADS // 2):
            q_pair = iq_ref[:, j * LANES:(j + 1) * LANES]
            s_even = jnp.maximum(_dot_nt(q_pair, ik_lo), 0.0)
            s_odd = jnp.maximum(_dot_nt(q_pair, ik_hi), 0.0)
            score = score + s_even * iw[:, 2 * j:2 * j + 1] + s_odd * iw[:, 2 * j + 1:2 * j + 2]
        bits = lax.bitcast_convert_type(score, I32)
        key = jnp.where(bits < 0, bits ^ jnp.int32(0x7FFFFFFF), bits)
        key_ref[c] = jnp.where(c * ck + col <= q_pos, key, jnp.int32(INT_MIN))
        return carry

    lax.fori_loop(0, n_chunks, score_chunk, 0)

    def count(pred):
        def body(c, part):
            hit = pred(key_ref[c], c).astype(I32)
            folded = hit[:, 0:LANES]
            for g in range(1, ck // LANES):
                folded = folded + hit[:, g * LANES:(g + 1) * LANES]
            return part + folded
        part = lax.fori_loop(0, n_chunks, body, jnp.zeros((tq, LANES), I32))
        return jnp.sum(part, axis=1, keepdims=True)

    def bisect(it, thr_u):
        cand_u = thr_u | jnp.left_shift(jnp.int32(1), 31 - it)
        cand_s = cand_u ^ jnp.int32(INT_MIN)
        cnt = count(lambda k, c: k >= cand_s)
        return jnp.where(cnt >= n_keep, cand_u, thr_u)

    thr_u = lax.fori_loop(0, 32, bisect, jnp.zeros((tq, 1), I32))
    thr = thr_u ^ jnp.int32(INT_MIN)

    n_gt = count(lambda k, c: k > thr)
    n_eq = count(lambda k, c: k == thr)
    need = n_keep - n_gt
    s_total = key_ref.shape[0] * ck
    jlim_ref[...] = jnp.full((tq, 1), s_total, I32)
    excess = jnp.max(jnp.where((n_eq > need) & (thr != INT_MIN), 1, 0))

    @pl.when(excess > 0)
    def _():
        n_bits = max(1, (s_total - 1).bit_length())

        def bisect_idx(it, j):
            cand = j | jnp.left_shift(jnp.int32(1), n_bits - 1 - it)
            cnt = count(lambda k, c: (k == thr) & (c * ck + col < cand))
            return jnp.where(cnt < need, cand, j)

        jlim_ref[...] = lax.fori_loop(0, n_bits, bisect_idx, jnp.zeros((tq, 1), I32))

    jlim = jlim_ref[...]

    _init_softmax(m_ref, l_ref, acc_ref)
    q4 = jnp.concatenate([aq_ref[:, h * HEAD_DIM:(h + 1) * HEAD_DIM] for h in range(HEADS)], axis=0)

    def attend(c, carry):
        start = pl.multiple_of(c * ck, ck)
        k_c = ak_ref[pl.ds(start, ck), :]
        v_c = av_ref[pl.ds(start, ck), :]
        key = key_ref[c]
        sel = ((key > thr) | ((key == thr) & (c * ck + col <= jlim))) & (key != INT_MIN)
        sel4 = jnp.concatenate([jnp.where(sel, 1.0, 0.0)] * HEADS, axis=0) > 0.5
        _online_update(_dot_nt(q4, k_c), sel4, v_c, m_ref, l_ref, acc_ref, 0)
        return carry

    lax.fori_loop(0, n_chunks, attend, 0)
    out = acc_ref[0] / l_ref[0]
    for h in range(HEADS):
        o_ref[:, h * HEAD_DIM:(h + 1) * HEAD_DIM] = out[h * tq:(h + 1) * tq].astype(o_ref.dtype)


def _dsa(p128, p64, pv, iw, n_keep):
    b, s, _ = p128.shape
    tq = min(DSA_TQ, s)
    ck = min(DSA_CK, s)
    return pl.pallas_call(
        functools.partial(_dsa_kernel, n_keep=n_keep),
        grid=(b, s // tq),
        in_specs=[
            pl.BlockSpec((None, tq, MIX_W), lambda bb, i: (bb, i, 0)),
            pl.BlockSpec((None, s, HEAD_DIM), lambda bb, i: (bb, 0, 20)),
            pl.BlockSpec((None, s, HEAD_DIM), lambda bb, i: (bb, 0, 12)),
            pl.BlockSpec((None, tq, 1024), lambda bb, i: (bb, i, 0)),
            pl.BlockSpec((None, s, LANES), lambda bb, i: (bb, 0, 16)),
            pl.BlockSpec((None, tq, LANES), lambda bb, i: (bb, i, 0)),
        ],
        out_specs=pl.BlockSpec((None, tq, MIX_W), lambda bb, i: (bb, i, 0)),
        out_shape=jax.ShapeDtypeStruct((b, s, MIX_W), BF16),
        scratch_shapes=[
            pltpu.VMEM((s // ck, tq, ck), I32),
            pltpu.VMEM((1, HEADS * tq, 1), F32),
            pltpu.VMEM((1, HEADS * tq, 1), F32),
            pltpu.VMEM((1, HEADS * tq, HEAD_DIM), F32),
            pltpu.VMEM((tq, 1), I32),
        ],
        compiler_params=_params(("arbitrary", "arbitrary")),
    )(p128, p128, pv, p64, p64, iw)


def _diff_kernel(q_ref, k_ref, v_ref, lam_ref, g_ref, o_ref, m_ref, l_ref, acc_ref, *, lam_init):
    t = q_ref.shape[0]
    i = pl.program_id(1)
    lane = lax.broadcasted_iota(I32, (1, LANES), 1)
    _init_softmax(m_ref, l_ref, acc_ref)

    def step(c, diagonal):
        start = pl.multiple_of(c * t, t)
        k_c = k_ref[pl.ds(start, t), :]
        v_c = v_ref[pl.ds(start, t), :]
        if diagonal:
            mask = (lax.broadcasted_iota(I32, (t, t), 1) <= lax.broadcasted_iota(I32, (t, t), 0))
        else:
            mask = None
        for h in range(HEADS):
            q_h = q_ref[:, h * HEAD_DIM:(h + 1) * HEAD_DIM]
            k_h = k_c[:, h * HEAD_DIM:(h + 1) * HEAD_DIM]
            v_h = v_c[:, h * HEAD_DIM:(h + 1) * HEAD_DIM]
            q_1 = jnp.where(lane < DIFF_DIM, q_h, jnp.zeros_like(q_h))
            q_2 = jnp.where(lane >= DIFF_DIM, q_h, jnp.zeros_like(q_h))
            _online_update(_dot_nt(q_1, k_h), mask, v_h, m_ref, l_ref, acc_ref, 2 * h)
            _online_update(_dot_nt(q_2, k_h), mask, v_h, m_ref, l_ref, acc_ref, 2 * h + 1)

    def full_step(c, carry):
        step(c, False)
        return carry

    lax.fori_loop(0, i, full_step, 0)
    step(i, True)

    lam_p = lam_ref[...]
    lam = (jnp.exp(jnp.sum(lam_p[0:1] * lam_p[1:2], axis=1, keepdims=True))
           - jnp.exp(jnp.sum(lam_p[2:3] * lam_p[3:4], axis=1, keepdims=True)) + lam_init)
    for h in range(HEADS):
        o = acc_ref[2 * h] / l_ref[2 * h] - lam * (acc_ref[2 * h + 1] / l_ref[2 * h + 1])
        o = o * lax.rsqrt(jnp.mean(o * o, axis=1, keepdims=True) + LN_EPS) * g_ref[...]
        o_ref[:, h * HEAD_DIM:(h + 1) * HEAD_DIM] = (o * (1.0 - lam_init)).astype(o_ref.dtype)


def _diff(p64, pv, lam_p, g, lam_init):
    b, s, _ = p64.shape
    t = min(DIFF_T, s)
    return pl.pallas_call(
        functools.partial(_diff_kernel, lam_init=lam_init),
        grid=(b, s // t),
        in_specs=[
            pl.BlockSpec((None, t, MIX_W), lambda bb, i: (bb, i, 2)),
            pl.BlockSpec((None, s, MIX_W), lambda bb, i: (bb, 0, 3)),
            pl.BlockSpec((None, s, MIX_W), lambda bb, i: (bb, 0, 0)),
            pl.BlockSpec((4, DIFF_DIM), lambda bb, i: (0, 0)),
            pl.BlockSpec((1, HEAD_DIM), lambda bb, i: (0, 0)),
        ],
        out_specs=pl.BlockSpec((None, t, MIX_W), lambda bb, i: (bb, i, 0)),
        out_shape=jax.ShapeDtypeStruct((b, s, MIX_W), BF16),
        scratch_shapes=[
            pltpu.VMEM((2 * HEADS, t, 1), F32),
            pltpu.VMEM((2 * HEADS, t, 1), F32),
            pltpu.VMEM((2 * HEADS, t, HEAD_DIM), F32),
        ],
        compiler_params=_params(("arbitrary", "arbitrary")),
    )(p64, p64, pv, lam_p, g)


def _kmean_kernel(k_ref, o_ref):
    n_blk = k_ref.shape[0] // MOBA_BLOCK
    o_ref[...] = jnp.zeros(o_ref.shape, o_ref.dtype)
    for n in range(n_blk):
        blk = k_ref[n * MOBA_BLOCK:(n + 1) * MOBA_BLOCK, :].astype(F32)
        o_ref[n:n + 1, :] = jnp.mean(blk, axis=0, keepdims=True)


def _kmean(p128):
    b, s, _ = p128.shape
    return pl.pallas_call(
        _kmean_kernel,
        grid=(b,),
        in_specs=[pl.BlockSpec((None, s, MIX_W), lambda bb: (bb, 0, 2))],
        out_specs=pl.BlockSpec((None, LANES, MIX_W), lambda bb: (bb, 0, 0)),
        out_shape=jax.ShapeDtypeStruct((b, LANES, MIX_W), F32),
        compiler_params=_params(("arbitrary",)),
    )(p128)


def _moba_kernel(q_ref, k_ref, v_ref, km_ref, o_ref, sel_ref, m_ref, l_ref, acc_ref):
    t = q_ref.shape[0]
    i = pl.program_id(1)
    lane = lax.broadcasted_iota(I32, (1, LANES), 1)
    _init_softmax(m_ref, l_ref, acc_ref)

    for h in range(HEADS):
        q_h = q_ref[:, h * HEAD_DIM:(h + 1) * HEAD_DIM]
        gate = _dot_nt(q_h, km_ref[:, h * HEAD_DIM:(h + 1) * HEAD_DIM].astype(BF16))
        gate = jnp.where(lane < i, gate, -jnp.inf)
        sel = jnp.zeros((t, LANES), F32)
        for _ in range(MOBA_TOPK):
            best = jnp.max(gate, axis=1, keepdims=True)
            first = jnp.min(jnp.where(gate == best, lane, LANES), axis=1, keepdims=True)
            pick = (lane == first) & (best > -jnp.inf)
            sel = jnp.where(pick, 1.0, sel)
            gate = jnp.where(pick, -jnp.inf, gate)
        sel_ref[h] = sel

    def step(c, diagonal):
        start = pl.multiple_of(c * t, t)
        k_c = k_ref[pl.ds(start, t), :]
        v_c = v_ref[pl.ds(start, t), :]
        for h in range(HEADS):
            q_h = q_ref[:, h * HEAD_DIM:(h + 1) * HEAD_DIM]
            if diagonal:
                mask = (lax.broadcasted_iota(I32, (t, t), 1) <= lax.broadcasted_iota(I32, (t, t), 0))
            else:
                picked = jnp.max(jnp.where(lane == c, sel_ref[h], 0.0), axis=1, keepdims=True)
                mask = picked > 0.0
            _online_update(_dot_nt(q_h, k_c[:, h * HEAD_DIM:(h + 1) * HEAD_DIM]), mask,
                           v_c[:, h * HEAD_DIM:(h + 1) * HEAD_DIM], m_ref, l_ref, acc_ref, h)

    def past_step(c, carry):
        step(c, False)
        return carry

    lax.fori_loop(0, i, past_step, 0)
    step(i, True)
    for h in range(HEADS):
        o_ref[:, h * HEAD_DIM:(h + 1) * HEAD_DIM] = (acc_ref[h] / l_ref[h]).astype(o_ref.dtype)


def _moba(p128, pv, kmean):
    b, s, _ = p128.shape
    t = MOBA_T
    return pl.pallas_call(
        _moba_kernel,
        grid=(b, s // t),
        in_specs=[
            pl.BlockSpec((None, t, MIX_W), lambda bb, i: (bb, i, 1)),
            pl.BlockSpec((None, s, MIX_W), lambda bb, i: (bb, 0, 2)),
            pl.BlockSpec((None, s, MIX_W), lambda bb, i: (bb, 0, 1)),
            pl.BlockSpec((None, LANES, MIX_W), lambda bb, i: (bb, 0, 0)),
        ],
        out_specs=pl.BlockSpec((None, t, MIX_W), lambda bb, i: (bb, i, 0)),
        out_shape=jax.ShapeDtypeStruct((b, s, MIX_W), BF16),
        scratch_shapes=[
            pltpu.VMEM((HEADS, t, LANES), F32),
            pltpu.VMEM((HEADS, t, 1), F32),
            pltpu.VMEM((HEADS, t, 1), F32),
            pltpu.VMEM((HEADS, t, HEAD_DIM), F32),
        ],
        compiler_params=_params(("arbitrary", "arbitrary")),
    )(p128, p128, pv, kmean)


def _dilated_kernel(q_ref, k_ref, v_ref, o_ref, m_ref, l_ref, acc_ref):
    t = q_ref.shape[0]
    i = pl.program_id(1)
    _init_softmax(m_ref, l_ref, acc_ref)
    max_window = max(w for w, _ in DILATED_PATTERNS)
    first = jnp.maximum(i - max_window // t, 0)
    rel = lax.broadcasted_iota(I32, (t, t), 0) - lax.broadcasted_iota(I32, (t, t), 1)

    def step(c, carry):
        start = pl.multiple_of(c * t, t)
        k_c = k_ref[pl.ds(start, t), :]
        v_c = v_ref[pl.ds(start, t), :]
        dist = rel + (i - c) * t
        mult = jnp.zeros((t, t), F32)
        for window, dilation in DILATED_PATTERNS:
            hit = (dist >= 0) & (dist <= window) & ((dist & (dilation - 1)) == 0)
            mult = mult + jnp.where(hit, 1.0, 0.0)
        mask = mult > 0.0
        for h in range(HEADS):
            q_h = q_ref[:, h * HEAD_DIM:(h + 1) * HEAD_DIM]
            _online_update(_dot_nt(q_h, k_c[:, h * HEAD_DIM:(h + 1) * HEAD_DIM]), mask,
                           v_c[:, h * HEAD_DIM:(h + 1) * HEAD_DIM], m_ref, l_ref, acc_ref, h, weight=mult)
        return carry

    lax.fori_loop(first, i + 1, step, 0)
    for h in range(HEADS):
        o_ref[:, h * HEAD_DIM:(h + 1) * HEAD_DIM] = (acc_ref[h] / l_ref[h]).astype(o_ref.dtype)


def _dilated(p128, pv):
    b, s, _ = p128.shape
    t = min(DIL_T, s)
    return pl.pallas_call(
        _dilated_kernel,
        grid=(b, s // t),
        in_specs=[
            pl.BlockSpec((None, t, MIX_W), lambda bb, i: (bb, i, 3)),
            pl.BlockSpec((None, s, MIX_W), lambda bb, i: (bb, 0, 4)),
            pl.BlockSpec((None, s, MIX_W), lambda bb, i: (bb, 0, 2)),
        ],
        out_specs=pl.BlockSpec((None, t, MIX_W), lambda bb, i: (bb, i, 0)),
        out_shape=jax.ShapeDtypeStruct((b, s, MIX_W), BF16),
        scratch_shapes=[
            pltpu.VMEM((HEADS, t, 1), F32),
            pltpu.VMEM((HEADS, t, 1), F32),
            pltpu.VMEM((HEADS, t, HEAD_DIM), F32),
        ],
        compiler_params=_params(("arbitrary", "arbitrary")),
    )(p128, p128, pv)


def _layer_norm(z, g, b):
    mu = jnp.mean(z, axis=1, keepdims=True)
    zc = z - mu
    var = jnp.mean(zc * zc, axis=1, keepdims=True)
    return zc * lax.rsqrt(var + LN_EPS) * g + b


def _outproj_ln_kernel(oa_ref, ob_ref, oc_ref, od_ref, w_ref, x_ref, g_ref, b_ref, xo_ref, xb_ref):
    y = jnp.dot(oa_ref[...], w_ref[0:MIX_W, :], preferred_element_type=F32)
    y = y + jnp.dot(ob_ref[...], w_ref[MIX_W:2 * MIX_W, :], preferred_element_type=F32)
    y = y + jnp.dot(oc_ref[...], w_ref[2 * MIX_W:3 * MIX_W, :], preferred_element_type=F32)
    y = y + jnp.dot(od_ref[...], w_ref[3 * MIX_W:4 * MIX_W, :], preferred_element_type=F32)
    out = _layer_norm(ALPHA * x_ref[...] + y, g_ref[...], b_ref[...])
    xo_ref[...] = out
    xb_ref[...] = out.astype(BF16)


def _outproj_ln(oa, ob, oc, od, w_bf, x, g, b):
    t, d = x.shape
    tm = min(OUT_TM, t)
    mix_spec = pl.BlockSpec((tm, MIX_W), lambda i: (i, 0))
    row_spec = pl.BlockSpec((tm, d), lambda i: (i, 0))
    vec_spec = pl.BlockSpec((1, d), lambda i: (0, 0))
    return pl.pallas_call(
        _outproj_ln_kernel,
        grid=(t // tm,),
        in_specs=[mix_spec, mix_spec, mix_spec, mix_spec,
                  pl.BlockSpec((d, d), lambda i: (0, 0)), row_spec, vec_spec, vec_spec],
        out_specs=[row_spec, row_spec],
        out_shape=[jax.ShapeDtypeStruct((t, d), F32), jax.ShapeDtypeStruct((t, d), BF16)],
        compiler_params=_params(("arbitrary",)),
    )(oa, ob, oc, od, w_bf, x, g, b)


def _first_argmax(vals):
    best, arg = vals[0], jnp.zeros(vals[0].shape, I32)
    for j in range(1, len(vals)):
        better = vals[j] > best
        arg = jnp.where(better, j, arg)
        best = jnp.where(better, vals[j], best)
    return arg, best


def _pick(rows, idx):
    out = rows[0]
    for j in range(1, len(rows)):
        out = jnp.where(idx == j, rows[j], out)
    return out


def _router_kernel(x_ref, rwt_ref, bias_ref, e_ref, gate_ref, rank_ref, cnt_ref, carry_ref):
    tm = x_ref.shape[0]

    @pl.when(pl.program_id(0) == 0)
    def _():
        carry_ref[...] = jnp.zeros(carry_ref.shape, F32)

    logits = lax.dot_general(rwt_ref[...], x_ref[...], (((1,), (1,)), ((), ())),
                             precision=lax.Precision.HIGHEST, preferred_element_type=F32)
    aff = 1.0 / (1.0 + jnp.exp(-logits))
    biased = aff + bias_ref[...]
    sel_rows = [biased[r:r + 1, :] for r in range(N_EXPERTS)]
    aff_rows = [aff[r:r + 1, :] for r in range(N_EXPERTS)]

    group_scores = []
    for g in range(N_GROUPS):
        r = sel_rows[g * EXPERTS_PER_GROUP:(g + 1) * EXPERTS_PER_GROUP]
        best_pair = r[0] + r[1]
        for a in range(EXPERTS_PER_GROUP):
            for c in range(a + 1, EXPERTS_PER_GROUP):
                if (a, c) != (0, 1):
                    best_pair = jnp.maximum(best_pair, r[a] + r[c])
        group_scores.append(best_pair)
    grp, _ = _first_argmax(group_scores)

    in_sel = [_pick([sel_rows[g * EXPERTS_PER_GROUP + j] for g in range(N_GROUPS)], grp)
              for j in range(EXPERTS_PER_GROUP)]
    in_aff = [_pick([aff_rows[g * EXPERTS_PER_GROUP + j] for g in range(N_GROUPS)], grp)
              for j in range(EXPERTS_PER_GROUP)]
    first, _ = _first_argmax(in_sel)
    second, _ = _first_argmax([jnp.where(first == j, -jnp.inf, in_sel[j]) for j in range(EXPERTS_PER_GROUP)])
    a0 = _pick(in_aff, first)
    a1 = _pick(in_aff, second)
    e0 = grp * EXPERTS_PER_GROUP + first
    e1 = grp * EXPERTS_PER_GROUP + second
    e_ref[0:1, :] = e0
    e_ref[1:2, :] = e1
    gate_ref[0:1, :] = a0 / (a0 + a1)
    gate_ref[1:2, :] = a1 / (a0 + a1)

    e_iota = lax.broadcasted_iota(I32, (N_EXPERTS, tm), 0)
    one_hot = ((e_iota == e0) | (e_iota == e1)).astype(BF16)
    before = (lax.broadcasted_iota(I32, (tm, tm), 0) < lax.broadcasted_iota(I32, (tm, tm), 1)).astype(BF16)
    prior = jnp.dot(one_hot, before, preferred_element_type=F32) + carry_ref[...]
    rank_ref[0:1, :] = jnp.sum(jnp.where(e_iota == e0, prior, 0.0), axis=0, keepdims=True).astype(I32)
    rank_ref[1:2, :] = jnp.sum(jnp.where(e_iota == e1, prior, 0.0), axis=0, keepdims=True).astype(I32)
    carry_ref[...] = carry_ref[...] + jnp.sum(one_hot.astype(F32), axis=1, keepdims=True)
    cnt_ref[...] = jnp.broadcast_to(carry_ref[...], cnt_ref.shape).astype(I32)


def _router(x, rw_t, bias):
    t, d = x.shape
    tm = min(ROUTER_TM, t)
    tok_spec = pl.BlockSpec((2, tm), lambda i: (0, i))
    return pl.pallas_call(
        _router_kernel,
        grid=(t // tm,),
        in_specs=[pl.BlockSpec((tm, d), lambda i: (i, 0)),
                  pl.BlockSpec((N_EXPERTS, d), lambda i: (0, 0)),
                  pl.BlockSpec((N_EXPERTS, 1), lambda i: (0, 0))],
        out_specs=[tok_spec, tok_spec, tok_spec, pl.BlockSpec((N_EXPERTS, LANES), lambda i: (0, 0))],
        out_shape=[jax.ShapeDtypeStruct((2, t), I32), jax.ShapeDtypeStruct((2, t), F32),
                   jax.ShapeDtypeStruct((2, t), I32), jax.ShapeDtypeStruct((N_EXPERTS, LANES), I32)],
        scratch_shapes=[pltpu.VMEM((N_EXPERTS, 1), F32)],
        compiler_params=_params(("arbitrary",)),
    )(x, rw_t, bias)


def _row_copy(src, src_row, dst, dst_row, sem):
    return pltpu.make_async_copy(src.at[pl.ds(src_row, 1)], dst.at[pl.ds(dst_row, 1)], sem)


def _dispatch_kernel(dest_ref, x_hbm, slots_in_hbm, slots_hbm, sem, *, tm, n_tok):
    del slots_in_hbm
    base = pl.program_id(0) * tm

    def issue(r, carry):
        tok = base + r
        _row_copy(x_hbm, tok, slots_hbm, dest_ref[tok], sem).start()
        _row_copy(x_hbm, tok, slots_hbm, dest_ref[n_tok + tok], sem).start()
        return carry

    def drain(r, carry):
        _row_copy(x_hbm, 0, slots_hbm, 0, sem).wait()
        _row_copy(x_hbm, 0, slots_hbm, 0, sem).wait()
        return carry

    lax.fori_loop(0, tm, issue, 0)
    lax.fori_loop(0, tm, drain, 0)


def _dispatch(dest_flat, x, n_slots):
    t, d = x.shape
    tm = min(DISPATCH_TM, t)
    return pl.pallas_call(
        functools.partial(_dispatch_kernel, tm=tm, n_tok=t),
        grid_spec=pltpu.PrefetchScalarGridSpec(
            num_scalar_prefetch=1,
            grid=(t // tm,),
            in_specs=[pl.BlockSpec(memory_space=pl.ANY), pl.BlockSpec(memory_space=pl.ANY)],
            out_specs=pl.BlockSpec(memory_space=pl.ANY),
            scratch_shapes=[pltpu.SemaphoreType.DMA(())],
        ),
        out_shape=jax.ShapeDtypeStruct((n_slots, d), x.dtype),
        input_output_aliases={2: 0},
        compiler_params=_params(("arbitrary",)),
    )(dest_flat, x, jnp.zeros((n_slots, d), x.dtype))


def _expert_kernel(be_ref, na_ref, xs_ref, wg_ref, wu_ref, wd_ref, ys_ref):
    del be_ref

    @pl.when(pl.program_id(0) < na_ref[0])
    def _():
        xb = xs_ref[...].astype(BF16)
        gate = jnp.dot(xb, wg_ref[...], preferred_element_type=F32)
        up = jnp.dot(xb, wu_ref[...], preferred_element_type=F32)
        hidden = gate * (1.0 / (1.0 + jnp.exp(-gate))) * up
        ys_ref[...] = jnp.dot(hidden.astype(BF16), wd_ref[...], preferred_element_type=F32)

    @pl.when(pl.program_id(0) >= na_ref[0])
    def _():
        ys_ref[...] = jnp.zeros(ys_ref.shape, F32)


def _experts(block_expert, n_active, xs, wg, wu, wd):
    n_slots, d = xs.shape
    tb = EXPERT_TB
    f = wg.shape[2]

    def row_map(n, be, na):
        return (jnp.minimum(n, na[0] - 1), 0)

    def w_map(n, be, na):
        return (be[jnp.minimum(n, na[0] - 1)], 0, 0)

    return pl.pallas_call(
        _expert_kernel,
        grid_spec=pltpu.PrefetchScalarGridSpec(
            num_scalar_prefetch=2,
            grid=(n_slots // tb,),
            in_specs=[pl.BlockSpec((tb, d), row_map),
                      pl.BlockSpec((None, d, f), w_map),
                      pl.BlockSpec((None, d, f), w_map),
                      pl.BlockSpec((None, f, d), w_map)],
            out_specs=pl.BlockSpec((tb, d), lambda n, be, na: (n, 0)),
        ),
        out_shape=jax.ShapeDtypeStruct((n_slots, d), F32),
        compiler_params=_params(("arbitrary",)),
    )(block_expert, n_active, xs, wg, wu, wd)


def _combine_ln_kernel(dest_ref, x_ref, gate_ref, g_ref, b_ref, ys_hbm, xo_ref, xb_ref, buf_ref, sem,
                       *, n_tok):
    tm = x_ref.shape[0]
    base = pl.program_id(0) * tm

    def issue(r, carry):
        tok = base + r
        _row_copy(ys_hbm, dest_ref[tok], buf_ref.at[0], r, sem).start()
        _row_copy(ys_hbm, dest_ref[n_tok + tok], buf_ref.at[1], r, sem).start()
        return carry

    def drain(r, carry):
        _row_copy(ys_hbm, 0, buf_ref.at[0], 0, sem).wait()
        _row_copy(ys_hbm, 0, buf_ref.at[1], 0, sem).wait()
        return carry

    lax.fori_loop(0, tm, issue, 0)
    lax.fori_loop(0, tm, drain, 0)
    gate = gate_ref[...]
    moe = buf_ref[0] * gate[:, 0:1] + buf_ref[1] * gate[:, 1:2]
    out = _layer_norm(ALPHA * x_ref[...] + moe, g_ref[...], b_ref[...])
    xo_ref[...] = out
    xb_ref[...] = out.astype(BF16)


def _combine_ln(dest_flat, x, gate_t, g, b, ys):
    t, d = x.shape
    tm = min(COMBINE_TM, t)
    row_spec = pl.BlockSpec((tm, d), lambda i, dest: (i, 0))
    vec_spec = pl.BlockSpec((1, d), lambda i, dest: (0, 0))
    return pl.pallas_call(
        functools.partial(_combine_ln_kernel, n_tok=t),
        grid_spec=pltpu.PrefetchScalarGridSpec(
            num_scalar_prefetch=1,
            grid=(t // tm,),
            in_specs=[row_spec, pl.BlockSpec((tm, 2), lambda i, dest: (i, 0)), vec_spec, vec_spec,
                      pl.BlockSpec(memory_space=pl.ANY)],
            out_specs=[row_spec, row_spec],
            scratch_shapes=[pltpu.VMEM((2, tm, d), F32), pltpu.SemaphoreType.DMA(())],
        ),
        out_shape=[jax.ShapeDtypeStruct((t, d), F32), jax.ShapeDtypeStruct((t, d), BF16)],
        compiler_params=_params(("arbitrary",)),
    )(dest_flat, x, gate_t, g, b, ys)


def _rope_tables(positions):
    pos = positions.reshape(-1).astype(F32)[:, None]

    def cos_sin(d):
        inv = jnp.power(ROPE_THETA, -jnp.arange(0, d, 2, dtype=F32) / d)
        ang = pos * inv
        return jnp.cos(ang), jnp.sin(ang)

    c, s = cos_sin(HEAD_DIM)
    scale = HEAD_DIM ** -0.25
    t128 = (jnp.concatenate([c, c], axis=1) * scale, jnp.concatenate([-s, s], axis=1) * scale)
    c, s = cos_sin(IDX_DIM)
    z = jnp.zeros_like(s)
    scale = DIFF_DIM ** -0.25
    t64 = (jnp.concatenate([c, c, c, c], axis=1) * scale,
           jnp.concatenate([-s, z, -s, z], axis=1) * scale,
           jnp.concatenate([z, s, z, s], axis=1) * scale)
    return t128, t64


def _split_w_in(w):
    def cols(name):
        lo, hi = _OFF[name]
        return w[:, lo:hi]

    d = w.shape[0]
    w128 = jnp.concatenate([cols(n) for n in ("a_q", "c_q", "c_k", "d_q", "d_k", "a_k")], axis=1)
    w64 = jnp.concatenate([cols("i_q"), cols("b_q"), cols("b_k"), cols("i_k"), cols("i_k"),
                           jnp.zeros((d, LANES), w.dtype)], axis=1)
    wv = jnp.concatenate([cols("b_v"), cols("c_v"), cols("d_v"), cols("a_v"),
                          jnp.zeros((d, LANES), w.dtype)], axis=1)
    wiw = jnp.concatenate([cols("i_w"), jnp.zeros((d, LANES - IDX_HEADS), w.dtype)], axis=1)
    return w128.astype(BF16), w64.astype(BF16), wv.astype(BF16), wiw.astype(BF16)


def _moe_layout(e, rank, counts, tb):
    t = e.shape[1]
    n_slots = 2 * t + N_EXPERTS * tb
    padded = (counts + tb - 1) // tb * tb
    pend = jnp.cumsum(padded)
    pstart = pend - padded
    dest = (pstart[e] + rank).astype(I32).reshape(-1)
    blocks = jnp.arange(n_slots // tb, dtype=I32) * tb
    block_expert = jnp.minimum(jnp.searchsorted(pend, blocks, side="right"), N_EXPERTS - 1).astype(I32)
    n_active = (pend[-1:] // tb).astype(I32)
    return dest, block_expert, n_active, n_slots


def kernel(x, positions, w_in, w_out, diff_lambda, diff_norm_g, ln_mix_g, ln_mix_b, router_w, router_bias,
           w_gate, w_up, w_down, ln_ffn_g, ln_ffn_b):
    b, s, d = x.shape
    t = b * s
    n_keep = min(DSA_TOPK_MAX, s // 4)
    t128, t64 = _rope_tables(positions)
    rw_t = router_w.T
    bias = router_bias.reshape(N_EXPERTS, 1).astype(F32)
    xf = x.reshape(t, d)
    xb = xf.astype(BF16)
    for layer in range(DEPTH):
        w128, w64, wv, wiw = _split_w_in(w_in[layer])
        p128 = _proj(xb, w128, t128, "rope128", 896, BF16).reshape(b, s, -1)
        p64 = _proj(xb, w64, t64, "rope64", 768, BF16).reshape(b, s, -1)
        pv = _proj(xb, wv, (), "none", 896, BF16).reshape(b, s, -1)
        iw = _proj(xb, wiw, (), "none", LANES, F32).reshape(b, s, -1)

        lam_init = 0.8 - 0.6 * math.exp(-0.3 * layer)
        o_a = _dsa(p128, p64, pv, iw, n_keep)
        o_b = _diff(p64, pv, diff_lambda[layer], diff_norm_g[layer].reshape(1, HEAD_DIM), lam_init)
        o_c = _moba(p128, pv, _kmean(p128))
        o_d = _dilated(p128, pv)
        xf, xb = _outproj_ln(o_a.reshape(t, MIX_W), o_b.reshape(t, MIX_W), o_c.reshape(t, MIX_W),
                             o_d.reshape(t, MIX_W), w_out[layer].astype(BF16), xf,
                             ln_mix_g[layer].reshape(1, d), ln_mix_b[layer].reshape(1, d))

        e, gate, rank, counts = _router(xf, rw_t, bias)
        dest, block_expert, n_active, n_slots = _moe_layout(e, rank, counts[:, 0], EXPERT_TB)
        xs = _dispatch(dest, xf, n_slots)
        ys = _experts(block_expert, n_active, xs, w_gate[layer].astype(BF16), w_up[layer].astype(BF16),
                      w_down[layer].astype(BF16))
        xf, xb = _combine_ln(dest, xf, gate.T, ln_ffn_g[layer].reshape(1, d), ln_ffn_b[layer].reshape(1, d), ys)
    return xf.reshape(b, s, d)
```

```python
import functools
import math

import numpy as np
import jax
import jax.numpy as jnp
from jax import lax
from jax.experimental import pallas as pl
from jax.experimental.pallas import tpu as pltpu

F32 = jnp.float32
BF16 = jnp.bfloat16
I32 = jnp.int32

D_MODEL = 2048
HEAD_DIM = 128
HEADS = 4
MIX_W = HEADS * HEAD_DIM
IDX_HEADS = 16
IDX_DIM = 64
DSA_TOPK_MAX = 256
DIFF_DIM = 64
MOBA_BLOCK = 256
MOBA_TOPK = 3
DILATED_PATTERNS = ((128, 1), (512, 4), (2048, 16))
ROPE_THETA = 10000.0
N_EXPERTS = 16
N_GROUPS = 4
EXPERTS_PER_GROUP = 4
LN_EPS = 1e-5
DEPTH = 2
ALPHA = (2 * DEPTH) ** 0.25
LOG2E = math.log2(math.e)

LANES = 128
SUBLANES = 8
NEG = -1e30
INT_MIN = -(2 ** 31)
NEG_INF_ORDER = 0x007FFFFF
VMEM_LIMIT = 56 * 1024 * 1024

PROJ_TM = 512
ATT_T = 256
OUT_TM = 256
ROUTER_TM = 256
DISPATCH_TM = 256
EXPERT_TB = 256
COMBINE_TM = 256
V_ROWS = 3 * MIX_W + HEAD_DIM

_OFF = {}
_o = 0
for _name, _n in (("a_q", 512), ("a_k", 128), ("a_v", 128), ("i_q", 1024), ("i_k", 64), ("i_w", 16),
                  ("b_q", 512), ("b_k", 512), ("b_v", 512), ("c_q", 512), ("c_k", 512), ("c_v", 512),
                  ("d_q", 512), ("d_k", 512), ("d_v", 512)):
    _OFF[_name] = (_o, _o + _n)
    _o += _n


def _params(sem=None):
    return pltpu.CompilerParams(dimension_semantics=sem, vmem_limit_bytes=VMEM_LIMIT)


def _dot_nt(a, b):
    return lax.dot_general(a, b, (((1,), (1,)), ((), ())), preferred_element_type=F32)


def _proj_kernel(x_ref, w_ref, scale_ref, *rest, mode):
    o_ref = rest[-1]
    acc = jnp.dot(x_ref[...], w_ref[...], preferred_element_type=F32)
    n_groups = acc.shape[1] // LANES
    for g in range(n_groups):
        y = acc[:, g * LANES:(g + 1) * LANES]
        if mode == "rope128":
            cos_ref, sin_ref = rest[0], rest[1]
            r = y * cos_ref[...] + pltpu.roll(y, 64, 1) * sin_ref[...]
        else:
            cos_ref, sina_ref, sinb_ref = rest[0], rest[1], rest[2]
            r = (y * cos_ref[...] + pltpu.roll(y, 96, 1) * sina_ref[...]
                 + pltpu.roll(y, 32, 1) * sinb_ref[...])
        r = r * scale_ref[:, g * LANES:(g + 1) * LANES]
        o_ref[:, g * LANES:(g + 1) * LANES] = r.astype(o_ref.dtype)


def _proj(x_bf, w_bf, col_scale, tables, mode, tn):
    t, d = x_bf.shape
    n = w_bf.shape[1]
    tm = min(PROJ_TM, t)
    in_specs = [pl.BlockSpec((tm, d), lambda j, i: (i, 0)),
                pl.BlockSpec((d, tn), lambda j, i: (0, j)),
                pl.BlockSpec((1, tn), lambda j, i: (0, j))]
    in_specs += [pl.BlockSpec((tm, LANES), lambda j, i: (i, 0)) for _ in tables]
    return pl.pallas_call(
        functools.partial(_proj_kernel, mode=mode),
        grid=(n // tn, t // tm),
        in_specs=in_specs,
        out_specs=pl.BlockSpec((tm, tn), lambda j, i: (i, j)),
        out_shape=jax.ShapeDtypeStruct((t, n), BF16),
        compiler_params=_params(("arbitrary", "arbitrary")),
    )(x_bf, w_bf, col_scale, *tables)


def _proj_t_kernel(x_ref, wt_ref, vt_ref, iwt_ref):
    acc = _dot_nt(wt_ref[...], x_ref[...])
    vt_ref[...] = acc[:V_ROWS].astype(vt_ref.dtype)
    iwt_ref[...] = acc[V_ROWS:]


def _proj_t(x_bf, wt_bf):
    t, d = x_bf.shape
    tm = min(ATT_T, t)
    rows = wt_bf.shape[0]
    return pl.pallas_call(
        _proj_t_kernel,
        grid=(t // tm,),
        in_specs=[pl.BlockSpec((tm, d), lambda i: (i, 0)),
                  pl.BlockSpec((rows, d), lambda i: (0, 0))],
        out_specs=[pl.BlockSpec((None, V_ROWS, tm), lambda i: (i, 0, 0)),
                   pl.BlockSpec((None, IDX_HEADS, tm), lambda i: (i, 0, 0))],
        out_shape=[jax.ShapeDtypeStruct((t // tm, V_ROWS, tm), BF16),
                   jax.ShapeDtypeStruct((t // tm, IDX_HEADS, tm), F32)],
        compiler_params=_params(("arbitrary",)),
    )(x_bf, wt_bf)


def _online_updates(scores, mask, values, m_ref, l_ref, acc_ref, weight=None):
    probs, alphas = [], []
    masks = mask if isinstance(mask, list) else [mask] * len(scores)
    for idx, (s_t, mask) in enumerate(zip(scores, masks)):
        if mask is not None:
            s_t = jnp.where(mask, s_t, NEG)
        m_prev = m_ref[idx]
        m_new = jnp.maximum(m_prev, jnp.max(s_t, axis=0, keepdims=True))
        p = jnp.exp2(s_t - m_new)
        if weight is not None:
            p = p * weight
        elif mask is not None:
            p = jnp.where(mask, p, 0.0)
        alpha = jnp.exp2(m_prev - m_new)
        l_ref[idx] = alpha * l_ref[idx] + jnp.sum(p, axis=0, keepdims=True)
        m_ref[idx] = m_new
        probs.append(p.astype(BF16))
        alphas.append(alpha)
    for idx, (p, alpha) in enumerate(zip(probs, alphas)):
        acc_ref[idx] = alpha * acc_ref[idx] + jnp.dot(values[idx], p, preferred_element_type=F32)


def _init_softmax(m_ref, l_ref, acc_ref):
    m_ref[...] = jnp.full(m_ref.shape, NEG, F32)
    l_ref[...] = jnp.zeros(l_ref.shape, F32)
    acc_ref[...] = jnp.zeros(acc_ref.shape, F32)


def _softmax_scratch(n, t):
    return [pltpu.VMEM((n, 1, t), F32), pltpu.VMEM((n, 1, t), F32), pltpu.VMEM((n, HEAD_DIM, t), F32)]


def _causal_t(t):
    return lax.broadcasted_iota(I32, (t, t), 0) <= lax.broadcasted_iota(I32, (t, t), 1)


def _head(ref_or_val, h):
    return ref_or_val[:, h * HEAD_DIM:(h + 1) * HEAD_DIM]


def _order_to_f32(u):
    key = u ^ jnp.int32(INT_MIN)
    bits = jnp.where(key < 0, key ^ jnp.int32(0x7FFFFFFF), key)
    return lax.bitcast_convert_type(bits, F32)


def _dsa_kernel(aq_ref, ak_ref, avt_ref, iq_ref, ik_ref, iwt_ref, o_ref,
                score_ref, m_ref, l_ref, acc_ref, jlim_ref, *, n_keep):
    t = aq_ref.shape[0]
    i = pl.program_id(1)
    n_chunks = i + 1
    lane = lax.broadcasted_iota(I32, (1, LANES), 1)
    k_off = lax.broadcasted_iota(I32, (t, 1), 0)
    q_pos = i * t + lax.broadcasted_iota(I32, (1, t), 1)
    iw_t = iwt_ref[...]

    def score_chunk(c, carry):
        ik_c = ik_ref[pl.ds(pl.multiple_of(c * t, t), t), :]
        ik_lo = jnp.where(lane < IDX_DIM, ik_c, jnp.zeros_like(ik_c))
        ik_hi = jnp.where(lane >= IDX_DIM, ik_c, jnp.zeros_like(ik_c))
        score = jnp.zeros((t, t), F32)
        for j in range(IDX_HEADS // 2):
            q_pair = iq_ref[:, j * LANES:(j + 1) * LANES]
            s_even = jnp.maximum(_dot_nt(ik_lo, q_pair), 0.0)
            s_odd = jnp.maximum(_dot_nt(ik_hi, q_pair), 0.0)
            score = score + s_even * iw_t[2 * j:2 * j + 1, :] + s_odd * iw_t[2 * j + 1:2 * j + 2, :]
        score_ref[c] = jnp.where(c * t + k_off <= q_pos, score, -jnp.inf)
        return carry

    lax.fori_loop(0, n_chunks, score_chunk, 0)

    def count(pred):
        def body(c, part):
            hit = jnp.where(pred(score_ref[c], c), 1, 0)
            return part + jnp.sum(hit.reshape(t // SUBLANES, SUBLANES, t), axis=0)
        part = lax.fori_loop(0, n_chunks, body, jnp.zeros((SUBLANES, t), I32))
        return jnp.sum(part, axis=0, keepdims=True)

    def bisect(it, thr_u):
        cand_u = thr_u | jnp.left_shift(jnp.int32(1), 31 - it)
        cand = _order_to_f32(cand_u)
        cnt = count(lambda sc, c: sc >= cand)
        return jnp.where(cnt >= n_keep, cand_u, thr_u)

    thr_u = lax.fori_loop(0, 32, bisect, jnp.zeros((1, t), I32))
    below = (thr_u >= 0) & (thr_u < NEG_INF_ORDER)
    thr = jnp.where(below, -jnp.inf, _order_to_f32(thr_u))

    n_gt = count(lambda sc, c: sc > thr)
    n_eq = count(lambda sc, c: sc == thr)
    need = n_keep - n_gt
    s_total = score_ref.shape[0] * t
    jlim_ref[...] = jnp.full((1, t), s_total, I32)
    excess = jnp.max(jnp.where((n_eq > need) & (thr > -jnp.inf), 1, 0))

    @pl.when(excess > 0)
    def _():
        n_bits = max(1, (s_total - 1).bit_length())

        def bisect_idx(it, j):
            cand = j | jnp.left_shift(jnp.int32(1), n_bits - 1 - it)
            cnt = count(lambda sc, c: (sc == thr) & (c * t + k_off < cand))
            return jnp.where(cnt < need, cand, j)

        jlim_ref[...] = lax.fori_loop(0, n_bits, bisect_idx, jnp.zeros((1, t), I32))

    jlim = jlim_ref[...]

    _init_softmax(m_ref, l_ref, acc_ref)

    def attend(c, carry):
        start = pl.multiple_of(c * t, t)
        k_c = ak_ref[pl.ds(start, t), :]
        v_t = avt_ref[c]
        sc = score_ref[c]
        k_pos = c * t + k_off
        sel = ((sc > thr) | ((sc == thr) & (k_pos <= jlim))) & (k_pos <= q_pos)
        scores = [_dot_nt(k_c, _head(aq_ref, h)) for h in range(HEADS)]
        _online_updates(scores, sel, [v_t] * HEADS, m_ref, l_ref, acc_ref)
        return carry

    lax.fori_loop(0, n_chunks, attend, 0)
    for h in range(HEADS):
        o_ref[:, h * HEAD_DIM:(h + 1) * HEAD_DIM] = (acc_ref[h] / l_ref[h]).T.astype(o_ref.dtype)


def _dsa(p128, p64, vt, iwt, n_keep):
    b, s, _ = p128.shape
    t = min(ATT_T, s)
    nc = s // t
    return pl.pallas_call(
        functools.partial(_dsa_kernel, n_keep=n_keep),
        grid=(b, nc),
        in_specs=[
            pl.BlockSpec((None, t, MIX_W), lambda bb, i: (bb, i, 0)),
            pl.BlockSpec((None, s, HEAD_DIM), lambda bb, i: (bb, 0, 20)),
            pl.BlockSpec((None, nc, HEAD_DIM, t), lambda bb, i: (bb, 0, 12, 0)),
            pl.BlockSpec((None, t, 1024), lambda bb, i: (bb, i, 0)),
            pl.BlockSpec((None, s, LANES), lambda bb, i: (bb, 0, 16)),
            pl.BlockSpec((None, None, IDX_HEADS, t), lambda bb, i: (bb, i, 0, 0)),
        ],
        out_specs=pl.BlockSpec((None, t, MIX_W), lambda bb, i: (bb, i, 0)),
        out_shape=jax.ShapeDtypeStruct((b, s, MIX_W), BF16),
        scratch_shapes=[pltpu.VMEM((nc, t, t), F32)] + _softmax_scratch(HEADS, t)
        + [pltpu.VMEM((1, t), I32)],
        compiler_params=_params(("arbitrary", "arbitrary")),
    )(p128, p128, vt, p64, p64, iwt)


def _diff_kernel(q_ref, k_ref, vt_ref, lam_ref, g_ref, o_ref, m_ref, l_ref, acc_ref, *, lam_init):
    t = q_ref.shape[0]
    i = pl.program_id(1)
    lane = lax.broadcasted_iota(I32, (1, LANES), 1)
    _init_softmax(m_ref, l_ref, acc_ref)

    def step(c, mask):
        k_c = k_ref[pl.ds(pl.multiple_of(c * t, t), t), :]
        v_c = vt_ref[c]
        scores, values = [], []
        for h in range(HEADS):
            q_h = _head(q_ref, h)
            k_h = _head(k_c, h)
            q_1 = jnp.where(lane < DIFF_DIM, q_h, jnp.zeros_like(q_h))
            q_2 = jnp.where(lane >= DIFF_DIM, q_h, jnp.zeros_like(q_h))
            scores += [_dot_nt(k_h, q_1), _dot_nt(k_h, q_2)]
            values += [v_c[h * HEAD_DIM:(h + 1) * HEAD_DIM, :]] * 2
        _online_updates(scores, mask, values, m_ref, l_ref, acc_ref)

    def full_step(c, carry):
        step(c, None)
        return carry

    lax.fori_loop(0, i, full_step, 0)
    step(i, _causal_t(t))

    lam_p = lam_ref[...]
    lam = (jnp.exp(jnp.sum(lam_p[0:1] * lam_p[1:2], axis=1, keepdims=True))
           - jnp.exp(jnp.sum(lam_p[2:3] * lam_p[3:4], axis=1, keepdims=True)) + lam_init)
    for h in range(HEADS):
        o = acc_ref[2 * h] / l_ref[2 * h] - lam * (acc_ref[2 * h + 1] / l_ref[2 * h + 1])
        o = o * lax.rsqrt(jnp.mean(o * o, axis=0, keepdims=True) + LN_EPS) * g_ref[...]
        o_ref[:, h * HEAD_DIM:(h + 1) * HEAD_DIM] = (o * (1.0 - lam_init)).T.astype(o_ref.dtype)


def _diff(p64, vt, lam_p, g_col, lam_init):
    b, s, _ = p64.shape
    t = min(ATT_T, s)
    nc = s // t
    return pl.pallas_call(
        functools.partial(_diff_kernel, lam_init=lam_init),
        grid=(b, nc),
        in_specs=[
            pl.BlockSpec((None, t, MIX_W), lambda bb, i: (bb, i, 2)),
            pl.BlockSpec((None, s, MIX_W), lambda bb, i: (bb, 0, 3)),
            pl.BlockSpec((None, nc, MIX_W, t), lambda bb, i: (bb, 0, 0, 0)),
            pl.BlockSpec((4, DIFF_DIM), lambda bb, i: (0, 0)),
            pl.BlockSpec((HEAD_DIM, 1), lambda bb, i: (0, 0)),
        ],
        out_specs=pl.BlockSpec((None, t, MIX_W), lambda bb, i: (bb, i, 0)),
        out_shape=jax.ShapeDtypeStruct((b, s, MIX_W), BF16),
        scratch_shapes=_softmax_scratch(2 * HEADS, t),
        compiler_params=_params(("arbitrary", "arbitrary")),
    )(p64, p64, vt, lam_p, g_col)


def _kmean_kernel(k_ref, o_ref):
    n_blk = k_ref.shape[0] // MOBA_BLOCK
    o_ref[...] = jnp.zeros(o_ref.shape, o_ref.dtype)
    for n in range(n_blk):
        blk = k_ref[n * MOBA_BLOCK:(n + 1) * MOBA_BLOCK, :].astype(F32)
        o_ref[n:n + 1, :] = jnp.mean(blk, axis=0, keepdims=True)


def _kmean(p128):
    b, s, _ = p128.shape
    rows = max(SUBLANES, s // MOBA_BLOCK)
    return pl.pallas_call(
        _kmean_kernel,
        grid=(b,),
        in_specs=[pl.BlockSpec((None, s, MIX_W), lambda bb: (bb, 0, 2))],
        out_specs=pl.BlockSpec((None, rows, MIX_W), lambda bb: (bb, 0, 0)),
        out_shape=jax.ShapeDtypeStruct((b, rows, MIX_W), F32),
        compiler_params=_params(("arbitrary",)),
    )(p128)


def _moba_kernel(q_ref, k_ref, vt_ref, km_ref, o_ref, sel_ref, m_ref, l_ref, acc_ref):
    t = q_ref.shape[0]
    n_blk = km_ref.shape[0]
    i = pl.program_id(1)
    blk = lax.broadcasted_iota(I32, (n_blk, 1), 0)
    _init_softmax(m_ref, l_ref, acc_ref)

    for h in range(HEADS):
        gate = _dot_nt(_head(km_ref, h).astype(BF16), _head(q_ref, h))
        gate = jnp.where(blk < i, gate, -jnp.inf)
        sel = jnp.zeros(gate.shape, F32)
        for _ in range(MOBA_TOPK):
            best = jnp.max(gate, axis=0, keepdims=True)
            first = jnp.min(jnp.where(gate == best, blk, n_blk), axis=0, keepdims=True)
            pick = (blk == first) & (best > -jnp.inf)
            sel = jnp.where(pick, 1.0, sel)
            gate = jnp.where(pick, -jnp.inf, gate)
        sel_ref[h] = sel

    def step(c, diagonal):
        k_c = k_ref[pl.ds(pl.multiple_of(c * t, t), t), :]
        v_c = vt_ref[c]
        scores = [_dot_nt(_head(k_c, h), _head(q_ref, h)) for h in range(HEADS)]
        values = [v_c[h * HEAD_DIM:(h + 1) * HEAD_DIM, :] for h in range(HEADS)]
        if diagonal:
            masks = _causal_t(t)
        else:
            masks = [sel_ref[h, pl.ds(c, 1), :] > 0.0 for h in range(HEADS)]
        _online_updates(scores, masks, values, m_ref, l_ref, acc_ref)

    def past_step(c, carry):
        step(c, False)
        return carry

    lax.fori_loop(0, i, past_step, 0)
    step(i, True)
    for h in range(HEADS):
        o_ref[:, h * HEAD_DIM:(h + 1) * HEAD_DIM] = (acc_ref[h] / l_ref[h]).T.astype(o_ref.dtype)


def _moba(p128, vt, kmean):
    b, s, _ = p128.shape
    t = MOBA_BLOCK
    nc = s // t
    rows = kmean.shape[1]
    return pl.pallas_call(
        _moba_kernel,
        grid=(b, nc),
        in_specs=[
            pl.BlockSpec((None, t, MIX_W), lambda bb, i: (bb, i, 1)),
            pl.BlockSpec((None, s, MIX_W), lambda bb, i: (bb, 0, 2)),
            pl.BlockSpec((None, nc, MIX_W, t), lambda bb, i: (bb, 0, 1, 0)),
            pl.BlockSpec((None, rows, MIX_W), lambda bb, i: (bb, 0, 0)),
        ],
        out_specs=pl.BlockSpec((None, t, MIX_W), lambda bb, i: (bb, i, 0)),
        out_shape=jax.ShapeDtypeStruct((b, s, MIX_W), BF16),
        scratch_shapes=[pltpu.VMEM((HEADS, rows, t), F32)] + _softmax_scratch(HEADS, t),
        compiler_params=_params(("arbitrary", "arbitrary")),
    )(p128, p128, vt, kmean)


def _dilated_kernel(q_ref, k_ref, vt_ref, o_ref, m_ref, l_ref, acc_ref):
    t = q_ref.shape[0]
    i = pl.program_id(1)
    _init_softmax(m_ref, l_ref, acc_ref)
    max_window = max(w for w, _ in DILATED_PATTERNS)
    first = jnp.maximum(i - max_window // t, 0)
    rel = lax.broadcasted_iota(I32, (t, t), 1) - lax.broadcasted_iota(I32, (t, t), 0)

    def step(c, carry):
        k_c = k_ref[pl.ds(pl.multiple_of(c * t, t), t), :]
        v_c = vt_ref[c]
        dist = rel + (i - c) * t
        mult = jnp.zeros((t, t), F32)
        for window, dilation in DILATED_PATTERNS:
            hit = (dist >= 0) & (dist <= window) & ((dist & (dilation - 1)) == 0)
            mult = mult + jnp.where(hit, 1.0, 0.0)
        mask = mult > 0.0
        scores = [_dot_nt(_head(k_c, h), _head(q_ref, h)) for h in range(HEADS)]
        values = [v_c[h * HEAD_DIM:(h + 1) * HEAD_DIM, :] for h in range(HEADS)]
        _online_updates(scores, mask, values, m_ref, l_ref, acc_ref, weight=mult)
        return carry

    lax.fori_loop(first, i + 1, step, 0)
    for h in range(HEADS):
        o_ref[:, h * HEAD_DIM:(h + 1) * HEAD_DIM] = (acc_ref[h] / l_ref[h]).T.astype(o_ref.dtype)


def _dilated(p128, vt):
    b, s, _ = p128.shape
    t = min(ATT_T, s)
    nc = s // t
    return pl.pallas_call(
        _dilated_kernel,
        grid=(b, nc),
        in_specs=[
            pl.BlockSpec((None, t, MIX_W), lambda bb, i: (bb, i, 3)),
            pl.BlockSpec((None, s, MIX_W), lambda bb, i: (bb, 0, 4)),
            pl.BlockSpec((None, nc, MIX_W, t), lambda bb, i: (bb, 0, 2, 0)),
        ],
        out_specs=pl.BlockSpec((None, t, MIX_W), lambda bb, i: (bb, i, 0)),
        out_shape=jax.ShapeDtypeStruct((b, s, MIX_W), BF16),
        scratch_shapes=_softmax_scratch(HEADS, t),
        compiler_params=_params(("arbitrary", "arbitrary")),
    )(p128, p128, vt)


def _layer_norm(z, g, b):
    mu = jnp.mean(z, axis=1, keepdims=True)
    zc = z - mu
    var = jnp.mean(zc * zc, axis=1, keepdims=True)
    return zc * lax.rsqrt(var + LN_EPS) * g + b


def _outproj_ln_kernel(oa_ref, ob_ref, oc_ref, od_ref, w_ref, x_ref, g_ref, b_ref, xo_ref, xb_ref):
    y = jnp.dot(oa_ref[...], w_ref[0:MIX_W, :], preferred_element_type=F32)
    y = y + jnp.dot(ob_ref[...], w_ref[MIX_W:2 * MIX_W, :], preferred_element_type=F32)
    y = y + jnp.dot(oc_ref[...], w_ref[2 * MIX_W:3 * MIX_W, :], preferred_element_type=F32)
    y = y + jnp.dot(od_ref[...], w_ref[3 * MIX_W:4 * MIX_W, :], preferred_element_type=F32)
    out = _layer_norm(ALPHA * x_ref[...] + y, g_ref[...], b_ref[...])
    xo_ref[...] = out
    xb_ref[...] = out.astype(BF16)


def _outproj_ln(oa, ob, oc, od, w_bf, x, g, b):
    t, d = x.shape
    tm = min(OUT_TM, t)
    mix_spec = pl.BlockSpec((tm, MIX_W), lambda i: (i, 0))
    row_spec = pl.BlockSpec((tm, d), lambda i: (i, 0))
    vec_spec = pl.BlockSpec((1, d), lambda i: (0, 0))
    return pl.pallas_call(
        _outproj_ln_kernel,
        grid=(t // tm,),
        in_specs=[mix_spec, mix_spec, mix_spec, mix_spec,
                  pl.BlockSpec((d, d), lambda i: (0, 0)), row_spec, vec_spec, vec_spec],
        out_specs=[row_spec, row_spec],
        out_shape=[jax.ShapeDtypeStruct((t, d), F32), jax.ShapeDtypeStruct((t, d), BF16)],
        compiler_params=_params(("arbitrary",)),
    )(oa, ob, oc, od, w_bf, x, g, b)


def _first_argmax(vals):
    best, arg = vals[0], jnp.zeros(vals[0].shape, I32)
    for j in range(1, len(vals)):
        better = vals[j] > best
        arg = jnp.where(better, j, arg)
        best = jnp.where(better, vals[j], best)
    return arg, best


def _pick(rows, idx):
    out = rows[0]
    for j in range(1, len(rows)):
        out = jnp.where(idx == j, rows[j], out)
    return out


def _router_kernel(x_ref, rwt_ref, bias_ref, e_ref, gate_ref, rank_ref, cnt_ref, carry_ref):
    tm = x_ref.shape[0]

    @pl.when(pl.program_id(0) == 0)
    def _():
        carry_ref[...] = jnp.zeros(carry_ref.shape, F32)

    logits = lax.dot_general(rwt_ref[...], x_ref[...], (((1,), (1,)), ((), ())),
                             precision=lax.Precision.HIGHEST, preferred_element_type=F32)
    aff = 1.0 / (1.0 + jnp.exp(-logits))
    biased = aff + bias_ref[...]
    sel_rows = [biased[r:r + 1, :] for r in range(N_EXPERTS)]
    aff_rows = [aff[r:r + 1, :] for r in range(N_EXPERTS)]

    group_scores = []
    for g in range(N_GROUPS):
        r = sel_rows[g * EXPERTS_PER_GROUP:(g + 1) * EXPERTS_PER_GROUP]
        best_pair = r[0] + r[1]
        for a in range(EXPERTS_PER_GROUP):
            for c in range(a + 1, EXPERTS_PER_GROUP):
                if (a, c) != (0, 1):
                    best_pair = jnp.maximum(best_pair, r[a] + r[c])
        group_scores.append(best_pair)
    grp, _ = _first_argmax(group_scores)

    in_sel = [_pick([sel_rows[g * EXPERTS_PER_GROUP + j] for g in range(N_GROUPS)], grp)
              for j in range(EXPERTS_PER_GROUP)]
    in_aff = [_pick([aff_rows[g * EXPERTS_PER_GROUP + j] for g in range(N_GROUPS)], grp)
              for j in range(EXPERTS_PER_GROUP)]
    first, _ = _first_argmax(in_sel)
    second, _ = _first_argmax([jnp.where(first == j, -jnp.inf, in_sel[j]) for j in range(EXPERTS_PER_GROUP)])
    a0 = _pick(in_aff, first)
    a1 = _pick(in_aff, second)
    e0 = grp * EXPERTS_PER_GROUP + first
    e1 = grp * EXPERTS_PER_GROUP + second
    e_ref[0:1, :] = e0
    e_ref[1:2, :] = e1
    gate_ref[0:1, :] = a0 / (a0 + a1)
    gate_ref[1:2, :] = a1 / (a0 + a1)

    e_iota = lax.broadcasted_iota(I32, (N_EXPERTS, tm), 0)
    one_hot = ((e_iota == e0) | (e_iota == e1)).astype(BF16)
    before = (lax.broadcasted_iota(I32, (tm, tm), 0) < lax.broadcasted_iota(I32, (tm, tm), 1)).astype(BF16)
    prior = jnp.dot(one_hot, before, preferred_element_type=F32) + carry_ref[...]
    rank_ref[0:1, :] = jnp.sum(jnp.where(e_iota == e0, prior, 0.0), axis=0, keepdims=True).astype(I32)
    rank_ref[1:2, :] = jnp.sum(jnp.where(e_iota == e1, prior, 0.0), axis=0, keepdims=True).astype(I32)
    carry_ref[...] = carry_ref[...] + jnp.sum(one_hot.astype(F32), axis=1, keepdims=True)
    cnt_ref[...] = jnp.broadcast_to(carry_ref[...], cnt_ref.shape).astype(I32)


def _router(x, rw_t, bias):
    t, d = x.shape
    tm = min(ROUTER_TM, t)
    tok_spec = pl.BlockSpec((2, tm), lambda i: (0, i))
    return pl.pallas_call(
        _router_kernel,
        grid=(t // tm,),
        in_specs=[pl.BlockSpec((tm, d), lambda i: (i, 0)),
                  pl.BlockSpec((N_EXPERTS, d), lambda i: (0, 0)),
                  pl.BlockSpec((N_EXPERTS, 1), lambda i: (0, 0))],
        out_specs=[tok_spec, tok_spec, tok_spec, pl.BlockSpec((N_EXPERTS, LANES), lambda i: (0, 0))],
        out_shape=[jax.ShapeDtypeStruct((2, t), I32), jax.ShapeDtypeStruct((2, t), F32),
                   jax.ShapeDtypeStruct((2, t), I32), jax.ShapeDtypeStruct((N_EXPERTS, LANES), I32)],
        scratch_shapes=[pltpu.VMEM((N_EXPERTS, 1), F32)],
        compiler_params=_params(("arbitrary",)),
    )(x, rw_t, bias)


def _row_copy(src, src_row, dst, dst_row, sem):
    return pltpu.make_async_copy(src.at[pl.ds(src_row, 1)], dst.at[pl.ds(dst_row, 1)], sem)


def _dispatch_kernel(dest_ref, x_ref, slots_in_hbm, slots_hbm, sem, *, tm, n_tok):
    del slots_in_hbm
    base = pl.program_id(0) * tm

    def issue(r, carry):
        tok = base + r
        _row_copy(x_ref, r, slots_hbm, dest_ref[tok], sem).start()
        _row_copy(x_ref, r, slots_hbm, dest_ref[n_tok + tok], sem).start()
        return carry

    def drain(r, carry):
        _row_copy(x_ref, 0, slots_hbm, 0, sem).wait()
        _row_copy(x_ref, 0, slots_hbm, 0, sem).wait()
        return carry

    lax.fori_loop(0, tm, issue, 0)
    lax.fori_loop(0, tm, drain, 0)


def _dispatch(dest_flat, x, n_slots):
    t, d = x.shape
    tm = min(DISPATCH_TM, t)
    return pl.pallas_call(
        functools.partial(_dispatch_kernel, tm=tm, n_tok=t),
        grid_spec=pltpu.PrefetchScalarGridSpec(
            num_scalar_prefetch=1,
            grid=(t // tm,),
            in_specs=[pl.BlockSpec((tm, d), lambda i, dest: (i, 0)), pl.BlockSpec(memory_space=pl.ANY)],
            out_specs=pl.BlockSpec(memory_space=pl.ANY),
            scratch_shapes=[pltpu.SemaphoreType.DMA(())],
        ),
        out_shape=jax.ShapeDtypeStruct((n_slots, d), x.dtype),
        input_output_aliases={2: 0},
        compiler_params=_params(("arbitrary",)),
    )(dest_flat, x, jnp.zeros((n_slots, d), x.dtype))


def _expert_kernel(be_ref, na_ref, xs_ref, wg_ref, wu_ref, wd_ref, ys_ref):
    del be_ref

    @pl.when(pl.program_id(0) < na_ref[0])
    def _():
        xb = xs_ref[...].astype(BF16)
        gate = jnp.dot(xb, wg_ref[...], preferred_element_type=F32)
        up = jnp.dot(xb, wu_ref[...], preferred_element_type=F32)
        hidden = gate * (1.0 / (1.0 + jnp.exp(-gate))) * up
        ys_ref[...] = jnp.dot(hidden.astype(BF16), wd_ref[...], preferred_element_type=F32)

    @pl.when(pl.program_id(0) >= na_ref[0])
    def _():
        ys_ref[...] = jnp.zeros(ys_ref.shape, F32)


def _experts(block_expert, n_active, xs, wg, wu, wd):
    n_slots, d = xs.shape
    tb = EXPERT_TB
    f = wg.shape[2]

    def last_active(n, na):
        return jnp.maximum(jnp.minimum(n, na[0] - 1), 0)

    def row_map(n, be, na):
        return (last_active(n, na), 0)

    def w_map(n, be, na):
        return (be[last_active(n, na)], 0, 0)

    return pl.pallas_call(
        _expert_kernel,
        grid_spec=pltpu.PrefetchScalarGridSpec(
            num_scalar_prefetch=2,
            grid=(n_slots // tb,),
            in_specs=[pl.BlockSpec((tb, d), row_map),
                      pl.BlockSpec((None, d, f), w_map),
                      pl.BlockSpec((None, d, f), w_map),
                      pl.BlockSpec((None, f, d), w_map)],
            out_specs=pl.BlockSpec((tb, d), lambda n, be, na: (n, 0)),
        ),
        out_shape=jax.ShapeDtypeStruct((n_slots, d), F32),
        compiler_params=_params(("arbitrary",)),
    )(block_expert, n_active, xs, wg, wu, wd)


def _combine_ln_kernel(dest_ref, x_ref, gate_ref, g_ref, b_ref, ys_hbm, xo_ref, xb_ref, buf_ref, sem,
                       *, n_tok):
    tm = x_ref.shape[0]
    base = pl.program_id(0) * tm

    def issue(r, carry):
        tok = base + r
        _row_copy(ys_hbm, dest_ref[tok], buf_ref.at[0], r, sem).start()
        _row_copy(ys_hbm, dest_ref[n_tok + tok], buf_ref.at[1], r, sem).start()
        return carry

    def drain(r, carry):
        _row_copy(ys_hbm, 0, buf_ref.at[0], 0, sem).wait()
        _row_copy(ys_hbm, 0, buf_ref.at[1], 0, sem).wait()
        return carry

    lax.fori_loop(0, tm, issue, 0)
    lax.fori_loop(0, tm, drain, 0)
    gate = gate_ref[...]
    moe = buf_ref[0] * gate[:, 0:1] + buf_ref[1] * gate[:, 1:2]
    out = _layer_norm(ALPHA * x_ref[...] + moe, g_ref[...], b_ref[...])
    xo_ref[...] = out
    xb_ref[...] = out.astype(BF16)


def _combine_ln(dest_flat, x, gate_t, g, b, ys):
    t, d = x.shape
    tm = min(COMBINE_TM, t)
    row_spec = pl.BlockSpec((tm, d), lambda i, dest: (i, 0))
    vec_spec = pl.BlockSpec((1, d), lambda i, dest: (0, 0))
    return pl.pallas_call(
        functools.partial(_combine_ln_kernel, n_tok=t),
        grid_spec=pltpu.PrefetchScalarGridSpec(
            num_scalar_prefetch=1,
            grid=(t // tm,),
            in_specs=[row_spec, pl.BlockSpec((tm, 2), lambda i, dest: (i, 0)), vec_spec, vec_spec,
                      pl.BlockSpec(memory_space=pl.ANY)],
            out_specs=[row_spec, row_spec],
            scratch_shapes=[pltpu.VMEM((2, tm, d), F32), pltpu.SemaphoreType.DMA(())],
        ),
        out_shape=[jax.ShapeDtypeStruct((t, d), F32), jax.ShapeDtypeStruct((t, d), BF16)],
        compiler_params=_params(("arbitrary",)),
    )(dest_flat, x, gate_t, g, b, ys)


def _rope_tables(positions):
    pos = positions.reshape(-1).astype(F32)[:, None]

    def cos_sin(d):
        inv = jnp.power(ROPE_THETA, -jnp.arange(0, d, 2, dtype=F32) / d)
        ang = pos * inv
        return jnp.cos(ang), jnp.sin(ang)

    c, s = cos_sin(HEAD_DIM)
    t128 = (jnp.concatenate([c, c], axis=1), jnp.concatenate([-s, s], axis=1))
    c, s = cos_sin(IDX_DIM)
    z = jnp.zeros_like(s)
    t64 = (jnp.concatenate([c, c, c, c], axis=1),
           jnp.concatenate([-s, z, -s, z], axis=1),
           jnp.concatenate([z, s, z, s], axis=1))
    return t128, t64


def _col_scales():
    q128 = HEAD_DIM ** -0.5 * LOG2E
    s128 = np.ones((1, 5 * MIX_W + HEAD_DIM), np.float32)
    s128[:, 0:2 * MIX_W] = q128
    s128[:, 3 * MIX_W:4 * MIX_W] = q128
    s64 = np.ones((1, 2304), np.float32)
    s64[:, 1024:1024 + MIX_W] = DIFF_DIM ** -0.5 * LOG2E
    return jnp.asarray(s128), jnp.asarray(s64)


def _split_w_in(w):
    def cols(name):
        lo, hi = _OFF[name]
        return w[:, lo:hi]

    d = w.shape[0]
    w128 = jnp.concatenate([cols(n) for n in ("a_q", "c_q", "c_k", "d_q", "d_k", "a_k")], axis=1)
    w64 = jnp.concatenate([cols("i_q"), cols("b_q"), cols("b_k"), cols("i_k"), cols("i_k"),
                           jnp.zeros((d, LANES), w.dtype)], axis=1)
    wt = jnp.concatenate([cols("b_v"), cols("c_v"), cols("d_v"), cols("a_v"), cols("i_w")], axis=1).T
    return w128.astype(BF16), w64.astype(BF16), wt.astype(BF16)


def _moe_layout(e, rank, counts, tb):
    t = e.shape[1]
    n_slots = 2 * t + N_EXPERTS * tb
    padded = (counts + tb - 1) // tb * tb
    pend = jnp.cumsum(padded)
    pstart = pend - padded
    dest = (pstart[e] + rank).astype(I32).reshape(-1)
    blocks = jnp.arange(n_slots // tb, dtype=I32) * tb
    block_expert = jnp.minimum(jnp.sum(blocks[:, None] >= pend[None, :], axis=1), N_EXPERTS - 1).astype(I32)
    n_active = (pend[-1:] // tb).astype(I32)
    return dest, block_expert, n_active, n_slots


def kernel(x, positions, w_in, w_out, diff_lambda, diff_norm_g, ln_mix_g, ln_mix_b, router_w, router_bias,
           w_gate, w_up, w_down, ln_ffn_g, ln_ffn_b):
    b, s, d = x.shape
    t = b * s
    n_keep = min(DSA_TOPK_MAX, s // 4)
    t128, t64 = _rope_tables(positions)
    s128, s64 = _col_scales()
    rw_t = router_w.T
    bias = router_bias.reshape(N_EXPERTS, 1).astype(F32)
    xf = x.reshape(t, d)
    xb = xf.astype(BF16)
    nc = s // min(ATT_T, s)
    for layer in range(DEPTH):
        w128, w64, wt = _split_w_in(w_in[layer])
        p128 = _proj(xb, w128, s128, t128, "rope128", 896).reshape(b, s, -1)
        p64 = _proj(xb, w64, s64, t64, "rope64", 768).reshape(b, s, -1)
        vt, iwt = _proj_t(xb, wt)
        vt = vt.reshape(b, nc, V_ROWS, -1)
        iwt = iwt.reshape(b, nc, IDX_HEADS, -1)

        lam_init = 0.8 - 0.6 * math.exp(-0.3 * layer)
        o_a = _dsa(p128, p64, vt, iwt, n_keep)
        o_b = _diff(p64, vt, diff_lambda[layer], diff_norm_g[layer].reshape(HEAD_DIM, 1), lam_init)
        o_c = _moba(p128, vt, _kmean(p128))
        o_d = _dilated(p128, vt)
        xf, xb = _outproj_ln(o_a.reshape(t, MIX_W), o_b.reshape(t, MIX_W), o_c.reshape(t, MIX_W),
                             o_d.reshape(t, MIX_W), w_out[layer].astype(BF16), xf,
                             ln_mix_g[layer].reshape(1, d), ln_mix_b[layer].reshape(1, d))

        e, gate, rank, counts = _router(xf, rw_t, bias)
        dest, block_expert, n_active, n_slots = _moe_layout(e, rank, counts[:, 0], EXPERT_TB)
        xs = _dispatch(dest, xf, n_slots)
        ys = _experts(block_expert, n_active, xs, w_gate[layer].astype(BF16), w_up[layer].astype(BF16),
                      w_down[layer].astype(BF16))
        xf, xb = _combine_ln(dest, xf, gate.T, ln_ffn_g[layer].reshape(1, d), ln_ffn_b[layer].reshape(1, d), ys)
    return xf.reshape(b, s, d)
```

```python
import functools
import math

import numpy as np
import jax
import jax.numpy as jnp
from jax import lax
from jax.experimental import pallas as pl
from jax.experimental.pallas import tpu as pltpu

F32 = jnp.float32
BF16 = jnp.bfloat16
I32 = jnp.int32

D_MODEL = 2048
HEAD_DIM = 128
HEADS = 4
MIX_W = HEADS * HEAD_DIM
IDX_HEADS = 16
IDX_DIM = 64
DSA_TOPK_MAX = 256
DIFF_DIM = 64
MOBA_BLOCK = 256
MOBA_TOPK = 3
DILATED_PATTERNS = ((128, 1), (512, 4), (2048, 16))
ROPE_THETA = 10000.0
N_EXPERTS = 16
N_GROUPS = 4
EXPERTS_PER_GROUP = 4
LN_EPS = 1e-5
DEPTH = 2
ALPHA = (2 * DEPTH) ** 0.25
LOG2E = math.log2(math.e)

LANES = 128
SUBLANES = 8
NEG = -1e30
M_FLOOR = -1e29
INT_MIN = -(2 ** 31)
NEG_INF_ORDER = 0x007FFFFF
VMEM_LIMIT = 56 * 1024 * 1024

PROJ_TM = 512
ATT_T = 256
OUT_TM = 256
ROUTER_TM = 256
DISPATCH_TM = 256
EXPERT_TB = 256
COMBINE_TM = 256
DMA_UNROLL = 8
V_ROWS = 3 * MIX_W + HEAD_DIM

_OFF = {}
_o = 0
for _name, _n in (("a_q", 512), ("a_k", 128), ("a_v", 128), ("i_q", 1024), ("i_k", 64), ("i_w", 16),
                  ("b_q", 512), ("b_k", 512), ("b_v", 512), ("c_q", 512), ("c_k", 512), ("c_v", 512),
                  ("d_q", 512), ("d_k", 512), ("d_v", 512)):
    _OFF[_name] = (_o, _o + _n)
    _o += _n


def _params(sem=None):
    return pltpu.CompilerParams(dimension_semantics=sem, vmem_limit_bytes=VMEM_LIMIT)


def _dot_nt(a, b):
    return lax.dot_general(a, b, (((1,), (1,)), ((), ())), preferred_element_type=F32)


def _proj_kernel(x_ref, w_ref, scale_ref, *rest, mode):
    o_ref = rest[-1]
    acc = jnp.dot(x_ref[...], w_ref[...], preferred_element_type=F32)
    n_groups = acc.shape[1] // LANES
    for g in range(n_groups):
        y = acc[:, g * LANES:(g + 1) * LANES]
        if mode == "rope128":
            cos_ref, sin_ref = rest[0], rest[1]
            r = y * cos_ref[...] + pltpu.roll(y, 64, 1) * sin_ref[...]
        else:
            cos_ref, sina_ref, sinb_ref = rest[0], rest[1], rest[2]
            r = (y * cos_ref[...] + pltpu.roll(y, 96, 1) * sina_ref[...]
                 + pltpu.roll(y, 32, 1) * sinb_ref[...])
        r = r * scale_ref[:, g * LANES:(g + 1) * LANES]
        o_ref[:, g * LANES:(g + 1) * LANES] = r.astype(o_ref.dtype)


def _proj(x_bf, w_bf, col_scale, tables, mode, tn):
    t, d = x_bf.shape
    n = w_bf.shape[1]
    tm = min(PROJ_TM, t)
    in_specs = [pl.BlockSpec((tm, d), lambda j, i: (i, 0)),
                pl.BlockSpec((d, tn), lambda j, i: (0, j)),
                pl.BlockSpec((1, tn), lambda j, i: (0, j))]
    in_specs += [pl.BlockSpec((tm, LANES), lambda j, i: (i, 0)) for _ in tables]
    return pl.pallas_call(
        functools.partial(_proj_kernel, mode=mode),
        grid=(n // tn, t // tm),
        in_specs=in_specs,
        out_specs=pl.BlockSpec((tm, tn), lambda j, i: (i, j)),
        out_shape=jax.ShapeDtypeStruct((t, n), BF16),
        compiler_params=_params(("arbitrary", "arbitrary")),
    )(x_bf, w_bf, col_scale, *tables)


def _proj_t_kernel(x_ref, wt_ref, vt_ref, iwt_ref):
    acc = _dot_nt(wt_ref[...], x_ref[...])
    vt_ref[...] = acc[:V_ROWS].astype(vt_ref.dtype)
    iwt_ref[...] = acc[V_ROWS:]


def _proj_t(x_bf, wt_bf):
    t, d = x_bf.shape
    tm = min(ATT_T, t)
    rows = wt_bf.shape[0]
    return pl.pallas_call(
        _proj_t_kernel,
        grid=(t // tm,),
        in_specs=[pl.BlockSpec((tm, d), lambda i: (i, 0)),
                  pl.BlockSpec((rows, d), lambda i: (0, 0))],
        out_specs=[pl.BlockSpec((None, V_ROWS, tm), lambda i: (i, 0, 0)),
                   pl.BlockSpec((None, IDX_HEADS, tm), lambda i: (i, 0, 0))],
        out_shape=[jax.ShapeDtypeStruct((t // tm, V_ROWS, tm), BF16),
                   jax.ShapeDtypeStruct((t // tm, IDX_HEADS, tm), F32)],
        compiler_params=_params(("arbitrary",)),
    )(x_bf, wt_bf)


def _online_updates(scores, mask, values, m_ref, l_ref, acc_ref, weight=None):
    probs, alphas = [], []
    masks = mask if isinstance(mask, list) else [mask] * len(scores)
    for idx, (s_t, mask) in enumerate(zip(scores, masks)):
        if mask is not None:
            s_t = jnp.where(mask, s_t, NEG)
        m_prev = m_ref[idx]
        m_new = jnp.maximum(m_prev, jnp.max(s_t, axis=0, keepdims=True))
        p = jnp.exp2(s_t - m_new)
        if weight is not None:
            p = p * weight
        alpha = jnp.exp2(m_prev - m_new)
        l_ref[idx] = alpha * l_ref[idx] + jnp.sum(p, axis=0, keepdims=True)
        m_ref[idx] = m_new
        probs.append(p.astype(BF16))
        alphas.append(alpha)
    for idx, (p, alpha) in enumerate(zip(probs, alphas)):
        acc_ref[idx] = alpha * acc_ref[idx] + jnp.dot(values[idx], p, preferred_element_type=F32)


def _init_softmax(m_ref, l_ref, acc_ref):
    m_ref[...] = jnp.full(m_ref.shape, M_FLOOR, F32)
    l_ref[...] = jnp.zeros(l_ref.shape, F32)
    acc_ref[...] = jnp.zeros(acc_ref.shape, F32)


def _softmax_scratch(n, t):
    return [pltpu.VMEM((n, 1, t), F32), pltpu.VMEM((n, 1, t), F32), pltpu.VMEM((n, HEAD_DIM, t), F32)]


def _causal_t(t):
    return lax.broadcasted_iota(I32, (t, t), 0) <= lax.broadcasted_iota(I32, (t, t), 1)


def _head(ref_or_val, h):
    return ref_or_val[:, h * HEAD_DIM:(h + 1) * HEAD_DIM]


def _order_to_f32(u):
    key = u ^ jnp.int32(INT_MIN)
    bits = jnp.where(key < 0, key ^ jnp.int32(0x7FFFFFFF), key)
    return lax.bitcast_convert_type(bits, F32)


def _dsa_kernel(aq_ref, ak_ref, avt_ref, iq_ref, ik_ref, iwt_ref, o_ref,
                score_ref, m_ref, l_ref, acc_ref, jlim_ref, *, n_keep):
    t = aq_ref.shape[0]
    i = pl.program_id(1)
    n_chunks = i + 1
    lane = lax.broadcasted_iota(I32, (1, LANES), 1)
    k_off = lax.broadcasted_iota(I32, (t, 1), 0)
    q_pos = i * t + lax.broadcasted_iota(I32, (1, t), 1)
    iw_t = iwt_ref[...]

    def score_chunk(c, carry):
        ik_c = ik_ref[pl.ds(pl.multiple_of(c * t, t), t), :]
        ik_lo = jnp.where(lane < IDX_DIM, ik_c, jnp.zeros_like(ik_c))
        ik_hi = jnp.where(lane >= IDX_DIM, ik_c, jnp.zeros_like(ik_c))
        score = jnp.zeros((t, t), F32)
        for j in range(IDX_HEADS // 2):
            q_pair = iq_ref[:, j * LANES:(j + 1) * LANES]
            s_even = jnp.maximum(_dot_nt(ik_lo, q_pair), 0.0)
            s_odd = jnp.maximum(_dot_nt(ik_hi, q_pair), 0.0)
            score = score + s_even * iw_t[2 * j:2 * j + 1, :] + s_odd * iw_t[2 * j + 1:2 * j + 2, :]
        score_ref[c] = jnp.where(c * t + k_off <= q_pos, score, -jnp.inf)
        return carry

    lax.fori_loop(0, n_chunks, score_chunk, 0)

    def count(pred):
        def body(c, part):
            hit = jnp.where(pred(score_ref[c], c), 1, 0)
            return part + jnp.sum(hit.reshape(t // SUBLANES, SUBLANES, t), axis=0)
        part = lax.fori_loop(0, n_chunks, body, jnp.zeros((SUBLANES, t), I32))
        return jnp.sum(part, axis=0, keepdims=True)

    def bisect(it, thr_u):
        cand_u = thr_u | jnp.left_shift(jnp.int32(1), 31 - it)
        cand = _order_to_f32(cand_u)
        cnt = count(lambda sc, c: sc >= cand)
        return jnp.where(cnt >= n_keep, cand_u, thr_u)

    thr_u = lax.fori_loop(0, 32, bisect, jnp.zeros((1, t), I32))
    below = (thr_u >= 0) & (thr_u < NEG_INF_ORDER)
    thr = jnp.where(below, -jnp.inf, _order_to_f32(thr_u))

    n_gt = count(lambda sc, c: sc > thr)
    n_eq = count(lambda sc, c: sc == thr)
    need = n_keep - n_gt
    s_total = score_ref.shape[0] * t
    jlim_ref[...] = jnp.full((1, t), s_total, I32)
    excess = jnp.max(jnp.where((n_eq > need) & (thr > -jnp.inf), 1, 0))

    @pl.when(excess > 0)
    def _():
        n_bits = max(1, (s_total - 1).bit_length())

        def bisect_idx(it, j):
            cand = j | jnp.left_shift(jnp.int32(1), n_bits - 1 - it)
            cnt = count(lambda sc, c: (sc == thr) & (c * t + k_off < cand))
            return jnp.where(cnt < need, cand, j)

        jlim_ref[...] = lax.fori_loop(0, n_bits, bisect_idx, jnp.zeros((1, t), I32))

    _init_softmax(m_ref, l_ref, acc_ref)
    jlim = jnp.where(thr > -jnp.inf, jlim_ref[...], -1)
    thr_all = jnp.where(thr > -jnp.inf, thr, jnp.finfo(F32).min)

    def attend(select):
        def body(c, carry):
            start = pl.multiple_of(c * t, t)
            k_c = ak_ref[pl.ds(start, t), :]
            scores = [_dot_nt(k_c, _head(aq_ref, h)) for h in range(HEADS)]
            sel = select(score_ref[c], c * t + k_off)
            _online_updates(scores, sel, [avt_ref[c]] * HEADS, m_ref, l_ref, acc_ref)
            return carry
        lax.fori_loop(0, n_chunks, body, 0)

    @pl.when(excess > 0)
    def _():
        attend(lambda sc, k_pos: (sc > thr) | ((sc == thr) & (k_pos <= jlim)))

    @pl.when(excess <= 0)
    def _():
        attend(lambda sc, k_pos: sc >= thr_all)
    for h in range(HEADS):
        o_ref[:, h * HEAD_DIM:(h + 1) * HEAD_DIM] = (acc_ref[h] / l_ref[h]).T.astype(o_ref.dtype)


def _dsa(p128, p64, vt, iwt, n_keep):
    b, s, _ = p128.shape
    t = min(ATT_T, s)
    nc = s // t
    return pl.pallas_call(
        functools.partial(_dsa_kernel, n_keep=n_keep),
        grid=(b, nc),
        in_specs=[
            pl.BlockSpec((None, t, MIX_W), lambda bb, i: (bb, i, 0)),
            pl.BlockSpec((None, s, HEAD_DIM), lambda bb, i: (bb, 0, 20)),
            pl.BlockSpec((None, nc, HEAD_DIM, t), lambda bb, i: (bb, 0, 12, 0)),
            pl.BlockSpec((None, t, 1024), lambda bb, i: (bb, i, 0)),
            pl.BlockSpec((None, s, LANES), lambda bb, i: (bb, 0, 16)),
            pl.BlockSpec((None, None, IDX_HEADS, t), lambda bb, i: (bb, i, 0, 0)),
        ],
        out_specs=pl.BlockSpec((None, t, MIX_W), lambda bb, i: (bb, i, 0)),
        out_shape=jax.ShapeDtypeStruct((b, s, MIX_W), BF16),
        scratch_shapes=[pltpu.VMEM((nc, t, t), F32)] + _softmax_scratch(HEADS, t)
        + [pltpu.VMEM((1, t), I32)],
        compiler_params=_params(("arbitrary", "arbitrary")),
    )(p128, p128, vt, p64, p64, iwt)


def _diff_kernel(q_ref, k_ref, vt_ref, lam_ref, g_ref, o_ref, m_ref, l_ref, acc_ref, *, lam_init):
    t = q_ref.shape[0]
    i = pl.program_id(1)
    lane = lax.broadcasted_iota(I32, (1, LANES), 1)
    _init_softmax(m_ref, l_ref, acc_ref)

    def step(c, mask):
        k_c = k_ref[pl.ds(pl.multiple_of(c * t, t), t), :]
        v_c = vt_ref[c]
        scores, values = [], []
        for h in range(HEADS):
            q_h = _head(q_ref, h)
            k_h = _head(k_c, h)
            q_1 = jnp.where(lane < DIFF_DIM, q_h, jnp.zeros_like(q_h))
            q_2 = jnp.where(lane >= DIFF_DIM, q_h, jnp.zeros_like(q_h))
            scores += [_dot_nt(k_h, q_1), _dot_nt(k_h, q_2)]
            values += [v_c[h * HEAD_DIM:(h + 1) * HEAD_DIM, :]] * 2
        _online_updates(scores, mask, values, m_ref, l_ref, acc_ref)

    def full_step(c, carry):
        step(c, None)
        return carry

    lax.fori_loop(0, i, full_step, 0)
    step(i, _causal_t(t))

    lam_p = lam_ref[...]
    lam = (jnp.exp(jnp.sum(lam_p[0:1] * lam_p[1:2], axis=1, keepdims=True))
           - jnp.exp(jnp.sum(lam_p[2:3] * lam_p[3:4], axis=1, keepdims=True)) + lam_init)
    for h in range(HEADS):
        o = acc_ref[2 * h] / l_ref[2 * h] - lam * (acc_ref[2 * h + 1] / l_ref[2 * h + 1])
        o = o * lax.rsqrt(jnp.mean(o * o, axis=0, keepdims=True) + LN_EPS) * g_ref[...]
        o_ref[:, h * HEAD_DIM:(h + 1) * HEAD_DIM] = (o * (1.0 - lam_init)).T.astype(o_ref.dtype)


def _diff(p64, vt, lam_p, g_col, lam_init):
    b, s, _ = p64.shape
    t = min(ATT_T, s)
    nc = s // t
    return pl.pallas_call(
        functools.partial(_diff_kernel, lam_init=lam_init),
        grid=(b, nc),
        in_specs=[
            pl.BlockSpec((None, t, MIX_W), lambda bb, i: (bb, i, 2)),
            pl.BlockSpec((None, s, MIX_W), lambda bb, i: (bb, 0, 3)),
            pl.BlockSpec((None, nc, MIX_W, t), lambda bb, i: (bb, 0, 0, 0)),
            pl.BlockSpec((4, DIFF_DIM), lambda bb, i: (0, 0)),
            pl.BlockSpec((HEAD_DIM, 1), lambda bb, i: (0, 0)),
        ],
        out_specs=pl.BlockSpec((None, t, MIX_W), lambda bb, i: (bb, i, 0)),
        out_shape=jax.ShapeDtypeStruct((b, s, MIX_W), BF16),
        scratch_shapes=_softmax_scratch(2 * HEADS, t),
        compiler_params=_params(("arbitrary", "arbitrary")),
    )(p64, p64, vt, lam_p, g_col)


def _kmean_kernel(k_ref, o_ref):
    n_blk = k_ref.shape[0] // MOBA_BLOCK
    o_ref[...] = jnp.zeros(o_ref.shape, o_ref.dtype)
    for n in range(n_blk):
        blk = k_ref[n * MOBA_BLOCK:(n + 1) * MOBA_BLOCK, :].astype(F32)
        o_ref[n:n + 1, :] = jnp.mean(blk, axis=0, keepdims=True)


def _kmean(p128):
    b, s, _ = p128.shape
    rows = max(SUBLANES, s // MOBA_BLOCK)
    return pl.pallas_call(
        _kmean_kernel,
        grid=(b,),
        in_specs=[pl.BlockSpec((None, s, MIX_W), lambda bb: (bb, 0, 2))],
        out_specs=pl.BlockSpec((None, rows, MIX_W), lambda bb: (bb, 0, 0)),
        out_shape=jax.ShapeDtypeStruct((b, rows, MIX_W), F32),
        compiler_params=_params(("arbitrary",)),
    )(p128)


def _moba_kernel(q_ref, k_ref, vt_ref, km_ref, o_ref, sel_ref, m_ref, l_ref, acc_ref):
    t = q_ref.shape[0]
    n_blk = km_ref.shape[0]
    i = pl.program_id(1)
    blk = lax.broadcasted_iota(I32, (n_blk, 1), 0)
    _init_softmax(m_ref, l_ref, acc_ref)

    for h in range(HEADS):
        gate = _dot_nt(_head(km_ref, h).astype(BF16), _head(q_ref, h))
        gate = jnp.where(blk < i, gate, -jnp.inf)
        sel = jnp.zeros(gate.shape, F32)
        for _ in range(MOBA_TOPK):
            best = jnp.max(gate, axis=0, keepdims=True)
            first = jnp.min(jnp.where(gate == best, blk, n_blk), axis=0, keepdims=True)
            pick = (blk == first) & (best > -jnp.inf)
            sel = jnp.where(pick, 1.0, sel)
            gate = jnp.where(pick, -jnp.inf, gate)
        sel_ref[h] = sel

    def step(c, diagonal):
        k_c = k_ref[pl.ds(pl.multiple_of(c * t, t), t), :]
        v_c = vt_ref[c]
        scores = [_dot_nt(_head(k_c, h), _head(q_ref, h)) for h in range(HEADS)]
        values = [v_c[h * HEAD_DIM:(h + 1) * HEAD_DIM, :] for h in range(HEADS)]
        if diagonal:
            masks = _causal_t(t)
        else:
            masks = [sel_ref[h, pl.ds(c, 1), :] > 0.0 for h in range(HEADS)]
        _online_updates(scores, masks, values, m_ref, l_ref, acc_ref)

    def past_step(c, carry):
        step(c, False)
        return carry

    lax.fori_loop(0, i, past_step, 0)
    step(i, True)
    for h in range(HEADS):
        o_ref[:, h * HEAD_DIM:(h + 1) * HEAD_DIM] = (acc_ref[h] / l_ref[h]).T.astype(o_ref.dtype)


def _moba(p128, vt, kmean):
    b, s, _ = p128.shape
    t = MOBA_BLOCK
    nc = s // t
    rows = kmean.shape[1]
    return pl.pallas_call(
        _moba_kernel,
        grid=(b, nc),
        in_specs=[
            pl.BlockSpec((None, t, MIX_W), lambda bb, i: (bb, i, 1)),
            pl.BlockSpec((None, s, MIX_W), lambda bb, i: (bb, 0, 2)),
            pl.BlockSpec((None, nc, MIX_W, t), lambda bb, i: (bb, 0, 1, 0)),
            pl.BlockSpec((None, rows, MIX_W), lambda bb, i: (bb, 0, 0)),
        ],
        out_specs=pl.BlockSpec((None, t, MIX_W), lambda bb, i: (bb, i, 0)),
        out_shape=jax.ShapeDtypeStruct((b, s, MIX_W), BF16),
        scratch_shapes=[pltpu.VMEM((HEADS, rows, t), F32)] + _softmax_scratch(HEADS, t),
        compiler_params=_params(("arbitrary", "arbitrary")),
    )(p128, p128, vt, kmean)


def _dilated_kernel(q_ref, k_ref, vt_ref, o_ref, m_ref, l_ref, acc_ref):
    t = q_ref.shape[0]
    i = pl.program_id(1)
    _init_softmax(m_ref, l_ref, acc_ref)
    max_window = max(w for w, _ in DILATED_PATTERNS)
    first = jnp.maximum(i - max_window // t, 0)
    rel = lax.broadcasted_iota(I32, (t, t), 1) - lax.broadcasted_iota(I32, (t, t), 0)

    def step(c, carry):
        k_c = k_ref[pl.ds(pl.multiple_of(c * t, t), t), :]
        v_c = vt_ref[c]
        dist = rel + (i - c) * t
        mult = jnp.zeros((t, t), F32)
        for window, dilation in DILATED_PATTERNS:
            hit = (dist >= 0) & (dist <= window) & ((dist & (dilation - 1)) == 0)
            mult = mult + jnp.where(hit, 1.0, 0.0)
        mask = mult > 0.0
        scores = [_dot_nt(_head(k_c, h), _head(q_ref, h)) for h in range(HEADS)]
        values = [v_c[h * HEAD_DIM:(h + 1) * HEAD_DIM, :] for h in range(HEADS)]
        _online_updates(scores, mask, values, m_ref, l_ref, acc_ref, weight=mult)
        return carry

    lax.fori_loop(first, i + 1, step, 0)
    for h in range(HEADS):
        o_ref[:, h * HEAD_DIM:(h + 1) * HEAD_DIM] = (acc_ref[h] / l_ref[h]).T.astype(o_ref.dtype)


def _dilated(p128, vt):
    b, s, _ = p128.shape
    t = min(ATT_T, s)
    nc = s // t
    return pl.pallas_call(
        _dilated_kernel,
        grid=(b, nc),
        in_specs=[
            pl.BlockSpec((None, t, MIX_W), lambda bb, i: (bb, i, 3)),
            pl.BlockSpec((None, s, MIX_W), lambda bb, i: (bb, 0, 4)),
            pl.BlockSpec((None, nc, MIX_W, t), lambda bb, i: (bb, 0, 2, 0)),
        ],
        out_specs=pl.BlockSpec((None, t, MIX_W), lambda bb, i: (bb, i, 0)),
        out_shape=jax.ShapeDtypeStruct((b, s, MIX_W), BF16),
        scratch_shapes=_softmax_scratch(HEADS, t),
        compiler_params=_params(("arbitrary", "arbitrary")),
    )(p128, p128, vt)


def _layer_norm(z, g, b):
    mu = jnp.mean(z, axis=1, keepdims=True)
    zc = z - mu
    var = jnp.mean(zc * zc, axis=1, keepdims=True)
    return zc * lax.rsqrt(var + LN_EPS) * g + b


def _outproj_ln_kernel(oa_ref, ob_ref, oc_ref, od_ref, w_ref, x_ref, g_ref, b_ref, xo_ref, xb_ref):
    y = jnp.dot(oa_ref[...], w_ref[0:MIX_W, :], preferred_element_type=F32)
    y = y + jnp.dot(ob_ref[...], w_ref[MIX_W:2 * MIX_W, :], preferred_element_type=F32)
    y = y + jnp.dot(oc_ref[...], w_ref[2 * MIX_W:3 * MIX_W, :], preferred_element_type=F32)
    y = y + jnp.dot(od_ref[...], w_ref[3 * MIX_W:4 * MIX_W, :], preferred_element_type=F32)
    out = _layer_norm(ALPHA * x_ref[...] + y, g_ref[...], b_ref[...])
    xo_ref[...] = out
    xb_ref[...] = out.astype(BF16)


def _outproj_ln(oa, ob, oc, od, w_bf, x, g, b):
    t, d = x.shape
    tm = min(OUT_TM, t)
    mix_spec = pl.BlockSpec((tm, MIX_W), lambda i: (i, 0))
    row_spec = pl.BlockSpec((tm, d), lambda i: (i, 0))
    vec_spec = pl.BlockSpec((1, d), lambda i: (0, 0))
    return pl.pallas_call(
        _outproj_ln_kernel,
        grid=(t // tm,),
        in_specs=[mix_spec, mix_spec, mix_spec, mix_spec,
                  pl.BlockSpec((d, d), lambda i: (0, 0)), row_spec, vec_spec, vec_spec],
        out_specs=[row_spec, row_spec],
        out_shape=[jax.ShapeDtypeStruct((t, d), F32), jax.ShapeDtypeStruct((t, d), BF16)],
        compiler_params=_params(("arbitrary",)),
    )(oa, ob, oc, od, w_bf, x, g, b)


def _first_argmax(vals):
    best, arg = vals[0], jnp.zeros(vals[0].shape, I32)
    for j in range(1, len(vals)):
        better = vals[j] > best
        arg = jnp.where(better, j, arg)
        best = jnp.where(better, vals[j], best)
    return arg, best


def _pick(rows, idx):
    out = rows[0]
    for j in range(1, len(rows)):
        out = jnp.where(idx == j, rows[j], out)
    return out


def _router_kernel(x_ref, rwt_ref, bias_ref, e_ref, gate_ref, rank_ref, cnt_ref, carry_ref):
    tm = x_ref.shape[0]

    @pl.when(pl.program_id(0) == 0)
    def _():
        carry_ref[...] = jnp.zeros(carry_ref.shape, F32)

    logits = lax.dot_general(rwt_ref[...], x_ref[...], (((1,), (1,)), ((), ())),
                             precision=lax.Precision.HIGHEST, preferred_element_type=F32)
    aff = 1.0 / (1.0 + jnp.exp(-logits))
    biased = aff + bias_ref[...]
    sel_rows = [biased[r:r + 1, :] for r in range(N_EXPERTS)]
    aff_rows = [aff[r:r + 1, :] for r in range(N_EXPERTS)]

    group_scores = []
    for g in range(N_GROUPS):
        r = sel_rows[g * EXPERTS_PER_GROUP:(g + 1) * EXPERTS_PER_GROUP]
        best_pair = r[0] + r[1]
        for a in range(EXPERTS_PER_GROUP):
            for c in range(a + 1, EXPERTS_PER_GROUP):
                if (a, c) != (0, 1):
                    best_pair = jnp.maximum(best_pair, r[a] + r[c])
        group_scores.append(best_pair)
    grp, _ = _first_argmax(group_scores)

    in_sel = [_pick([sel_rows[g * EXPERTS_PER_GROUP + j] for g in range(N_GROUPS)], grp)
              for j in range(EXPERTS_PER_GROUP)]
    in_aff = [_pick([aff_rows[g * EXPERTS_PER_GROUP + j] for g in range(N_GROUPS)], grp)
              for j in range(EXPERTS_PER_GROUP)]
    first, _ = _first_argmax(in_sel)
    second, _ = _first_argmax([jnp.where(first == j, -jnp.inf, in_sel[j]) for j in range(EXPERTS_PER_GROUP)])
    a0 = _pick(in_aff, first)
    a1 = _pick(in_aff, second)
    e0 = grp * EXPERTS_PER_GROUP + first
    e1 = grp * EXPERTS_PER_GROUP + second
    e_ref[0:1, :] = e0
    e_ref[1:2, :] = e1
    gate_ref[0:1, :] = a0 / (a0 + a1)
    gate_ref[1:2, :] = a1 / (a0 + a1)

    e_iota = lax.broadcasted_iota(I32, (N_EXPERTS, tm), 0)
    one_hot = ((e_iota == e0) | (e_iota == e1)).astype(BF16)
    before = (lax.broadcasted_iota(I32, (tm, tm), 0) < lax.broadcasted_iota(I32, (tm, tm), 1)).astype(BF16)
    prior = jnp.dot(one_hot, before, preferred_element_type=F32) + carry_ref[...]
    rank_ref[0:1, :] = jnp.sum(jnp.where(e_iota == e0, prior, 0.0), axis=0, keepdims=True).astype(I32)
    rank_ref[1:2, :] = jnp.sum(jnp.where(e_iota == e1, prior, 0.0), axis=0, keepdims=True).astype(I32)
    carry_ref[...] = carry_ref[...] + jnp.sum(one_hot.astype(F32), axis=1, keepdims=True)
    cnt_ref[...] = jnp.broadcast_to(carry_ref[...], cnt_ref.shape).astype(I32)


def _router(x, rw_t, bias):
    t, d = x.shape
    tm = min(ROUTER_TM, t)
    tok_spec = pl.BlockSpec((2, tm), lambda i: (0, i))
    return pl.pallas_call(
        _router_kernel,
        grid=(t // tm,),
        in_specs=[pl.BlockSpec((tm, d), lambda i: (i, 0)),
                  pl.BlockSpec((N_EXPERTS, d), lambda i: (0, 0)),
                  pl.BlockSpec((N_EXPERTS, 1), lambda i: (0, 0))],
        out_specs=[tok_spec, tok_spec, tok_spec, pl.BlockSpec((N_EXPERTS, LANES), lambda i: (0, 0))],
        out_shape=[jax.ShapeDtypeStruct((2, t), I32), jax.ShapeDtypeStruct((2, t), F32),
                   jax.ShapeDtypeStruct((2, t), I32), jax.ShapeDtypeStruct((N_EXPERTS, LANES), I32)],
        scratch_shapes=[pltpu.VMEM((N_EXPERTS, 1), F32)],
        compiler_params=_params(("arbitrary",)),
    )(x, rw_t, bias)


def _row_copy(src, src_row, dst, dst_row, sem):
    return pltpu.make_async_copy(src.at[pl.ds(src_row, 1)], dst.at[pl.ds(dst_row, 1)], sem)


def _dispatch_kernel(dest_ref, x_ref, slots_in_hbm, slots_hbm, sem, *, tm, n_tok):
    del slots_in_hbm
    base = pl.program_id(0) * tm

    def issue(r, carry):
        tok = base + r
        _row_copy(x_ref, r, slots_hbm, dest_ref[tok], sem).start()
        _row_copy(x_ref, r, slots_hbm, dest_ref[n_tok + tok], sem).start()
        return carry

    def drain(r, carry):
        _row_copy(x_ref, 0, slots_hbm, 0, sem).wait()
        _row_copy(x_ref, 0, slots_hbm, 0, sem).wait()
        return carry

    lax.fori_loop(0, tm, issue, 0, unroll=DMA_UNROLL)
    lax.fori_loop(0, tm, drain, 0, unroll=DMA_UNROLL)


def _dispatch(dest_flat, x, n_slots):
    t, d = x.shape
    tm = min(DISPATCH_TM, t)
    return pl.pallas_call(
        functools.partial(_dispatch_kernel, tm=tm, n_tok=t),
        grid_spec=pltpu.PrefetchScalarGridSpec(
            num_scalar_prefetch=1,
            grid=(t // tm,),
            in_specs=[pl.BlockSpec((tm, d), lambda i, dest: (i, 0)), pl.BlockSpec(memory_space=pl.ANY)],
            out_specs=pl.BlockSpec(memory_space=pl.ANY),
            scratch_shapes=[pltpu.SemaphoreType.DMA(())],
        ),
        out_shape=jax.ShapeDtypeStruct((n_slots, d), x.dtype),
        input_output_aliases={2: 0},
        compiler_params=_params(("arbitrary",)),
    )(dest_flat, x, jnp.zeros((n_slots, d), x.dtype))


def _expert_kernel(be_ref, na_ref, xs_ref, wg_ref, wu_ref, wd_ref, ys_ref):
    del be_ref

    @pl.when(pl.program_id(0) < na_ref[0])
    def _():
        xb = xs_ref[...].astype(BF16)
        gate = jnp.dot(xb, wg_ref[...], preferred_element_type=F32)
        up = jnp.dot(xb, wu_ref[...], preferred_element_type=F32)
        hidden = gate * (1.0 / (1.0 + jnp.exp(-gate))) * up
        ys_ref[...] = jnp.dot(hidden.astype(BF16), wd_ref[...], preferred_element_type=F32)

    @pl.when(pl.program_id(0) >= na_ref[0])
    def _():
        ys_ref[...] = jnp.zeros(ys_ref.shape, F32)


def _experts(block_expert, n_active, xs, wg, wu, wd):
    n_slots, d = xs.shape
    tb = EXPERT_TB
    f = wg.shape[2]

    def last_active(n, na):
        return jnp.maximum(jnp.minimum(n, na[0] - 1), 0)

    def row_map(n, be, na):
        return (last_active(n, na), 0)

    def w_map(n, be, na):
        return (be[last_active(n, na)], 0, 0)

    return pl.pallas_call(
        _expert_kernel,
        grid_spec=pltpu.PrefetchScalarGridSpec(
            num_scalar_prefetch=2,
            grid=(n_slots // tb,),
            in_specs=[pl.BlockSpec((tb, d), row_map),
                      pl.BlockSpec((None, d, f), w_map),
                      pl.BlockSpec((None, d, f), w_map),
                      pl.BlockSpec((None, f, d), w_map)],
            out_specs=pl.BlockSpec((tb, d), lambda n, be, na: (n, 0)),
        ),
        out_shape=jax.ShapeDtypeStruct((n_slots, d), F32),
        compiler_params=_params(("arbitrary",)),
    )(block_expert, n_active, xs, wg, wu, wd)


def _combine_ln_kernel(dest_ref, x_ref, gate_ref, g_ref, b_ref, ys_hbm, xo_ref, xb_ref, buf_ref, sem,
                       *, n_tok):
    tm = x_ref.shape[0]
    base = pl.program_id(0) * tm

    def issue(r, carry):
        tok = base + r
        _row_copy(ys_hbm, dest_ref[tok], buf_ref.at[0], r, sem).start()
        _row_copy(ys_hbm, dest_ref[n_tok + tok], buf_ref.at[1], r, sem).start()
        return carry

    def drain(r, carry):
        _row_copy(ys_hbm, 0, buf_ref.at[0], 0, sem).wait()
        _row_copy(ys_hbm, 0, buf_ref.at[1], 0, sem).wait()
        return carry

    lax.fori_loop(0, tm, issue, 0, unroll=DMA_UNROLL)
    lax.fori_loop(0, tm, drain, 0, unroll=DMA_UNROLL)
    gate = gate_ref[...]
    moe = buf_ref[0] * gate[:, 0:1] + buf_ref[1] * gate[:, 1:2]
    out = _layer_norm(ALPHA * x_ref[...] + moe, g_ref[...], b_ref[...])
    xo_ref[...] = out
    xb_ref[...] = out.astype(BF16)


def _combine_ln(dest_flat, x, gate_t, g, b, ys):
    t, d = x.shape
    tm = min(COMBINE_TM, t)
    row_spec = pl.BlockSpec((tm, d), lambda i, dest: (i, 0))
    vec_spec = pl.BlockSpec((1, d), lambda i, dest: (0, 0))
    return pl.pallas_call(
        functools.partial(_combine_ln_kernel, n_tok=t),
        grid_spec=pltpu.PrefetchScalarGridSpec(
            num_scalar_prefetch=1,
            grid=(t // tm,),
            in_specs=[row_spec, pl.BlockSpec((tm, 2), lambda i, dest: (i, 0)), vec_spec, vec_spec,
                      pl.BlockSpec(memory_space=pl.ANY)],
            out_specs=[row_spec, row_spec],
            scratch_shapes=[pltpu.VMEM((2, tm, d), F32), pltpu.SemaphoreType.DMA(())],
        ),
        out_shape=[jax.ShapeDtypeStruct((t, d), F32), jax.ShapeDtypeStruct((t, d), BF16)],
        compiler_params=_params(("arbitrary",)),
    )(dest_flat, x, gate_t, g, b, ys)


def _rope_tables(positions):
    pos = positions.reshape(-1).astype(F32)[:, None]

    def cos_sin(d):
        inv = jnp.power(ROPE_THETA, -jnp.arange(0, d, 2, dtype=F32) / d)
        ang = pos * inv
        return jnp.cos(ang), jnp.sin(ang)

    c, s = cos_sin(HEAD_DIM)
    t128 = (jnp.concatenate([c, c], axis=1), jnp.concatenate([-s, s], axis=1))
    c, s = cos_sin(IDX_DIM)
    z = jnp.zeros_like(s)
    t64 = (jnp.concatenate([c, c, c, c], axis=1),
           jnp.concatenate([-s, z, -s, z], axis=1),
           jnp.concatenate([z, s, z, s], axis=1))
    return t128, t64


def _col_scales():
    q128 = HEAD_DIM ** -0.5 * LOG2E
    s128 = np.ones((1, 5 * MIX_W + HEAD_DIM), np.float32)
    s128[:, 0:2 * MIX_W] = q128
    s128[:, 3 * MIX_W:4 * MIX_W] = q128
    s64 = np.ones((1, 2304), np.float32)
    s64[:, 1024:1024 + MIX_W] = DIFF_DIM ** -0.5 * LOG2E
    return jnp.asarray(s128), jnp.asarray(s64)


def _split_w_in(w):
    def cols(name):
        lo, hi = _OFF[name]
        return w[:, lo:hi]

    d = w.shape[0]
    w128 = jnp.concatenate([cols(n) for n in ("a_q", "c_q", "c_k", "d_q", "d_k", "a_k")], axis=1)
    w64 = jnp.concatenate([cols("i_q"), cols("b_q"), cols("b_k"), cols("i_k"), cols("i_k"),
                           jnp.zeros((d, LANES), w.dtype)], axis=1)
    wt = jnp.concatenate([cols("b_v"), cols("c_v"), cols("d_v"), cols("a_v"), cols("i_w")], axis=1).T
    return w128.astype(BF16), w64.astype(BF16), wt.astype(BF16)


def _moe_layout(e, rank, counts, tb):
    t = e.shape[1]
    n_slots = 2 * t + N_EXPERTS * tb
    padded = (counts + tb - 1) // tb * tb
    pend = jnp.cumsum(padded)
    pstart = pend - padded
    experts = jnp.arange(N_EXPERTS, dtype=I32)[:, None, None]
    seg_start = jnp.sum(jnp.where(e[None] == experts, pstart[:, None, None], 0), axis=0)
    dest = (seg_start + rank).astype(I32).reshape(-1)
    blocks = jnp.arange(n_slots // tb, dtype=I32) * tb
    block_expert = jnp.minimum(jnp.sum(blocks[:, None] >= pend[None, :], axis=1), N_EXPERTS - 1).astype(I32)
    n_active = (pend[-1:] // tb).astype(I32)
    return dest, block_expert, n_active, n_slots


def kernel(x, positions, w_in, w_out, diff_lambda, diff_norm_g, ln_mix_g, ln_mix_b, router_w, router_bias,
           w_gate, w_up, w_down, ln_ffn_g, ln_ffn_b):
    b, s, d = x.shape
    t = b * s
    n_keep = min(DSA_TOPK_MAX, s // 4)
    t128, t64 = _rope_tables(positions)
    s128, s64 = _col_scales()
    rw_t = router_w.T
    bias = router_bias.reshape(N_EXPERTS, 1).astype(F32)
    xf = x.reshape(t, d)
    xb = xf.astype(BF16)
    nc = s // min(ATT_T, s)
    for layer in range(DEPTH):
        w128, w64, wt = _split_w_in(w_in[layer])
        p128 = _proj(xb, w128, s128, t128, "rope128", 896).reshape(b, s, -1)
        p64 = _proj(xb, w64, s64, t64, "rope64", 768).reshape(b, s, -1)
        vt, iwt = _proj_t(xb, wt)
        vt = vt.reshape(b, nc, V_ROWS, -1)
        iwt = iwt.reshape(b, nc, IDX_HEADS, -1)

        lam_init = 0.8 - 0.6 * math.exp(-0.3 * layer)
        o_a = _dsa(p128, p64, vt, iwt, n_keep)
        o_b = _diff(p64, vt, diff_lambda[layer], diff_norm_g[layer].reshape(HEAD_DIM, 1), lam_init)
        o_c = _moba(p128, vt, _kmean(p128))
        o_d = _dilated(p128, vt)
        xf, xb = _outproj_ln(o_a.reshape(t, MIX_W), o_b.reshape(t, MIX_W), o_c.reshape(t, MIX_W),
                             o_d.reshape(t, MIX_W), w_out[layer].astype(BF16), xf,
                             ln_mix_g[layer].reshape(1, d), ln_mix_b[layer].reshape(1, d))

        e, gate, rank, counts = _router(xf, rw_t, bias)
        dest, block_expert, n_active, n_slots = _moe_layout(e, rank, counts[:, 0], EXPERT_TB)
        xs = _dispatch(dest, xf, n_slots)
        ys = _experts(block_expert, n_active, xs, w_gate[layer].astype(BF16), w_up[layer].astype(BF16),
                      w_down[layer].astype(BF16))
        xf, xb = _combine_ln(dest, xf, gate.T, ln_ffn_g[layer].reshape(1, d), ln_ffn_b[layer].reshape(1, d), ys)
    return xf.reshape(b, s, d)
```

```python
import functools
import math

import numpy as np
import jax
import jax.numpy as jnp
from jax import lax
from jax.experimental import pallas as pl
from jax.experimental.pallas import tpu as pltpu

F32 = jnp.float32
BF16 = jnp.bfloat16
I32 = jnp.int32

D_MODEL = 2048
HEAD_DIM = 128
HEADS = 4
MIX_W = HEADS * HEAD_DIM
IDX_HEADS = 16
IDX_DIM = 64
DSA_TOPK_MAX = 256
DIFF_DIM = 64
MOBA_BLOCK = 256
MOBA_TOPK = 3
DILATED_PATTERNS = ((128, 1), (512, 4), (2048, 16))
ROPE_THETA = 10000.0
N_EXPERTS = 16
N_GROUPS = 4
EXPERTS_PER_GROUP = 4
LN_EPS = 1e-5
DEPTH = 2
ALPHA = (2 * DEPTH) ** 0.25
LOG2E = math.log2(math.e)

LANES = 128
SUBLANES = 8
NEG = -1e30
M_FLOOR = -1e29
INT_MIN = -(2 ** 31)
NEG_INF_ORDER = 0x007FFFFF
VMEM_LIMIT = 56 * 1024 * 1024

PROJ_TM = 512
ATT_T = 256
DIFF_TQ = 512
DSA_TQ = 512
OUT_TM = 256
ROUTER_TM = 256
DISPATCH_TM = 256
EXPERT_TB = 256
COMBINE_TM = 256
DMA_UNROLL = 8
V_ROWS = 3 * MIX_W + HEAD_DIM

_OFF = {}
_o = 0
for _name, _n in (("a_q", 512), ("a_k", 128), ("a_v", 128), ("i_q", 1024), ("i_k", 64), ("i_w", 16),
                  ("b_q", 512), ("b_k", 512), ("b_v", 512), ("c_q", 512), ("c_k", 512), ("c_v", 512),
                  ("d_q", 512), ("d_k", 512), ("d_v", 512)):
    _OFF[_name] = (_o, _o + _n)
    _o += _n


def _params(sem=None):
    return pltpu.CompilerParams(dimension_semantics=sem, vmem_limit_bytes=VMEM_LIMIT)


def _dot_nt(a, b):
    return lax.dot_general(a, b, (((1,), (1,)), ((), ())), preferred_element_type=F32)


def _proj_kernel(x_ref, w_ref, scale_ref, *rest, mode):
    o_ref = rest[-1]
    acc = jnp.dot(x_ref[...], w_ref[...], preferred_element_type=F32)
    n_groups = acc.shape[1] // LANES
    for g in range(n_groups):
        y = acc[:, g * LANES:(g + 1) * LANES]
        if mode == "rope128":
            cos_ref, sin_ref = rest[0], rest[1]
            r = y * cos_ref[...] + pltpu.roll(y, 64, 1) * sin_ref[...]
        else:
            cos_ref, sina_ref, sinb_ref = rest[0], rest[1], rest[2]
            r = (y * cos_ref[...] + pltpu.roll(y, 96, 1) * sina_ref[...]
                 + pltpu.roll(y, 32, 1) * sinb_ref[...])
        r = r * scale_ref[:, g * LANES:(g + 1) * LANES]
        o_ref[:, g * LANES:(g + 1) * LANES] = r.astype(o_ref.dtype)


def _proj(x_bf, w_bf, col_scale, tables, mode, tn):
    t, d = x_bf.shape
    n = w_bf.shape[1]
    tm = min(PROJ_TM, t)
    in_specs = [pl.BlockSpec((tm, d), lambda j, i: (i, 0)),
                pl.BlockSpec((d, tn), lambda j, i: (0, j)),
                pl.BlockSpec((1, tn), lambda j, i: (0, j))]
    in_specs += [pl.BlockSpec((tm, LANES), lambda j, i: (i, 0)) for _ in tables]
    return pl.pallas_call(
        functools.partial(_proj_kernel, mode=mode),
        grid=(n // tn, t // tm),
        in_specs=in_specs,
        out_specs=pl.BlockSpec((tm, tn), lambda j, i: (i, j)),
        out_shape=jax.ShapeDtypeStruct((t, n), BF16),
        compiler_params=_params(("arbitrary", "arbitrary")),
    )(x_bf, w_bf, col_scale, *tables)


def _proj_t_kernel(x_ref, wt_ref, vt_ref, iwt_ref):
    acc = _dot_nt(wt_ref[...], x_ref[...])
    vt_ref[...] = acc[:V_ROWS].astype(vt_ref.dtype)
    iwt_ref[...] = acc[V_ROWS:]


def _proj_t(x_bf, wt_bf):
    t, d = x_bf.shape
    tm = min(ATT_T, t)
    rows = wt_bf.shape[0]
    return pl.pallas_call(
        _proj_t_kernel,
        grid=(t // tm,),
        in_specs=[pl.BlockSpec((tm, d), lambda i: (i, 0)),
                  pl.BlockSpec((rows, d), lambda i: (0, 0))],
        out_specs=[pl.BlockSpec((None, V_ROWS, tm), lambda i: (i, 0, 0)),
                   pl.BlockSpec((None, IDX_HEADS, tm), lambda i: (i, 0, 0))],
        out_shape=[jax.ShapeDtypeStruct((t // tm, V_ROWS, tm), BF16),
                   jax.ShapeDtypeStruct((t // tm, IDX_HEADS, tm), F32)],
        compiler_params=_params(("arbitrary",)),
    )(x_bf, wt_bf)


def _online_updates(scores, mask, values, m_ref, l_ref, acc_ref, weight=None):
    probs, alphas = [], []
    masks = mask if isinstance(mask, list) else [mask] * len(scores)
    for idx, (s_t, mask) in enumerate(zip(scores, masks)):
        if mask is not None:
            s_t = jnp.where(mask, s_t, NEG)
        m_prev = m_ref[idx]
        m_new = jnp.maximum(m_prev, jnp.max(s_t, axis=0, keepdims=True))
        p = jnp.exp2(s_t - m_new)
        if weight is not None:
            p = p * weight
        alpha = jnp.exp2(m_prev - m_new)
        l_ref[idx] = alpha * l_ref[idx] + jnp.sum(p, axis=0, keepdims=True)
        m_ref[idx] = m_new
        probs.append(p.astype(BF16))
        alphas.append(alpha)
    for idx, (p, alpha) in enumerate(zip(probs, alphas)):
        acc_ref[idx] = alpha * acc_ref[idx] + jnp.dot(values[idx], p, preferred_element_type=F32)


def _init_softmax(m_ref, l_ref, acc_ref):
    m_ref[...] = jnp.full(m_ref.shape, M_FLOOR, F32)
    l_ref[...] = jnp.zeros(l_ref.shape, F32)
    acc_ref[...] = jnp.zeros(acc_ref.shape, F32)


def _softmax_scratch(n, t):
    return [pltpu.VMEM((n, 1, t), F32), pltpu.VMEM((n, 1, t), F32), pltpu.VMEM((n, HEAD_DIM, t), F32)]


def _causal_mask(tk, tq, k_start):
    return k_start + lax.broadcasted_iota(I32, (tk, tq), 0) <= lax.broadcasted_iota(I32, (tk, tq), 1)


def _causal_t(t):
    return _causal_mask(t, t, 0)


def _head(ref_or_val, h):
    return ref_or_val[:, h * HEAD_DIM:(h + 1) * HEAD_DIM]


def _order_to_f32(u):
    key = u ^ jnp.int32(INT_MIN)
    bits = jnp.where(key < 0, key ^ jnp.int32(0x7FFFFFFF), key)
    return lax.bitcast_convert_type(bits, F32)


def _dsa_kernel(aq_ref, ak_ref, avt_ref, iq_ref, ik_ref, iwt_ref, o_ref,
                score_ref, m_ref, l_ref, acc_ref, jlim_ref, *, n_keep):
    tq = aq_ref.shape[0]
    t = avt_ref.shape[2]
    i = pl.program_id(1)
    n_chunks = (i + 1) * (tq // t)
    lane = lax.broadcasted_iota(I32, (1, LANES), 1)
    k_off = lax.broadcasted_iota(I32, (t, 1), 0)
    q_pos = i * tq + lax.broadcasted_iota(I32, (1, tq), 1)
    iw_t = jnp.concatenate([iwt_ref[n] for n in range(tq // t)], axis=1)

    def score_chunk(c, carry):
        ik_c = ik_ref[pl.ds(pl.multiple_of(c * t, t), t), :]
        ik_lo = jnp.where(lane < IDX_DIM, ik_c, jnp.zeros_like(ik_c))
        ik_hi = jnp.where(lane >= IDX_DIM, ik_c, jnp.zeros_like(ik_c))
        score = jnp.zeros((t, tq), F32)
        for j in range(IDX_HEADS // 2):
            q_pair = iq_ref[:, j * LANES:(j + 1) * LANES]
            s_even = jnp.maximum(_dot_nt(ik_lo, q_pair), 0.0)
            s_odd = jnp.maximum(_dot_nt(ik_hi, q_pair), 0.0)
            score = score + s_even * iw_t[2 * j:2 * j + 1, :] + s_odd * iw_t[2 * j + 1:2 * j + 2, :]
        score_ref[c] = jnp.where(c * t + k_off <= q_pos, score, -jnp.inf)
        return carry

    lax.fori_loop(0, n_chunks, score_chunk, 0)

    def count(pred):
        def body(c, part):
            hit = jnp.where(pred(score_ref[c], c), 1, 0)
            return part + jnp.sum(hit.reshape(t // SUBLANES, SUBLANES, tq), axis=0)
        part = lax.fori_loop(0, n_chunks, body, jnp.zeros((SUBLANES, tq), I32))
        return jnp.sum(part, axis=0, keepdims=True)

    def bisect(it, thr_u):
        cand_u = thr_u | jnp.left_shift(jnp.int32(1), 31 - it)
        cand = _order_to_f32(cand_u)
        cnt = count(lambda sc, c: sc >= cand)
        return jnp.where(cnt >= n_keep, cand_u, thr_u)

    thr_u = lax.fori_loop(0, 32, bisect, jnp.zeros((1, tq), I32))
    below = (thr_u >= 0) & (thr_u < NEG_INF_ORDER)
    thr = jnp.where(below, -jnp.inf, _order_to_f32(thr_u))

    n_gt = count(lambda sc, c: sc > thr)
    n_eq = count(lambda sc, c: sc == thr)
    need = n_keep - n_gt
    s_total = score_ref.shape[0] * t
    jlim_ref[...] = jnp.full((1, tq), s_total, I32)
    excess = jnp.max(jnp.where((n_eq > need) & (thr > -jnp.inf), 1, 0))

    @pl.when(excess > 0)
    def _():
        n_bits = max(1, (s_total - 1).bit_length())

        def bisect_idx(it, j):
            cand = j | jnp.left_shift(jnp.int32(1), n_bits - 1 - it)
            cnt = count(lambda sc, c: (sc == thr) & (c * t + k_off < cand))
            return jnp.where(cnt < need, cand, j)

        jlim_ref[...] = lax.fori_loop(0, n_bits, bisect_idx, jnp.zeros((1, tq), I32))

    _init_softmax(m_ref, l_ref, acc_ref)
    jlim = jnp.where(thr > -jnp.inf, jlim_ref[...], -1)
    thr_all = jnp.where(thr > -jnp.inf, thr, jnp.finfo(F32).min)

    def attend(select):
        def body(c, carry):
            start = pl.multiple_of(c * t, t)
            k_c = ak_ref[pl.ds(start, t), :]
            scores = [_dot_nt(k_c, _head(aq_ref, h)) for h in range(HEADS)]
            sel = select(score_ref[c], c * t + k_off)
            _online_updates(scores, sel, [avt_ref[c]] * HEADS, m_ref, l_ref, acc_ref)
            return carry
        lax.fori_loop(0, n_chunks, body, 0)

    @pl.when(excess > 0)
    def _():
        attend(lambda sc, k_pos: (sc > thr) | ((sc == thr) & (k_pos <= jlim)))

    @pl.when(excess <= 0)
    def _():
        attend(lambda sc, k_pos: sc >= thr_all)
    for h in range(HEADS):
        o_ref[:, h * HEAD_DIM:(h + 1) * HEAD_DIM] = (acc_ref[h] / l_ref[h]).T.astype(o_ref.dtype)


def _dsa(p128, p64, vt, iwt, n_keep):
    b, s, _ = p128.shape
    t = min(ATT_T, s)
    tq = min(DSA_TQ, s)
    nc = s // t
    return pl.pallas_call(
        functools.partial(_dsa_kernel, n_keep=n_keep),
        grid=(b, s // tq),
        in_specs=[
            pl.BlockSpec((None, tq, MIX_W), lambda bb, i: (bb, i, 0)),
            pl.BlockSpec((None, s, HEAD_DIM), lambda bb, i: (bb, 0, 20)),
            pl.BlockSpec((None, nc, HEAD_DIM, t), lambda bb, i: (bb, 0, 12, 0)),
            pl.BlockSpec((None, tq, 1024), lambda bb, i: (bb, i, 0)),
            pl.BlockSpec((None, s, LANES), lambda bb, i: (bb, 0, 16)),
            pl.BlockSpec((None, tq // t, IDX_HEADS, t), lambda bb, i: (bb, i, 0, 0)),
        ],
        out_specs=pl.BlockSpec((None, tq, MIX_W), lambda bb, i: (bb, i, 0)),
        out_shape=jax.ShapeDtypeStruct((b, s, MIX_W), BF16),
        scratch_shapes=[pltpu.VMEM((nc, t, tq), F32)] + _softmax_scratch(HEADS, tq)
        + [pltpu.VMEM((1, tq), I32)],
        compiler_params=_params(("arbitrary", "arbitrary")),
    )(p128, p128, vt, p64, p64, iwt)


def _diff_kernel(q_ref, k_ref, vt_ref, lam_ref, g_ref, o_ref, m_ref, l_ref, acc_ref, *, lam_init):
    tq = q_ref.shape[0]
    t = vt_ref.shape[2]
    per_tile = tq // t
    i = pl.program_id(1)
    lane = lax.broadcasted_iota(I32, (1, LANES), 1)
    _init_softmax(m_ref, l_ref, acc_ref)

    def step(c, mask):
        k_c = k_ref[pl.ds(pl.multiple_of(c * t, t), t), :]
        v_c = vt_ref[c]
        scores, values = [], []
        for h in range(HEADS):
            q_h = _head(q_ref, h)
            k_h = _head(k_c, h)
            q_1 = jnp.where(lane < DIFF_DIM, q_h, jnp.zeros_like(q_h))
            q_2 = jnp.where(lane >= DIFF_DIM, q_h, jnp.zeros_like(q_h))
            scores += [_dot_nt(k_h, q_1), _dot_nt(k_h, q_2)]
            values += [v_c[h * HEAD_DIM:(h + 1) * HEAD_DIM, :]] * 2
        _online_updates(scores, mask, values, m_ref, l_ref, acc_ref)

    def full_step(c, carry):
        step(c, None)
        return carry

    lax.fori_loop(0, i * per_tile, full_step, 0)
    for d in range(per_tile):
        step(i * per_tile + d, _causal_mask(t, tq, d * t))

    lam_p = lam_ref[...]
    lam = (jnp.exp(jnp.sum(lam_p[0:1] * lam_p[1:2], axis=1, keepdims=True))
           - jnp.exp(jnp.sum(lam_p[2:3] * lam_p[3:4], axis=1, keepdims=True)) + lam_init)
    for h in range(HEADS):
        o = acc_ref[2 * h] / l_ref[2 * h] - lam * (acc_ref[2 * h + 1] / l_ref[2 * h + 1])
        o = o * lax.rsqrt(jnp.mean(o * o, axis=0, keepdims=True) + LN_EPS) * g_ref[...]
        o_ref[:, h * HEAD_DIM:(h + 1) * HEAD_DIM] = (o * (1.0 - lam_init)).T.astype(o_ref.dtype)


def _diff(p64, vt, lam_p, g_col, lam_init):
    b, s, _ = p64.shape
    t = min(ATT_T, s)
    tq = min(DIFF_TQ, s)
    nc = s // t
    return pl.pallas_call(
        functools.partial(_diff_kernel, lam_init=lam_init),
        grid=(b, s // tq),
        in_specs=[
            pl.BlockSpec((None, tq, MIX_W), lambda bb, i: (bb, i, 2)),
            pl.BlockSpec((None, s, MIX_W), lambda bb, i: (bb, 0, 3)),
            pl.BlockSpec((None, nc, MIX_W, t), lambda bb, i: (bb, 0, 0, 0)),
            pl.BlockSpec((4, DIFF_DIM), lambda bb, i: (0, 0)),
            pl.BlockSpec((HEAD_DIM, 1), lambda bb, i: (0, 0)),
        ],
        out_specs=pl.BlockSpec((None, tq, MIX_W), lambda bb, i: (bb, i, 0)),
        out_shape=jax.ShapeDtypeStruct((b, s, MIX_W), BF16),
        scratch_shapes=_softmax_scratch(2 * HEADS, tq),
        compiler_params=_params(("arbitrary", "arbitrary")),
    )(p64, p64, vt, lam_p, g_col)


def _kmean_kernel(k_ref, o_ref):
    n_blk = k_ref.shape[0] // MOBA_BLOCK
    o_ref[...] = jnp.zeros(o_ref.shape, o_ref.dtype)
    for n in range(n_blk):
        blk = k_ref[n * MOBA_BLOCK:(n + 1) * MOBA_BLOCK, :].astype(F32)
        o_ref[n:n + 1, :] = jnp.mean(blk, axis=0, keepdims=True)


def _kmean(p128):
    b, s, _ = p128.shape
    rows = max(SUBLANES, s // MOBA_BLOCK)
    return pl.pallas_call(
        _kmean_kernel,
        grid=(b,),
        in_specs=[pl.BlockSpec((None, s, MIX_W), lambda bb: (bb, 0, 2))],
        out_specs=pl.BlockSpec((None, rows, MIX_W), lambda bb: (bb, 0, 0)),
        out_shape=jax.ShapeDtypeStruct((b, rows, MIX_W), F32),
        compiler_params=_params(("arbitrary",)),
    )(p128)


def _moba_kernel(q_ref, k_ref, vt_ref, km_ref, o_ref, sel_ref, m_ref, l_ref, acc_ref):
    t = q_ref.shape[0]
    n_blk = km_ref.shape[0]
    i = pl.program_id(1)
    blk = lax.broadcasted_iota(I32, (n_blk, 1), 0)
    _init_softmax(m_ref, l_ref, acc_ref)

    for h in range(HEADS):
        gate = _dot_nt(_head(km_ref, h).astype(BF16), _head(q_ref, h))
        gate = jnp.where(blk < i, gate, -jnp.inf)
        sel = jnp.zeros(gate.shape, F32)
        for _ in range(MOBA_TOPK):
            best = jnp.max(gate, axis=0, keepdims=True)
            first = jnp.min(jnp.where(gate == best, blk, n_blk), axis=0, keepdims=True)
            pick = (blk == first) & (best > -jnp.inf)
            sel = jnp.where(pick, 1.0, sel)
            gate = jnp.where(pick, -jnp.inf, gate)
        sel_ref[h] = sel

    def step(c, diagonal):
        k_c = k_ref[pl.ds(pl.multiple_of(c * t, t), t), :]
        v_c = vt_ref[c]
        scores = [_dot_nt(_head(k_c, h), _head(q_ref, h)) for h in range(HEADS)]
        values = [v_c[h * HEAD_DIM:(h + 1) * HEAD_DIM, :] for h in range(HEADS)]
        if diagonal:
            masks = _causal_t(t)
        else:
            masks = [sel_ref[h, pl.ds(c, 1), :] > 0.0 for h in range(HEADS)]
        _online_updates(scores, masks, values, m_ref, l_ref, acc_ref)

    def past_step(c, carry):
        step(c, False)
        return carry

    lax.fori_loop(0, i, past_step, 0)
    step(i, True)
    for h in range(HEADS):
        o_ref[:, h * HEAD_DIM:(h + 1) * HEAD_DIM] = (acc_ref[h] / l_ref[h]).T.astype(o_ref.dtype)


def _moba(p128, vt, kmean):
    b, s, _ = p128.shape
    t = MOBA_BLOCK
    nc = s // t
    rows = kmean.shape[1]
    return pl.pallas_call(
        _moba_kernel,
        grid=(b, nc),
        in_specs=[
            pl.BlockSpec((None, t, MIX_W), lambda bb, i: (bb, i, 1)),
            pl.BlockSpec((None, s, MIX_W), lambda bb, i: (bb, 0, 2)),
            pl.BlockSpec((None, nc, MIX_W, t), lambda bb, i: (bb, 0, 1, 0)),
            pl.BlockSpec((None, rows, MIX_W), lambda bb, i: (bb, 0, 0)),
        ],
        out_specs=pl.BlockSpec((None, t, MIX_W), lambda bb, i: (bb, i, 0)),
        out_shape=jax.ShapeDtypeStruct((b, s, MIX_W), BF16),
        scratch_shapes=[pltpu.VMEM((HEADS, rows, t), F32)] + _softmax_scratch(HEADS, t),
        compiler_params=_params(("arbitrary", "arbitrary")),
    )(p128, p128, vt, kmean)


def _dilated_kernel(q_ref, k_ref, vt_ref, o_ref, m_ref, l_ref, acc_ref):
    t = q_ref.shape[0]
    i = pl.program_id(1)
    _init_softmax(m_ref, l_ref, acc_ref)
    max_window = max(w for w, _ in DILATED_PATTERNS)
    first = jnp.maximum(i - max_window // t, 0)
    rel = lax.broadcasted_iota(I32, (t, t), 1) - lax.broadcasted_iota(I32, (t, t), 0)

    def step(c, patterns):
        k_c = k_ref[pl.ds(pl.multiple_of(c * t, t), t), :]
        v_c = vt_ref[c]
        dist = rel + (i - c) * t
        mult = jnp.zeros((t, t), F32)
        for window, dilation in patterns:
            hit = (dist >= 0) & (dist <= window) & ((dist & (dilation - 1)) == 0)
            mult = mult + jnp.where(hit, 1.0, 0.0)
        mask = mult > 0.0
        scores = [_dot_nt(_head(k_c, h), _head(q_ref, h)) for h in range(HEADS)]
        values = [v_c[h * HEAD_DIM:(h + 1) * HEAD_DIM, :] for h in range(HEADS)]
        _online_updates(scores, mask, values, m_ref, l_ref, acc_ref, weight=mult)

    widest = max(DILATED_PATTERNS)
    others = [p for p in DILATED_PATTERNS if p != widest]
    n_near = (max(w for w, _ in others) + t - 1) // t + 1
    near_first = jnp.maximum(i + 1 - n_near, first)

    def far_step(c, carry):
        step(c, [widest])
        return carry

    def near_step(c, carry):
        step(c, DILATED_PATTERNS)
        return carry

    lax.fori_loop(first, near_first, far_step, 0)
    lax.fori_loop(near_first, i + 1, near_step, 0)
    for h in range(HEADS):
        o_ref[:, h * HEAD_DIM:(h + 1) * HEAD_DIM] = (acc_ref[h] / l_ref[h]).T.astype(o_ref.dtype)


def _dilated(p128, vt):
    b, s, _ = p128.shape
    t = min(ATT_T, s)
    nc = s // t
    return pl.pallas_call(
        _dilated_kernel,
        grid=(b, nc),
        in_specs=[
            pl.BlockSpec((None, t, MIX_W), lambda bb, i: (bb, i, 3)),
            pl.BlockSpec((None, s, MIX_W), lambda bb, i: (bb, 0, 4)),
            pl.BlockSpec((None, nc, MIX_W, t), lambda bb, i: (bb, 0, 2, 0)),
        ],
        out_specs=pl.BlockSpec((None, t, MIX_W), lambda bb, i: (bb, i, 0)),
        out_shape=jax.ShapeDtypeStruct((b, s, MIX_W), BF16),
        scratch_shapes=_softmax_scratch(HEADS, t),
        compiler_params=_params(("arbitrary", "arbitrary")),
    )(p128, p128, vt)


def _layer_norm(z, g, b):
    mu = jnp.mean(z, axis=1, keepdims=True)
    zc = z - mu
    var = jnp.mean(zc * zc, axis=1, keepdims=True)
    return zc * lax.rsqrt(var + LN_EPS) * g + b


def _outproj_ln_kernel(oa_ref, ob_ref, oc_ref, od_ref, w_ref, x_ref, g_ref, b_ref, xo_ref, xb_ref):
    y = jnp.dot(oa_ref[...], w_ref[0:MIX_W, :], preferred_element_type=F32)
    y = y + jnp.dot(ob_ref[...], w_ref[MIX_W:2 * MIX_W, :], preferred_element_type=F32)
    y = y + jnp.dot(oc_ref[...], w_ref[2 * MIX_W:3 * MIX_W, :], preferred_element_type=F32)
    y = y + jnp.dot(od_ref[...], w_ref[3 * MIX_W:4 * MIX_W, :], preferred_element_type=F32)
    out = _layer_norm(ALPHA * x_ref[...] + y, g_ref[...], b_ref[...])
    xo_ref[...] = out
    xb_ref[...] = out.astype(BF16)


def _outproj_ln(oa, ob, oc, od, w_bf, x, g, b):
    t, d = x.shape
    tm = min(OUT_TM, t)
    mix_spec = pl.BlockSpec((tm, MIX_W), lambda i: (i, 0))
    row_spec = pl.BlockSpec((tm, d), lambda i: (i, 0))
    vec_spec = pl.BlockSpec((1, d), lambda i: (0, 0))
    return pl.pallas_call(
        _outproj_ln_kernel,
        grid=(t // tm,),
        in_specs=[mix_spec, mix_spec, mix_spec, mix_spec,
                  pl.BlockSpec((d, d), lambda i: (0, 0)), row_spec, vec_spec, vec_spec],
        out_specs=[row_spec, row_spec],
        out_shape=[jax.ShapeDtypeStruct((t, d), F32), jax.ShapeDtypeStruct((t, d), BF16)],
        compiler_params=_params(("arbitrary",)),
    )(oa, ob, oc, od, w_bf, x, g, b)


def _first_argmax(vals):
    best, arg = vals[0], jnp.zeros(vals[0].shape, I32)
    for j in range(1, len(vals)):
        better = vals[j] > best
        arg = jnp.where(better, j, arg)
        best = jnp.where(better, vals[j], best)
    return arg, best


def _pick(rows, idx):
    out = rows[0]
    for j in range(1, len(rows)):
        out = jnp.where(idx == j, rows[j], out)
    return out


def _router_kernel(x_ref, rwt_ref, bias_ref, e_ref, gate_ref, rank_ref, cnt_ref, carry_ref):
    tm = x_ref.shape[0]

    @pl.when(pl.program_id(0) == 0)
    def _():
        carry_ref[...] = jnp.zeros(carry_ref.shape, F32)

    logits = lax.dot_general(rwt_ref[...], x_ref[...], (((1,), (1,)), ((), ())),
                             precision=lax.Precision.HIGHEST, preferred_element_type=F32)
    aff = 1.0 / (1.0 + jnp.exp(-logits))
    biased = aff + bias_ref[...]
    sel_rows = [biased[r:r + 1, :] for r in range(N_EXPERTS)]
    aff_rows = [aff[r:r + 1, :] for r in range(N_EXPERTS)]

    group_scores = []
    for g in range(N_GROUPS):
        r = sel_rows[g * EXPERTS_PER_GROUP:(g + 1) * EXPERTS_PER_GROUP]
        best_pair = r[0] + r[1]
        for a in range(EXPERTS_PER_GROUP):
            for c in range(a + 1, EXPERTS_PER_GROUP):
                if (a, c) != (0, 1):
                    best_pair = jnp.maximum(best_pair, r[a] + r[c])
        group_scores.append(best_pair)
    grp, _ = _first_argmax(group_scores)

    in_sel = [_pick([sel_rows[g * EXPERTS_PER_GROUP + j] for g in range(N_GROUPS)], grp)
              for j in range(EXPERTS_PER_GROUP)]
    in_aff = [_pick([aff_rows[g * EXPERTS_PER_GROUP + j] for g in range(N_GROUPS)], grp)
              for j in range(EXPERTS_PER_GROUP)]
    first, _ = _first_argmax(in_sel)
    second, _ = _first_argmax([jnp.where(first == j, -jnp.inf, in_sel[j]) for j in range(EXPERTS_PER_GROUP)])
    a0 = _pick(in_aff, first)
    a1 = _pick(in_aff, second)
    e0 = grp * EXPERTS_PER_GROUP + first
    e1 = grp * EXPERTS_PER_GROUP + second
    e_ref[0:1, :] = e0
    e_ref[1:2, :] = e1
    gate_ref[0:1, :] = a0 / (a0 + a1)
    gate_ref[1:2, :] = a1 / (a0 + a1)

    e_iota = lax.broadcasted_iota(I32, (N_EXPERTS, tm), 0)
    one_hot = ((e_iota == e0) | (e_iota == e1)).astype(BF16)
    before = (lax.broadcasted_iota(I32, (tm, tm), 0) < lax.broadcasted_iota(I32, (tm, tm), 1)).astype(BF16)
    prior = jnp.dot(one_hot, before, preferred_element_type=F32) + carry_ref[...]
    rank_ref[0:1, :] = jnp.sum(jnp.where(e_iota == e0, prior, 0.0), axis=0, keepdims=True).astype(I32)
    rank_ref[1:2, :] = jnp.sum(jnp.where(e_iota == e1, prior, 0.0), axis=0, keepdims=True).astype(I32)
    carry_ref[...] = carry_ref[...] + jnp.sum(one_hot.astype(F32), axis=1, keepdims=True)
    cnt_ref[...] = jnp.broadcast_to(carry_ref[...], cnt_ref.shape).astype(I32)


def _router(x, rw_t, bias):
    t, d = x.shape
    tm = min(ROUTER_TM, t)
    tok_spec = pl.BlockSpec((2, tm), lambda i: (0, i))
    return pl.pallas_call(
        _router_kernel,
        grid=(t // tm,),
        in_specs=[pl.BlockSpec((tm, d), lambda i: (i, 0)),
                  pl.BlockSpec((N_EXPERTS, d), lambda i: (0, 0)),
                  pl.BlockSpec((N_EXPERTS, 1), lambda i: (0, 0))],
        out_specs=[tok_spec, tok_spec, tok_spec, pl.BlockSpec((N_EXPERTS, LANES), lambda i: (0, 0))],
        out_shape=[jax.ShapeDtypeStruct((2, t), I32), jax.ShapeDtypeStruct((2, t), F32),
                   jax.ShapeDtypeStruct((2, t), I32), jax.ShapeDtypeStruct((N_EXPERTS, LANES), I32)],
        scratch_shapes=[pltpu.VMEM((N_EXPERTS, 1), F32)],
        compiler_params=_params(("arbitrary",)),
    )(x, rw_t, bias)


def _row_copy(src, src_row, dst, dst_row, sem):
    return pltpu.make_async_copy(src.at[pl.ds(src_row, 1)], dst.at[pl.ds(dst_row, 1)], sem)


def _dispatch_kernel(dest_ref, x_ref, slots_in_hbm, slots_hbm, sem, *, tm, n_tok):
    del slots_in_hbm
    base = pl.program_id(0) * tm

    def issue(r, carry):
        tok = base + r
        _row_copy(x_ref, r, slots_hbm, dest_ref[tok], sem).start()
        _row_copy(x_ref, r, slots_hbm, dest_ref[n_tok + tok], sem).start()
        return carry

    def drain(r, carry):
        _row_copy(x_ref, 0, slots_hbm, 0, sem).wait()
        _row_copy(x_ref, 0, slots_hbm, 0, sem).wait()
        return carry

    lax.fori_loop(0, tm, issue, 0, unroll=DMA_UNROLL)
    lax.fori_loop(0, tm, drain, 0, unroll=DMA_UNROLL)


def _dispatch(dest_flat, x, n_slots):
    t, d = x.shape
    tm = min(DISPATCH_TM, t)
    return pl.pallas_call(
        functools.partial(_dispatch_kernel, tm=tm, n_tok=t),
        grid_spec=pltpu.PrefetchScalarGridSpec(
            num_scalar_prefetch=1,
            grid=(t // tm,),
            in_specs=[pl.BlockSpec((tm, d), lambda i, dest: (i, 0)), pl.BlockSpec(memory_space=pl.ANY)],
            out_specs=pl.BlockSpec(memory_space=pl.ANY),
            scratch_shapes=[pltpu.SemaphoreType.DMA(())],
        ),
        out_shape=jax.ShapeDtypeStruct((n_slots, d), x.dtype),
        input_output_aliases={2: 0},
        compiler_params=_params(("arbitrary",)),
    )(dest_flat, x, jnp.zeros((n_slots, d), x.dtype))


def _expert_kernel(be_ref, na_ref, xs_ref, wg_ref, wu_ref, wd_ref, ys_ref):
    del be_ref

    @pl.when(pl.program_id(0) < na_ref[0])
    def _():
        xb = xs_ref[...].astype(BF16)
        gate = jnp.dot(xb, wg_ref[...], preferred_element_type=F32)
        up = jnp.dot(xb, wu_ref[...], preferred_element_type=F32)
        hidden = gate * (1.0 / (1.0 + jnp.exp(-gate))) * up
        ys_ref[...] = jnp.dot(hidden.astype(BF16), wd_ref[...], preferred_element_type=F32)

    @pl.when(pl.program_id(0) >= na_ref[0])
    def _():
        ys_ref[...] = jnp.zeros(ys_ref.shape, F32)


def _experts(block_expert, n_active, xs, wg, wu, wd):
    n_slots, d = xs.shape
    tb = EXPERT_TB
    f = wg.shape[2]

    def last_active(n, na):
        return jnp.maximum(jnp.minimum(n, na[0] - 1), 0)

    def row_map(n, be, na):
        return (last_active(n, na), 0)

    def w_map(n, be, na):
        return (be[last_active(n, na)], 0, 0)

    return pl.pallas_call(
        _expert_kernel,
        grid_spec=pltpu.PrefetchScalarGridSpec(
            num_scalar_prefetch=2,
            grid=(n_slots // tb,),
            in_specs=[pl.BlockSpec((tb, d), row_map),
                      pl.BlockSpec((None, d, f), w_map),
                      pl.BlockSpec((None, d, f), w_map),
                      pl.BlockSpec((None, f, d), w_map)],
            out_specs=pl.BlockSpec((tb, d), lambda n, be, na: (n, 0)),
        ),
        out_shape=jax.ShapeDtypeStruct((n_slots, d), F32),
        compiler_params=_params(("arbitrary",)),
    )(block_expert, n_active, xs, wg, wu, wd)


def _combine_ln_kernel(dest_ref, x_ref, gate_ref, g_ref, b_ref, ys_hbm, xo_ref, xb_ref, buf_ref, sem,
                       *, n_tok):
    tm = x_ref.shape[0]
    base = pl.program_id(0) * tm

    def issue(r, carry):
        tok = base + r
        _row_copy(ys_hbm, dest_ref[tok], buf_ref.at[0], r, sem).start()
        _row_copy(ys_hbm, dest_ref[n_tok + tok], buf_ref.at[1], r, sem).start()
        return carry

    def drain(r, carry):
        _row_copy(ys_hbm, 0, buf_ref.at[0], 0, sem).wait()
        _row_copy(ys_hbm, 0, buf_ref.at[1], 0, sem).wait()
        return carry

    lax.fori_loop(0, tm, issue, 0, unroll=DMA_UNROLL)
    lax.fori_loop(0, tm, drain, 0, unroll=DMA_UNROLL)
    gate = gate_ref[...]
    moe = buf_ref[0] * gate[:, 0:1] + buf_ref[1] * gate[:, 1:2]
    out = _layer_norm(ALPHA * x_ref[...] + moe, g_ref[...], b_ref[...])
    xo_ref[...] = out
    xb_ref[...] = out.astype(BF16)


def _combine_ln(dest_flat, x, gate_t, g, b, ys):
    t, d = x.shape
    tm = min(COMBINE_TM, t)
    row_spec = pl.BlockSpec((tm, d), lambda i, dest: (i, 0))
    vec_spec = pl.BlockSpec((1, d), lambda i, dest: (0, 0))
    return pl.pallas_call(
        functools.partial(_combine_ln_kernel, n_tok=t),
        grid_spec=pltpu.PrefetchScalarGridSpec(
            num_scalar_prefetch=1,
            grid=(t // tm,),
            in_specs=[row_spec, pl.BlockSpec((tm, 2), lambda i, dest: (i, 0)), vec_spec, vec_spec,
                      pl.BlockSpec(memory_space=pl.ANY)],
            out_specs=[row_spec, row_spec],
            scratch_shapes=[pltpu.VMEM((2, tm, d), F32), pltpu.SemaphoreType.DMA(())],
        ),
        out_shape=[jax.ShapeDtypeStruct((t, d), F32), jax.ShapeDtypeStruct((t, d), BF16)],
        compiler_params=_params(("arbitrary",)),
    )(dest_flat, x, gate_t, g, b, ys)


def _rope_tables(positions):
    pos = positions.reshape(-1).astype(F32)[:, None]

    def cos_sin(d):
        inv = jnp.power(ROPE_THETA, -jnp.arange(0, d, 2, dtype=F32) / d)
        ang = pos * inv
        return jnp.cos(ang), jnp.sin(ang)

    c, s = cos_sin(HEAD_DIM)
    t128 = (jnp.concatenate([c, c], axis=1), jnp.concatenate([-s, s], axis=1))
    c, s = cos_sin(IDX_DIM)
    z = jnp.zeros_like(s)
    t64 = (jnp.concatenate([c, c, c, c], axis=1),
           jnp.concatenate([-s, z, -s, z], axis=1),
           jnp.concatenate([z, s, z, s], axis=1))
    return t128, t64


def _col_scales():
    q128 = HEAD_DIM ** -0.5 * LOG2E
    s128 = np.ones((1, 5 * MIX_W + HEAD_DIM), np.float32)
    s128[:, 0:2 * MIX_W] = q128
    s128[:, 3 * MIX_W:4 * MIX_W] = q128
    s64 = np.ones((1, 2304), np.float32)
    s64[:, 1024:1024 + MIX_W] = DIFF_DIM ** -0.5 * LOG2E
    return jnp.asarray(s128), jnp.asarray(s64)


def _split_w_in(w):
    def cols(name):
        lo, hi = _OFF[name]
        return w[:, lo:hi]

    d = w.shape[0]
    w128 = jnp.concatenate([cols(n) for n in ("a_q", "c_q", "c_k", "d_q", "d_k", "a_k")], axis=1)
    w64 = jnp.concatenate([cols("i_q"), cols("b_q"), cols("b_k"), cols("i_k"), cols("i_k"),
                           jnp.zeros((d, LANES), w.dtype)], axis=1)
    wt = jnp.concatenate([cols("b_v"), cols("c_v"), cols("d_v"), cols("a_v"), cols("i_w")], axis=1).T
    return w128, w64, wt


def _moe_layout(e, rank, counts, tb):
    t = e.shape[1]
    n_slots = 2 * t + N_EXPERTS * tb
    padded = (counts + tb - 1) // tb * tb
    pend = jnp.cumsum(padded)
    pstart = pend - padded
    experts = jnp.arange(N_EXPERTS, dtype=I32)[:, None, None]
    seg_start = jnp.sum(jnp.where(e[None] == experts, pstart[:, None, None], 0), axis=0)
    dest = (seg_start + rank).astype(I32).reshape(-1)
    blocks = jnp.arange(n_slots // tb, dtype=I32) * tb
    block_expert = jnp.minimum(jnp.sum(blocks[:, None] >= pend[None, :], axis=1), N_EXPERTS - 1).astype(I32)
    n_active = (pend[-1:] // tb).astype(I32)
    return dest, block_expert, n_active, n_slots


def kernel(x, positions, w_in, w_out, diff_lambda, diff_norm_g, ln_mix_g, ln_mix_b, router_w, router_bias,
           w_gate, w_up, w_down, ln_ffn_g, ln_ffn_b):
    b, s, d = x.shape
    t = b * s
    n_keep = min(DSA_TOPK_MAX, s // 4)
    t128, t64 = _rope_tables(positions)
    s128, s64 = _col_scales()
    rw_t = router_w.T
    bias = router_bias.reshape(N_EXPERTS, 1).astype(F32)
    xf = x.reshape(t, d)
    xb = xf.astype(BF16)
    nc = s // min(ATT_T, s)
    w_in_bf = w_in.astype(BF16)
    for layer in range(DEPTH):
        w128, w64, wt = _split_w_in(w_in_bf[layer])
        p128 = _proj(xb, w128, s128, t128, "rope128", 896).reshape(b, s, -1)
        p64 = _proj(xb, w64, s64, t64, "rope64", 768).reshape(b, s, -1)
        vt, iwt = _proj_t(xb, wt)
        vt = vt.reshape(b, nc, V_ROWS, -1)
        iwt = iwt.reshape(b, nc, IDX_HEADS, -1)

        lam_init = 0.8 - 0.6 * math.exp(-0.3 * layer)
        o_a = _dsa(p128, p64, vt, iwt, n_keep)
        o_b = _diff(p64, vt, diff_lambda[layer], diff_norm_g[layer].reshape(HEAD_DIM, 1), lam_init)
        o_c = _moba(p128, vt, _kmean(p128))
        o_d = _dilated(p128, vt)
        xf, xb = _outproj_ln(o_a.reshape(t, MIX_W), o_b.reshape(t, MIX_W), o_c.reshape(t, MIX_W),
                             o_d.reshape(t, MIX_W), w_out[layer].astype(BF16), xf,
                             ln_mix_g[layer].reshape(1, d), ln_mix_b[layer].reshape(1, d))

        e, gate, rank, counts = _router(xf, rw_t, bias)
        dest, block_expert, n_active, n_slots = _moe_layout(e, rank, counts[:, 0], EXPERT_TB)
        xs = _dispatch(dest, xf, n_slots)
        ys = _experts(block_expert, n_active, xs, w_gate[layer].astype(BF16), w_up[layer].astype(BF16),
                      w_down[layer].astype(BF16))
        xf, xb = _combine_ln(dest, xf, gate.T, ln_ffn_g[layer].reshape(1, d), ln_ffn_b[layer].reshape(1, d), ys)
    return xf.reshape(b, s, d)
```

```python
import functools
import math

import numpy as np
import jax
import jax.numpy as jnp
from jax import lax
from jax.experimental import pallas as pl
from jax.experimental.pallas import tpu as pltpu

F32 = jnp.float32
BF16 = jnp.bfloat16
I32 = jnp.int32

D_MODEL = 2048
HEAD_DIM = 128
HEADS = 4
MIX_W = HEADS * HEAD_DIM
IDX_HEADS = 16
IDX_DIM = 64
DSA_TOPK_MAX = 256
DIFF_DIM = 64
MOBA_BLOCK = 256
MOBA_TOPK = 3
DILATED_PATTERNS = ((128, 1), (512, 4), (2048, 16))
ROPE_THETA = 10000.0
N_EXPERTS = 16
N_GROUPS = 4
EXPERTS_PER_GROUP = 4
LN_EPS = 1e-5
DEPTH = 2
ALPHA = (2 * DEPTH) ** 0.25
LOG2E = math.log2(math.e)

LANES = 128
SUBLANES = 8
NEG = -1e30
M_FLOOR = -1e29
INT_MIN = -(2 ** 31)
NEG_INF_ORDER = 0x007FFFFF
VMEM_LIMIT = 56 * 1024 * 1024

PROJ_TM = 512
ATT_T = 256
DIFF_TQ = 512
DSA_TQ = 512
OUT_TM = 256
ROUTER_TM = 256
DISPATCH_TM = 256
EXPERT_TB = 256
COMBINE_TM = 256
DMA_UNROLL = 8
V_ROWS = 3 * MIX_W + HEAD_DIM

_OFF = {}
_o = 0
for _name, _n in (("a_q", 512), ("a_k", 128), ("a_v", 128), ("i_q", 1024), ("i_k", 64), ("i_w", 16),
                  ("b_q", 512), ("b_k", 512), ("b_v", 512), ("c_q", 512), ("c_k", 512), ("c_v", 512),
                  ("d_q", 512), ("d_k", 512), ("d_v", 512)):
    _OFF[_name] = (_o, _o + _n)
    _o += _n


def _params(sem=None):
    return pltpu.CompilerParams(dimension_semantics=sem, vmem_limit_bytes=VMEM_LIMIT)


def _dot_nt(a, b):
    return lax.dot_general(a, b, (((1,), (1,)), ((), ())), preferred_element_type=F32)


def _proj_kernel(x_ref, w_ref, scale_ref, *rest, mode):
    o_ref = rest[-1]
    acc = jnp.dot(x_ref[...], w_ref[...], preferred_element_type=F32)
    n_groups = acc.shape[1] // LANES
    for g in range(n_groups):
        y = acc[:, g * LANES:(g + 1) * LANES]
        if mode == "rope128":
            cos_ref, sin_ref = rest[0], rest[1]
            r = y * cos_ref[...] + pltpu.roll(y, 64, 1) * sin_ref[...]
        else:
            cos_ref, sina_ref, sinb_ref = rest[0], rest[1], rest[2]
            r = (y * cos_ref[...] + pltpu.roll(y, 96, 1) * sina_ref[...]
                 + pltpu.roll(y, 32, 1) * sinb_ref[...])
        r = r * scale_ref[:, g * LANES:(g + 1) * LANES]
        o_ref[:, g * LANES:(g + 1) * LANES] = r.astype(o_ref.dtype)


def _proj(x_bf, w_bf, col_scale, tables, mode, tn):
    t, d = x_bf.shape
    n = w_bf.shape[1]
    tm = min(PROJ_TM, t)
    in_specs = [pl.BlockSpec((tm, d), lambda j, i: (i, 0)),
                pl.BlockSpec((d, tn), lambda j, i: (0, j)),
                pl.BlockSpec((1, tn), lambda j, i: (0, j))]
    in_specs += [pl.BlockSpec((tm, LANES), lambda j, i: (i, 0)) for _ in tables]
    return pl.pallas_call(
        functools.partial(_proj_kernel, mode=mode),
        grid=(n // tn, t // tm),
        in_specs=in_specs,
        out_specs=pl.BlockSpec((tm, tn), lambda j, i: (i, j)),
        out_shape=jax.ShapeDtypeStruct((t, n), BF16),
        compiler_params=_params(("arbitrary", "arbitrary")),
    )(x_bf, w_bf, col_scale, *tables)


def _proj_t_kernel(x_ref, wt_ref, vt_ref, iwt_ref):
    acc = _dot_nt(wt_ref[...], x_ref[...])
    vt_ref[...] = acc[:V_ROWS].astype(vt_ref.dtype)
    iwt_ref[...] = acc[V_ROWS:]


def _proj_t(x_bf, wt_bf):
    t, d = x_bf.shape
    tm = min(ATT_T, t)
    rows = wt_bf.shape[0]
    return pl.pallas_call(
        _proj_t_kernel,
        grid=(t // tm,),
        in_specs=[pl.BlockSpec((tm, d), lambda i: (i, 0)),
                  pl.BlockSpec((rows, d), lambda i: (0, 0))],
        out_specs=[pl.BlockSpec((None, V_ROWS, tm), lambda i: (i, 0, 0)),
                   pl.BlockSpec((None, IDX_HEADS, tm), lambda i: (i, 0, 0))],
        out_shape=[jax.ShapeDtypeStruct((t // tm, V_ROWS, tm), BF16),
                   jax.ShapeDtypeStruct((t // tm, IDX_HEADS, tm), F32)],
        compiler_params=_params(("arbitrary",)),
    )(x_bf, wt_bf)


def _online_updates(scores, mask, values, m_ref, l_ref, acc_ref, weight=None):
    probs, alphas = [], []
    masks = mask if isinstance(mask, list) else [mask] * len(scores)
    for idx, (s_t, mask) in enumerate(zip(scores, masks)):
        if mask is not None:
            s_t = jnp.where(mask, s_t, NEG)
        m_prev = m_ref[idx]
        m_new = jnp.maximum(m_prev, jnp.max(s_t, axis=0, keepdims=True))
        p = jnp.exp2(s_t - m_new)
        if weight is not None:
            p = p * weight
        alpha = jnp.exp2(m_prev - m_new)
        l_ref[idx] = alpha * l_ref[idx] + jnp.sum(p, axis=0, keepdims=True)
        m_ref[idx] = m_new
        probs.append(p.astype(BF16))
        alphas.append(alpha)
    for idx, (p, alpha) in enumerate(zip(probs, alphas)):
        acc_ref[idx] = alpha * acc_ref[idx] + jnp.dot(values[idx], p, preferred_element_type=F32)


def _init_softmax(m_ref, l_ref, acc_ref):
    m_ref[...] = jnp.full(m_ref.shape, M_FLOOR, F32)
    l_ref[...] = jnp.zeros(l_ref.shape, F32)
    acc_ref[...] = jnp.zeros(acc_ref.shape, F32)


def _softmax_scratch(n, t):
    return [pltpu.VMEM((n, 1, t), F32), pltpu.VMEM((n, 1, t), F32), pltpu.VMEM((n, HEAD_DIM, t), F32)]


def _causal_mask(tk, tq, k_start):
    return k_start + lax.broadcasted_iota(I32, (tk, tq), 0) <= lax.broadcasted_iota(I32, (tk, tq), 1)


def _causal_t(t):
    return _causal_mask(t, t, 0)


def _head(ref_or_val, h):
    return ref_or_val[:, h * HEAD_DIM:(h + 1) * HEAD_DIM]


def _order_to_f32(u):
    key = u ^ jnp.int32(INT_MIN)
    bits = jnp.where(key < 0, key ^ jnp.int32(0x7FFFFFFF), key)
    return lax.bitcast_convert_type(bits, F32)


def _dsa_kernel(aq_ref, ak_ref, avt_ref, iq_ref, ik_ref, iwt_ref, o_ref,
                score_ref, m_ref, l_ref, acc_ref, jlim_ref, *, n_keep):
    tq = aq_ref.shape[0]
    t = avt_ref.shape[2]
    i = pl.program_id(1)
    n_chunks = (i + 1) * (tq // t)
    lane = lax.broadcasted_iota(I32, (1, LANES), 1)
    k_off = lax.broadcasted_iota(I32, (t, 1), 0)
    q_pos = i * tq + lax.broadcasted_iota(I32, (1, tq), 1)
    iw_t = jnp.concatenate([iwt_ref[n] for n in range(tq // t)], axis=1)

    def score_chunk(c, carry):
        ik_c = ik_ref[pl.ds(pl.multiple_of(c * t, t), t), :]
        ik_lo = jnp.where(lane < IDX_DIM, ik_c, jnp.zeros_like(ik_c))
        ik_hi = jnp.where(lane >= IDX_DIM, ik_c, jnp.zeros_like(ik_c))
        score = jnp.zeros((t, tq), F32)
        for j in range(IDX_HEADS // 2):
            q_pair = iq_ref[:, j * LANES:(j + 1) * LANES]
            s_even = jnp.maximum(_dot_nt(ik_lo, q_pair), 0.0)
            s_odd = jnp.maximum(_dot_nt(ik_hi, q_pair), 0.0)
            score = score + s_even * iw_t[2 * j:2 * j + 1, :] + s_odd * iw_t[2 * j + 1:2 * j + 2, :]
        score_ref[c] = jnp.where(c * t + k_off <= q_pos, score, -jnp.inf)
        return carry

    lax.fori_loop(0, n_chunks, score_chunk, 0)

    def count(pred):
        def body(c, part):
            hit = jnp.where(pred(score_ref[c], c), 1, 0)
            return part + jnp.sum(hit.reshape(t // SUBLANES, SUBLANES, tq), axis=0)
        part = lax.fori_loop(0, n_chunks, body, jnp.zeros((SUBLANES, tq), I32))
        return jnp.sum(part, axis=0, keepdims=True)

    def bisect(it, thr_u):
        cand_u = thr_u | jnp.left_shift(jnp.int32(1), 31 - it)
        cand = _order_to_f32(cand_u)
        cnt = count(lambda sc, c: sc >= cand)
        return jnp.where(cnt >= n_keep, cand_u, thr_u)

    thr_u = lax.fori_loop(0, 32, bisect, jnp.zeros((1, tq), I32))
    below = (thr_u >= 0) & (thr_u < NEG_INF_ORDER)
    thr = jnp.where(below, -jnp.inf, _order_to_f32(thr_u))

    n_gt = count(lambda sc, c: sc > thr)
    n_eq = count(lambda sc, c: sc == thr)
    need = n_keep - n_gt
    s_total = score_ref.shape[0] * t
    jlim_ref[...] = jnp.full((1, tq), s_total, I32)
    excess = jnp.max(jnp.where((n_eq > need) & (thr > -jnp.inf), 1, 0))

    @pl.when(excess > 0)
    def _():
        n_bits = max(1, (s_total - 1).bit_length())

        def bisect_idx(it, j):
            cand = j | jnp.left_shift(jnp.int32(1), n_bits - 1 - it)
            cnt = count(lambda sc, c: (sc == thr) & (c * t + k_off < cand))
            return jnp.where(cnt < need, cand, j)

        jlim_ref[...] = lax.fori_loop(0, n_bits, bisect_idx, jnp.zeros((1, tq), I32))

    _init_softmax(m_ref, l_ref, acc_ref)
    jlim = jnp.where(thr > -jnp.inf, jlim_ref[...], -1)
    thr_all = jnp.where(thr > -jnp.inf, thr, jnp.finfo(F32).min)

    def attend(select):
        def body(c, carry):
            start = pl.multiple_of(c * t, t)
            k_c = ak_ref[pl.ds(start, t), :]
            scores = [_dot_nt(k_c, _head(aq_ref, h)) for h in range(HEADS)]
            sel = select(score_ref[c], c * t + k_off)
            _online_updates(scores, sel, [avt_ref[c]] * HEADS, m_ref, l_ref, acc_ref)
            return carry
        lax.fori_loop(0, n_chunks, body, 0)

    @pl.when(excess > 0)
    def _():
        attend(lambda sc, k_pos: (sc > thr) | ((sc == thr) & (k_pos <= jlim)))

    @pl.when(excess <= 0)
    def _():
        attend(lambda sc, k_pos: sc >= thr_all)
    for h in range(HEADS):
        o_ref[:, h * HEAD_DIM:(h + 1) * HEAD_DIM] = (acc_ref[h] / l_ref[h]).T.astype(o_ref.dtype)


def _dsa(p128, p64, vt, iwt, n_keep):
    b, s, _ = p128.shape
    t = min(ATT_T, s)
    tq = min(DSA_TQ, s)
    nc = s // t
    return pl.pallas_call(
        functools.partial(_dsa_kernel, n_keep=n_keep),
        grid=(b, s // tq),
        in_specs=[
            pl.BlockSpec((None, tq, MIX_W), lambda bb, i: (bb, i, 0)),
            pl.BlockSpec((None, s, HEAD_DIM), lambda bb, i: (bb, 0, 20)),
            pl.BlockSpec((None, nc, HEAD_DIM, t), lambda bb, i: (bb, 0, 12, 0)),
            pl.BlockSpec((None, tq, 1024), lambda bb, i: (bb, i, 0)),
            pl.BlockSpec((None, s, LANES), lambda bb, i: (bb, 0, 16)),
            pl.BlockSpec((None, tq // t, IDX_HEADS, t), lambda bb, i: (bb, i, 0, 0)),
        ],
        out_specs=pl.BlockSpec((None, tq, MIX_W), lambda bb, i: (bb, i, 0)),
        out_shape=jax.ShapeDtypeStruct((b, s, MIX_W), BF16),
        scratch_shapes=[pltpu.VMEM((nc, t, tq), F32)] + _softmax_scratch(HEADS, tq)
        + [pltpu.VMEM((1, tq), I32)],
        compiler_params=_params(("arbitrary", "arbitrary")),
    )(p128, p128, vt, p64, p64, iwt)


def _diff_kernel(q_ref, k_ref, vt_ref, lam_ref, g_ref, o_ref, m_ref, l_ref, acc_ref, *, lam_init):
    tq = q_ref.shape[0]
    t = vt_ref.shape[2]
    per_tile = tq // t
    i = pl.program_id(1)
    lane = lax.broadcasted_iota(I32, (1, LANES), 1)
    _init_softmax(m_ref, l_ref, acc_ref)

    def step(c, mask):
        k_c = k_ref[pl.ds(pl.multiple_of(c * t, t), t), :]
        v_c = vt_ref[c]
        scores, values = [], []
        for h in range(HEADS):
            q_h = _head(q_ref, h)
            k_h = _head(k_c, h)
            q_1 = jnp.where(lane < DIFF_DIM, q_h, jnp.zeros_like(q_h))
            q_2 = jnp.where(lane >= DIFF_DIM, q_h, jnp.zeros_like(q_h))
            scores += [_dot_nt(k_h, q_1), _dot_nt(k_h, q_2)]
            values += [v_c[h * HEAD_DIM:(h + 1) * HEAD_DIM, :]] * 2
        _online_updates(scores, mask, values, m_ref, l_ref, acc_ref)

    def full_step(c, carry):
        step(c, None)
        return carry

    lax.fori_loop(0, i * per_tile, full_step, 0)
    for d in range(per_tile):
        step(i * per_tile + d, _causal_mask(t, tq, d * t))

    lam_p = lam_ref[...]
    lam = (jnp.exp(jnp.sum(lam_p[0:1] * lam_p[1:2], axis=1, keepdims=True))
           - jnp.exp(jnp.sum(lam_p[2:3] * lam_p[3:4], axis=1, keepdims=True)) + lam_init)
    for h in range(HEADS):
        o = acc_ref[2 * h] / l_ref[2 * h] - lam * (acc_ref[2 * h + 1] / l_ref[2 * h + 1])
        o = o * lax.rsqrt(jnp.mean(o * o, axis=0, keepdims=True) + LN_EPS) * g_ref[...]
        o_ref[:, h * HEAD_DIM:(h + 1) * HEAD_DIM] = (o * (1.0 - lam_init)).T.astype(o_ref.dtype)


def _diff(p64, vt, lam_p, g_col, lam_init):
    b, s, _ = p64.shape
    t = min(ATT_T, s)
    tq = min(DIFF_TQ, s)
    nc = s // t
    return pl.pallas_call(
        functools.partial(_diff_kernel, lam_init=lam_init),
        grid=(b, s // tq),
        in_specs=[
            pl.BlockSpec((None, tq, MIX_W), lambda bb, i: (bb, i, 2)),
            pl.BlockSpec((None, s, MIX_W), lambda bb, i: (bb, 0, 3)),
            pl.BlockSpec((None, nc, MIX_W, t), lambda bb, i: (bb, 0, 0, 0)),
            pl.BlockSpec((4, DIFF_DIM), lambda bb, i: (0, 0)),
            pl.BlockSpec((HEAD_DIM, 1), lambda bb, i: (0, 0)),
        ],
        out_specs=pl.BlockSpec((None, tq, MIX_W), lambda bb, i: (bb, i, 0)),
        out_shape=jax.ShapeDtypeStruct((b, s, MIX_W), BF16),
        scratch_shapes=_softmax_scratch(2 * HEADS, tq),
        compiler_params=_params(("arbitrary", "arbitrary")),
    )(p64, p64, vt, lam_p, g_col)


def _kmean_kernel(k_ref, o_ref):
    n_blk = k_ref.shape[0] // MOBA_BLOCK
    o_ref[...] = jnp.zeros(o_ref.shape, o_ref.dtype)
    for n in range(n_blk):
        blk = k_ref[n * MOBA_BLOCK:(n + 1) * MOBA_BLOCK, :].astype(F32)
        o_ref[n:n + 1, :] = jnp.mean(blk, axis=0, keepdims=True)


def _kmean(p128):
    b, s, _ = p128.shape
    rows = max(SUBLANES, s // MOBA_BLOCK)
    return pl.pallas_call(
        _kmean_kernel,
        grid=(b,),
        in_specs=[pl.BlockSpec((None, s, MIX_W), lambda bb: (bb, 0, 2))],
        out_specs=pl.BlockSpec((None, rows, MIX_W), lambda bb: (bb, 0, 0)),
        out_shape=jax.ShapeDtypeStruct((b, rows, MIX_W), F32),
        compiler_params=_params(("arbitrary",)),
    )(p128)


def _moba_kernel(q_ref, k_ref, vt_ref, km_ref, o_ref, sel_ref, m_ref, l_ref, acc_ref):
    t = q_ref.shape[0]
    n_blk = km_ref.shape[0]
    i = pl.program_id(1)
    blk = lax.broadcasted_iota(I32, (n_blk, 1), 0)
    _init_softmax(m_ref, l_ref, acc_ref)

    for h in range(HEADS):
        gate = _dot_nt(_head(km_ref, h).astype(BF16), _head(q_ref, h))
        gate = jnp.where(blk < i, gate, -jnp.inf)
        sel = jnp.zeros(gate.shape, F32)
        for _ in range(MOBA_TOPK):
            best = jnp.max(gate, axis=0, keepdims=True)
            first = jnp.min(jnp.where(gate == best, blk, n_blk), axis=0, keepdims=True)
            pick = (blk == first) & (best > -jnp.inf)
            sel = jnp.where(pick, 1.0, sel)
            gate = jnp.where(pick, -jnp.inf, gate)
        sel_ref[h] = sel

    def step(c, diagonal):
        k_c = k_ref[pl.ds(pl.multiple_of(c * t, t), t), :]
        v_c = vt_ref[c]
        scores = [_dot_nt(_head(k_c, h), _head(q_ref, h)) for h in range(HEADS)]
        values = [v_c[h * HEAD_DIM:(h + 1) * HEAD_DIM, :] for h in range(HEADS)]
        if diagonal:
            masks = _causal_t(t)
        else:
            masks = [sel_ref[h, pl.ds(c, 1), :] > 0.0 for h in range(HEADS)]
        _online_updates(scores, masks, values, m_ref, l_ref, acc_ref)

    def past_step(c, carry):
        step(c, False)
        return carry

    lax.fori_loop(0, i, past_step, 0)
    step(i, True)
    for h in range(HEADS):
        o_ref[:, h * HEAD_DIM:(h + 1) * HEAD_DIM] = (acc_ref[h] / l_ref[h]).T.astype(o_ref.dtype)


def _moba(p128, vt, kmean):
    b, s, _ = p128.shape
    t = MOBA_BLOCK
    nc = s // t
    rows = kmean.shape[1]
    return pl.pallas_call(
        _moba_kernel,
        grid=(b, nc),
        in_specs=[
            pl.BlockSpec((None, t, MIX_W), lambda bb, i: (bb, i, 1)),
            pl.BlockSpec((None, s, MIX_W), lambda bb, i: (bb, 0, 2)),
            pl.BlockSpec((None, nc, MIX_W, t), lambda bb, i: (bb, 0, 1, 0)),
            pl.BlockSpec((None, rows, MIX_W), lambda bb, i: (bb, 0, 0)),
        ],
        out_specs=pl.BlockSpec((None, t, MIX_W), lambda bb, i: (bb, i, 0)),
        out_shape=jax.ShapeDtypeStruct((b, s, MIX_W), BF16),
        scratch_shapes=[pltpu.VMEM((HEADS, rows, t), F32)] + _softmax_scratch(HEADS, t),
        compiler_params=_params(("arbitrary", "arbitrary")),
    )(p128, p128, vt, kmean)


def _dilated_kernel(q_ref, k_ref, vt_ref, o_ref, m_ref, l_ref, acc_ref):
    t = q_ref.shape[0]
    i = pl.program_id(1)
    _init_softmax(m_ref, l_ref, acc_ref)
    max_window = max(w for w, _ in DILATED_PATTERNS)
    first = jnp.maximum(i - max_window // t, 0)
    rel = lax.broadcasted_iota(I32, (t, t), 1) - lax.broadcasted_iota(I32, (t, t), 0)

    def step(c, patterns):
        k_c = k_ref[pl.ds(pl.multiple_of(c * t, t), t), :]
        v_c = vt_ref[c]
        dist = rel + (i - c) * t
        mult = jnp.zeros((t, t), F32)
        for window, dilation in patterns:
            hit = (dist >= 0) & (dist <= window) & ((dist & (dilation - 1)) == 0)
            mult = mult + jnp.where(hit, 1.0, 0.0)
        mask = mult > 0.0
        scores = [_dot_nt(_head(k_c, h), _head(q_ref, h)) for h in range(HEADS)]
        values = [v_c[h * HEAD_DIM:(h + 1) * HEAD_DIM, :] for h in range(HEADS)]
        _online_updates(scores, mask, values, m_ref, l_ref, acc_ref, weight=mult)

    widest = max(DILATED_PATTERNS)
    others = [p for p in DILATED_PATTERNS if p != widest]
    n_near = (max(w for w, _ in others) + t - 1) // t + 1
    near_first = jnp.maximum(i + 1 - n_near, first)

    def far_step(c, carry):
        step(c, [widest])
        return carry

    def near_step(c, carry):
        step(c, DILATED_PATTERNS)
        return carry

    lax.fori_loop(first, near_first, far_step, 0)
    lax.fori_loop(near_first, i + 1, near_step, 0)
    for h in range(HEADS):
        o_ref[:, h * HEAD_DIM:(h + 1) * HEAD_DIM] = (acc_ref[h] / l_ref[h]).T.astype(o_ref.dtype)


def _dilated(p128, vt):
    b, s, _ = p128.shape
    t = min(ATT_T, s)
    nc = s // t
    return pl.pallas_call(
        _dilated_kernel,
        grid=(b, nc),
        in_specs=[
            pl.BlockSpec((None, t, MIX_W), lambda bb, i: (bb, i, 3)),
            pl.BlockSpec((None, s, MIX_W), lambda bb, i: (bb, 0, 4)),
            pl.BlockSpec((None, nc, MIX_W, t), lambda bb, i: (bb, 0, 2, 0)),
        ],
        out_specs=pl.BlockSpec((None, t, MIX_W), lambda bb, i: (bb, i, 0)),
        out_shape=jax.ShapeDtypeStruct((b, s, MIX_W), BF16),
        scratch_shapes=_softmax_scratch(HEADS, t),
        compiler_params=_params(("arbitrary", "arbitrary")),
    )(p128, p128, vt)


def _layer_norm(z, g, b):
    mu = jnp.mean(z, axis=1, keepdims=True)
    zc = z - mu
    var = jnp.mean(zc * zc, axis=1, keepdims=True)
    return zc * lax.rsqrt(var + LN_EPS) * g + b


def _outproj_ln_kernel(oa_ref, ob_ref, oc_ref, od_ref, w_ref, x_ref, g_ref, b_ref, xo_ref, xb_ref):
    y = jnp.dot(oa_ref[...], w_ref[0:MIX_W, :], preferred_element_type=F32)
    y = y + jnp.dot(ob_ref[...], w_ref[MIX_W:2 * MIX_W, :], preferred_element_type=F32)
    y = y + jnp.dot(oc_ref[...], w_ref[2 * MIX_W:3 * MIX_W, :], preferred_element_type=F32)
    y = y + jnp.dot(od_ref[...], w_ref[3 * MIX_W:4 * MIX_W, :], preferred_element_type=F32)
    out = _layer_norm(ALPHA * x_ref[...] + y, g_ref[...], b_ref[...])
    xo_ref[...] = out
    xb_ref[...] = out.astype(BF16)


def _outproj_ln(oa, ob, oc, od, w_bf, x, g, b):
    t, d = x.shape
    tm = min(OUT_TM, t)
    mix_spec = pl.BlockSpec((tm, MIX_W), lambda i: (i, 0))
    row_spec = pl.BlockSpec((tm, d), lambda i: (i, 0))
    vec_spec = pl.BlockSpec((1, d), lambda i: (0, 0))
    return pl.pallas_call(
        _outproj_ln_kernel,
        grid=(t // tm,),
        in_specs=[mix_spec, mix_spec, mix_spec, mix_spec,
                  pl.BlockSpec((d, d), lambda i: (0, 0)), row_spec, vec_spec, vec_spec],
        out_specs=[row_spec, row_spec],
        out_shape=[jax.ShapeDtypeStruct((t, d), F32), jax.ShapeDtypeStruct((t, d), BF16)],
        compiler_params=_params(("arbitrary",)),
    )(oa, ob, oc, od, w_bf, x, g, b)


def _first_argmax(vals):
    best, arg = vals[0], jnp.zeros(vals[0].shape, I32)
    for j in range(1, len(vals)):
        better = vals[j] > best
        arg = jnp.where(better, j, arg)
        best = jnp.where(better, vals[j], best)
    return arg, best


def _pick(rows, idx):
    out = rows[0]
    for j in range(1, len(rows)):
        out = jnp.where(idx == j, rows[j], out)
    return out


def _router_kernel(x_ref, rwt_ref, bias_ref, e_ref, gate_ref, rank_ref, cnt_ref, carry_ref):
    tm = x_ref.shape[0]

    @pl.when(pl.program_id(0) == 0)
    def _():
        carry_ref[...] = jnp.zeros(carry_ref.shape, F32)

    logits = lax.dot_general(rwt_ref[...], x_ref[...], (((1,), (1,)), ((), ())),
                             precision=lax.Precision.HIGHEST, preferred_element_type=F32)
    aff = 1.0 / (1.0 + jnp.exp(-logits))
    biased = aff + bias_ref[...]
    sel_rows = [biased[r:r + 1, :] for r in range(N_EXPERTS)]
    aff_rows = [aff[r:r + 1, :] for r in range(N_EXPERTS)]

    group_scores = []
    for g in range(N_GROUPS):
        r = sel_rows[g * EXPERTS_PER_GROUP:(g + 1) * EXPERTS_PER_GROUP]
        best_pair = r[0] + r[1]
        for a in range(EXPERTS_PER_GROUP):
            for c in range(a + 1, EXPERTS_PER_GROUP):
                if (a, c) != (0, 1):
                    best_pair = jnp.maximum(best_pair, r[a] + r[c])
        group_scores.append(best_pair)
    grp, _ = _first_argmax(group_scores)

    in_sel = [_pick([sel_rows[g * EXPERTS_PER_GROUP + j] for g in range(N_GROUPS)], grp)
              for j in range(EXPERTS_PER_GROUP)]
    in_aff = [_pick([aff_rows[g * EXPERTS_PER_GROUP + j] for g in range(N_GROUPS)], grp)
              for j in range(EXPERTS_PER_GROUP)]
    first, _ = _first_argmax(in_sel)
    second, _ = _first_argmax([jnp.where(first == j, -jnp.inf, in_sel[j]) for j in range(EXPERTS_PER_GROUP)])
    a0 = _pick(in_aff, first)
    a1 = _pick(in_aff, second)
    e0 = grp * EXPERTS_PER_GROUP + first
    e1 = grp * EXPERTS_PER_GROUP + second
    e_ref[0:1, :] = e0
    e_ref[1:2, :] = e1
    gate_ref[0:1, :] = a0 / (a0 + a1)
    gate_ref[1:2, :] = a1 / (a0 + a1)

    e_iota = lax.broadcasted_iota(I32, (N_EXPERTS, tm), 0)
    one_hot = ((e_iota == e0) | (e_iota == e1)).astype(BF16)
    before = (lax.broadcasted_iota(I32, (tm, tm), 0) < lax.broadcasted_iota(I32, (tm, tm), 1)).astype(BF16)
    prior = jnp.dot(one_hot, before, preferred_element_type=F32) + carry_ref[...]
    rank_ref[0:1, :] = jnp.sum(jnp.where(e_iota == e0, prior, 0.0), axis=0, keepdims=True).astype(I32)
    rank_ref[1:2, :] = jnp.sum(jnp.where(e_iota == e1, prior, 0.0), axis=0, keepdims=True).astype(I32)
    carry_ref[...] = carry_ref[...] + jnp.sum(one_hot.astype(F32), axis=1, keepdims=True)
    cnt_ref[...] = jnp.broadcast_to(carry_ref[...], cnt_ref.shape).astype(I32)


def _router(x, rw_t, bias):
    t, d = x.shape
    tm = min(ROUTER_TM, t)
    tok_spec = pl.BlockSpec((2, tm), lambda i: (0, i))
    return pl.pallas_call(
        _router_kernel,
        grid=(t // tm,),
        in_specs=[pl.BlockSpec((tm, d), lambda i: (i, 0)),
                  pl.BlockSpec((N_EXPERTS, d), lambda i: (0, 0)),
                  pl.BlockSpec((N_EXPERTS, 1), lambda i: (0, 0))],
        out_specs=[tok_spec, tok_spec, tok_spec, pl.BlockSpec((N_EXPERTS, LANES), lambda i: (0, 0))],
        out_shape=[jax.ShapeDtypeStruct((2, t), I32), jax.ShapeDtypeStruct((2, t), F32),
                   jax.ShapeDtypeStruct((2, t), I32), jax.ShapeDtypeStruct((N_EXPERTS, LANES), I32)],
        scratch_shapes=[pltpu.VMEM((N_EXPERTS, 1), F32)],
        compiler_params=_params(("arbitrary",)),
    )(x, rw_t, bias)


def _row_copy(src, src_row, dst, dst_row, sem):
    return pltpu.make_async_copy(src.at[pl.ds(src_row, 1)], dst.at[pl.ds(dst_row, 1)], sem)


def _dispatch_kernel(dest_ref, x_ref, slots_in_hbm, slots_hbm, sem, *, tm, n_tok):
    del slots_in_hbm
    base = pl.program_id(0) * tm

    def issue(r, carry):
        tok = base + r
        _row_copy(x_ref, r, slots_hbm, dest_ref[tok], sem).start()
        _row_copy(x_ref, r, slots_hbm, dest_ref[n_tok + tok], sem).start()
        return carry

    def drain(r, carry):
        _row_copy(x_ref, 0, slots_hbm, 0, sem).wait()
        _row_copy(x_ref, 0, slots_hbm, 0, sem).wait()
        return carry

    lax.fori_loop(0, tm, issue, 0, unroll=DMA_UNROLL)
    lax.fori_loop(0, tm, drain, 0, unroll=DMA_UNROLL)


def _dispatch(dest_flat, x, n_slots):
    t, d = x.shape
    tm = min(DISPATCH_TM, t)
    return pl.pallas_call(
        functools.partial(_dispatch_kernel, tm=tm, n_tok=t),
        grid_spec=pltpu.PrefetchScalarGridSpec(
            num_scalar_prefetch=1,
            grid=(t // tm,),
            in_specs=[pl.BlockSpec((tm, d), lambda i, dest: (i, 0)), pl.BlockSpec(memory_space=pl.ANY)],
            out_specs=pl.BlockSpec(memory_space=pl.ANY),
            scratch_shapes=[pltpu.SemaphoreType.DMA(())],
        ),
        out_shape=jax.ShapeDtypeStruct((n_slots, d), x.dtype),
        input_output_aliases={2: 0},
        compiler_params=_params(("arbitrary",)),
    )(dest_flat, x, jnp.zeros((n_slots, d), x.dtype))


def _expert_kernel(be_ref, na_ref, xs_ref, wg_ref, wu_ref, wd_ref, ys_ref):
    del be_ref

    @pl.when(pl.program_id(0) < na_ref[0])
    def _():
        xb = xs_ref[...].astype(BF16)
        gate = jnp.dot(xb, wg_ref[...], preferred_element_type=F32)
        up = jnp.dot(xb, wu_ref[...], preferred_element_type=F32)
        hidden = gate * (1.0 / (1.0 + jnp.exp(-gate))) * up
        ys_ref[...] = jnp.dot(hidden.astype(BF16), wd_ref[...], preferred_element_type=F32)

    @pl.when(pl.program_id(0) >= na_ref[0])
    def _():
        ys_ref[...] = jnp.zeros(ys_ref.shape, F32)


def _experts(block_expert, n_active, xs, wg, wu, wd, layer):
    n_slots, d = xs.shape
    tb = EXPERT_TB
    f = wg.shape[3]

    def last_active(n, na):
        return jnp.maximum(jnp.minimum(n, na[0] - 1), 0)

    def row_map(n, be, na):
        return (last_active(n, na), 0)

    def w_map(n, be, na):
        return (layer, be[last_active(n, na)], 0, 0)

    return pl.pallas_call(
        _expert_kernel,
        grid_spec=pltpu.PrefetchScalarGridSpec(
            num_scalar_prefetch=2,
            grid=(n_slots // tb,),
            in_specs=[pl.BlockSpec((tb, d), row_map),
                      pl.BlockSpec((None, None, d, f), w_map),
                      pl.BlockSpec((None, None, d, f), w_map),
                      pl.BlockSpec((None, None, f, d), w_map)],
            out_specs=pl.BlockSpec((tb, d), lambda n, be, na: (n, 0)),
        ),
        out_shape=jax.ShapeDtypeStruct((n_slots, d), F32),
        compiler_params=_params(("arbitrary",)),
    )(block_expert, n_active, xs, wg, wu, wd)


def _combine_ln_kernel(dest_ref, x_ref, gate_ref, g_ref, b_ref, ys_hbm, xo_ref, xb_ref, buf_ref, sem,
                       *, n_tok):
    tm = x_ref.shape[0]
    base = pl.program_id(0) * tm

    def issue(r, carry):
        tok = base + r
        _row_copy(ys_hbm, dest_ref[tok], buf_ref.at[0], r, sem).start()
        _row_copy(ys_hbm, dest_ref[n_tok + tok], buf_ref.at[1], r, sem).start()
        return carry

    def drain(r, carry):
        _row_copy(ys_hbm, 0, buf_ref.at[0], 0, sem).wait()
        _row_copy(ys_hbm, 0, buf_ref.at[1], 0, sem).wait()
        return carry

    lax.fori_loop(0, tm, issue, 0, unroll=DMA_UNROLL)
    lax.fori_loop(0, tm, drain, 0, unroll=DMA_UNROLL)
    gate = gate_ref[...]
    moe = buf_ref[0] * gate[:, 0:1] + buf_ref[1] * gate[:, 1:2]
    out = _layer_norm(ALPHA * x_ref[...] + moe, g_ref[...], b_ref[...])
    xo_ref[...] = out
    xb_ref[...] = out.astype(BF16)


def _combine_ln(dest_flat, x, gate_t, g, b, ys):
    t, d = x.shape
    tm = min(COMBINE_TM, t)
    row_spec = pl.BlockSpec((tm, d), lambda i, dest: (i, 0))
    vec_spec = pl.BlockSpec((1, d), lambda i, dest: (0, 0))
    return pl.pallas_call(
        functools.partial(_combine_ln_kernel, n_tok=t),
        grid_spec=pltpu.PrefetchScalarGridSpec(
            num_scalar_prefetch=1,
            grid=(t // tm,),
            in_specs=[row_spec, pl.BlockSpec((tm, 2), lambda i, dest: (i, 0)), vec_spec, vec_spec,
                      pl.BlockSpec(memory_space=pl.ANY)],
            out_specs=[row_spec, row_spec],
            scratch_shapes=[pltpu.VMEM((2, tm, d), F32), pltpu.SemaphoreType.DMA(())],
        ),
        out_shape=[jax.ShapeDtypeStruct((t, d), F32), jax.ShapeDtypeStruct((t, d), BF16)],
        compiler_params=_params(("arbitrary",)),
    )(dest_flat, x, gate_t, g, b, ys)


def _rope_tables(positions):
    pos = positions.reshape(-1).astype(F32)[:, None]

    def cos_sin(d):
        inv = jnp.power(ROPE_THETA, -jnp.arange(0, d, 2, dtype=F32) / d)
        ang = pos * inv
        return jnp.cos(ang), jnp.sin(ang)

    c, s = cos_sin(HEAD_DIM)
    t128 = (jnp.concatenate([c, c], axis=1), jnp.concatenate([-s, s], axis=1))
    c, s = cos_sin(IDX_DIM)
    z = jnp.zeros_like(s)
    t64 = (jnp.concatenate([c, c, c, c], axis=1),
           jnp.concatenate([-s, z, -s, z], axis=1),
           jnp.concatenate([z, s, z, s], axis=1))
    return t128, t64


def _col_scales():
    q128 = HEAD_DIM ** -0.5 * LOG2E
    s128 = np.ones((1, 5 * MIX_W + HEAD_DIM), np.float32)
    s128[:, 0:2 * MIX_W] = q128
    s128[:, 3 * MIX_W:4 * MIX_W] = q128
    s64 = np.ones((1, 2304), np.float32)
    s64[:, 1024:1024 + MIX_W] = DIFF_DIM ** -0.5 * LOG2E
    return jnp.asarray(s128), jnp.asarray(s64)


def _split_w_in(w):
    def cols(name):
        lo, hi = _OFF[name]
        return w[:, lo:hi]

    d = w.shape[0]
    w128 = jnp.concatenate([cols(n) for n in ("a_q", "c_q", "c_k", "d_q", "d_k", "a_k")], axis=1)
    w64 = jnp.concatenate([cols("i_q"), cols("b_q"), cols("b_k"), cols("i_k"), cols("i_k"),
                           jnp.zeros((d, LANES), w.dtype)], axis=1)
    wt = jnp.concatenate([cols("b_v"), cols("c_v"), cols("d_v"), cols("a_v"), cols("i_w")], axis=1).T
    return w128.astype(BF16), w64.astype(BF16), wt.astype(BF16)


def _moe_layout(e, rank, counts, tb):
    t = e.shape[1]
    n_slots = 2 * t + N_EXPERTS * tb
    padded = (counts + tb - 1) // tb * tb
    pend = jnp.cumsum(padded)
    pstart = pend - padded
    experts = jnp.arange(N_EXPERTS, dtype=I32)[:, None, None]
    seg_start = jnp.sum(jnp.where(e[None] == experts, pstart[:, None, None], 0), axis=0)
    dest = (seg_start + rank).astype(I32).reshape(-1)
    blocks = jnp.arange(n_slots // tb, dtype=I32) * tb
    block_expert = jnp.minimum(jnp.sum(blocks[:, None] >= pend[None, :], axis=1), N_EXPERTS - 1).astype(I32)
    n_active = (pend[-1:] // tb).astype(I32)
    return dest, block_expert, n_active, n_slots


def kernel(x, positions, w_in, w_out, diff_lambda, diff_norm_g, ln_mix_g, ln_mix_b, router_w, router_bias,
           w_gate, w_up, w_down, ln_ffn_g, ln_ffn_b):
    b, s, d = x.shape
    t = b * s
    n_keep = min(DSA_TOPK_MAX, s // 4)
    t128, t64 = _rope_tables(positions)
    s128, s64 = _col_scales()
    rw_t = router_w.T
    bias = router_bias.reshape(N_EXPERTS, 1).astype(F32)
    xf = x.reshape(t, d)
    xb = xf.astype(BF16)
    nc = s // min(ATT_T, s)
    wg_bf, wu_bf, wd_bf = w_gate.astype(BF16), w_up.astype(BF16), w_down.astype(BF16)
    for layer in range(DEPTH):
        w128, w64, wt = _split_w_in(w_in[layer])
        p128 = _proj(xb, w128, s128, t128, "rope128", 896).reshape(b, s, -1)
        p64 = _proj(xb, w64, s64, t64, "rope64", 768).reshape(b, s, -1)
        vt, iwt = _proj_t(xb, wt)
        vt = vt.reshape(b, nc, V_ROWS, -1)
        iwt = iwt.reshape(b, nc, IDX_HEADS, -1)

        lam_init = 0.8 - 0.6 * math.exp(-0.3 * layer)
        o_a = _dsa(p128, p64, vt, iwt, n_keep)
        o_b = _diff(p64, vt, diff_lambda[layer], diff_norm_g[layer].reshape(HEAD_DIM, 1), lam_init)
        o_c = _moba(p128, vt, _kmean(p128))
        o_d = _dilated(p128, vt)
        xf, xb = _outproj_ln(o_a.reshape(t, MIX_W), o_b.reshape(t, MIX_W), o_c.reshape(t, MIX_W),
                             o_d.reshape(t, MIX_W), w_out[layer].astype(BF16), xf,
                             ln_mix_g[layer].reshape(1, d), ln_mix_b[layer].reshape(1, d))

        e, gate, rank, counts = _router(xf, rw_t, bias)
        dest, block_expert, n_active, n_slots = _moe_layout(e, rank, counts[:, 0], EXPERT_TB)
        xs = _dispatch(dest, xf, n_slots)
        ys = _experts(block_expert, n_active, xs, wg_bf, wu_bf, wd_bf, layer)
        xf, xb = _combine_ln(dest, xf, gate.T, ln_ffn_g[layer].reshape(1, d), ln_ffn_b[layer].reshape(1, d), ys)
    return xf.reshape(b, s, d)
```

```python
import functools
import math

import numpy as np
import jax
import jax.numpy as jnp
from jax import lax
from jax.experimental import pallas as pl
from jax.experimental.pallas import tpu as pltpu

F32 = jnp.float32
BF16 = jnp.bfloat16
I32 = jnp.int32

D_MODEL = 2048
HEAD_DIM = 128
HEADS = 4
MIX_W = HEADS * HEAD_DIM
IDX_HEADS = 16
IDX_DIM = 64
DSA_TOPK_MAX = 256
DIFF_DIM = 64
MOBA_BLOCK = 256
MOBA_TOPK = 3
DILATED_PATTERNS = ((128, 1), (512, 4), (2048, 16))
ROPE_THETA = 10000.0
N_EXPERTS = 16
N_GROUPS = 4
EXPERTS_PER_GROUP = 4
LN_EPS = 1e-5
DEPTH = 2
ALPHA = (2 * DEPTH) ** 0.25
LOG2E = math.log2(math.e)

LANES = 128
SUBLANES = 8
NEG = -1e30
M_FLOOR = -1e29
INT_MIN = -(2 ** 31)
NEG_INF_ORDER = 0x007FFFFF
VMEM_LIMIT = 56 * 1024 * 1024

PROJ_TM = 512
ATT_T = 256
DIFF_TQ = 512
DSA_TQ = 512
OUT_TM = 256
ROUTER_TM = 256
DISPATCH_TM = 256
EXPERT_TB = 256
COMBINE_TM = 256
DMA_UNROLL = 8
V_ROWS = 3 * MIX_W + HEAD_DIM

_OFF = {}
_o = 0
for _name, _n in (("a_q", 512), ("a_k", 128), ("a_v", 128), ("i_q", 1024), ("i_k", 64), ("i_w", 16),
                  ("b_q", 512), ("b_k", 512), ("b_v", 512), ("c_q", 512), ("c_k", 512), ("c_v", 512),
                  ("d_q", 512), ("d_k", 512), ("d_v", 512)):
    _OFF[_name] = (_o, _o + _n)
    _o += _n


def _params(sem=None):
    return pltpu.CompilerParams(dimension_semantics=sem, vmem_limit_bytes=VMEM_LIMIT)


def _dot_nt(a, b):
    return lax.dot_general(a, b, (((1,), (1,)), ((), ())), preferred_element_type=F32)


def _proj_kernel(x_ref, w_ref, scale_ref, *rest, mode):
    o_ref = rest[-1]
    acc = jnp.dot(x_ref[...], w_ref[...], preferred_element_type=F32)
    n_groups = acc.shape[1] // LANES
    for g in range(n_groups):
        y = acc[:, g * LANES:(g + 1) * LANES]
        if mode == "rope128":
            cos_ref, sin_ref = rest[0], rest[1]
            r = y * cos_ref[...] + pltpu.roll(y, 64, 1) * sin_ref[...]
        else:
            cos_ref, sina_ref, sinb_ref = rest[0], rest[1], rest[2]
            r = (y * cos_ref[...] + pltpu.roll(y, 96, 1) * sina_ref[...]
                 + pltpu.roll(y, 32, 1) * sinb_ref[...])
        r = r * scale_ref[:, g * LANES:(g + 1) * LANES]
        o_ref[:, g * LANES:(g + 1) * LANES] = r.astype(o_ref.dtype)


def _proj(x_bf, w_bf, col_scale, tables, mode, tn):
    t, d = x_bf.shape
    n = w_bf.shape[1]
    tm = min(PROJ_TM, t)
    in_specs = [pl.BlockSpec((tm, d), lambda j, i: (i, 0)),
                pl.BlockSpec((d, tn), lambda j, i: (0, j)),
                pl.BlockSpec((1, tn), lambda j, i: (0, j))]
    in_specs += [pl.BlockSpec((tm, LANES), lambda j, i: (i, 0)) for _ in tables]
    return pl.pallas_call(
        functools.partial(_proj_kernel, mode=mode),
        grid=(n // tn, t // tm),
        in_specs=in_specs,
        out_specs=pl.BlockSpec((tm, tn), lambda j, i: (i, j)),
        out_shape=jax.ShapeDtypeStruct((t, n), BF16),
        compiler_params=_params(("arbitrary", "arbitrary")),
    )(x_bf, w_bf, col_scale, *tables)


def _proj_t_kernel(x_ref, wt_ref, vt_ref, iwt_ref):
    acc = _dot_nt(wt_ref[...], x_ref[...])
    vt_ref[...] = acc[:V_ROWS].astype(vt_ref.dtype)
    iwt_ref[...] = acc[V_ROWS:]


def _proj_t(x_bf, wt_bf):
    t, d = x_bf.shape
    tm = min(ATT_T, t)
    rows = wt_bf.shape[0]
    return pl.pallas_call(
        _proj_t_kernel,
        grid=(t // tm,),
        in_specs=[pl.BlockSpec((tm, d), lambda i: (i, 0)),
                  pl.BlockSpec((rows, d), lambda i: (0, 0))],
        out_specs=[pl.BlockSpec((None, V_ROWS, tm), lambda i: (i, 0, 0)),
                   pl.BlockSpec((None, IDX_HEADS, tm), lambda i: (i, 0, 0))],
        out_shape=[jax.ShapeDtypeStruct((t // tm, V_ROWS, tm), BF16),
                   jax.ShapeDtypeStruct((t // tm, IDX_HEADS, tm), F32)],
        compiler_params=_params(("arbitrary",)),
    )(x_bf, wt_bf)


def _online_updates(scores, mask, values, m_ref, l_ref, acc_ref, weight=None):
    probs, alphas = [], []
    masks = mask if isinstance(mask, list) else [mask] * len(scores)
    for idx, (s_t, mask) in enumerate(zip(scores, masks)):
        if mask is not None:
            s_t = jnp.where(mask, s_t, NEG)
        m_prev = m_ref[idx]
        m_new = jnp.maximum(m_prev, jnp.max(s_t, axis=0, keepdims=True))
        p = jnp.exp2(s_t - m_new)
        if weight is not None:
            p = p * weight
        alpha = jnp.exp2(m_prev - m_new)
        l_ref[idx] = alpha * l_ref[idx] + jnp.sum(p, axis=0, keepdims=True)
        m_ref[idx] = m_new
        probs.append(p.astype(BF16))
        alphas.append(alpha)
    for idx, (p, alpha) in enumerate(zip(probs, alphas)):
        acc_ref[idx] = alpha * acc_ref[idx] + jnp.dot(values[idx], p, preferred_element_type=F32)


def _init_softmax(m_ref, l_ref, acc_ref):
    m_ref[...] = jnp.full(m_ref.shape, M_FLOOR, F32)
    l_ref[...] = jnp.zeros(l_ref.shape, F32)
    acc_ref[...] = jnp.zeros(acc_ref.shape, F32)


def _softmax_scratch(n, t):
    return [pltpu.VMEM((n, 1, t), F32), pltpu.VMEM((n, 1, t), F32), pltpu.VMEM((n, HEAD_DIM, t), F32)]


def _causal_mask(tk, tq, k_start):
    return k_start + lax.broadcasted_iota(I32, (tk, tq), 0) <= lax.broadcasted_iota(I32, (tk, tq), 1)


def _causal_t(t):
    return _causal_mask(t, t, 0)


def _head(ref_or_val, h):
    return ref_or_val[:, h * HEAD_DIM:(h + 1) * HEAD_DIM]


def _order_to_f32(u):
    key = u ^ jnp.int32(INT_MIN)
    bits = jnp.where(key < 0, key ^ jnp.int32(0x7FFFFFFF), key)
    return lax.bitcast_convert_type(bits, F32)


def _dsa_kernel(aq_ref, ak_ref, avt_ref, iq_ref, ik_ref, iwt_ref, o_ref,
                score_ref, m_ref, l_ref, acc_ref, jlim_ref, *, n_keep):
    tq = aq_ref.shape[0]
    t = avt_ref.shape[2]
    i = pl.program_id(1)
    n_chunks = (i + 1) * (tq // t)
    lane = lax.broadcasted_iota(I32, (1, LANES), 1)
    k_off = lax.broadcasted_iota(I32, (t, 1), 0)
    q_pos = i * tq + lax.broadcasted_iota(I32, (1, tq), 1)
    iw_t = jnp.concatenate([iwt_ref[n] for n in range(tq // t)], axis=1)

    def score_chunk(c, carry):
        ik_c = ik_ref[pl.ds(pl.multiple_of(c * t, t), t), :]
        ik_lo = jnp.where(lane < IDX_DIM, ik_c, jnp.zeros_like(ik_c))
        ik_hi = jnp.where(lane >= IDX_DIM, ik_c, jnp.zeros_like(ik_c))
        score = jnp.zeros((t, tq), F32)
        for j in range(IDX_HEADS // 2):
            q_pair = iq_ref[:, j * LANES:(j + 1) * LANES]
            s_even = jnp.maximum(_dot_nt(ik_lo, q_pair), 0.0)
            s_odd = jnp.maximum(_dot_nt(ik_hi, q_pair), 0.0)
            score = score + s_even * iw_t[2 * j:2 * j + 1, :] + s_odd * iw_t[2 * j + 1:2 * j + 2, :]
        score_ref[c] = jnp.where(c * t + k_off <= q_pos, score, -jnp.inf)
        return carry

    lax.fori_loop(0, n_chunks, score_chunk, 0)

    def count(pred):
        def body(c, part):
            hit = jnp.where(pred(score_ref[c], c), 1, 0)
            return part + jnp.sum(hit.reshape(t // SUBLANES, SUBLANES, tq), axis=0)
        part = lax.fori_loop(0, n_chunks, body, jnp.zeros((SUBLANES, tq), I32))
        return jnp.sum(part, axis=0, keepdims=True)

    def bisect(it, thr_u):
        cand_u = thr_u | jnp.left_shift(jnp.int32(1), 31 - it)
        cand = _order_to_f32(cand_u)
        cnt = count(lambda sc, c: sc >= cand)
        return jnp.where(cnt >= n_keep, cand_u, thr_u)

    thr_u = lax.fori_loop(0, 32, bisect, jnp.zeros((1, tq), I32))
    below = (thr_u >= 0) & (thr_u < NEG_INF_ORDER)
    thr = jnp.where(below, -jnp.inf, _order_to_f32(thr_u))

    n_gt = count(lambda sc, c: sc > thr)
    n_eq = count(lambda sc, c: sc == thr)
    need = n_keep - n_gt
    s_total = score_ref.shape[0] * t
    jlim_ref[...] = jnp.full((1, tq), s_total, I32)
    excess = jnp.max(jnp.where((n_eq > need) & (thr > -jnp.inf), 1, 0))

    @pl.when(excess > 0)
    def _():
        n_bits = max(1, (s_total - 1).bit_length())

        def bisect_idx(it, j):
            cand = j | jnp.left_shift(jnp.int32(1), n_bits - 1 - it)
            cnt = count(lambda sc, c: (sc == thr) & (c * t + k_off < cand))
            return jnp.where(cnt < need, cand, j)

        jlim_ref[...] = lax.fori_loop(0, n_bits, bisect_idx, jnp.zeros((1, tq), I32))

    _init_softmax(m_ref, l_ref, acc_ref)
    jlim = jnp.where(thr > -jnp.inf, jlim_ref[...], -1)
    thr_all = jnp.where(thr > -jnp.inf, thr, jnp.finfo(F32).min)

    def attend(select):
        def body(c, carry):
            start = pl.multiple_of(c * t, t)
            k_c = ak_ref[pl.ds(start, t), :]
            scores = [_dot_nt(k_c, _head(aq_ref, h)) for h in range(HEADS)]
            sel = select(score_ref[c], c * t + k_off)
            _online_updates(scores, sel, [avt_ref[c]] * HEADS, m_ref, l_ref, acc_ref)
            return carry
        lax.fori_loop(0, n_chunks, body, 0)

    @pl.when(excess > 0)
    def _():
        attend(lambda sc, k_pos: (sc > thr) | ((sc == thr) & (k_pos <= jlim)))

    @pl.when(excess <= 0)
    def _():
        attend(lambda sc, k_pos: sc >= thr_all)
    for h in range(HEADS):
        o_ref[:, h * HEAD_DIM:(h + 1) * HEAD_DIM] = (acc_ref[h] / l_ref[h]).T.astype(o_ref.dtype)


def _dsa(p128, p64, vt, iwt, n_keep):
    b, s, _ = p128.shape
    t = min(ATT_T, s)
    tq = min(DSA_TQ, s)
    nc = s // t
    return pl.pallas_call(
        functools.partial(_dsa_kernel, n_keep=n_keep),
        grid=(b, s // tq),
        in_specs=[
            pl.BlockSpec((None, tq, MIX_W), lambda bb, i: (bb, i, 0)),
            pl.BlockSpec((None, s, HEAD_DIM), lambda bb, i: (bb, 0, 20)),
            pl.BlockSpec((None, nc, HEAD_DIM, t), lambda bb, i: (bb, 0, 12, 0)),
            pl.BlockSpec((None, tq, 1024), lambda bb, i: (bb, i, 0)),
            pl.BlockSpec((None, s, LANES), lambda bb, i: (bb, 0, 16)),
            pl.BlockSpec((None, tq // t, IDX_HEADS, t), lambda bb, i: (bb, i, 0, 0)),
        ],
        out_specs=pl.BlockSpec((None, tq, MIX_W), lambda bb, i: (bb, i, 0)),
        out_shape=jax.ShapeDtypeStruct((b, s, MIX_W), BF16),
        scratch_shapes=[pltpu.VMEM((nc, t, tq), F32)] + _softmax_scratch(HEADS, tq)
        + [pltpu.VMEM((1, tq), I32)],
        compiler_params=_params(("arbitrary", "arbitrary")),
    )(p128, p128, vt, p64, p64, iwt)


def _diff_kernel(q_ref, k_ref, vt_ref, lam_ref, g_ref, o_ref, m_ref, l_ref, acc_ref, *, lam_init):
    tq = q_ref.shape[0]
    t = vt_ref.shape[2]
    per_tile = tq // t
    i = pl.program_id(1)
    lane = lax.broadcasted_iota(I32, (1, LANES), 1)
    _init_softmax(m_ref, l_ref, acc_ref)

    def step(c, mask):
        k_c = k_ref[pl.ds(pl.multiple_of(c * t, t), t), :]
        v_c = vt_ref[c]
        scores, values = [], []
        for h in range(HEADS):
            q_h = _head(q_ref, h)
            k_h = _head(k_c, h)
            q_1 = jnp.where(lane < DIFF_DIM, q_h, jnp.zeros_like(q_h))
            q_2 = jnp.where(lane >= DIFF_DIM, q_h, jnp.zeros_like(q_h))
            scores += [_dot_nt(k_h, q_1), _dot_nt(k_h, q_2)]
            values += [v_c[h * HEAD_DIM:(h + 1) * HEAD_DIM, :]] * 2
        _online_updates(scores, mask, values, m_ref, l_ref, acc_ref)

    def full_step(c, carry):
        step(c, None)
        return carry

    lax.fori_loop(0, i * per_tile, full_step, 0)
    for d in range(per_tile):
        step(i * per_tile + d, _causal_mask(t, tq, d * t))

    lam_p = lam_ref[...]
    lam = (jnp.exp(jnp.sum(lam_p[0:1] * lam_p[1:2], axis=1, keepdims=True))
           - jnp.exp(jnp.sum(lam_p[2:3] * lam_p[3:4], axis=1, keepdims=True)) + lam_init)
    for h in range(HEADS):
        o = acc_ref[2 * h] / l_ref[2 * h] - lam * (acc_ref[2 * h + 1] / l_ref[2 * h + 1])
        o = o * lax.rsqrt(jnp.mean(o * o, axis=0, keepdims=True) + LN_EPS) * g_ref[...]
        o_ref[:, h * HEAD_DIM:(h + 1) * HEAD_DIM] = (o * (1.0 - lam_init)).T.astype(o_ref.dtype)


def _diff(p64, vt, lam_p, g_col, lam_init):
    b, s, _ = p64.shape
    t = min(ATT_T, s)
    tq = min(DIFF_TQ, s)
    nc = s // t
    return pl.pallas_call(
        functools.partial(_diff_kernel, lam_init=lam_init),
        grid=(b, s // tq),
        in_specs=[
            pl.BlockSpec((None, tq, MIX_W), lambda bb, i: (bb, i, 2)),
            pl.BlockSpec((None, s, MIX_W), lambda bb, i: (bb, 0, 3)),
            pl.BlockSpec((None, nc, MIX_W, t), lambda bb, i: (bb, 0, 0, 0)),
            pl.BlockSpec((4, DIFF_DIM), lambda bb, i: (0, 0)),
            pl.BlockSpec((HEAD_DIM, 1), lambda bb, i: (0, 0)),
        ],
        out_specs=pl.BlockSpec((None, tq, MIX_W), lambda bb, i: (bb, i, 0)),
        out_shape=jax.ShapeDtypeStruct((b, s, MIX_W), BF16),
        scratch_shapes=_softmax_scratch(2 * HEADS, tq),
        compiler_params=_params(("arbitrary", "arbitrary")),
    )(p64, p64, vt, lam_p, g_col)


def _kmean_kernel(k_ref, o_ref):
    n_blk = k_ref.shape[0] // MOBA_BLOCK
    o_ref[...] = jnp.zeros(o_ref.shape, o_ref.dtype)
    for n in range(n_blk):
        blk = k_ref[n * MOBA_BLOCK:(n + 1) * MOBA_BLOCK, :].astype(F32)
        o_ref[n:n + 1, :] = jnp.mean(blk, axis=0, keepdims=True)


def _kmean(p128):
    b, s, _ = p128.shape
    rows = max(SUBLANES, s // MOBA_BLOCK)
    return pl.pallas_call(
        _kmean_kernel,
        grid=(b,),
        in_specs=[pl.BlockSpec((None, s, MIX_W), lambda bb: (bb, 0, 2))],
        out_specs=pl.BlockSpec((None, rows, MIX_W), lambda bb: (bb, 0, 0)),
        out_shape=jax.ShapeDtypeStruct((b, rows, MIX_W), F32),
        compiler_params=_params(("arbitrary",)),
    )(p128)


def _moba_kernel(q_ref, k_ref, vt_ref, km_ref, o_ref, sel_ref, m_ref, l_ref, acc_ref):
    t = q_ref.shape[0]
    n_blk = km_ref.shape[0]
    i = pl.program_id(1)
    blk = lax.broadcasted_iota(I32, (n_blk, 1), 0)
    _init_softmax(m_ref, l_ref, acc_ref)

    for h in range(HEADS):
        gate = _dot_nt(_head(km_ref, h).astype(BF16), _head(q_ref, h))
        gate = jnp.where(blk < i, gate, -jnp.inf)
        sel = jnp.zeros(gate.shape, F32)
        for _ in range(MOBA_TOPK):
            best = jnp.max(gate, axis=0, keepdims=True)
            first = jnp.min(jnp.where(gate == best, blk, n_blk), axis=0, keepdims=True)
            pick = (blk == first) & (best > -jnp.inf)
            sel = jnp.where(pick, 1.0, sel)
            gate = jnp.where(pick, -jnp.inf, gate)
        sel_ref[h] = sel

    def step(c, diagonal):
        k_c = k_ref[pl.ds(pl.multiple_of(c * t, t), t), :]
        v_c = vt_ref[c]
        scores = [_dot_nt(_head(k_c, h), _head(q_ref, h)) for h in range(HEADS)]
        values = [v_c[h * HEAD_DIM:(h + 1) * HEAD_DIM, :] for h in range(HEADS)]
        if diagonal:
            masks = _causal_t(t)
        else:
            masks = [sel_ref[h, pl.ds(c, 1), :] > 0.0 for h in range(HEADS)]
        _online_updates(scores, masks, values, m_ref, l_ref, acc_ref)

    def past_step(c, carry):
        step(c, False)
        return carry

    lax.fori_loop(0, i, past_step, 0)
    step(i, True)
    for h in range(HEADS):
        o_ref[:, h * HEAD_DIM:(h + 1) * HEAD_DIM] = (acc_ref[h] / l_ref[h]).T.astype(o_ref.dtype)


def _moba(p128, vt, kmean):
    b, s, _ = p128.shape
    t = MOBA_BLOCK
    nc = s // t
    rows = kmean.shape[1]
    return pl.pallas_call(
        _moba_kernel,
        grid=(b, nc),
        in_specs=[
            pl.BlockSpec((None, t, MIX_W), lambda bb, i: (bb, i, 1)),
            pl.BlockSpec((None, s, MIX_W), lambda bb, i: (bb, 0, 2)),
            pl.BlockSpec((None, nc, MIX_W, t), lambda bb, i: (bb, 0, 1, 0)),
            pl.BlockSpec((None, rows, MIX_W), lambda bb, i: (bb, 0, 0)),
        ],
        out_specs=pl.BlockSpec((None, t, MIX_W), lambda bb, i: (bb, i, 0)),
        out_shape=jax.ShapeDtypeStruct((b, s, MIX_W), BF16),
        scratch_shapes=[pltpu.VMEM((HEADS, rows, t), F32)] + _softmax_scratch(HEADS, t),
        compiler_params=_params(("arbitrary", "arbitrary")),
    )(p128, p128, vt, kmean)


def _dilated_kernel(q_ref, k_ref, vt_ref, o_ref, m_ref, l_ref, acc_ref):
    t = q_ref.shape[0]
    i = pl.program_id(1)
    _init_softmax(m_ref, l_ref, acc_ref)
    max_window = max(w for w, _ in DILATED_PATTERNS)
    first = jnp.maximum(i - max_window // t, 0)
    rel = lax.broadcasted_iota(I32, (t, t), 1) - lax.broadcasted_iota(I32, (t, t), 0)

    def step(c, patterns):
        k_c = k_ref[pl.ds(pl.multiple_of(c * t, t), t), :]
        v_c = vt_ref[c]
        dist = rel + (i - c) * t
        mult = jnp.zeros((t, t), F32)
        for window, dilation in patterns:
            hit = (dist >= 0) & (dist <= window) & ((dist & (dilation - 1)) == 0)
            mult = mult + jnp.where(hit, 1.0, 0.0)
        mask = mult > 0.0
        scores = [_dot_nt(_head(k_c, h), _head(q_ref, h)) for h in range(HEADS)]
        values = [v_c[h * HEAD_DIM:(h + 1) * HEAD_DIM, :] for h in range(HEADS)]
        _online_updates(scores, mask, values, m_ref, l_ref, acc_ref, weight=mult)

    widest = max(DILATED_PATTERNS)
    others = [p for p in DILATED_PATTERNS if p != widest]
    n_near = (max(w for w, _ in others) + t - 1) // t + 1
    near_first = jnp.maximum(i + 1 - n_near, first)

    def far_step(c, carry):
        step(c, [widest])
        return carry

    def near_step(c, carry):
        step(c, DILATED_PATTERNS)
        return carry

    lax.fori_loop(first, near_first, far_step, 0)
    lax.fori_loop(near_first, i + 1, near_step, 0)
    for h in range(HEADS):
        o_ref[:, h * HEAD_DIM:(h + 1) * HEAD_DIM] = (acc_ref[h] / l_ref[h]).T.astype(o_ref.dtype)


def _dilated(p128, vt):
    b, s, _ = p128.shape
    t = min(ATT_T, s)
    nc = s // t
    return pl.pallas_call(
        _dilated_kernel,
        grid=(b, nc),
        in_specs=[
            pl.BlockSpec((None, t, MIX_W), lambda bb, i: (bb, i, 3)),
            pl.BlockSpec((None, s, MIX_W), lambda bb, i: (bb, 0, 4)),
            pl.BlockSpec((None, nc, MIX_W, t), lambda bb, i: (bb, 0, 2, 0)),
        ],
        out_specs=pl.BlockSpec((None, t, MIX_W), lambda bb, i: (bb, i, 0)),
        out_shape=jax.ShapeDtypeStruct((b, s, MIX_W), BF16),
        scratch_shapes=_softmax_scratch(HEADS, t),
        compiler_params=_params(("arbitrary", "arbitrary")),
    )(p128, p128, vt)


def _layer_norm(z, g, b):
    mu = jnp.mean(z, axis=1, keepdims=True)
    zc = z - mu
    var = jnp.mean(zc * zc, axis=1, keepdims=True)
    return zc * lax.rsqrt(var + LN_EPS) * g + b


def _outproj_ln_kernel(oa_ref, ob_ref, oc_ref, od_ref, w_ref, x_ref, g_ref, b_ref, xo_ref, xb_ref):
    y = jnp.dot(oa_ref[...], w_ref[0:MIX_W, :], preferred_element_type=F32)
    y = y + jnp.dot(ob_ref[...], w_ref[MIX_W:2 * MIX_W, :], preferred_element_type=F32)
    y = y + jnp.dot(oc_ref[...], w_ref[2 * MIX_W:3 * MIX_W, :], preferred_element_type=F32)
    y = y + jnp.dot(od_ref[...], w_ref[3 * MIX_W:4 * MIX_W, :], preferred_element_type=F32)
    out = _layer_norm(ALPHA * x_ref[...] + y, g_ref[...], b_ref[...])
    xo_ref[...] = out
    xb_ref[...] = out.astype(BF16)


def _outproj_ln(oa, ob, oc, od, w_bf, x, g, b):
    t, d = x.shape
    tm = min(OUT_TM, t)
    mix_spec = pl.BlockSpec((tm, MIX_W), lambda i: (i, 0))
    row_spec = pl.BlockSpec((tm, d), lambda i: (i, 0))
    vec_spec = pl.BlockSpec((1, d), lambda i: (0, 0))
    return pl.pallas_call(
        _outproj_ln_kernel,
        grid=(t // tm,),
        in_specs=[mix_spec, mix_spec, mix_spec, mix_spec,
                  pl.BlockSpec((d, d), lambda i: (0, 0)), row_spec, vec_spec, vec_spec],
        out_specs=[row_spec, row_spec],
        out_shape=[jax.ShapeDtypeStruct((t, d), F32), jax.ShapeDtypeStruct((t, d), BF16)],
        compiler_params=_params(("arbitrary",)),
    )(oa, ob, oc, od, w_bf, x, g, b)


def _first_argmax(vals):
    best, arg = vals[0], jnp.zeros(vals[0].shape, I32)
    for j in range(1, len(vals)):
        better = vals[j] > best
        arg = jnp.where(better, j, arg)
        best = jnp.where(better, vals[j], best)
    return arg, best


def _pick(rows, idx):
    out = rows[0]
    for j in range(1, len(rows)):
        out = jnp.where(idx == j, rows[j], out)
    return out


def _router_kernel(x_ref, rwt_ref, bias_ref, e_ref, gate_ref, rank_ref, cnt_ref, carry_ref):
    tm = x_ref.shape[0]

    @pl.when(pl.program_id(0) == 0)
    def _():
        carry_ref[...] = jnp.zeros(carry_ref.shape, F32)

    logits = lax.dot_general(rwt_ref[...], x_ref[...], (((1,), (1,)), ((), ())),
                             precision=lax.Precision.HIGHEST, preferred_element_type=F32)
    aff = 1.0 / (1.0 + jnp.exp(-logits))
    biased = aff + bias_ref[...]
    sel_rows = [biased[r:r + 1, :] for r in range(N_EXPERTS)]
    aff_rows = [aff[r:r + 1, :] for r in range(N_EXPERTS)]

    group_scores = []
    for g in range(N_GROUPS):
        r = sel_rows[g * EXPERTS_PER_GROUP:(g + 1) * EXPERTS_PER_GROUP]
        best_pair = r[0] + r[1]
        for a in range(EXPERTS_PER_GROUP):
            for c in range(a + 1, EXPERTS_PER_GROUP):
                if (a, c) != (0, 1):
                    best_pair = jnp.maximum(best_pair, r[a] + r[c])
        group_scores.append(best_pair)
    grp, _ = _first_argmax(group_scores)

    in_sel = [_pick([sel_rows[g * EXPERTS_PER_GROUP + j] for g in range(N_GROUPS)], grp)
              for j in range(EXPERTS_PER_GROUP)]
    in_aff = [_pick([aff_rows[g * EXPERTS_PER_GROUP + j] for g in range(N_GROUPS)], grp)
              for j in range(EXPERTS_PER_GROUP)]
    first, _ = _first_argmax(in_sel)
    second, _ = _first_argmax([jnp.where(first == j, -jnp.inf, in_sel[j]) for j in range(EXPERTS_PER_GROUP)])
    a0 = _pick(in_aff, first)
    a1 = _pick(in_aff, second)
    e0 = grp * EXPERTS_PER_GROUP + first
    e1 = grp * EXPERTS_PER_GROUP + second
    e_ref[0:1, :] = e0
    e_ref[1:2, :] = e1
    gate_ref[0:1, :] = a0 / (a0 + a1)
    gate_ref[1:2, :] = a1 / (a0 + a1)

    e_iota = lax.broadcasted_iota(I32, (N_EXPERTS, tm), 0)
    one_hot = ((e_iota == e0) | (e_iota == e1)).astype(BF16)
    before = (lax.broadcasted_iota(I32, (tm, tm), 0) < lax.broadcasted_iota(I32, (tm, tm), 1)).astype(BF16)
    prior = jnp.dot(one_hot, before, preferred_element_type=F32) + carry_ref[...]
    rank_ref[0:1, :] = jnp.sum(jnp.where(e_iota == e0, prior, 0.0), axis=0, keepdims=True).astype(I32)
    rank_ref[1:2, :] = jnp.sum(jnp.where(e_iota == e1, prior, 0.0), axis=0, keepdims=True).astype(I32)
    carry_ref[...] = carry_ref[...] + jnp.sum(one_hot.astype(F32), axis=1, keepdims=True)
    cnt_ref[...] = jnp.broadcast_to(carry_ref[...], cnt_ref.shape).astype(I32)


def _router(x, rw_t, bias):
    t, d = x.shape
    tm = min(ROUTER_TM, t)
    tok_spec = pl.BlockSpec((2, tm), lambda i: (0, i))
    return pl.pallas_call(
        _router_kernel,
        grid=(t // tm,),
        in_specs=[pl.BlockSpec((tm, d), lambda i: (i, 0)),
                  pl.BlockSpec((N_EXPERTS, d), lambda i: (0, 0)),
                  pl.BlockSpec((N_EXPERTS, 1), lambda i: (0, 0))],
        out_specs=[tok_spec, tok_spec, tok_spec, pl.BlockSpec((N_EXPERTS, LANES), lambda i: (0, 0))],
        out_shape=[jax.ShapeDtypeStruct((2, t), I32), jax.ShapeDtypeStruct((2, t), F32),
                   jax.ShapeDtypeStruct((2, t), I32), jax.ShapeDtypeStruct((N_EXPERTS, LANES), I32)],
        scratch_shapes=[pltpu.VMEM((N_EXPERTS, 1), F32)],
        compiler_params=_params(("arbitrary",)),
    )(x, rw_t, bias)


def _row_copy(src, src_row, dst, dst_row, sem):
    return pltpu.make_async_copy(src.at[pl.ds(src_row, 1)], dst.at[pl.ds(dst_row, 1)], sem)


def _dispatch_kernel(dest_ref, x_ref, slots_in_hbm, slots_hbm, sem, *, tm, n_tok):
    del slots_in_hbm
    base = pl.program_id(0) * tm

    def issue(r, carry):
        tok = base + r
        _row_copy(x_ref, r, slots_hbm, dest_ref[tok], sem).start()
        _row_copy(x_ref, r, slots_hbm, dest_ref[n_tok + tok], sem).start()
        return carry

    def drain(r, carry):
        _row_copy(x_ref, 0, slots_hbm, 0, sem).wait()
        _row_copy(x_ref, 0, slots_hbm, 0, sem).wait()
        return carry

    lax.fori_loop(0, tm, issue, 0, unroll=DMA_UNROLL)
    lax.fori_loop(0, tm, drain, 0, unroll=DMA_UNROLL)


def _dispatch(dest_flat, x, n_slots):
    t, d = x.shape
    tm = min(DISPATCH_TM, t)
    return pl.pallas_call(
        functools.partial(_dispatch_kernel, tm=tm, n_tok=t),
        grid_spec=pltpu.PrefetchScalarGridSpec(
            num_scalar_prefetch=1,
            grid=(t // tm,),
            in_specs=[pl.BlockSpec((tm, d), lambda i, dest: (i, 0)), pl.BlockSpec(memory_space=pl.ANY)],
            out_specs=pl.BlockSpec(memory_space=pl.ANY),
            scratch_shapes=[pltpu.SemaphoreType.DMA(())],
        ),
        out_shape=jax.ShapeDtypeStruct((n_slots, d), x.dtype),
        input_output_aliases={2: 0},
        compiler_params=_params(("arbitrary",)),
    )(dest_flat, x, jnp.zeros((n_slots, d), x.dtype))


def _last_active(n, na_ref):
    return jnp.maximum(jnp.minimum(n, na_ref[0] - 1), 0)


def _expert_stage(be_ref, na_ref, w_refs, w_bf_refs, compute, out_ref):
    n = pl.program_id(0)
    active = n < na_ref[0]
    new_expert = (n == 0) | (be_ref[n] != be_ref[jnp.maximum(n - 1, 0)])

    @pl.when(active & new_expert)
    def _():
        for w_ref, w_bf_ref in zip(w_refs, w_bf_refs):
            w_bf_ref[...] = w_ref[...].astype(BF16)

    @pl.when(active)
    def _():
        out_ref[...] = compute().astype(out_ref.dtype)

    @pl.when(jnp.logical_not(active))
    def _():
        out_ref[...] = jnp.zeros(out_ref.shape, out_ref.dtype)


def _expert_up_kernel(be_ref, na_ref, xs_ref, wg_ref, wu_ref, h_ref, wg_bf, wu_bf):
    def compute():
        xb = xs_ref[...].astype(BF16)
        gate = jnp.dot(xb, wg_bf[...], preferred_element_type=F32)
        up = jnp.dot(xb, wu_bf[...], preferred_element_type=F32)
        return gate * (1.0 / (1.0 + jnp.exp(-gate))) * up

    _expert_stage(be_ref, na_ref, (wg_ref, wu_ref), (wg_bf, wu_bf), compute, h_ref)


def _expert_down_kernel(be_ref, na_ref, h_ref, wd_ref, ys_ref, wd_bf):
    def compute():
        return jnp.dot(h_ref[...], wd_bf[...], preferred_element_type=F32)

    _expert_stage(be_ref, na_ref, (wd_ref,), (wd_bf,), compute, ys_ref)


def _experts(block_expert, n_active, xs, wg, wu, wd, layer):
    n_slots, d = xs.shape
    tb = EXPERT_TB
    f = wg.shape[3]

    def row_map(n, be, na):
        return (_last_active(n, na), 0)

    def w_map(n, be, na):
        return (layer, be[_last_active(n, na)], 0, 0)

    def out_map(n, be, na):
        return (n, 0)

    hidden = pl.pallas_call(
        _expert_up_kernel,
        grid_spec=pltpu.PrefetchScalarGridSpec(
            num_scalar_prefetch=2,
            grid=(n_slots // tb,),
            in_specs=[pl.BlockSpec((tb, d), row_map),
                      pl.BlockSpec((None, None, d, f), w_map),
                      pl.BlockSpec((None, None, d, f), w_map)],
            out_specs=pl.BlockSpec((tb, f), out_map),
            scratch_shapes=[pltpu.VMEM((d, f), BF16), pltpu.VMEM((d, f), BF16)],
        ),
        out_shape=jax.ShapeDtypeStruct((n_slots, f), BF16),
        compiler_params=_params(("arbitrary",)),
    )(block_expert, n_active, xs, wg, wu)
    return pl.pallas_call(
        _expert_down_kernel,
        grid_spec=pltpu.PrefetchScalarGridSpec(
            num_scalar_prefetch=2,
            grid=(n_slots // tb,),
            in_specs=[pl.BlockSpec((tb, f), row_map),
                      pl.BlockSpec((None, None, f, d), w_map)],
            out_specs=pl.BlockSpec((tb, d), out_map),
            scratch_shapes=[pltpu.VMEM((f, d), BF16)],
        ),
        out_shape=jax.ShapeDtypeStruct((n_slots, d), F32),
        compiler_params=_params(("arbitrary",)),
    )(block_expert, n_active, hidden, wd)


def _combine_ln_kernel(dest_ref, x_ref, gate_ref, g_ref, b_ref, ys_hbm, xo_ref, xb_ref, buf_ref, sem,
                       *, n_tok):
    tm = x_ref.shape[0]
    base = pl.program_id(0) * tm

    def issue(r, carry):
        tok = base + r
        _row_copy(ys_hbm, dest_ref[tok], buf_ref.at[0], r, sem).start()
        _row_copy(ys_hbm, dest_ref[n_tok + tok], buf_ref.at[1], r, sem).start()
        return carry

    def drain(r, carry):
        _row_copy(ys_hbm, 0, buf_ref.at[0], 0, sem).wait()
        _row_copy(ys_hbm, 0, buf_ref.at[1], 0, sem).wait()
        return carry

    lax.fori_loop(0, tm, issue, 0, unroll=DMA_UNROLL)
    lax.fori_loop(0, tm, drain, 0, unroll=DMA_UNROLL)
    gate = gate_ref[...]
    moe = buf_ref[0] * gate[:, 0:1] + buf_ref[1] * gate[:, 1:2]
    out = _layer_norm(ALPHA * x_ref[...] + moe, g_ref[...], b_ref[...])
    xo_ref[...] = out
    xb_ref[...] = out.astype(BF16)


def _combine_ln(dest_flat, x, gate_t, g, b, ys):
    t, d = x.shape
    tm = min(COMBINE_TM, t)
    row_spec = pl.BlockSpec((tm, d), lambda i, dest: (i, 0))
    vec_spec = pl.BlockSpec((1, d), lambda i, dest: (0, 0))
    return pl.pallas_call(
        functools.partial(_combine_ln_kernel, n_tok=t),
        grid_spec=pltpu.PrefetchScalarGridSpec(
            num_scalar_prefetch=1,
            grid=(t // tm,),
            in_specs=[row_spec, pl.BlockSpec((tm, 2), lambda i, dest: (i, 0)), vec_spec, vec_spec,
                      pl.BlockSpec(memory_space=pl.ANY)],
            out_specs=[row_spec, row_spec],
            scratch_shapes=[pltpu.VMEM((2, tm, d), F32), pltpu.SemaphoreType.DMA(())],
        ),
        out_shape=[jax.ShapeDtypeStruct((t, d), F32), jax.ShapeDtypeStruct((t, d), BF16)],
        compiler_params=_params(("arbitrary",)),
    )(dest_flat, x, gate_t, g, b, ys)


def _rope_tables(positions):
    pos = positions.reshape(-1).astype(F32)[:, None]

    def cos_sin(d):
        inv = jnp.power(ROPE_THETA, -jnp.arange(0, d, 2, dtype=F32) / d)
        ang = pos * inv
        return jnp.cos(ang), jnp.sin(ang)

    c, s = cos_sin(HEAD_DIM)
    t128 = (jnp.concatenate([c, c], axis=1), jnp.concatenate([-s, s], axis=1))
    c, s = cos_sin(IDX_DIM)
    z = jnp.zeros_like(s)
    t64 = (jnp.concatenate([c, c, c, c], axis=1),
           jnp.concatenate([-s, z, -s, z], axis=1),
           jnp.concatenate([z, s, z, s], axis=1))
    return t128, t64


def _col_scales():
    q128 = HEAD_DIM ** -0.5 * LOG2E
    s128 = np.ones((1, 5 * MIX_W + HEAD_DIM), np.float32)
    s128[:, 0:2 * MIX_W] = q128
    s128[:, 3 * MIX_W:4 * MIX_W] = q128
    s64 = np.ones((1, 2304), np.float32)
    s64[:, 1024:1024 + MIX_W] = DIFF_DIM ** -0.5 * LOG2E
    return jnp.asarray(s128), jnp.asarray(s64)


def _split_w_in(w):
    def cols(name):
        lo, hi = _OFF[name]
        return w[:, lo:hi]

    d = w.shape[0]
    w128 = jnp.concatenate([cols(n) for n in ("a_q", "c_q", "c_k", "d_q", "d_k", "a_k")], axis=1)
    w64 = jnp.concatenate([cols("i_q"), cols("b_q"), cols("b_k"), cols("i_k"), cols("i_k"),
                           jnp.zeros((d, LANES), w.dtype)], axis=1)
    wt = jnp.concatenate([cols("b_v"), cols("c_v"), cols("d_v"), cols("a_v"), cols("i_w")], axis=1).T
    return w128.astype(BF16), w64.astype(BF16), wt.astype(BF16)


def _moe_layout(e, rank, counts, tb):
    t = e.shape[1]
    n_slots = 2 * t + N_EXPERTS * tb
    padded = (counts + tb - 1) // tb * tb
    pend = jnp.cumsum(padded)
    pstart = pend - padded
    experts = jnp.arange(N_EXPERTS, dtype=I32)[:, None, None]
    seg_start = jnp.sum(jnp.where(e[None] == experts, pstart[:, None, None], 0), axis=0)
    dest = (seg_start + rank).astype(I32).reshape(-1)
    blocks = jnp.arange(n_slots // tb, dtype=I32) * tb
    block_expert = jnp.minimum(jnp.sum(blocks[:, None] >= pend[None, :], axis=1), N_EXPERTS - 1).astype(I32)
    n_active = (pend[-1:] // tb).astype(I32)
    return dest, block_expert, n_active, n_slots


def kernel(x, positions, w_in, w_out, diff_lambda, diff_norm_g, ln_mix_g, ln_mix_b, router_w, router_bias,
           w_gate, w_up, w_down, ln_ffn_g, ln_ffn_b):
    b, s, d = x.shape
    t = b * s
    n_keep = min(DSA_TOPK_MAX, s // 4)
    t128, t64 = _rope_tables(positions)
    s128, s64 = _col_scales()
    rw_t = router_w.T
    bias = router_bias.reshape(N_EXPERTS, 1).astype(F32)
    xf = x.reshape(t, d)
    xb = xf.astype(BF16)
    nc = s // min(ATT_T, s)
    for layer in range(DEPTH):
        w128, w64, wt = _split_w_in(w_in[layer])
        p128 = _proj(xb, w128, s128, t128, "rope128", 896).reshape(b, s, -1)
        p64 = _proj(xb, w64, s64, t64, "rope64", 768).reshape(b, s, -1)
        vt, iwt = _proj_t(xb, wt)
        vt = vt.reshape(b, nc, V_ROWS, -1)
        iwt = iwt.reshape(b, nc, IDX_HEADS, -1)

        lam_init = 0.8 - 0.6 * math.exp(-0.3 * layer)
        o_a = _dsa(p128, p64, vt, iwt, n_keep)
        o_b = _diff(p64, vt, diff_lambda[layer], diff_norm_g[layer].reshape(HEAD_DIM, 1), lam_init)
        o_c = _moba(p128, vt, _kmean(p128))
        o_d = _dilated(p128, vt)
        xf, xb = _outproj_ln(o_a.reshape(t, MIX_W), o_b.reshape(t, MIX_W), o_c.reshape(t, MIX_W),
                             o_d.reshape(t, MIX_W), w_out[layer].astype(BF16), xf,
                             ln_mix_g[layer].reshape(1, d), ln_mix_b[layer].reshape(1, d))

        e, gate, rank, counts = _router(xf, rw_t, bias)
        dest, block_expert, n_active, n_slots = _moe_layout(e, rank, counts[:, 0], EXPERT_TB)
        xs = _dispatch(dest, xf, n_slots)
        ys = _experts(block_expert, n_active, xs, w_gate, w_up, w_down, layer)
        xf, xb = _combine_ln(dest, xf, gate.T, ln_ffn_g[layer].reshape(1, d), ln_ffn_b[layer].reshape(1, d), ys)
    return xf.reshape(b, s, d)
```

```python
import functools
import math

import numpy as np
import jax
import jax.numpy as jnp
from jax import lax
from jax.experimental import pallas as pl
from jax.experimental.pallas import tpu as pltpu

F32 = jnp.float32
BF16 = jnp.bfloat16
I32 = jnp.int32

D_MODEL = 2048
HEAD_DIM = 128
HEADS = 4
MIX_W = HEADS * HEAD_DIM
IDX_HEADS = 16
IDX_DIM = 64
DSA_TOPK_MAX = 256
DIFF_DIM = 64
MOBA_BLOCK = 256
MOBA_TOPK = 3
DILATED_PATTERNS = ((128, 1), (512, 4), (2048, 16))
ROPE_THETA = 10000.0
N_EXPERTS = 16
N_GROUPS = 4
EXPERTS_PER_GROUP = 4
LN_EPS = 1e-5
DEPTH = 2
ALPHA = (2 * DEPTH) ** 0.25
LOG2E = math.log2(math.e)

LANES = 128
SUBLANES = 8
NEG = -1e30
M_FLOOR = -1e29
INT_MIN = -(2 ** 31)
NEG_INF_ORDER = 0x007FFFFF
VMEM_LIMIT = 56 * 1024 * 1024

PROJ_TM = 512
ATT_T = 256
DIFF_TQ = 512
DSA_TQ = 512
OUT_TM = 256
ROUTER_TM = 256
DISPATCH_TM = 256
EXPERT_TB = 256
COMBINE_TM = 256
DMA_UNROLL = 8
V_ROWS = 3 * MIX_W + HEAD_DIM

_OFF = {}
_o = 0
for _name, _n in (("a_q", 512), ("a_k", 128), ("a_v", 128), ("i_q", 1024), ("i_k", 64), ("i_w", 16),
                  ("b_q", 512), ("b_k", 512), ("b_v", 512), ("c_q", 512), ("c_k", 512), ("c_v", 512),
                  ("d_q", 512), ("d_k", 512), ("d_v", 512)):
    _OFF[_name] = (_o, _o + _n)
    _o += _n


def _params(sem=None):
    return pltpu.CompilerParams(dimension_semantics=sem, vmem_limit_bytes=VMEM_LIMIT)


def _dot_nt(a, b):
    return lax.dot_general(a, b, (((1,), (1,)), ((), ())), preferred_element_type=F32)


def _proj_kernel(x_ref, w_ref, scale_ref, *rest, mode):
    o_ref = rest[-1]
    acc = jnp.dot(x_ref[...].astype(BF16), w_ref[...], preferred_element_type=F32)
    n_groups = acc.shape[1] // LANES
    for g in range(n_groups):
        y = acc[:, g * LANES:(g + 1) * LANES]
        if mode == "rope128":
            cos_ref, sin_ref = rest[0], rest[1]
            r = y * cos_ref[...] + pltpu.roll(y, 64, 1) * sin_ref[...]
        else:
            cos_ref, sina_ref, sinb_ref = rest[0], rest[1], rest[2]
            r = (y * cos_ref[...] + pltpu.roll(y, 96, 1) * sina_ref[...]
                 + pltpu.roll(y, 32, 1) * sinb_ref[...])
        r = r * scale_ref[:, g * LANES:(g + 1) * LANES]
        o_ref[:, g * LANES:(g + 1) * LANES] = r.astype(o_ref.dtype)


def _proj(x_bf, w_bf, col_scale, tables, mode, tn):
    t, d = x_bf.shape
    n = w_bf.shape[1]
    tm = min(PROJ_TM, t)
    in_specs = [pl.BlockSpec((tm, d), lambda j, i: (i, 0)),
                pl.BlockSpec((d, tn), lambda j, i: (0, j)),
                pl.BlockSpec((1, tn), lambda j, i: (0, j))]
    in_specs += [pl.BlockSpec((tm, LANES), lambda j, i: (i, 0)) for _ in tables]
    return pl.pallas_call(
        functools.partial(_proj_kernel, mode=mode),
        grid=(n // tn, t // tm),
        in_specs=in_specs,
        out_specs=pl.BlockSpec((tm, tn), lambda j, i: (i, j)),
        out_shape=jax.ShapeDtypeStruct((t, n), BF16),
        compiler_params=_params(("arbitrary", "arbitrary")),
    )(x_bf, w_bf, col_scale, *tables)


def _proj_t_kernel(x_ref, wt_ref, vt_ref, iwt_ref):
    acc = _dot_nt(wt_ref[...], x_ref[...].astype(BF16))
    vt_ref[...] = acc[:V_ROWS].astype(vt_ref.dtype)
    iwt_ref[...] = acc[V_ROWS:]


def _proj_t(x_bf, wt_bf):
    t, d = x_bf.shape
    tm = min(ATT_T, t)
    rows = wt_bf.shape[0]
    return pl.pallas_call(
        _proj_t_kernel,
        grid=(t // tm,),
        in_specs=[pl.BlockSpec((tm, d), lambda i: (i, 0)),
                  pl.BlockSpec((rows, d), lambda i: (0, 0))],
        out_specs=[pl.BlockSpec((None, V_ROWS, tm), lambda i: (i, 0, 0)),
                   pl.BlockSpec((None, IDX_HEADS, tm), lambda i: (i, 0, 0))],
        out_shape=[jax.ShapeDtypeStruct((t // tm, V_ROWS, tm), BF16),
                   jax.ShapeDtypeStruct((t // tm, IDX_HEADS, tm), F32)],
        compiler_params=_params(("arbitrary",)),
    )(x_bf, wt_bf)


def _online_updates(scores, mask, values, m_ref, l_ref, acc_ref, weight=None):
    probs, alphas = [], []
    masks = mask if isinstance(mask, list) else [mask] * len(scores)
    for idx, (s_t, mask) in enumerate(zip(scores, masks)):
        if mask is not None:
            s_t = jnp.where(mask, s_t, NEG)
        m_prev = m_ref[idx]
        m_new = jnp.maximum(m_prev, jnp.max(s_t, axis=0, keepdims=True))
        p = jnp.exp2(s_t - m_new)
        if weight is not None:
            p = p * weight
        alpha = jnp.exp2(m_prev - m_new)
        l_ref[idx] = alpha * l_ref[idx] + jnp.sum(p, axis=0, keepdims=True)
        m_ref[idx] = m_new
        probs.append(p.astype(BF16))
        alphas.append(alpha)
    for idx, (p, alpha) in enumerate(zip(probs, alphas)):
        acc_ref[idx] = alpha * acc_ref[idx] + jnp.dot(values[idx], p, preferred_element_type=F32)


def _init_softmax(m_ref, l_ref, acc_ref):
    m_ref[...] = jnp.full(m_ref.shape, M_FLOOR, F32)
    l_ref[...] = jnp.zeros(l_ref.shape, F32)
    acc_ref[...] = jnp.zeros(acc_ref.shape, F32)


def _softmax_scratch(n, t):
    return [pltpu.VMEM((n, 1, t), F32), pltpu.VMEM((n, 1, t), F32), pltpu.VMEM((n, HEAD_DIM, t), F32)]


def _causal_mask(tk, tq, k_start):
    return k_start + lax.broadcasted_iota(I32, (tk, tq), 0) <= lax.broadcasted_iota(I32, (tk, tq), 1)


def _causal_t(t):
    return _causal_mask(t, t, 0)


def _head(ref_or_val, h):
    return ref_or_val[:, h * HEAD_DIM:(h + 1) * HEAD_DIM]


def _order_to_f32(u):
    key = u ^ jnp.int32(INT_MIN)
    bits = jnp.where(key < 0, key ^ jnp.int32(0x7FFFFFFF), key)
    return lax.bitcast_convert_type(bits, F32)


def _dsa_kernel(aq_ref, ak_ref, avt_ref, iq_ref, ik_ref, iwt_ref, o_ref,
                score_ref, m_ref, l_ref, acc_ref, jlim_ref, *, n_keep):
    tq = aq_ref.shape[0]
    t = avt_ref.shape[2]
    i = pl.program_id(1)
    n_chunks = (i + 1) * (tq // t)
    lane = lax.broadcasted_iota(I32, (1, LANES), 1)
    k_off = lax.broadcasted_iota(I32, (t, 1), 0)
    q_pos = i * tq + lax.broadcasted_iota(I32, (1, tq), 1)
    iw_t = jnp.concatenate([iwt_ref[n] for n in range(tq // t)], axis=1)

    def score_chunk(c, carry):
        ik_c = ik_ref[pl.ds(pl.multiple_of(c * t, t), t), :]
        ik_lo = jnp.where(lane < IDX_DIM, ik_c, jnp.zeros_like(ik_c))
        ik_hi = jnp.where(lane >= IDX_DIM, ik_c, jnp.zeros_like(ik_c))
        score = jnp.zeros((t, tq), F32)
        for j in range(IDX_HEADS // 2):
            q_pair = iq_ref[:, j * LANES:(j + 1) * LANES]
            s_even = jnp.maximum(_dot_nt(ik_lo, q_pair), 0.0)
            s_odd = jnp.maximum(_dot_nt(ik_hi, q_pair), 0.0)
            score = score + s_even * iw_t[2 * j:2 * j + 1, :] + s_odd * iw_t[2 * j + 1:2 * j + 2, :]
        score_ref[c] = jnp.where(c * t + k_off <= q_pos, score, -jnp.inf)
        return carry

    lax.fori_loop(0, n_chunks, score_chunk, 0)

    def count(pred):
        def body(c, part):
            hit = jnp.where(pred(score_ref[c], c), 1, 0)
            return part + jnp.sum(hit.reshape(t // SUBLANES, SUBLANES, tq), axis=0)
        part = lax.fori_loop(0, n_chunks, body, jnp.zeros((SUBLANES, tq), I32))
        return jnp.sum(part, axis=0, keepdims=True)

    def bisect(it, thr_u):
        cand_u = thr_u | jnp.left_shift(jnp.int32(1), 31 - it)
        cand = _order_to_f32(cand_u)
        cnt = count(lambda sc, c: sc >= cand)
        return jnp.where(cnt >= n_keep, cand_u, thr_u)

    thr_u = lax.fori_loop(0, 32, bisect, jnp.zeros((1, tq), I32))
    below = (thr_u >= 0) & (thr_u < NEG_INF_ORDER)
    thr = jnp.where(below, -jnp.inf, _order_to_f32(thr_u))

    n_gt = count(lambda sc, c: sc > thr)
    n_eq = count(lambda sc, c: sc == thr)
    need = n_keep - n_gt
    s_total = score_ref.shape[0] * t
    jlim_ref[...] = jnp.full((1, tq), s_total, I32)
    excess = jnp.max(jnp.where((n_eq > need) & (thr > -jnp.inf), 1, 0))

    @pl.when(excess > 0)
    def _():
        n_bits = max(1, (s_total - 1).bit_length())

        def bisect_idx(it, j):
            cand = j | jnp.left_shift(jnp.int32(1), n_bits - 1 - it)
            cnt = count(lambda sc, c: (sc == thr) & (c * t + k_off < cand))
            return jnp.where(cnt < need, cand, j)

        jlim_ref[...] = lax.fori_loop(0, n_bits, bisect_idx, jnp.zeros((1, tq), I32))

    _init_softmax(m_ref, l_ref, acc_ref)
    jlim = jnp.where(thr > -jnp.inf, jlim_ref[...], -1)
    thr_all = jnp.where(thr > -jnp.inf, thr, jnp.finfo(F32).min)

    def attend(select):
        def body(c, carry):
            start = pl.multiple_of(c * t, t)
            k_c = ak_ref[pl.ds(start, t), :]
            scores = [_dot_nt(k_c, _head(aq_ref, h)) for h in range(HEADS)]
            sel = select(score_ref[c], c * t + k_off)
            _online_updates(scores, sel, [avt_ref[c]] * HEADS, m_ref, l_ref, acc_ref)
            return carry
        lax.fori_loop(0, n_chunks, body, 0)

    @pl.when(excess > 0)
    def _():
        attend(lambda sc, k_pos: (sc > thr) | ((sc == thr) & (k_pos <= jlim)))

    @pl.when(excess <= 0)
    def _():
        attend(lambda sc, k_pos: sc >= thr_all)
    for h in range(HEADS):
        o_ref[:, h * HEAD_DIM:(h + 1) * HEAD_DIM] = (acc_ref[h] / l_ref[h]).T.astype(o_ref.dtype)


def _dsa(p128, p64, vt, iwt, n_keep):
    b, s, _ = p128.shape
    t = min(ATT_T, s)
    tq = min(DSA_TQ, s)
    nc = s // t
    return pl.pallas_call(
        functools.partial(_dsa_kernel, n_keep=n_keep),
        grid=(b, s // tq),
        in_specs=[
            pl.BlockSpec((None, tq, MIX_W), lambda bb, i: (bb, i, 0)),
            pl.BlockSpec((None, s, HEAD_DIM), lambda bb, i: (bb, 0, 20)),
            pl.BlockSpec((None, nc, HEAD_DIM, t), lambda bb, i: (bb, 0, 12, 0)),
            pl.BlockSpec((None, tq, 1024), lambda bb, i: (bb, i, 0)),
            pl.BlockSpec((None, s, LANES), lambda bb, i: (bb, 0, 16)),
            pl.BlockSpec((None, tq // t, IDX_HEADS, t), lambda bb, i: (bb, i, 0, 0)),
        ],
        out_specs=pl.BlockSpec((None, tq, MIX_W), lambda bb, i: (bb, i, 0)),
        out_shape=jax.ShapeDtypeStruct((b, s, MIX_W), BF16),
        scratch_shapes=[pltpu.VMEM((nc, t, tq), F32)] + _softmax_scratch(HEADS, tq)
        + [pltpu.VMEM((1, tq), I32)],
        compiler_params=_params(("arbitrary", "arbitrary")),
    )(p128, p128, vt, p64, p64, iwt)


def _diff_kernel(q_ref, k_ref, vt_ref, lam_ref, g_ref, o_ref, m_ref, l_ref, acc_ref, *, lam_init):
    tq = q_ref.shape[0]
    t = vt_ref.shape[2]
    per_tile = tq // t
    i = pl.program_id(1)
    lane = lax.broadcasted_iota(I32, (1, LANES), 1)
    _init_softmax(m_ref, l_ref, acc_ref)

    def step(c, mask):
        k_c = k_ref[pl.ds(pl.multiple_of(c * t, t), t), :]
        v_c = vt_ref[c]
        scores, values = [], []
        for h in range(HEADS):
            q_h = _head(q_ref, h)
            k_h = _head(k_c, h)
            q_1 = jnp.where(lane < DIFF_DIM, q_h, jnp.zeros_like(q_h))
            q_2 = jnp.where(lane >= DIFF_DIM, q_h, jnp.zeros_like(q_h))
            scores += [_dot_nt(k_h, q_1), _dot_nt(k_h, q_2)]
            values += [v_c[h * HEAD_DIM:(h + 1) * HEAD_DIM, :]] * 2
        _online_updates(scores, mask, values, m_ref, l_ref, acc_ref)

    def full_step(c, carry):
        step(c, None)
        return carry

    lax.fori_loop(0, i * per_tile, full_step, 0)
    for d in range(per_tile):
        step(i * per_tile + d, _causal_mask(t, tq, d * t))

    lam_p = lam_ref[...]
    lam = (jnp.exp(jnp.sum(lam_p[0:1] * lam_p[1:2], axis=1, keepdims=True))
           - jnp.exp(jnp.sum(lam_p[2:3] * lam_p[3:4], axis=1, keepdims=True)) + lam_init)
    for h in range(HEADS):
        o = acc_ref[2 * h] / l_ref[2 * h] - lam * (acc_ref[2 * h + 1] / l_ref[2 * h + 1])
        o = o * lax.rsqrt(jnp.mean(o * o, axis=0, keepdims=True) + LN_EPS) * g_ref[...]
        o_ref[:, h * HEAD_DIM:(h + 1) * HEAD_DIM] = (o * (1.0 - lam_init)).T.astype(o_ref.dtype)


def _diff(p64, vt, lam_p, g_col, lam_init):
    b, s, _ = p64.shape
    t = min(ATT_T, s)
    tq = min(DIFF_TQ, s)
    nc = s // t
    return pl.pallas_call(
        functools.partial(_diff_kernel, lam_init=lam_init),
        grid=(b, s // tq),
        in_specs=[
            pl.BlockSpec((None, tq, MIX_W), lambda bb, i: (bb, i, 2)),
            pl.BlockSpec((None, s, MIX_W), lambda bb, i: (bb, 0, 3)),
            pl.BlockSpec((None, nc, MIX_W, t), lambda bb, i: (bb, 0, 0, 0)),
            pl.BlockSpec((4, DIFF_DIM), lambda bb, i: (0, 0)),
            pl.BlockSpec((HEAD_DIM, 1), lambda bb, i: (0, 0)),
        ],
        out_specs=pl.BlockSpec((None, tq, MIX_W), lambda bb, i: (bb, i, 0)),
        out_shape=jax.ShapeDtypeStruct((b, s, MIX_W), BF16),
        scratch_shapes=_softmax_scratch(2 * HEADS, tq),
        compiler_params=_params(("arbitrary", "arbitrary")),
    )(p64, p64, vt, lam_p, g_col)


def _kmean_kernel(k_ref, o_ref):
    n_blk = k_ref.shape[0] // MOBA_BLOCK
    o_ref[...] = jnp.zeros(o_ref.shape, o_ref.dtype)
    for n in range(n_blk):
        blk = k_ref[n * MOBA_BLOCK:(n + 1) * MOBA_BLOCK, :].astype(F32)
        o_ref[n:n + 1, :] = jnp.mean(blk, axis=0, keepdims=True)


def _kmean(p128):
    b, s, _ = p128.shape
    rows = max(SUBLANES, s // MOBA_BLOCK)
    return pl.pallas_call(
        _kmean_kernel,
        grid=(b,),
        in_specs=[pl.BlockSpec((None, s, MIX_W), lambda bb: (bb, 0, 2))],
        out_specs=pl.BlockSpec((None, rows, MIX_W), lambda bb: (bb, 0, 0)),
        out_shape=jax.ShapeDtypeStruct((b, rows, MIX_W), F32),
        compiler_params=_params(("arbitrary",)),
    )(p128)


def _moba_kernel(q_ref, k_ref, vt_ref, km_ref, o_ref, sel_ref, m_ref, l_ref, acc_ref):
    t = q_ref.shape[0]
    n_blk = km_ref.shape[0]
    i = pl.program_id(1)
    blk = lax.broadcasted_iota(I32, (n_blk, 1), 0)
    _init_softmax(m_ref, l_ref, acc_ref)

    for h in range(HEADS):
        gate = _dot_nt(_head(km_ref, h).astype(BF16), _head(q_ref, h))
        gate = jnp.where(blk < i, gate, -jnp.inf)
        sel = jnp.zeros(gate.shape, F32)
        for _ in range(MOBA_TOPK):
            best = jnp.max(gate, axis=0, keepdims=True)
            first = jnp.min(jnp.where(gate == best, blk, n_blk), axis=0, keepdims=True)
            pick = (blk == first) & (best > -jnp.inf)
            sel = jnp.where(pick, 1.0, sel)
            gate = jnp.where(pick, -jnp.inf, gate)
        sel_ref[h] = sel

    def step(c, diagonal):
        k_c = k_ref[pl.ds(pl.multiple_of(c * t, t), t), :]
        v_c = vt_ref[c]
        scores = [_dot_nt(_head(k_c, h), _head(q_ref, h)) for h in range(HEADS)]
        values = [v_c[h * HEAD_DIM:(h + 1) * HEAD_DIM, :] for h in range(HEADS)]
        if diagonal:
            masks = _causal_t(t)
        else:
            masks = [sel_ref[h, pl.ds(c, 1), :] > 0.0 for h in range(HEADS)]
        _online_updates(scores, masks, values, m_ref, l_ref, acc_ref)

    def past_step(c, carry):
        step(c, False)
        return carry

    lax.fori_loop(0, i, past_step, 0)
    step(i, True)
    for h in range(HEADS):
        o_ref[:, h * HEAD_DIM:(h + 1) * HEAD_DIM] = (acc_ref[h] / l_ref[h]).T.astype(o_ref.dtype)


def _moba(p128, vt, kmean):
    b, s, _ = p128.shape
    t = MOBA_BLOCK
    nc = s // t
    rows = kmean.shape[1]
    return pl.pallas_call(
        _moba_kernel,
        grid=(b, nc),
        in_specs=[
            pl.BlockSpec((None, t, MIX_W), lambda bb, i: (bb, i, 1)),
            pl.BlockSpec((None, s, MIX_W), lambda bb, i: (bb, 0, 2)),
            pl.BlockSpec((None, nc, MIX_W, t), lambda bb, i: (bb, 0, 1, 0)),
            pl.BlockSpec((None, rows, MIX_W), lambda bb, i: (bb, 0, 0)),
        ],
        out_specs=pl.BlockSpec((None, t, MIX_W), lambda bb, i: (bb, i, 0)),
        out_shape=jax.ShapeDtypeStruct((b, s, MIX_W), BF16),
        scratch_shapes=[pltpu.VMEM((HEADS, rows, t), F32)] + _softmax_scratch(HEADS, t),
        compiler_params=_params(("arbitrary", "arbitrary")),
    )(p128, p128, vt, kmean)


def _dilated_kernel(q_ref, k_ref, vt_ref, o_ref, m_ref, l_ref, acc_ref):
    t = q_ref.shape[0]
    i = pl.program_id(1)
    _init_softmax(m_ref, l_ref, acc_ref)
    max_window = max(w for w, _ in DILATED_PATTERNS)
    first = jnp.maximum(i - max_window // t, 0)
    rel = lax.broadcasted_iota(I32, (t, t), 1) - lax.broadcasted_iota(I32, (t, t), 0)

    def step(c, patterns):
        k_c = k_ref[pl.ds(pl.multiple_of(c * t, t), t), :]
        v_c = vt_ref[c]
        dist = rel + (i - c) * t
        mult = jnp.zeros((t, t), F32)
        for window, dilation in patterns:
            hit = (dist >= 0) & (dist <= window) & ((dist & (dilation - 1)) == 0)
            mult = mult + jnp.where(hit, 1.0, 0.0)
        mask = mult > 0.0
        scores = [_dot_nt(_head(k_c, h), _head(q_ref, h)) for h in range(HEADS)]
        values = [v_c[h * HEAD_DIM:(h + 1) * HEAD_DIM, :] for h in range(HEADS)]
        _online_updates(scores, mask, values, m_ref, l_ref, acc_ref, weight=mult)

    widest = max(DILATED_PATTERNS)
    others = [p for p in DILATED_PATTERNS if p != widest]
    n_near = (max(w for w, _ in others) + t - 1) // t + 1
    near_first = jnp.maximum(i + 1 - n_near, first)

    def far_step(c, carry):
        step(c, [widest])
        return carry

    def near_step(c, carry):
        step(c, DILATED_PATTERNS)
        return carry

    lax.fori_loop(first, near_first, far_step, 0)
    lax.fori_loop(near_first, i + 1, near_step, 0)
    for h in range(HEADS):
        o_ref[:, h * HEAD_DIM:(h + 1) * HEAD_DIM] = (acc_ref[h] / l_ref[h]).T.astype(o_ref.dtype)


def _dilated(p128, vt):
    b, s, _ = p128.shape
    t = min(ATT_T, s)
    nc = s // t
    return pl.pallas_call(
        _dilated_kernel,
        grid=(b, nc),
        in_specs=[
            pl.BlockSpec((None, t, MIX_W), lambda bb, i: (bb, i, 3)),
            pl.BlockSpec((None, s, MIX_W), lambda bb, i: (bb, 0, 4)),
            pl.BlockSpec((None, nc, MIX_W, t), lambda bb, i: (bb, 0, 2, 0)),
        ],
        out_specs=pl.BlockSpec((None, t, MIX_W), lambda bb, i: (bb, i, 0)),
        out_shape=jax.ShapeDtypeStruct((b, s, MIX_W), BF16),
        scratch_shapes=_softmax_scratch(HEADS, t),
        compiler_params=_params(("arbitrary", "arbitrary")),
    )(p128, p128, vt)


def _layer_norm(z, g, b):
    mu = jnp.mean(z, axis=1, keepdims=True)
    zc = z - mu
    var = jnp.mean(zc * zc, axis=1, keepdims=True)
    return zc * lax.rsqrt(var + LN_EPS) * g + b


def _outproj_ln_kernel(oa_ref, ob_ref, oc_ref, od_ref, w_ref, x_ref, g_ref, b_ref, xo_ref, xb_ref):
    y = jnp.dot(oa_ref[...], w_ref[0:MIX_W, :], preferred_element_type=F32)
    y = y + jnp.dot(ob_ref[...], w_ref[MIX_W:2 * MIX_W, :], preferred_element_type=F32)
    y = y + jnp.dot(oc_ref[...], w_ref[2 * MIX_W:3 * MIX_W, :], preferred_element_type=F32)
    y = y + jnp.dot(od_ref[...], w_ref[3 * MIX_W:4 * MIX_W, :], preferred_element_type=F32)
    out = _layer_norm(ALPHA * x_ref[...] + y, g_ref[...], b_ref[...])
    xo_ref[...] = out
    xb_ref[...] = out.astype(BF16)


def _outproj_ln(oa, ob, oc, od, w_bf, x, g, b):
    t, d = x.shape
    tm = min(OUT_TM, t)
    mix_spec = pl.BlockSpec((tm, MIX_W), lambda i: (i, 0))
    row_spec = pl.BlockSpec((tm, d), lambda i: (i, 0))
    vec_spec = pl.BlockSpec((1, d), lambda i: (0, 0))
    return pl.pallas_call(
        _outproj_ln_kernel,
        grid=(t // tm,),
        in_specs=[mix_spec, mix_spec, mix_spec, mix_spec,
                  pl.BlockSpec((d, d), lambda i: (0, 0)), row_spec, vec_spec, vec_spec],
        out_specs=[row_spec, row_spec],
        out_shape=[jax.ShapeDtypeStruct((t, d), F32), jax.ShapeDtypeStruct((t, d), BF16)],
        compiler_params=_params(("arbitrary",)),
    )(oa, ob, oc, od, w_bf, x, g, b)


def _first_argmax(vals):
    best, arg = vals[0], jnp.zeros(vals[0].shape, I32)
    for j in range(1, len(vals)):
        better = vals[j] > best
        arg = jnp.where(better, j, arg)
        best = jnp.where(better, vals[j], best)
    return arg, best


def _pick(rows, idx):
    out = rows[0]
    for j in range(1, len(rows)):
        out = jnp.where(idx == j, rows[j], out)
    return out


def _router_kernel(x_ref, rwt_ref, bias_ref, e_ref, gate_ref, rank_ref, cnt_ref, carry_ref):
    tm = x_ref.shape[0]

    @pl.when(pl.program_id(0) == 0)
    def _():
        carry_ref[...] = jnp.zeros(carry_ref.shape, F32)

    logits = lax.dot_general(rwt_ref[...], x_ref[...], (((1,), (1,)), ((), ())),
                             precision=lax.Precision.HIGHEST, preferred_element_type=F32)
    aff = 1.0 / (1.0 + jnp.exp(-logits))
    biased = aff + bias_ref[...]
    sel_rows = [biased[r:r + 1, :] for r in range(N_EXPERTS)]
    aff_rows = [aff[r:r + 1, :] for r in range(N_EXPERTS)]

    group_scores = []
    for g in range(N_GROUPS):
        r = sel_rows[g * EXPERTS_PER_GROUP:(g + 1) * EXPERTS_PER_GROUP]
        best_pair = r[0] + r[1]
        for a in range(EXPERTS_PER_GROUP):
            for c in range(a + 1, EXPERTS_PER_GROUP):
                if (a, c) != (0, 1):
                    best_pair = jnp.maximum(best_pair, r[a] + r[c])
        group_scores.append(best_pair)
    grp, _ = _first_argmax(group_scores)

    in_sel = [_pick([sel_rows[g * EXPERTS_PER_GROUP + j] for g in range(N_GROUPS)], grp)
              for j in range(EXPERTS_PER_GROUP)]
    in_aff = [_pick([aff_rows[g * EXPERTS_PER_GROUP + j] for g in range(N_GROUPS)], grp)
              for j in range(EXPERTS_PER_GROUP)]
    first, _ = _first_argmax(in_sel)
    second, _ = _first_argmax([jnp.where(first == j, -jnp.inf, in_sel[j]) for j in range(EXPERTS_PER_GROUP)])
    a0 = _pick(in_aff, first)
    a1 = _pick(in_aff, second)
    e0 = grp * EXPERTS_PER_GROUP + first
    e1 = grp * EXPERTS_PER_GROUP + second
    e_ref[0:1, :] = e0
    e_ref[1:2, :] = e1
    gate_ref[0:1, :] = a0 / (a0 + a1)
    gate_ref[1:2, :] = a1 / (a0 + a1)

    e_iota = lax.broadcasted_iota(I32, (N_EXPERTS, tm), 0)
    one_hot = ((e_iota == e0) | (e_iota == e1)).astype(BF16)
    before = (lax.broadcasted_iota(I32, (tm, tm), 0) < lax.broadcasted_iota(I32, (tm, tm), 1)).astype(BF16)
    prior = jnp.dot(one_hot, before, preferred_element_type=F32) + carry_ref[...]
    rank_ref[0:1, :] = jnp.sum(jnp.where(e_iota == e0, prior, 0.0), axis=0, keepdims=True).astype(I32)
    rank_ref[1:2, :] = jnp.sum(jnp.where(e_iota == e1, prior, 0.0), axis=0, keepdims=True).astype(I32)
    carry_ref[...] = carry_ref[...] + jnp.sum(one_hot.astype(F32), axis=1, keepdims=True)
    cnt_ref[...] = jnp.broadcast_to(carry_ref[...], cnt_ref.shape).astype(I32)


def _router(x, rw_t, bias):
    t, d = x.shape
    tm = min(ROUTER_TM, t)
    tok_spec = pl.BlockSpec((2, tm), lambda i: (0, i))
    return pl.pallas_call(
        _router_kernel,
        grid=(t // tm,),
        in_specs=[pl.BlockSpec((tm, d), lambda i: (i, 0)),
                  pl.BlockSpec((N_EXPERTS, d), lambda i: (0, 0)),
                  pl.BlockSpec((N_EXPERTS, 1), lambda i: (0, 0))],
        out_specs=[tok_spec, tok_spec, tok_spec, pl.BlockSpec((N_EXPERTS, LANES), lambda i: (0, 0))],
        out_shape=[jax.ShapeDtypeStruct((2, t), I32), jax.ShapeDtypeStruct((2, t), F32),
                   jax.ShapeDtypeStruct((2, t), I32), jax.ShapeDtypeStruct((N_EXPERTS, LANES), I32)],
        scratch_shapes=[pltpu.VMEM((N_EXPERTS, 1), F32)],
        compiler_params=_params(("arbitrary",)),
    )(x, rw_t, bias)


def _row_copy(src, src_row, dst, dst_row, sem):
    return pltpu.make_async_copy(src.at[pl.ds(src_row, 1)], dst.at[pl.ds(dst_row, 1)], sem)


def _dispatch_kernel(dest_ref, pend_ref, na_ref, x_ref, slots_hbm, zeros_ref, sem, zero_sem,
                     *, tm, n_tok, tb):
    base = pl.program_id(0) * tm

    @pl.when(pl.program_id(0) == 0)
    def _():
        zeros_ref[...] = jnp.zeros(zeros_ref.shape, zeros_ref.dtype)
        n_blocks = slots_hbm.shape[0] // tb

        def block_fill(start):
            return pltpu.make_async_copy(zeros_ref, slots_hbm.at[pl.ds(start, tb)], zero_sem)

        def for_each_fill(act):
            for e in range(N_EXPERTS):
                seg_start = pend_ref[e - 1] if e else 0

                @pl.when(pend_ref[e] > seg_start)
                def _():
                    act(pl.multiple_of(pend_ref[e] - tb, tb))

            def tail(n, carry):
                act(pl.multiple_of(n * tb, tb))
                return carry

            lax.fori_loop(na_ref[0], n_blocks, tail, 0)

        for_each_fill(lambda start: block_fill(start).start())
        for_each_fill(lambda start: block_fill(start).wait())

    def issue(r, carry):
        tok = base + r
        _row_copy(x_ref, r, slots_hbm, dest_ref[tok], sem).start()
        _row_copy(x_ref, r, slots_hbm, dest_ref[n_tok + tok], sem).start()
        return carry

    def drain(r, carry):
        _row_copy(x_ref, 0, slots_hbm, 0, sem).wait()
        _row_copy(x_ref, 0, slots_hbm, 0, sem).wait()
        return carry

    lax.fori_loop(0, tm, issue, 0, unroll=DMA_UNROLL)
    lax.fori_loop(0, tm, drain, 0, unroll=DMA_UNROLL)


def _dispatch(dest_flat, pend, n_active, x, n_slots):
    t, d = x.shape
    tm = min(DISPATCH_TM, t)
    tb = EXPERT_TB
    return pl.pallas_call(
        functools.partial(_dispatch_kernel, tm=tm, n_tok=t, tb=tb),
        grid_spec=pltpu.PrefetchScalarGridSpec(
            num_scalar_prefetch=3,
            grid=(t // tm,),
            in_specs=[pl.BlockSpec((tm, d), lambda i, dest, pend, na: (i, 0))],
            out_specs=pl.BlockSpec(memory_space=pl.ANY),
            scratch_shapes=[pltpu.VMEM((tb, d), x.dtype), pltpu.SemaphoreType.DMA(()),
                            pltpu.SemaphoreType.DMA(())],
        ),
        out_shape=jax.ShapeDtypeStruct((n_slots, d), x.dtype),
        compiler_params=_params(("arbitrary",)),
    )(dest_flat, pend, n_active, x)


def _last_active(n, na_ref):
    return jnp.maximum(jnp.minimum(n, na_ref[0] - 1), 0)


def _expert_stage(be_ref, na_ref, w_refs, w_bf_refs, compute, out_ref):
    n = pl.program_id(0)
    active = n < na_ref[0]
    new_expert = (n == 0) | (be_ref[n] != be_ref[jnp.maximum(n - 1, 0)])

    @pl.when(active & new_expert)
    def _():
        for w_ref, w_bf_ref in zip(w_refs, w_bf_refs):
            w_bf_ref[...] = w_ref[...].astype(BF16)

    @pl.when(active)
    def _():
        out_ref[...] = compute().astype(out_ref.dtype)

    @pl.when(jnp.logical_not(active))
    def _():
        out_ref[...] = jnp.zeros(out_ref.shape, out_ref.dtype)


def _expert_up_kernel(be_ref, na_ref, xs_ref, wg_ref, wu_ref, h_ref, wg_bf, wu_bf):
    def compute():
        xb = xs_ref[...].astype(BF16)
        gate = jnp.dot(xb, wg_bf[...], preferred_element_type=F32)
        up = jnp.dot(xb, wu_bf[...], preferred_element_type=F32)
        return gate * (1.0 / (1.0 + jnp.exp(-gate))) * up

    _expert_stage(be_ref, na_ref, (wg_ref, wu_ref), (wg_bf, wu_bf), compute, h_ref)


def _expert_down_kernel(be_ref, na_ref, h_ref, wd_ref, ys_ref, wd_bf):
    def compute():
        return jnp.dot(h_ref[...], wd_bf[...], preferred_element_type=F32)

    _expert_stage(be_ref, na_ref, (wd_ref,), (wd_bf,), compute, ys_ref)


def _experts(block_expert, n_active, xs, wg, wu, wd, layer):
    n_slots, d = xs.shape
    tb = EXPERT_TB
    f = wg.shape[3]

    def row_map(n, be, na):
        return (_last_active(n, na), 0)

    def w_map(n, be, na):
        return (layer, be[_last_active(n, na)], 0, 0)

    def out_map(n, be, na):
        return (n, 0)

    hidden = pl.pallas_call(
        _expert_up_kernel,
        grid_spec=pltpu.PrefetchScalarGridSpec(
            num_scalar_prefetch=2,
            grid=(n_slots // tb,),
            in_specs=[pl.BlockSpec((tb, d), row_map),
                      pl.BlockSpec((None, None, d, f), w_map),
                      pl.BlockSpec((None, None, d, f), w_map)],
            out_specs=pl.BlockSpec((tb, f), out_map),
            scratch_shapes=[pltpu.VMEM((d, f), BF16), pltpu.VMEM((d, f), BF16)],
        ),
        out_shape=jax.ShapeDtypeStruct((n_slots, f), BF16),
        compiler_params=_params(("arbitrary",)),
    )(block_expert, n_active, xs, wg, wu)
    return pl.pallas_call(
        _expert_down_kernel,
        grid_spec=pltpu.PrefetchScalarGridSpec(
            num_scalar_prefetch=2,
            grid=(n_slots // tb,),
            in_specs=[pl.BlockSpec((tb, f), row_map),
                      pl.BlockSpec((None, None, f, d), w_map)],
            out_specs=pl.BlockSpec((tb, d), out_map),
            scratch_shapes=[pltpu.VMEM((f, d), BF16)],
        ),
        out_shape=jax.ShapeDtypeStruct((n_slots, d), F32),
        compiler_params=_params(("arbitrary",)),
    )(block_expert, n_active, hidden, wd)


def _combine_ln_kernel(dest_ref, x_ref, gate_ref, g_ref, b_ref, ys_hbm, xo_ref, xb_ref, buf_ref, sem,
                       *, n_tok):
    tm = x_ref.shape[0]
    i = pl.program_id(0)
    slot = i % 2

    def request(tile, to_slot):
        def issue(r, carry):
            tok = tile * tm + r
            _row_copy(ys_hbm, dest_ref[tok], buf_ref.at[to_slot, 0], r, sem.at[to_slot]).start()
            _row_copy(ys_hbm, dest_ref[n_tok + tok], buf_ref.at[to_slot, 1], r, sem.at[to_slot]).start()
            return carry
        lax.fori_loop(0, tm, issue, 0, unroll=DMA_UNROLL)

    @pl.when(i == 0)
    def _():
        request(0, 0)

    @pl.when(i + 1 < pl.num_programs(0))
    def _():
        request(i + 1, 1 - slot)

    def drain(r, carry):
        _row_copy(ys_hbm, 0, buf_ref.at[slot, 0], 0, sem.at[slot]).wait()
        _row_copy(ys_hbm, 0, buf_ref.at[slot, 1], 0, sem.at[slot]).wait()
        return carry

    lax.fori_loop(0, tm, drain, 0, unroll=DMA_UNROLL)
    gate = gate_ref[...]
    moe = buf_ref[slot, 0] * gate[:, 0:1] + buf_ref[slot, 1] * gate[:, 1:2]
    out = _layer_norm(ALPHA * x_ref[...] + moe, g_ref[...], b_ref[...])
    xo_ref[...] = out
    xb_ref[...] = out.astype(BF16)


def _combine_ln(dest_flat, x, gate_t, g, b, ys):
    t, d = x.shape
    tm = min(COMBINE_TM, t)
    row_spec = pl.BlockSpec((tm, d), lambda i, dest: (i, 0))
    vec_spec = pl.BlockSpec((1, d), lambda i, dest: (0, 0))
    return pl.pallas_call(
        functools.partial(_combine_ln_kernel, n_tok=t),
        grid_spec=pltpu.PrefetchScalarGridSpec(
            num_scalar_prefetch=1,
            grid=(t // tm,),
            in_specs=[row_spec, pl.BlockSpec((tm, 2), lambda i, dest: (i, 0)), vec_spec, vec_spec,
                      pl.BlockSpec(memory_space=pl.ANY)],
            out_specs=[row_spec, row_spec],
            scratch_shapes=[pltpu.VMEM((2, 2, tm, d), F32), pltpu.SemaphoreType.DMA((2,))],
        ),
        out_shape=[jax.ShapeDtypeStruct((t, d), F32), jax.ShapeDtypeStruct((t, d), BF16)],
        compiler_params=_params(("arbitrary",)),
    )(dest_flat, x, gate_t, g, b, ys)


def _rope_tables(positions):
    pos = positions.reshape(-1).astype(F32)[:, None]

    def cos_sin(d):
        inv = jnp.power(ROPE_THETA, -jnp.arange(0, d, 2, dtype=F32) / d)
        ang = pos * inv
        return jnp.cos(ang), jnp.sin(ang)

    c, s = cos_sin(HEAD_DIM)
    t128 = (jnp.concatenate([c, c], axis=1), jnp.concatenate([-s, s], axis=1))
    c, s = cos_sin(IDX_DIM)
    z = jnp.zeros_like(s)
    t64 = (jnp.concatenate([c, c, c, c], axis=1),
           jnp.concatenate([-s, z, -s, z], axis=1),
           jnp.concatenate([z, s, z, s], axis=1))
    return t128, t64


def _col_scales():
    q128 = HEAD_DIM ** -0.5 * LOG2E
    s128 = np.ones((1, 5 * MIX_W + HEAD_DIM), np.float32)
    s128[:, 0:2 * MIX_W] = q128
    s128[:, 3 * MIX_W:4 * MIX_W] = q128
    s64 = np.ones((1, 2304), np.float32)
    s64[:, 1024:1024 + MIX_W] = DIFF_DIM ** -0.5 * LOG2E
    return jnp.asarray(s128), jnp.asarray(s64)


def _split_w_in(w):
    def cols(name):
        lo, hi = _OFF[name]
        return w[:, lo:hi]

    d = w.shape[0]
    w128 = jnp.concatenate([cols(n) for n in ("a_q", "c_q", "c_k", "d_q", "d_k", "a_k")], axis=1)
    w64 = jnp.concatenate([cols("i_q"), cols("b_q"), cols("b_k"), cols("i_k"), cols("i_k"),
                           jnp.zeros((d, LANES), w.dtype)], axis=1)
    wt = jnp.concatenate([cols("b_v"), cols("c_v"), cols("d_v"), cols("a_v"), cols("i_w")], axis=1).T
    return w128.astype(BF16), w64.astype(BF16), wt.astype(BF16)


def _moe_layout(e, rank, counts, tb):
    t = e.shape[1]
    n_slots = 2 * t + N_EXPERTS * tb
    padded = (counts + tb - 1) // tb * tb
    pend = jnp.cumsum(padded)
    pstart = pend - padded
    experts = jnp.arange(N_EXPERTS, dtype=I32)[:, None, None]
    seg_start = jnp.sum(jnp.where(e[None] == experts, pstart[:, None, None], 0), axis=0)
    dest = (seg_start + rank).astype(I32).reshape(-1)
    blocks = jnp.arange(n_slots // tb, dtype=I32) * tb
    block_expert = jnp.minimum(jnp.sum(blocks[:, None] >= pend[None, :], axis=1), N_EXPERTS - 1).astype(I32)
    n_active = (pend[-1:] // tb).astype(I32)
    return dest, pend.astype(I32), block_expert, n_active, n_slots


def kernel(x, positions, w_in, w_out, diff_lambda, diff_norm_g, ln_mix_g, ln_mix_b, router_w, router_bias,
           w_gate, w_up, w_down, ln_ffn_g, ln_ffn_b):
    b, s, d = x.shape
    t = b * s
    n_keep = min(DSA_TOPK_MAX, s // 4)
    t128, t64 = _rope_tables(positions)
    s128, s64 = _col_scales()
    rw_t = router_w.T
    bias = router_bias.reshape(N_EXPERTS, 1).astype(F32)
    xf = x.reshape(t, d)
    xb = xf
    nc = s // min(ATT_T, s)
    for layer in range(DEPTH):
        w128, w64, wt = _split_w_in(w_in[layer])
        p128 = _proj(xb, w128, s128, t128, "rope128", 896).reshape(b, s, -1)
        p64 = _proj(xb, w64, s64, t64, "rope64", 768).reshape(b, s, -1)
        vt, iwt = _proj_t(xb, wt)
        vt = vt.reshape(b, nc, V_ROWS, -1)
        iwt = iwt.reshape(b, nc, IDX_HEADS, -1)

        lam_init = 0.8 - 0.6 * math.exp(-0.3 * layer)
        o_a = _dsa(p128, p64, vt, iwt, n_keep)
        o_b = _diff(p64, vt, diff_lambda[layer], diff_norm_g[layer].reshape(HEAD_DIM, 1), lam_init)
        o_c = _moba(p128, vt, _kmean(p128))
        o_d = _dilated(p128, vt)
        xf, xb = _outproj_ln(o_a.reshape(t, MIX_W), o_b.reshape(t, MIX_W), o_c.reshape(t, MIX_W),
                             o_d.reshape(t, MIX_W), w_out[layer].astype(BF16), xf,
                             ln_mix_g[layer].reshape(1, d), ln_mix_b[layer].reshape(1, d))

        e, gate, rank, counts = _router(xf, rw_t, bias)
        dest, pend, block_expert, n_active, n_slots = _moe_layout(e, rank, counts[:, 0], EXPERT_TB)
        xs = _dispatch(dest, pend, n_active, xf, n_slots)
        ys = _experts(block_expert, n_active, xs, w_gate, w_up, w_down, layer)
        xf, xb = _combine_ln(dest, xf, gate.T, ln_ffn_g[layer].reshape(1, d), ln_ffn_b[layer].reshape(1, d), ys)
    return xf.reshape(b, s, d)
```

```python
import functools
import math

import numpy as np
import jax
import jax.numpy as jnp
from jax import lax
from jax.experimental import pallas as pl
from jax.experimental.pallas import tpu as pltpu

F32 = jnp.float32
BF16 = jnp.bfloat16
I32 = jnp.int32

D_MODEL = 2048
HEAD_DIM = 128
HEADS = 4
MIX_W = HEADS * HEAD_DIM
IDX_HEADS = 16
IDX_DIM = 64
DSA_TOPK_MAX = 256
DIFF_DIM = 64
MOBA_BLOCK = 256
MOBA_TOPK = 3
DILATED_PATTERNS = ((128, 1), (512, 4), (2048, 16))
ROPE_THETA = 10000.0
N_EXPERTS = 16
N_GROUPS = 4
EXPERTS_PER_GROUP = 4
LN_EPS = 1e-5
DEPTH = 2
ALPHA = (2 * DEPTH) ** 0.25
LOG2E = math.log2(math.e)

LANES = 128
SUBLANES = 8
NEG = -1e30
M_FLOOR = -1e29
INT_MIN = -(2 ** 31)
NEG_INF_ORDER = 0x007FFFFF
VMEM_LIMIT = 56 * 1024 * 1024

PROJ_TM = 512
ATT_T = 256
DIFF_TQ = 512
DSA_TQ = 512
MOBA_TQ = 512
DIL_TQ = 512
OUT_TM = 256
ROUTER_TM = 256
DISPATCH_TM = 256
EXPERT_TB = 256
COMBINE_TM = 256
DMA_UNROLL = 8
V_ROWS = 3 * MIX_W + HEAD_DIM

_OFF = {}
_o = 0
for _name, _n in (("a_q", 512), ("a_k", 128), ("a_v", 128), ("i_q", 1024), ("i_k", 64), ("i_w", 16),
                  ("b_q", 512), ("b_k", 512), ("b_v", 512), ("c_q", 512), ("c_k", 512), ("c_v", 512),
                  ("d_q", 512), ("d_k", 512), ("d_v", 512)):
    _OFF[_name] = (_o, _o + _n)
    _o += _n


def _params(sem=None):
    return pltpu.CompilerParams(dimension_semantics=sem, vmem_limit_bytes=VMEM_LIMIT)


def _dot_nt(a, b):
    return lax.dot_general(a, b, (((1,), (1,)), ((), ())), preferred_element_type=F32)


def _proj_kernel(x_ref, w_ref, scale_ref, *rest, mode):
    o_ref = rest[-1]
    acc = jnp.dot(x_ref[...].astype(BF16), w_ref[...], preferred_element_type=F32)
    n_groups = acc.shape[1] // LANES
    for g in range(n_groups):
        y = acc[:, g * LANES:(g + 1) * LANES]
        if mode == "rope128":
            cos_ref, sin_ref = rest[0], rest[1]
            r = y * cos_ref[...] + pltpu.roll(y, 64, 1) * sin_ref[...]
        else:
            cos_ref, sina_ref, sinb_ref = rest[0], rest[1], rest[2]
            r = (y * cos_ref[...] + pltpu.roll(y, 96, 1) * sina_ref[...]
                 + pltpu.roll(y, 32, 1) * sinb_ref[...])
        r = r * scale_ref[:, g * LANES:(g + 1) * LANES]
        o_ref[:, g * LANES:(g + 1) * LANES] = r.astype(o_ref.dtype)


def _proj(x_bf, w_bf, col_scale, tables, mode, tn):
    t, d = x_bf.shape
    n = w_bf.shape[1]
    tm = min(PROJ_TM, t)
    in_specs = [pl.BlockSpec((tm, d), lambda j, i: (i, 0)),
                pl.BlockSpec((d, tn), lambda j, i: (0, j)),
                pl.BlockSpec((1, tn), lambda j, i: (0, j))]
    in_specs += [pl.BlockSpec((tm, LANES), lambda j, i: (i, 0)) for _ in tables]
    return pl.pallas_call(
        functools.partial(_proj_kernel, mode=mode),
        grid=(n // tn, t // tm),
        in_specs=in_specs,
        out_specs=pl.BlockSpec((tm, tn), lambda j, i: (i, j)),
        out_shape=jax.ShapeDtypeStruct((t, n), BF16),
        compiler_params=_params(("arbitrary", "arbitrary")),
    )(x_bf, w_bf, col_scale, *tables)


def _proj_t_kernel(x_ref, wt_ref, vt_ref, iwt_ref):
    acc = _dot_nt(wt_ref[...], x_ref[...].astype(BF16))
    vt_ref[...] = acc[:V_ROWS].astype(vt_ref.dtype)
    iwt_ref[...] = acc[V_ROWS:]


def _proj_t(x_bf, wt_bf):
    t, d = x_bf.shape
    tm = min(ATT_T, t)
    rows = wt_bf.shape[0]
    return pl.pallas_call(
        _proj_t_kernel,
        grid=(t // tm,),
        in_specs=[pl.BlockSpec((tm, d), lambda i: (i, 0)),
                  pl.BlockSpec((rows, d), lambda i: (0, 0))],
        out_specs=[pl.BlockSpec((None, V_ROWS, tm), lambda i: (i, 0, 0)),
                   pl.BlockSpec((None, IDX_HEADS, tm), lambda i: (i, 0, 0))],
        out_shape=[jax.ShapeDtypeStruct((t // tm, V_ROWS, tm), BF16),
                   jax.ShapeDtypeStruct((t // tm, IDX_HEADS, tm), F32)],
        compiler_params=_params(("arbitrary",)),
    )(x_bf, wt_bf)


def _online_updates(scores, mask, values, m_ref, l_ref, acc_ref, weight=None):
    probs, alphas = [], []
    masks = mask if isinstance(mask, list) else [mask] * len(scores)
    for idx, (s_t, mask) in enumerate(zip(scores, masks)):
        if mask is not None:
            s_t = jnp.where(mask, s_t, NEG)
        m_prev = m_ref[idx]
        m_new = jnp.maximum(m_prev, jnp.max(s_t, axis=0, keepdims=True))
        p = jnp.exp2(s_t - m_new)
        if weight is not None:
            p = p * weight
        alpha = jnp.exp2(m_prev - m_new)
        l_ref[idx] = alpha * l_ref[idx] + jnp.sum(p, axis=0, keepdims=True)
        m_ref[idx] = m_new
        probs.append(p.astype(BF16))
        alphas.append(alpha)
    for idx, (p, alpha) in enumerate(zip(probs, alphas)):
        acc_ref[idx] = alpha * acc_ref[idx] + jnp.dot(values[idx], p, preferred_element_type=F32)


def _init_softmax(m_ref, l_ref, acc_ref):
    m_ref[...] = jnp.full(m_ref.shape, M_FLOOR, F32)
    l_ref[...] = jnp.zeros(l_ref.shape, F32)
    acc_ref[...] = jnp.zeros(acc_ref.shape, F32)


def _softmax_scratch(n, t):
    return [pltpu.VMEM((n, 1, t), F32), pltpu.VMEM((n, 1, t), F32), pltpu.VMEM((n, HEAD_DIM, t), F32)]


def _causal_mask(tk, tq, k_start):
    return k_start + lax.broadcasted_iota(I32, (tk, tq), 0) <= lax.broadcasted_iota(I32, (tk, tq), 1)


def _head(ref_or_val, h):
    return ref_or_val[:, h * HEAD_DIM:(h + 1) * HEAD_DIM]


def _order_to_f32(u):
    key = u ^ jnp.int32(INT_MIN)
    bits = jnp.where(key < 0, key ^ jnp.int32(0x7FFFFFFF), key)
    return lax.bitcast_convert_type(bits, F32)


def _dsa_kernel(aq_ref, ak_ref, avt_ref, iq_ref, ik_ref, iwt_ref, o_ref,
                score_ref, m_ref, l_ref, acc_ref, jlim_ref, *, n_keep):
    tq = aq_ref.shape[0]
    t = avt_ref.shape[2]
    i = pl.program_id(1)
    n_chunks = (i + 1) * (tq // t)
    lane = lax.broadcasted_iota(I32, (1, LANES), 1)
    k_off = lax.broadcasted_iota(I32, (t, 1), 0)
    q_pos = i * tq + lax.broadcasted_iota(I32, (1, tq), 1)
    iw_t = jnp.concatenate([iwt_ref[n] for n in range(tq // t)], axis=1)

    def score_chunk(c, carry):
        ik_c = ik_ref[pl.ds(pl.multiple_of(c * t, t), t), :]
        ik_lo = jnp.where(lane < IDX_DIM, ik_c, jnp.zeros_like(ik_c))
        ik_hi = jnp.where(lane >= IDX_DIM, ik_c, jnp.zeros_like(ik_c))
        score = jnp.zeros((t, tq), F32)
        for j in range(IDX_HEADS // 2):
            q_pair = iq_ref[:, j * LANES:(j + 1) * LANES]
            s_even = jnp.maximum(_dot_nt(ik_lo, q_pair), 0.0)
            s_odd = jnp.maximum(_dot_nt(ik_hi, q_pair), 0.0)
            score = score + s_even * iw_t[2 * j:2 * j + 1, :] + s_odd * iw_t[2 * j + 1:2 * j + 2, :]
        score_ref[c] = jnp.where(c * t + k_off <= q_pos, score, -jnp.inf)
        return carry

    lax.fori_loop(0, n_chunks, score_chunk, 0)

    def count(pred):
        def body(c, part):
            hit = jnp.where(pred(score_ref[c], c), 1, 0)
            return part + jnp.sum(hit.reshape(t // SUBLANES, SUBLANES, tq), axis=0)
        part = lax.fori_loop(0, n_chunks, body, jnp.zeros((SUBLANES, tq), I32))
        return jnp.sum(part, axis=0, keepdims=True)

    def bisect(it, thr_u):
        cand_u = thr_u | jnp.left_shift(jnp.int32(1), 31 - it)
        cand = _order_to_f32(cand_u)
        cnt = count(lambda sc, c: sc >= cand)
        return jnp.where(cnt >= n_keep, cand_u, thr_u)

    thr_u = lax.fori_loop(0, 32, bisect, jnp.zeros((1, tq), I32))
    below = (thr_u >= 0) & (thr_u < NEG_INF_ORDER)
    thr = jnp.where(below, -jnp.inf, _order_to_f32(thr_u))

    n_gt = count(lambda sc, c: sc > thr)
    n_eq = count(lambda sc, c: sc == thr)
    need = n_keep - n_gt
    s_total = score_ref.shape[0] * t
    jlim_ref[...] = jnp.full((1, tq), s_total, I32)
    excess = jnp.max(jnp.where((n_eq > need) & (thr > -jnp.inf), 1, 0))

    @pl.when(excess > 0)
    def _():
        n_bits = max(1, (s_total - 1).bit_length())

        def bisect_idx(it, j):
            cand = j | jnp.left_shift(jnp.int32(1), n_bits - 1 - it)
            cnt = count(lambda sc, c: (sc == thr) & (c * t + k_off < cand))
            return jnp.where(cnt < need, cand, j)

        jlim_ref[...] = lax.fori_loop(0, n_bits, bisect_idx, jnp.zeros((1, tq), I32))

    _init_softmax(m_ref, l_ref, acc_ref)
    jlim = jnp.where(thr > -jnp.inf, jlim_ref[...], -1)
    thr_all = jnp.where(thr > -jnp.inf, thr, jnp.finfo(F32).min)

    def attend(select):
        def body(c, carry):
            start = pl.multiple_of(c * t, t)
            k_c = ak_ref[pl.ds(start, t), :]
            scores = [_dot_nt(k_c, _head(aq_ref, h)) for h in range(HEADS)]
            sel = select(score_ref[c], c * t + k_off)
            _online_updates(scores, sel, [avt_ref[c]] * HEADS, m_ref, l_ref, acc_ref)
            return carry
        lax.fori_loop(0, n_chunks, body, 0)

    @pl.when(excess > 0)
    def _():
        attend(lambda sc, k_pos: (sc > thr) | ((sc == thr) & (k_pos <= jlim)))

    @pl.when(excess <= 0)
    def _():
        attend(lambda sc, k_pos: sc >= thr_all)
    for h in range(HEADS):
        o_ref[:, h * HEAD_DIM:(h + 1) * HEAD_DIM] = (acc_ref[h] / l_ref[h]).T.astype(o_ref.dtype)


def _dsa(p128, p64, vt, iwt, n_keep):
    b, s, _ = p128.shape
    t = min(ATT_T, s)
    tq = min(DSA_TQ, s)
    nc = s // t
    return pl.pallas_call(
        functools.partial(_dsa_kernel, n_keep=n_keep),
        grid=(b, s // tq),
        in_specs=[
            pl.BlockSpec((None, tq, MIX_W), lambda bb, i: (bb, i, 0)),
            pl.BlockSpec((None, s, HEAD_DIM), lambda bb, i: (bb, 0, 20)),
            pl.BlockSpec((None, nc, HEAD_DIM, t), lambda bb, i: (bb, 0, 12, 0)),
            pl.BlockSpec((None, tq, 1024), lambda bb, i: (bb, i, 0)),
            pl.BlockSpec((None, s, LANES), lambda bb, i: (bb, 0, 16)),
            pl.BlockSpec((None, tq // t, IDX_HEADS, t), lambda bb, i: (bb, i, 0, 0)),
        ],
        out_specs=pl.BlockSpec((None, tq, MIX_W), lambda bb, i: (bb, i, 0)),
        out_shape=jax.ShapeDtypeStruct((b, s, MIX_W), BF16),
        scratch_shapes=[pltpu.VMEM((nc, t, tq), F32)] + _softmax_scratch(HEADS, tq)
        + [pltpu.VMEM((1, tq), I32)],
        compiler_params=_params(("arbitrary", "arbitrary")),
    )(p128, p128, vt, p64, p64, iwt)


def _diff_kernel(q_ref, k_ref, vt_ref, lam_ref, g_ref, o_ref, m_ref, l_ref, acc_ref, *, lam_init):
    tq = q_ref.shape[0]
    t = vt_ref.shape[2]
    per_tile = tq // t
    i = pl.program_id(1)
    lane = lax.broadcasted_iota(I32, (1, LANES), 1)
    _init_softmax(m_ref, l_ref, acc_ref)

    def step(c, mask):
        k_c = k_ref[pl.ds(pl.multiple_of(c * t, t), t), :]
        v_c = vt_ref[c]
        scores, values = [], []
        for h in range(HEADS):
            q_h = _head(q_ref, h)
            k_h = _head(k_c, h)
            q_1 = jnp.where(lane < DIFF_DIM, q_h, jnp.zeros_like(q_h))
            q_2 = jnp.where(lane >= DIFF_DIM, q_h, jnp.zeros_like(q_h))
            scores += [_dot_nt(k_h, q_1), _dot_nt(k_h, q_2)]
            values += [v_c[h * HEAD_DIM:(h + 1) * HEAD_DIM, :]] * 2
        _online_updates(scores, mask, values, m_ref, l_ref, acc_ref)

    def full_step(c, carry):
        step(c, None)
        return carry

    lax.fori_loop(0, i * per_tile, full_step, 0)
    for d in range(per_tile):
        step(i * per_tile + d, _causal_mask(t, tq, d * t))

    lam_p = lam_ref[...]
    lam = (jnp.exp(jnp.sum(lam_p[0:1] * lam_p[1:2], axis=1, keepdims=True))
           - jnp.exp(jnp.sum(lam_p[2:3] * lam_p[3:4], axis=1, keepdims=True)) + lam_init)
    for h in range(HEADS):
        o = acc_ref[2 * h] / l_ref[2 * h] - lam * (acc_ref[2 * h + 1] / l_ref[2 * h + 1])
        o = o * lax.rsqrt(jnp.mean(o * o, axis=0, keepdims=True) + LN_EPS) * g_ref[...]
        o_ref[:, h * HEAD_DIM:(h + 1) * HEAD_DIM] = (o * (1.0 - lam_init)).T.astype(o_ref.dtype)


def _diff(p64, vt, lam_p, g_col, lam_init):
    b, s, _ = p64.shape
    t = min(ATT_T, s)
    tq = min(DIFF_TQ, s)
    nc = s // t
    return pl.pallas_call(
        functools.partial(_diff_kernel, lam_init=lam_init),
        grid=(b, s // tq),
        in_specs=[
            pl.BlockSpec((None, tq, MIX_W), lambda bb, i: (bb, i, 2)),
            pl.BlockSpec((None, s, MIX_W), lambda bb, i: (bb, 0, 3)),
            pl.BlockSpec((None, nc, MIX_W, t), lambda bb, i: (bb, 0, 0, 0)),
            pl.BlockSpec((4, DIFF_DIM), lambda bb, i: (0, 0)),
            pl.BlockSpec((HEAD_DIM, 1), lambda bb, i: (0, 0)),
        ],
        out_specs=pl.BlockSpec((None, tq, MIX_W), lambda bb, i: (bb, i, 0)),
        out_shape=jax.ShapeDtypeStruct((b, s, MIX_W), BF16),
        scratch_shapes=_softmax_scratch(2 * HEADS, tq),
        compiler_params=_params(("arbitrary", "arbitrary")),
    )(p64, p64, vt, lam_p, g_col)


def _kmean_kernel(k_ref, o_ref):
    n_blk = k_ref.shape[0] // MOBA_BLOCK
    o_ref[...] = jnp.zeros(o_ref.shape, o_ref.dtype)
    for n in range(n_blk):
        blk = k_ref[n * MOBA_BLOCK:(n + 1) * MOBA_BLOCK, :].astype(F32)
        o_ref[n:n + 1, :] = jnp.mean(blk, axis=0, keepdims=True)


def _kmean(p128):
    b, s, _ = p128.shape
    rows = max(SUBLANES, s // MOBA_BLOCK)
    return pl.pallas_call(
        _kmean_kernel,
        grid=(b,),
        in_specs=[pl.BlockSpec((None, s, MIX_W), lambda bb: (bb, 0, 2))],
        out_specs=pl.BlockSpec((None, rows, MIX_W), lambda bb: (bb, 0, 0)),
        out_shape=jax.ShapeDtypeStruct((b, rows, MIX_W), F32),
        compiler_params=_params(("arbitrary",)),
    )(p128)


def _moba_kernel(q_ref, k_ref, vt_ref, km_ref, o_ref, sel_ref, m_ref, l_ref, acc_ref):
    tq = q_ref.shape[0]
    t = vt_ref.shape[2]
    per_tile = tq // t
    n_blk = km_ref.shape[0]
    i = pl.program_id(1)
    blk = lax.broadcasted_iota(I32, (n_blk, 1), 0)
    own_rel = lax.broadcasted_iota(I32, (1, tq), 1) // t
    own = i * per_tile + own_rel
    _init_softmax(m_ref, l_ref, acc_ref)

    for h in range(HEADS):
        gate = _dot_nt(_head(km_ref, h).astype(BF16), _head(q_ref, h))
        gate = jnp.where(blk < own, gate, -jnp.inf)
        sel = jnp.zeros(gate.shape, F32)
        for _ in range(MOBA_TOPK):
            best = jnp.max(gate, axis=0, keepdims=True)
            first = jnp.min(jnp.where(gate == best, blk, n_blk), axis=0, keepdims=True)
            pick = (blk == first) & (best > -jnp.inf)
            sel = jnp.where(pick, 1.0, sel)
            gate = jnp.where(pick, -jnp.inf, gate)
        sel_ref[h] = sel

    def step(c, tile_block):
        k_c = k_ref[pl.ds(pl.multiple_of(c * t, t), t), :]
        v_c = vt_ref[c]
        scores = [_dot_nt(_head(k_c, h), _head(q_ref, h)) for h in range(HEADS)]
        values = [v_c[h * HEAD_DIM:(h + 1) * HEAD_DIM, :] for h in range(HEADS)]
        picked = [sel_ref[h, pl.ds(c, 1), :] > 0.0 for h in range(HEADS)]
        if tile_block is None:
            masks = picked
        else:
            causal = (own_rel == tile_block) & _causal_mask(t, tq, tile_block * t)
            masks = [causal | ((own_rel > tile_block) & p) for p in picked]
        _online_updates(scores, masks, values, m_ref, l_ref, acc_ref)

    def past_step(c, carry):
        step(c, None)
        return carry

    lax.fori_loop(0, i * per_tile, past_step, 0)
    for d in range(per_tile):
        step(i * per_tile + d, d)
    for h in range(HEADS):
        o_ref[:, h * HEAD_DIM:(h + 1) * HEAD_DIM] = (acc_ref[h] / l_ref[h]).T.astype(o_ref.dtype)


def _moba(p128, vt, kmean):
    b, s, _ = p128.shape
    t = MOBA_BLOCK
    tq = min(MOBA_TQ, s)
    nc = s // t
    rows = kmean.shape[1]
    return pl.pallas_call(
        _moba_kernel,
        grid=(b, s // tq),
        in_specs=[
            pl.BlockSpec((None, tq, MIX_W), lambda bb, i: (bb, i, 1)),
            pl.BlockSpec((None, s, MIX_W), lambda bb, i: (bb, 0, 2)),
            pl.BlockSpec((None, nc, MIX_W, t), lambda bb, i: (bb, 0, 1, 0)),
            pl.BlockSpec((None, rows, MIX_W), lambda bb, i: (bb, 0, 0)),
        ],
        out_specs=pl.BlockSpec((None, tq, MIX_W), lambda bb, i: (bb, i, 0)),
        out_shape=jax.ShapeDtypeStruct((b, s, MIX_W), BF16),
        scratch_shapes=[pltpu.VMEM((HEADS, rows, tq), F32)] + _softmax_scratch(HEADS, tq),
        compiler_params=_params(("arbitrary", "arbitrary")),
    )(p128, p128, vt, kmean)


def _dilated_kernel(q_ref, k_ref, vt_ref, o_ref, m_ref, l_ref, acc_ref):
    tq = q_ref.shape[0]
    t = vt_ref.shape[2]
    i = pl.program_id(1)
    _init_softmax(m_ref, l_ref, acc_ref)
    rel = lax.broadcasted_iota(I32, (t, tq), 1) - lax.broadcasted_iota(I32, (t, tq), 0)

    def first_chunk_within(window):
        return jnp.maximum(i * tq - window, 0) // t

    def step(c, patterns):
        k_c = k_ref[pl.ds(pl.multiple_of(c * t, t), t), :]
        v_c = vt_ref[c]
        dist = rel + (i * tq - c * t)
        mult = jnp.zeros((t, tq), F32)
        for window, dilation in patterns:
            hit = (dist >= 0) & (dist <= window) & ((dist & (dilation - 1)) == 0)
            mult = mult + jnp.where(hit, 1.0, 0.0)
        mask = mult > 0.0
        scores = [_dot_nt(_head(k_c, h), _head(q_ref, h)) for h in range(HEADS)]
        values = [v_c[h * HEAD_DIM:(h + 1) * HEAD_DIM, :] for h in range(HEADS)]
        _online_updates(scores, mask, values, m_ref, l_ref, acc_ref, weight=mult)

    widest = max(DILATED_PATTERNS)
    others = [p for p in DILATED_PATTERNS if p != widest]
    first = first_chunk_within(widest[0])
    near_first = first_chunk_within(max(w for w, _ in others))

    def far_step(c, carry):
        step(c, [widest])
        return carry

    def near_step(c, carry):
        step(c, DILATED_PATTERNS)
        return carry

    lax.fori_loop(first, near_first, far_step, 0)
    lax.fori_loop(near_first, (i + 1) * (tq // t), near_step, 0)
    for h in range(HEADS):
        o_ref[:, h * HEAD_DIM:(h + 1) * HEAD_DIM] = (acc_ref[h] / l_ref[h]).T.astype(o_ref.dtype)


def _dilated(p128, vt):
    b, s, _ = p128.shape
    t = min(ATT_T, s)
    tq = min(DIL_TQ, s)
    nc = s // t
    return pl.pallas_call(
        _dilated_kernel,
        grid=(b, s // tq),
        in_specs=[
            pl.BlockSpec((None, tq, MIX_W), lambda bb, i: (bb, i, 3)),
            pl.BlockSpec((None, s, MIX_W), lambda bb, i: (bb, 0, 4)),
            pl.BlockSpec((None, nc, MIX_W, t), lambda bb, i: (bb, 0, 2, 0)),
        ],
        out_specs=pl.BlockSpec((None, tq, MIX_W), lambda bb, i: (bb, i, 0)),
        out_shape=jax.ShapeDtypeStruct((b, s, MIX_W), BF16),
        scratch_shapes=_softmax_scratch(HEADS, tq),
        compiler_params=_params(("arbitrary", "arbitrary")),
    )(p128, p128, vt)


def _layer_norm(z, g, b):
    mu = jnp.mean(z, axis=1, keepdims=True)
    zc = z - mu
    var = jnp.mean(zc * zc, axis=1, keepdims=True)
    return zc * lax.rsqrt(var + LN_EPS) * g + b


def _outproj_ln_kernel(oa_ref, ob_ref, oc_ref, od_ref, w_ref, x_ref, g_ref, b_ref, xo_ref, xb_ref):
    y = jnp.dot(oa_ref[...], w_ref[0:MIX_W, :], preferred_element_type=F32)
    y = y + jnp.dot(ob_ref[...], w_ref[MIX_W:2 * MIX_W, :], preferred_element_type=F32)
    y = y + jnp.dot(oc_ref[...], w_ref[2 * MIX_W:3 * MIX_W, :], preferred_element_type=F32)
    y = y + jnp.dot(od_ref[...], w_ref[3 * MIX_W:4 * MIX_W, :], preferred_element_type=F32)
    out = _layer_norm(ALPHA * x_ref[...] + y, g_ref[...], b_ref[...])
    xo_ref[...] = out
    xb_ref[...] = out.astype(BF16)


def _outproj_ln(oa, ob, oc, od, w_bf, x, g, b):
    t, d = x.shape
    tm = min(OUT_TM, t)
    mix_spec = pl.BlockSpec((tm, MIX_W), lambda i: (i, 0))
    row_spec = pl.BlockSpec((tm, d), lambda i: (i, 0))
    vec_spec = pl.BlockSpec((1, d), lambda i: (0, 0))
    return pl.pallas_call(
        _outproj_ln_kernel,
        grid=(t // tm,),
        in_specs=[mix_spec, mix_spec, mix_spec, mix_spec,
                  pl.BlockSpec((d, d), lambda i: (0, 0)), row_spec, vec_spec, vec_spec],
        out_specs=[row_spec, row_spec],
        out_shape=[jax.ShapeDtypeStruct((t, d), F32), jax.ShapeDtypeStruct((t, d), BF16)],
        compiler_params=_params(("arbitrary",)),
    )(oa, ob, oc, od, w_bf, x, g, b)


def _first_argmax(vals):
    best, arg = vals[0], jnp.zeros(vals[0].shape, I32)
    for j in range(1, len(vals)):
        better = vals[j] > best
        arg = jnp.where(better, j, arg)
        best = jnp.where(better, vals[j], best)
    return arg, best


def _pick(rows, idx):
    out = rows[0]
    for j in range(1, len(rows)):
        out = jnp.where(idx == j, rows[j], out)
    return out


def _router_kernel(x_ref, rwt_ref, bias_ref, e_ref, gate_ref, rank_ref, cnt_ref, carry_ref):
    tm = x_ref.shape[0]

    @pl.when(pl.program_id(0) == 0)
    def _():
        carry_ref[...] = jnp.zeros(carry_ref.shape, F32)

    logits = lax.dot_general(rwt_ref[...], x_ref[...], (((1,), (1,)), ((), ())),
                             precision=lax.Precision.HIGHEST, preferred_element_type=F32)
    aff = 1.0 / (1.0 + jnp.exp(-logits))
    biased = aff + bias_ref[...]
    sel_rows = [biased[r:r + 1, :] for r in range(N_EXPERTS)]
    aff_rows = [aff[r:r + 1, :] for r in range(N_EXPERTS)]

    group_scores = []
    for g in range(N_GROUPS):
        r = sel_rows[g * EXPERTS_PER_GROUP:(g + 1) * EXPERTS_PER_GROUP]
        best_pair = r[0] + r[1]
        for a in range(EXPERTS_PER_GROUP):
            for c in range(a + 1, EXPERTS_PER_GROUP):
                if (a, c) != (0, 1):
                    best_pair = jnp.maximum(best_pair, r[a] + r[c])
        group_scores.append(best_pair)
    grp, _ = _first_argmax(group_scores)

    in_sel = [_pick([sel_rows[g * EXPERTS_PER_GROUP + j] for g in range(N_GROUPS)], grp)
              for j in range(EXPERTS_PER_GROUP)]
    in_aff = [_pick([aff_rows[g * EXPERTS_PER_GROUP + j] for g in range(N_GROUPS)], grp)
              for j in range(EXPERTS_PER_GROUP)]
    first, _ = _first_argmax(in_sel)
    second, _ = _first_argmax([jnp.where(first == j, -jnp.inf, in_sel[j]) for j in range(EXPERTS_PER_GROUP)])
    a0 = _pick(in_aff, first)
    a1 = _pick(in_aff, second)
    e0 = grp * EXPERTS_PER_GROUP + first
    e1 = grp * EXPERTS_PER_GROUP + second
    e_ref[0:1, :] = e0
    e_ref[1:2, :] = e1
    gate_ref[0:1, :] = a0 / (a0 + a1)
    gate_ref[1:2, :] = a1 / (a0 + a1)

    e_iota = lax.broadcasted_iota(I32, (N_EXPERTS, tm), 0)
    one_hot = ((e_iota == e0) | (e_iota == e1)).astype(BF16)
    before = (lax.broadcasted_iota(I32, (tm, tm), 0) < lax.broadcasted_iota(I32, (tm, tm), 1)).astype(BF16)
    prior = jnp.dot(one_hot, before, preferred_element_type=F32) + carry_ref[...]
    rank_ref[0:1, :] = jnp.sum(jnp.where(e_iota == e0, prior, 0.0), axis=0, keepdims=True).astype(I32)
    rank_ref[1:2, :] = jnp.sum(jnp.where(e_iota == e1, prior, 0.0), axis=0, keepdims=True).astype(I32)
    carry_ref[...] = carry_ref[...] + jnp.sum(one_hot.astype(F32), axis=1, keepdims=True)
    cnt_ref[...] = jnp.broadcast_to(carry_ref[...], cnt_ref.shape).astype(I32)


def _router(x, rw_t, bias):
    t, d = x.shape
    tm = min(ROUTER_TM, t)
    tok_spec = pl.BlockSpec((2, tm), lambda i: (0, i))
    return pl.pallas_call(
        _router_kernel,
        grid=(t // tm,),
        in_specs=[pl.BlockSpec((tm, d), lambda i: (i, 0)),
                  pl.BlockSpec((N_EXPERTS, d), lambda i: (0, 0)),
                  pl.BlockSpec((N_EXPERTS, 1), lambda i: (0, 0))],
        out_specs=[tok_spec, tok_spec, tok_spec, pl.BlockSpec((N_EXPERTS, LANES), lambda i: (0, 0))],
        out_shape=[jax.ShapeDtypeStruct((2, t), I32), jax.ShapeDtypeStruct((2, t), F32),
                   jax.ShapeDtypeStruct((2, t), I32), jax.ShapeDtypeStruct((N_EXPERTS, LANES), I32)],
        scratch_shapes=[pltpu.VMEM((N_EXPERTS, 1), F32)],
        compiler_params=_params(("arbitrary",)),
    )(x, rw_t, bias)


def _row_copy(src, src_row, dst, dst_row, sem):
    return pltpu.make_async_copy(src.at[pl.ds(src_row, 1)], dst.at[pl.ds(dst_row, 1)], sem)


def _dispatch_kernel(dest_ref, pend_ref, na_ref, x_ref, slots_hbm, zeros_ref, sem, zero_sem,
                     *, tm, n_tok, tb):
    base = pl.program_id(0) * tm

    @pl.when(pl.program_id(0) == 0)
    def _():
        zeros_ref[...] = jnp.zeros(zeros_ref.shape, zeros_ref.dtype)
        n_blocks = slots_hbm.shape[0] // tb

        def block_fill(start):
            return pltpu.make_async_copy(zeros_ref, slots_hbm.at[pl.ds(start, tb)], zero_sem)

        def for_each_fill(act):
            for e in range(N_EXPERTS):
                seg_start = pend_ref[e - 1] if e else 0

                @pl.when(pend_ref[e] > seg_start)
                def _():
                    act(pl.multiple_of(pend_ref[e] - tb, tb))

            def tail(n, carry):
                act(pl.multiple_of(n * tb, tb))
                return carry

            lax.fori_loop(na_ref[0], n_blocks, tail, 0)

        for_each_fill(lambda start: block_fill(start).start())
        for_each_fill(lambda start: block_fill(start).wait())

    def issue(r, carry):
        tok = base + r
        _row_copy(x_ref, r, slots_hbm, dest_ref[tok], sem).start()
        _row_copy(x_ref, r, slots_hbm, dest_ref[n_tok + tok], sem).start()
        return carry

    def drain(r, carry):
        _row_copy(x_ref, 0, slots_hbm, 0, sem).wait()
        _row_copy(x_ref, 0, slots_hbm, 0, sem).wait()
        return carry

    lax.fori_loop(0, tm, issue, 0, unroll=DMA_UNROLL)
    lax.fori_loop(0, tm, drain, 0, unroll=DMA_UNROLL)


def _dispatch(dest_flat, pend, n_active, x, n_slots):
    t, d = x.shape
    tm = min(DISPATCH_TM, t)
    tb = EXPERT_TB
    return pl.pallas_call(
        functools.partial(_dispatch_kernel, tm=tm, n_tok=t, tb=tb),
        grid_spec=pltpu.PrefetchScalarGridSpec(
            num_scalar_prefetch=3,
            grid=(t // tm,),
            in_specs=[pl.BlockSpec((tm, d), lambda i, dest, pend, na: (i, 0))],
            out_specs=pl.BlockSpec(memory_space=pl.ANY),
            scratch_shapes=[pltpu.VMEM((tb, d), x.dtype), pltpu.SemaphoreType.DMA(()),
                            pltpu.SemaphoreType.DMA(())],
        ),
        out_shape=jax.ShapeDtypeStruct((n_slots, d), x.dtype),
        compiler_params=_params(("arbitrary",)),
    )(dest_flat, pend, n_active, x)


def _last_active(n, na_ref):
    return jnp.maximum(jnp.minimum(n, na_ref[0] - 1), 0)


def _expert_stage(be_ref, na_ref, w_refs, w_bf_refs, compute, out_ref):
    n = pl.program_id(0)
    active = n < na_ref[0]
    new_expert = (n == 0) | (be_ref[n] != be_ref[jnp.maximum(n - 1, 0)])

    @pl.when(active & new_expert)
    def _():
        for w_ref, w_bf_ref in zip(w_refs, w_bf_refs):
            w_bf_ref[...] = w_ref[...].astype(BF16)

    @pl.when(active)
    def _():
        out_ref[...] = compute().astype(out_ref.dtype)

    @pl.when(jnp.logical_not(active))
    def _():
        out_ref[...] = jnp.zeros(out_ref.shape, out_ref.dtype)


def _expert_up_kernel(be_ref, na_ref, xs_ref, wg_ref, wu_ref, h_ref, wg_bf, wu_bf):
    def compute():
        xb = xs_ref[...].astype(BF16)
        gate = jnp.dot(xb, wg_bf[...], preferred_element_type=F32)
        up = jnp.dot(xb, wu_bf[...], preferred_element_type=F32)
        return gate * (1.0 / (1.0 + jnp.exp(-gate))) * up

    _expert_stage(be_ref, na_ref, (wg_ref, wu_ref), (wg_bf, wu_bf), compute, h_ref)


def _expert_down_kernel(be_ref, na_ref, h_ref, wd_ref, ys_ref, wd_bf):
    def compute():
        return jnp.dot(h_ref[...], wd_bf[...], preferred_element_type=F32)

    _expert_stage(be_ref, na_ref, (wd_ref,), (wd_bf,), compute, ys_ref)


def _experts(block_expert, n_active, xs, wg, wu, wd, layer):
    n_slots, d = xs.shape
    tb = EXPERT_TB
    f = wg.shape[3]

    def row_map(n, be, na):
        return (_last_active(n, na), 0)

    def w_map(n, be, na):
        return (layer, be[_last_active(n, na)], 0, 0)

    def out_map(n, be, na):
        return (n, 0)

    hidden = pl.pallas_call(
        _expert_up_kernel,
        grid_spec=pltpu.PrefetchScalarGridSpec(
            num_scalar_prefetch=2,
            grid=(n_slots // tb,),
            in_specs=[pl.BlockSpec((tb, d), row_map),
                      pl.BlockSpec((None, None, d, f), w_map),
                      pl.BlockSpec((None, None, d, f), w_map)],
            out_specs=pl.BlockSpec((tb, f), out_map),
            scratch_shapes=[pltpu.VMEM((d, f), BF16), pltpu.VMEM((d, f), BF16)],
        ),
        out_shape=jax.ShapeDtypeStruct((n_slots, f), BF16),
        compiler_params=_params(("arbitrary",)),
    )(block_expert, n_active, xs, wg, wu)
    return pl.pallas_call(
        _expert_down_kernel,
        grid_spec=pltpu.PrefetchScalarGridSpec(
            num_scalar_prefetch=2,
            grid=(n_slots // tb,),
            in_specs=[pl.BlockSpec((tb, f), row_map),
                      pl.BlockSpec((None, None, f, d), w_map)],
            out_specs=pl.BlockSpec((tb, d), out_map),
            scratch_shapes=[pltpu.VMEM((f, d), BF16)],
        ),
        out_shape=jax.ShapeDtypeStruct((n_slots, d), F32),
        compiler_params=_params(("arbitrary",)),
    )(block_expert, n_active, hidden, wd)


def _combine_ln_kernel(dest_ref, x_ref, gate_ref, g_ref, b_ref, ys_hbm, xo_ref, xb_ref, buf_ref, sem,
                       *, n_tok):
    tm = x_ref.shape[0]
    i = pl.program_id(0)
    slot = i % 2

    def request(tile, to_slot):
        def issue(r, carry):
            tok = tile * tm + r
            _row_copy(ys_hbm, dest_ref[tok], buf_ref.at[to_slot, 0], r, sem.at[to_slot]).start()
            _row_copy(ys_hbm, dest_ref[n_tok + tok], buf_ref.at[to_slot, 1], r, sem.at[to_slot]).start()
            return carry
        lax.fori_loop(0, tm, issue, 0, unroll=DMA_UNROLL)

    @pl.when(i == 0)
    def _():
        request(0, 0)

    @pl.when(i + 1 < pl.num_programs(0))
    def _():
        request(i + 1, 1 - slot)

    def drain(r, carry):
        _row_copy(ys_hbm, 0, buf_ref.at[slot, 0], 0, sem.at[slot]).wait()
        _row_copy(ys_hbm, 0, buf_ref.at[slot, 1], 0, sem.at[slot]).wait()
        return carry

    lax.fori_loop(0, tm, drain, 0, unroll=DMA_UNROLL)
    gate = gate_ref[...]
    moe = buf_ref[slot, 0] * gate[:, 0:1] + buf_ref[slot, 1] * gate[:, 1:2]
    out = _layer_norm(ALPHA * x_ref[...] + moe, g_ref[...], b_ref[...])
    xo_ref[...] = out
    xb_ref[...] = out.astype(BF16)


def _combine_ln(dest_flat, x, gate_t, g, b, ys):
    t, d = x.shape
    tm = min(COMBINE_TM, t)
    row_spec = pl.BlockSpec((tm, d), lambda i, dest: (i, 0))
    vec_spec = pl.BlockSpec((1, d), lambda i, dest: (0, 0))
    return pl.pallas_call(
        functools.partial(_combine_ln_kernel, n_tok=t),
        grid_spec=pltpu.PrefetchScalarGridSpec(
            num_scalar_prefetch=1,
            grid=(t // tm,),
            in_specs=[row_spec, pl.BlockSpec((tm, 2), lambda i, dest: (i, 0)), vec_spec, vec_spec,
                      pl.BlockSpec(memory_space=pl.ANY)],
            out_specs=[row_spec, row_spec],
            scratch_shapes=[pltpu.VMEM((2, 2, tm, d), F32), pltpu.SemaphoreType.DMA((2,))],
        ),
        out_shape=[jax.ShapeDtypeStruct((t, d), F32), jax.ShapeDtypeStruct((t, d), BF16)],
        compiler_params=_params(("arbitrary",)),
    )(dest_flat, x, gate_t, g, b, ys)


def _rope_tables(positions):
    pos = positions.reshape(-1).astype(F32)[:, None]

    def cos_sin(d):
        inv = jnp.power(ROPE_THETA, -jnp.arange(0, d, 2, dtype=F32) / d)
        ang = pos * inv
        return jnp.cos(ang), jnp.sin(ang)

    c, s = cos_sin(HEAD_DIM)
    t128 = (jnp.concatenate([c, c], axis=1), jnp.concatenate([-s, s], axis=1))
    c, s = cos_sin(IDX_DIM)
    z = jnp.zeros_like(s)
    t64 = (jnp.concatenate([c, c, c, c], axis=1),
           jnp.concatenate([-s, z, -s, z], axis=1),
           jnp.concatenate([z, s, z, s], axis=1))
    return t128, t64


def _col_scales():
    q128 = HEAD_DIM ** -0.5 * LOG2E
    s128 = np.ones((1, 5 * MIX_W + HEAD_DIM), np.float32)
    s128[:, 0:2 * MIX_W] = q128
    s128[:, 3 * MIX_W:4 * MIX_W] = q128
    s64 = np.ones((1, 2304), np.float32)
    s64[:, 1024:1024 + MIX_W] = DIFF_DIM ** -0.5 * LOG2E
    return jnp.asarray(s128), jnp.asarray(s64)


def _split_w_in(w):
    def cols(name):
        lo, hi = _OFF[name]
        return w[:, lo:hi]

    d = w.shape[0]
    w128 = jnp.concatenate([cols(n) for n in ("a_q", "c_q", "c_k", "d_q", "d_k", "a_k")], axis=1)
    w64 = jnp.concatenate([cols("i_q"), cols("b_q"), cols("b_k"), cols("i_k"), cols("i_k"),
                           jnp.zeros((d, LANES), w.dtype)], axis=1)
    wt = jnp.concatenate([cols("b_v"), cols("c_v"), cols("d_v"), cols("a_v"), cols("i_w")], axis=1).T
    return w128.astype(BF16), w64.astype(BF16), wt.astype(BF16)


def _moe_layout(e, rank, counts, tb):
    t = e.shape[1]
    n_slots = 2 * t + N_EXPERTS * tb
    padded = (counts + tb - 1) // tb * tb
    pend = jnp.cumsum(padded)
    pstart = pend - padded
    experts = jnp.arange(N_EXPERTS, dtype=I32)[:, None, None]
    seg_start = jnp.sum(jnp.where(e[None] == experts, pstart[:, None, None], 0), axis=0)
    dest = (seg_start + rank).astype(I32).reshape(-1)
    blocks = jnp.arange(n_slots // tb, dtype=I32) * tb
    block_expert = jnp.minimum(jnp.sum(blocks[:, None] >= pend[None, :], axis=1), N_EXPERTS - 1).astype(I32)
    n_active = (pend[-1:] // tb).astype(I32)
    return dest, pend.astype(I32), block_expert, n_active, n_slots


def kernel(x, positions, w_in, w_out, diff_lambda, diff_norm_g, ln_mix_g, ln_mix_b, router_w, router_bias,
           w_gate, w_up, w_down, ln_ffn_g, ln_ffn_b):
    b, s, d = x.shape
    t = b * s
    n_keep = min(DSA_TOPK_MAX, s // 4)
    t128, t64 = _rope_tables(positions)
    s128, s64 = _col_scales()
    rw_t = router_w.T
    bias = router_bias.reshape(N_EXPERTS, 1).astype(F32)
    xf = x.reshape(t, d)
    xb = xf
    nc = s // min(ATT_T, s)
    for layer in range(DEPTH):
        w128, w64, wt = _split_w_in(w_in[layer])
        p128 = _proj(xb, w128, s128, t128, "rope128", 896).reshape(b, s, -1)
        p64 = _proj(xb, w64, s64, t64, "rope64", 768).reshape(b, s, -1)
        vt, iwt = _proj_t(xb, wt)
        vt = vt.reshape(b, nc, V_ROWS, -1)
        iwt = iwt.reshape(b, nc, IDX_HEADS, -1)

        lam_init = 0.8 - 0.6 * math.exp(-0.3 * layer)
        o_a = _dsa(p128, p64, vt, iwt, n_keep)
        o_b = _diff(p64, vt, diff_lambda[layer], diff_norm_g[layer].reshape(HEAD_DIM, 1), lam_init)
        o_c = _moba(p128, vt, _kmean(p128))
        o_d = _dilated(p128, vt)
        xf, xb = _outproj_ln(o_a.reshape(t, MIX_W), o_b.reshape(t, MIX_W), o_c.reshape(t, MIX_W),
                             o_d.reshape(t, MIX_W), w_out[layer].astype(BF16), xf,
                             ln_mix_g[layer].reshape(1, d), ln_mix_b[layer].reshape(1, d))

        e, gate, rank, counts = _router(xf, rw_t, bias)
        dest, pend, block_expert, n_active, n_slots = _moe_layout(e, rank, counts[:, 0], EXPERT_TB)
        xs = _dispatch(dest, pend, n_active, xf, n_slots)
        ys = _experts(block_expert, n_active, xs, w_gate, w_up, w_down, layer)
        xf, xb = _combine_ln(dest, xf, gate.T, ln_ffn_g[layer].reshape(1, d), ln_ffn_b[layer].reshape(1, d), ys)
    return xf.reshape(b, s, d)
```

```python
import functools
import math

import numpy as np
import jax
import jax.numpy as jnp
from jax import lax
from jax.experimental import pallas as pl
from jax.experimental.pallas import tpu as pltpu

F32 = jnp.float32
BF16 = jnp.bfloat16
I32 = jnp.int32

D_MODEL = 2048
HEAD_DIM = 128
HEADS = 4
MIX_W = HEADS * HEAD_DIM
IDX_HEADS = 16
IDX_DIM = 64
DSA_TOPK_MAX = 256
DIFF_DIM = 64
MOBA_BLOCK = 256
MOBA_TOPK = 3
DILATED_PATTERNS = ((128, 1), (512, 4), (2048, 16))
ROPE_THETA = 10000.0
N_EXPERTS = 16
N_GROUPS = 4
EXPERTS_PER_GROUP = 4
LN_EPS = 1e-5
DEPTH = 2
ALPHA = (2 * DEPTH) ** 0.25
LOG2E = math.log2(math.e)

LANES = 128
SUBLANES = 8
NEG = -1e30
M_FLOOR = -1e29
INT_MIN = -(2 ** 31)
NEG_INF_ORDER = 0x007FFFFF
VMEM_LIMIT = 56 * 1024 * 1024

PROJ_TM = 512
PROJ_PARTS = 4
ATT_T = 256
DIFF_TQ = 512
DSA_TQ = 512
MOBA_TQ = 512
DIL_TQ = 512
OUT_TM = 512
OUT_PARTS = 2
EXPERT_PARTS = 2
ROUTER_TM = 256
DISPATCH_TM = 256
EXPERT_TB = 512
COMBINE_TM = 256
DMA_UNROLL = 8
V_ROWS = 3 * MIX_W + HEAD_DIM

_OFF = {}
_o = 0
for _name, _n in (("a_q", 512), ("a_k", 128), ("a_v", 128), ("i_q", 1024), ("i_k", 64), ("i_w", 16),
                  ("b_q", 512), ("b_k", 512), ("b_v", 512), ("c_q", 512), ("c_k", 512), ("c_v", 512),
                  ("d_q", 512), ("d_k", 512), ("d_v", 512)):
    _OFF[_name] = (_o, _o + _n)
    _o += _n


def _params(sem=None):
    return pltpu.CompilerParams(dimension_semantics=sem, vmem_limit_bytes=VMEM_LIMIT)


def _dot_nt(a, b):
    return lax.dot_general(a, b, (((1,), (1,)), ((), ())), preferred_element_type=F32)


def _proj_kernel(x_ref, w_ref, scale_ref, *rest, mode):
    o_ref = rest[-1]
    tm = x_ref.shape[0]
    rows_per_part = tm // PROJ_PARTS
    n_groups = w_ref.shape[1] // LANES
    for part in range(PROJ_PARTS):
        rows = slice(part * rows_per_part, (part + 1) * rows_per_part)
        acc = jnp.dot(x_ref[rows, :].astype(BF16), w_ref[...], preferred_element_type=F32)
        for g in range(n_groups):
            cols = slice(g * LANES, (g + 1) * LANES)
            y = acc[:, cols]
            if mode == "rope128":
                cos_ref, sin_ref = rest[0], rest[1]
                r = y * cos_ref[rows, :] + pltpu.roll(y, 64, 1) * sin_ref[rows, :]
            else:
                cos_ref, sina_ref, sinb_ref = rest[0], rest[1], rest[2]
                r = (y * cos_ref[rows, :] + pltpu.roll(y, 96, 1) * sina_ref[rows, :]
                     + pltpu.roll(y, 32, 1) * sinb_ref[rows, :])
            r = r * scale_ref[:, cols]
            o_ref[rows, cols] = r.astype(o_ref.dtype)


def _proj(x_bf, w_bf, col_scale, tables, mode, tn):
    t, d = x_bf.shape
    n = w_bf.shape[1]
    tm = min(PROJ_TM, t)
    in_specs = [pl.BlockSpec((tm, d), lambda j, i: (i, 0)),
                pl.BlockSpec((d, tn), lambda j, i: (0, j)),
                pl.BlockSpec((1, tn), lambda j, i: (0, j))]
    in_specs += [pl.BlockSpec((tm, LANES), lambda j, i: (i, 0)) for _ in tables]
    return pl.pallas_call(
        functools.partial(_proj_kernel, mode=mode),
        grid=(n // tn, t // tm),
        in_specs=in_specs,
        out_specs=pl.BlockSpec((tm, tn), lambda j, i: (i, j)),
        out_shape=jax.ShapeDtypeStruct((t, n), BF16),
        compiler_params=_params(("arbitrary", "arbitrary")),
    )(x_bf, w_bf, col_scale, *tables)


def _proj_t_kernel(x_ref, wt_ref, vt_ref, iwt_ref):
    acc = _dot_nt(wt_ref[...], x_ref[...].astype(BF16))
    vt_ref[...] = acc[:V_ROWS].astype(vt_ref.dtype)
    iwt_ref[...] = acc[V_ROWS:]


def _proj_t(x_bf, wt_bf):
    t, d = x_bf.shape
    tm = min(ATT_T, t)
    rows = wt_bf.shape[0]
    return pl.pallas_call(
        _proj_t_kernel,
        grid=(t // tm,),
        in_specs=[pl.BlockSpec((tm, d), lambda i: (i, 0)),
                  pl.BlockSpec((rows, d), lambda i: (0, 0))],
        out_specs=[pl.BlockSpec((None, V_ROWS, tm), lambda i: (i, 0, 0)),
                   pl.BlockSpec((None, IDX_HEADS, tm), lambda i: (i, 0, 0))],
        out_shape=[jax.ShapeDtypeStruct((t // tm, V_ROWS, tm), BF16),
                   jax.ShapeDtypeStruct((t // tm, IDX_HEADS, tm), F32)],
        compiler_params=_params(("arbitrary",)),
    )(x_bf, wt_bf)


def _online_updates(scores, mask, values, m_ref, l_ref, acc_ref, weight=None):
    probs, alphas = [], []
    masks = mask if isinstance(mask, list) else [mask] * len(scores)
    for idx, (s_t, mask) in enumerate(zip(scores, masks)):
        if mask is not None:
            s_t = jnp.where(mask, s_t, NEG)
        m_prev = m_ref[idx]
        m_new = jnp.maximum(m_prev, jnp.max(s_t, axis=0, keepdims=True))
        p = jnp.exp2(s_t - m_new)
        if weight is not None:
            p = p * weight
        alpha = jnp.exp2(m_prev - m_new)
        l_ref[idx] = alpha * l_ref[idx] + jnp.sum(p, axis=0, keepdims=True)
        m_ref[idx] = m_new
        probs.append(p.astype(BF16))
        alphas.append(alpha)
    for idx, (p, alpha) in enumerate(zip(probs, alphas)):
        acc_ref[idx] = alpha * acc_ref[idx] + jnp.dot(values[idx], p, preferred_element_type=F32)


def _init_softmax(m_ref, l_ref, acc_ref):
    m_ref[...] = jnp.full(m_ref.shape, M_FLOOR, F32)
    l_ref[...] = jnp.zeros(l_ref.shape, F32)
    acc_ref[...] = jnp.zeros(acc_ref.shape, F32)


def _softmax_scratch(n, t):
    return [pltpu.VMEM((n, 1, t), F32), pltpu.VMEM((n, 1, t), F32), pltpu.VMEM((n, HEAD_DIM, t), F32)]


def _causal_mask(tk, tq, k_start):
    return k_start + lax.broadcasted_iota(I32, (tk, tq), 0) <= lax.broadcasted_iota(I32, (tk, tq), 1)


def _head(ref_or_val, h):
    return ref_or_val[:, h * HEAD_DIM:(h + 1) * HEAD_DIM]


def _order_to_f32(u):
    key = u ^ jnp.int32(INT_MIN)
    bits = jnp.where(key < 0, key ^ jnp.int32(0x7FFFFFFF), key)
    return lax.bitcast_convert_type(bits, F32)


def _dsa_kernel(aq_ref, ak_ref, avt_ref, iq_ref, ik_ref, iwt_ref, o_ref,
                score_ref, m_ref, l_ref, acc_ref, jlim_ref, *, n_keep):
    tq = aq_ref.shape[0]
    t = avt_ref.shape[2]
    i = pl.program_id(1)
    n_chunks = (i + 1) * (tq // t)
    lane = lax.broadcasted_iota(I32, (1, LANES), 1)
    k_off = lax.broadcasted_iota(I32, (t, 1), 0)
    q_pos = i * tq + lax.broadcasted_iota(I32, (1, tq), 1)
    iw_t = jnp.concatenate([iwt_ref[n] for n in range(tq // t)], axis=1)

    def score_chunk(c, carry):
        ik_c = ik_ref[pl.ds(pl.multiple_of(c * t, t), t), :]
        ik_lo = jnp.where(lane < IDX_DIM, ik_c, jnp.zeros_like(ik_c))
        ik_hi = jnp.where(lane >= IDX_DIM, ik_c, jnp.zeros_like(ik_c))
        score = jnp.zeros((t, tq), F32)
        for j in range(IDX_HEADS // 2):
            q_pair = iq_ref[:, j * LANES:(j + 1) * LANES]
            s_even = jnp.maximum(_dot_nt(ik_lo, q_pair), 0.0)
            s_odd = jnp.maximum(_dot_nt(ik_hi, q_pair), 0.0)
            score = score + s_even * iw_t[2 * j:2 * j + 1, :] + s_odd * iw_t[2 * j + 1:2 * j + 2, :]
        score_ref[c] = jnp.where(c * t + k_off <= q_pos, score, -jnp.inf)
        return carry

    lax.fori_loop(0, n_chunks, score_chunk, 0)

    def count(pred):
        def body(c, part):
            hit = jnp.where(pred(score_ref[c], c), 1, 0)
            return part + jnp.sum(hit.reshape(t // SUBLANES, SUBLANES, tq), axis=0)
        part = lax.fori_loop(0, n_chunks, body, jnp.zeros((SUBLANES, tq), I32))
        return jnp.sum(part, axis=0, keepdims=True)

    def bisect(it, thr_u):
        cand_u = thr_u | jnp.left_shift(jnp.int32(1), 31 - it)
        cand = _order_to_f32(cand_u)
        cnt = count(lambda sc, c: sc >= cand)
        return jnp.where(cnt >= n_keep, cand_u, thr_u)

    thr_u = lax.fori_loop(0, 32, bisect, jnp.zeros((1, tq), I32))
    below = (thr_u >= 0) & (thr_u < NEG_INF_ORDER)
    thr = jnp.where(below, -jnp.inf, _order_to_f32(thr_u))

    n_gt = count(lambda sc, c: sc > thr)
    n_eq = count(lambda sc, c: sc == thr)
    need = n_keep - n_gt
    s_total = score_ref.shape[0] * t
    jlim_ref[...] = jnp.full((1, tq), s_total, I32)
    excess = jnp.max(jnp.where((n_eq > need) & (thr > -jnp.inf), 1, 0))

    @pl.when(excess > 0)
    def _():
        n_bits = max(1, (s_total - 1).bit_length())

        def bisect_idx(it, j):
            cand = j | jnp.left_shift(jnp.int32(1), n_bits - 1 - it)
            cnt = count(lambda sc, c: (sc == thr) & (c * t + k_off < cand))
            return jnp.where(cnt < need, cand, j)

        jlim_ref[...] = lax.fori_loop(0, n_bits, bisect_idx, jnp.zeros((1, tq), I32))

    _init_softmax(m_ref, l_ref, acc_ref)
    jlim = jnp.where(thr > -jnp.inf, jlim_ref[...], -1)
    thr_all = jnp.where(thr > -jnp.inf, thr, jnp.finfo(F32).min)

    def attend(select):
        def body(c, carry):
            start = pl.multiple_of(c * t, t)
            k_c = ak_ref[pl.ds(start, t), :]
            scores = [_dot_nt(k_c, _head(aq_ref, h)) for h in range(HEADS)]
            sel = select(score_ref[c], c * t + k_off)
            _online_updates(scores, sel, [avt_ref[c]] * HEADS, m_ref, l_ref, acc_ref)
            return carry
        lax.fori_loop(0, n_chunks, body, 0)

    @pl.when(excess > 0)
    def _():
        attend(lambda sc, k_pos: (sc > thr) | ((sc == thr) & (k_pos <= jlim)))

    @pl.when(excess <= 0)
    def _():
        attend(lambda sc, k_pos: sc >= thr_all)
    for h in range(HEADS):
        o_ref[:, h * HEAD_DIM:(h + 1) * HEAD_DIM] = (acc_ref[h] / l_ref[h]).T.astype(o_ref.dtype)


def _dsa(p128, p64, vt, iwt, n_keep):
    b, s, _ = p128.shape
    t = min(ATT_T, s)
    tq = min(DSA_TQ, s)
    nc = s // t
    return pl.pallas_call(
        functools.partial(_dsa_kernel, n_keep=n_keep),
        grid=(b, s // tq),
        in_specs=[
            pl.BlockSpec((None, tq, MIX_W), lambda bb, i: (bb, i, 0)),
            pl.BlockSpec((None, s, HEAD_DIM), lambda bb, i: (bb, 0, 20)),
            pl.BlockSpec((None, nc, HEAD_DIM, t), lambda bb, i: (bb, 0, 12, 0)),
            pl.BlockSpec((None, tq, 1024), lambda bb, i: (bb, i, 0)),
            pl.BlockSpec((None, s, LANES), lambda bb, i: (bb, 0, 16)),
            pl.BlockSpec((None, tq // t, IDX_HEADS, t), lambda bb, i: (bb, i, 0, 0)),
        ],
        out_specs=pl.BlockSpec((None, tq, MIX_W), lambda bb, i: (bb, i, 0)),
        out_shape=jax.ShapeDtypeStruct((b, s, MIX_W), BF16),
        scratch_shapes=[pltpu.VMEM((nc, t, tq), F32)] + _softmax_scratch(HEADS, tq)
        + [pltpu.VMEM((1, tq), I32)],
        compiler_params=_params(("arbitrary", "arbitrary")),
    )(p128, p128, vt, p64, p64, iwt)


def _diff_kernel(q_ref, k_ref, vt_ref, lam_ref, g_ref, o_ref, m_ref, l_ref, acc_ref, *, lam_init):
    tq = q_ref.shape[0]
    t = vt_ref.shape[2]
    per_tile = tq // t
    i = pl.program_id(1)
    lane = lax.broadcasted_iota(I32, (1, LANES), 1)
    _init_softmax(m_ref, l_ref, acc_ref)

    def step(c, mask):
        k_c = k_ref[pl.ds(pl.multiple_of(c * t, t), t), :]
        v_c = vt_ref[c]
        scores, values = [], []
        for h in range(HEADS):
            q_h = _head(q_ref, h)
            k_h = _head(k_c, h)
            q_1 = jnp.where(lane < DIFF_DIM, q_h, jnp.zeros_like(q_h))
            q_2 = jnp.where(lane >= DIFF_DIM, q_h, jnp.zeros_like(q_h))
            scores += [_dot_nt(k_h, q_1), _dot_nt(k_h, q_2)]
            values += [v_c[h * HEAD_DIM:(h + 1) * HEAD_DIM, :]] * 2
        _online_updates(scores, mask, values, m_ref, l_ref, acc_ref)

    def full_step(c, carry):
        step(c, None)
        return carry

    lax.fori_loop(0, i * per_tile, full_step, 0)
    for d in range(per_tile):
        step(i * per_tile + d, _causal_mask(t, tq, d * t))

    lam_p = lam_ref[...]
    lam = (jnp.exp(jnp.sum(lam_p[0:1] * lam_p[1:2], axis=1, keepdims=True))
           - jnp.exp(jnp.sum(lam_p[2:3] * lam_p[3:4], axis=1, keepdims=True)) + lam_init)
    for h in range(HEADS):
        o = acc_ref[2 * h] / l_ref[2 * h] - lam * (acc_ref[2 * h + 1] / l_ref[2 * h + 1])
        o = o * lax.rsqrt(jnp.mean(o * o, axis=0, keepdims=True) + LN_EPS) * g_ref[...]
        o_ref[:, h * HEAD_DIM:(h + 1) * HEAD_DIM] = (o * (1.0 - lam_init)).T.astype(o_ref.dtype)


def _diff(p64, vt, lam_p, g_col, lam_init):
    b, s, _ = p64.shape
    t = min(ATT_T, s)
    tq = min(DIFF_TQ, s)
    nc = s // t
    return pl.pallas_call(
        functools.partial(_diff_kernel, lam_init=lam_init),
        grid=(b, s // tq),
        in_specs=[
            pl.BlockSpec((None, tq, MIX_W), lambda bb, i: (bb, i, 2)),
            pl.BlockSpec((None, s, MIX_W), lambda bb, i: (bb, 0, 3)),
            pl.BlockSpec((None, nc, MIX_W, t), lambda bb, i: (bb, 0, 0, 0)),
            pl.BlockSpec((4, DIFF_DIM), lambda bb, i: (0, 0)),
            pl.BlockSpec((HEAD_DIM, 1), lambda bb, i: (0, 0)),
        ],
        out_specs=pl.BlockSpec((None, tq, MIX_W), lambda bb, i: (bb, i, 0)),
        out_shape=jax.ShapeDtypeStruct((b, s, MIX_W), BF16),
        scratch_shapes=_softmax_scratch(2 * HEADS, tq),
        compiler_params=_params(("arbitrary", "arbitrary")),
    )(p64, p64, vt, lam_p, g_col)


def _kmean_kernel(k_ref, o_ref):
    n_blk = k_ref.shape[0] // MOBA_BLOCK
    o_ref[...] = jnp.zeros(o_ref.shape, o_ref.dtype)
    for n in range(n_blk):
        blk = k_ref[n * MOBA_BLOCK:(n + 1) * MOBA_BLOCK, :].astype(F32)
        o_ref[n:n + 1, :] = jnp.mean(blk, axis=0, keepdims=True)


def _kmean(p128):
    b, s, _ = p128.shape
    rows = max(SUBLANES, s // MOBA_BLOCK)
    return pl.pallas_call(
        _kmean_kernel,
        grid=(b,),
        in_specs=[pl.BlockSpec((None, s, MIX_W), lambda bb: (bb, 0, 2))],
        out_specs=pl.BlockSpec((None, rows, MIX_W), lambda bb: (bb, 0, 0)),
        out_shape=jax.ShapeDtypeStruct((b, rows, MIX_W), F32),
        compiler_params=_params(("arbitrary",)),
    )(p128)


def _moba_kernel(q_ref, k_ref, vt_ref, km_ref, o_ref, sel_ref, m_ref, l_ref, acc_ref):
    tq = q_ref.shape[0]
    t = vt_ref.shape[2]
    per_tile = tq // t
    n_blk = km_ref.shape[0]
    i = pl.program_id(1)
    blk = lax.broadcasted_iota(I32, (n_blk, 1), 0)
    own_rel = lax.broadcasted_iota(I32, (1, tq), 1) // t
    own = i * per_tile + own_rel
    _init_softmax(m_ref, l_ref, acc_ref)

    for h in range(HEADS):
        gate = _dot_nt(_head(km_ref, h).astype(BF16), _head(q_ref, h))
        gate = jnp.where(blk < own, gate, -jnp.inf)
        sel = jnp.zeros(gate.shape, F32)
        for _ in range(MOBA_TOPK):
            best = jnp.max(gate, axis=0, keepdims=True)
            first = jnp.min(jnp.where(gate == best, blk, n_blk), axis=0, keepdims=True)
            pick = (blk == first) & (best > -jnp.inf)
            sel = jnp.where(pick, 1.0, sel)
            gate = jnp.where(pick, -jnp.inf, gate)
        sel_ref[h] = sel

    def step(c, tile_block):
        k_c = k_ref[pl.ds(pl.multiple_of(c * t, t), t), :]
        v_c = vt_ref[c]
        scores = [_dot_nt(_head(k_c, h), _head(q_ref, h)) for h in range(HEADS)]
        values = [v_c[h * HEAD_DIM:(h + 1) * HEAD_DIM, :] for h in range(HEADS)]
        picked = [sel_ref[h, pl.ds(c, 1), :] > 0.0 for h in range(HEADS)]
        if tile_block is None:
            masks = picked
        else:
            causal = (own_rel == tile_block) & _causal_mask(t, tq, tile_block * t)
            masks = [causal | ((own_rel > tile_block) & p) for p in picked]
        _online_updates(scores, masks, values, m_ref, l_ref, acc_ref)

    def past_step(c, carry):
        step(c, None)
        return carry

    lax.fori_loop(0, i * per_tile, past_step, 0)
    for d in range(per_tile):
        step(i * per_tile + d, d)
    for h in range(HEADS):
        o_ref[:, h * HEAD_DIM:(h + 1) * HEAD_DIM] = (acc_ref[h] / l_ref[h]).T.astype(o_ref.dtype)


def _moba(p128, vt, kmean):
    b, s, _ = p128.shape
    t = MOBA_BLOCK
    tq = min(MOBA_TQ, s)
    nc = s // t
    rows = kmean.shape[1]
    return pl.pallas_call(
        _moba_kernel,
        grid=(b, s // tq),
        in_specs=[
            pl.BlockSpec((None, tq, MIX_W), lambda bb, i: (bb, i, 1)),
            pl.BlockSpec((None, s, MIX_W), lambda bb, i: (bb, 0, 2)),
            pl.BlockSpec((None, nc, MIX_W, t), lambda bb, i: (bb, 0, 1, 0)),
            pl.BlockSpec((None, rows, MIX_W), lambda bb, i: (bb, 0, 0)),
        ],
        out_specs=pl.BlockSpec((None, tq, MIX_W), lambda bb, i: (bb, i, 0)),
        out_shape=jax.ShapeDtypeStruct((b, s, MIX_W), BF16),
        scratch_shapes=[pltpu.VMEM((HEADS, rows, tq), F32)] + _softmax_scratch(HEADS, tq),
        compiler_params=_params(("arbitrary", "arbitrary")),
    )(p128, p128, vt, kmean)


def _dilated_kernel(q_ref, k_ref, vt_ref, o_ref, m_ref, l_ref, acc_ref):
    tq = q_ref.shape[0]
    t = vt_ref.shape[2]
    i = pl.program_id(1)
    _init_softmax(m_ref, l_ref, acc_ref)
    rel = lax.broadcasted_iota(I32, (t, tq), 1) - lax.broadcasted_iota(I32, (t, tq), 0)

    def first_chunk_within(window):
        return jnp.maximum(i * tq - window, 0) // t

    def step(c, patterns):
        k_c = k_ref[pl.ds(pl.multiple_of(c * t, t), t), :]
        v_c = vt_ref[c]
        dist = rel + (i * tq - c * t)
        mult = jnp.zeros((t, tq), F32)
        for window, dilation in patterns:
            hit = (dist >= 0) & (dist <= window) & ((dist & (dilation - 1)) == 0)
            mult = mult + jnp.where(hit, 1.0, 0.0)
        mask = mult > 0.0
        scores = [_dot_nt(_head(k_c, h), _head(q_ref, h)) for h in range(HEADS)]
        values = [v_c[h * HEAD_DIM:(h + 1) * HEAD_DIM, :] for h in range(HEADS)]
        _online_updates(scores, mask, values, m_ref, l_ref, acc_ref, weight=mult)

    widest = max(DILATED_PATTERNS)
    others = [p for p in DILATED_PATTERNS if p != widest]
    first = first_chunk_within(widest[0])
    near_first = first_chunk_within(max(w for w, _ in others))

    def far_step(c, carry):
        step(c, [widest])
        return carry

    def near_step(c, carry):
        step(c, DILATED_PATTERNS)
        return carry

    lax.fori_loop(first, near_first, far_step, 0)
    lax.fori_loop(near_first, (i + 1) * (tq // t), near_step, 0)
    for h in range(HEADS):
        o_ref[:, h * HEAD_DIM:(h + 1) * HEAD_DIM] = (acc_ref[h] / l_ref[h]).T.astype(o_ref.dtype)


def _dilated(p128, vt):
    b, s, _ = p128.shape
    t = min(ATT_T, s)
    tq = min(DIL_TQ, s)
    nc = s // t
    return pl.pallas_call(
        _dilated_kernel,
        grid=(b, s // tq),
        in_specs=[
            pl.BlockSpec((None, tq, MIX_W), lambda bb, i: (bb, i, 3)),
            pl.BlockSpec((None, s, MIX_W), lambda bb, i: (bb, 0, 4)),
            pl.BlockSpec((None, nc, MIX_W, t), lambda bb, i: (bb, 0, 2, 0)),
        ],
        out_specs=pl.BlockSpec((None, tq, MIX_W), lambda bb, i: (bb, i, 0)),
        out_shape=jax.ShapeDtypeStruct((b, s, MIX_W), BF16),
        scratch_shapes=_softmax_scratch(HEADS, tq),
        compiler_params=_params(("arbitrary", "arbitrary")),
    )(p128, p128, vt)


def _layer_norm(z, g, b):
    mu = jnp.mean(z, axis=1, keepdims=True)
    zc = z - mu
    var = jnp.mean(zc * zc, axis=1, keepdims=True)
    return zc * lax.rsqrt(var + LN_EPS) * g + b


def _outproj_ln_kernel(oa_ref, ob_ref, oc_ref, od_ref, w_ref, x_ref, g_ref, b_ref, xo_ref, xb_ref):
    rows_per_part = x_ref.shape[0] // OUT_PARTS
    for part in range(OUT_PARTS):
        rows = slice(part * rows_per_part, (part + 1) * rows_per_part)
        y = jnp.dot(oa_ref[rows, :], w_ref[0:MIX_W, :], preferred_element_type=F32)
        y = y + jnp.dot(ob_ref[rows, :], w_ref[MIX_W:2 * MIX_W, :], preferred_element_type=F32)
        y = y + jnp.dot(oc_ref[rows, :], w_ref[2 * MIX_W:3 * MIX_W, :], preferred_element_type=F32)
        y = y + jnp.dot(od_ref[rows, :], w_ref[3 * MIX_W:4 * MIX_W, :], preferred_element_type=F32)
        out = _layer_norm(ALPHA * x_ref[rows, :] + y, g_ref[...], b_ref[...])
        xo_ref[rows, :] = out
        xb_ref[rows, :] = out.astype(BF16)


def _outproj_ln(oa, ob, oc, od, w_bf, x, g, b):
    t, d = x.shape
    tm = min(OUT_TM, t)
    mix_spec = pl.BlockSpec((tm, MIX_W), lambda i: (i, 0))
    row_spec = pl.BlockSpec((tm, d), lambda i: (i, 0))
    vec_spec = pl.BlockSpec((1, d), lambda i: (0, 0))
    return pl.pallas_call(
        _outproj_ln_kernel,
        grid=(t // tm,),
        in_specs=[mix_spec, mix_spec, mix_spec, mix_spec,
                  pl.BlockSpec((d, d), lambda i: (0, 0)), row_spec, vec_spec, vec_spec],
        out_specs=[row_spec, row_spec],
        out_shape=[jax.ShapeDtypeStruct((t, d), F32), jax.ShapeDtypeStruct((t, d), BF16)],
        compiler_params=_params(("arbitrary",)),
    )(oa, ob, oc, od, w_bf, x, g, b)


def _first_argmax(vals):
    best, arg = vals[0], jnp.zeros(vals[0].shape, I32)
    for j in range(1, len(vals)):
        better = vals[j] > best
        arg = jnp.where(better, j, arg)
        best = jnp.where(better, vals[j], best)
    return arg, best


def _pick(rows, idx):
    out = rows[0]
    for j in range(1, len(rows)):
        out = jnp.where(idx == j, rows[j], out)
    return out


def _router_kernel(x_ref, rwt_ref, bias_ref, e_ref, gate_ref, rank_ref, cnt_ref, carry_ref):
    tm = x_ref.shape[0]

    @pl.when(pl.program_id(0) == 0)
    def _():
        carry_ref[...] = jnp.zeros(carry_ref.shape, F32)

    logits = lax.dot_general(rwt_ref[...], x_ref[...], (((1,), (1,)), ((), ())),
                             precision=lax.Precision.HIGHEST, preferred_element_type=F32)
    aff = 1.0 / (1.0 + jnp.exp(-logits))
    biased = aff + bias_ref[...]
    sel_rows = [biased[r:r + 1, :] for r in range(N_EXPERTS)]
    aff_rows = [aff[r:r + 1, :] for r in range(N_EXPERTS)]

    group_scores = []
    for g in range(N_GROUPS):
        r = sel_rows[g * EXPERTS_PER_GROUP:(g + 1) * EXPERTS_PER_GROUP]
        best_pair = r[0] + r[1]
        for a in range(EXPERTS_PER_GROUP):
            for c in range(a + 1, EXPERTS_PER_GROUP):
                if (a, c) != (0, 1):
                    best_pair = jnp.maximum(best_pair, r[a] + r[c])
        group_scores.append(best_pair)
    grp, _ = _first_argmax(group_scores)

    in_sel = [_pick([sel_rows[g * EXPERTS_PER_GROUP + j] for g in range(N_GROUPS)], grp)
              for j in range(EXPERTS_PER_GROUP)]
    in_aff = [_pick([aff_rows[g * EXPERTS_PER_GROUP + j] for g in range(N_GROUPS)], grp)
              for j in range(EXPERTS_PER_GROUP)]
    first, _ = _first_argmax(in_sel)
    second, _ = _first_argmax([jnp.where(first == j, -jnp.inf, in_sel[j]) for j in range(EXPERTS_PER_GROUP)])
    a0 = _pick(in_aff, first)
    a1 = _pick(in_aff, second)
    e0 = grp * EXPERTS_PER_GROUP + first
    e1 = grp * EXPERTS_PER_GROUP + second
    e_ref[0:1, :] = e0
    e_ref[1:2, :] = e1
    gate_ref[0:1, :] = a0 / (a0 + a1)
    gate_ref[1:2, :] = a1 / (a0 + a1)

    e_iota = lax.broadcasted_iota(I32, (N_EXPERTS, tm), 0)
    one_hot = ((e_iota == e0) | (e_iota == e1)).astype(BF16)
    before = (lax.broadcasted_iota(I32, (tm, tm), 0) < lax.broadcasted_iota(I32, (tm, tm), 1)).astype(BF16)
    prior = jnp.dot(one_hot, before, preferred_element_type=F32) + carry_ref[...]
    rank_ref[0:1, :] = jnp.sum(jnp.where(e_iota == e0, prior, 0.0), axis=0, keepdims=True).astype(I32)
    rank_ref[1:2, :] = jnp.sum(jnp.where(e_iota == e1, prior, 0.0), axis=0, keepdims=True).astype(I32)
    carry_ref[...] = carry_ref[...] + jnp.sum(one_hot.astype(F32), axis=1, keepdims=True)
    cnt_ref[...] = jnp.broadcast_to(carry_ref[...], cnt_ref.shape).astype(I32)


def _router(x, rw_t, bias):
    t, d = x.shape
    tm = min(ROUTER_TM, t)
    tok_spec = pl.BlockSpec((2, tm), lambda i: (0, i))
    return pl.pallas_call(
        _router_kernel,
        grid=(t // tm,),
        in_specs=[pl.BlockSpec((tm, d), lambda i: (i, 0)),
                  pl.BlockSpec((N_EXPERTS, d), lambda i: (0, 0)),
                  pl.BlockSpec((N_EXPERTS, 1), lambda i: (0, 0))],
        out_specs=[tok_spec, tok_spec, tok_spec, pl.BlockSpec((N_EXPERTS, LANES), lambda i: (0, 0))],
        out_shape=[jax.ShapeDtypeStruct((2, t), I32), jax.ShapeDtypeStruct((2, t), F32),
                   jax.ShapeDtypeStruct((2, t), I32), jax.ShapeDtypeStruct((N_EXPERTS, LANES), I32)],
        scratch_shapes=[pltpu.VMEM((N_EXPERTS, 1), F32)],
        compiler_params=_params(("arbitrary",)),
    )(x, rw_t, bias)


def _row_copy(src, src_row, dst, dst_row, sem):
    return pltpu.make_async_copy(src.at[pl.ds(src_row, 1)], dst.at[pl.ds(dst_row, 1)], sem)


def _dispatch_kernel(dest_ref, pend_ref, na_ref, x_ref, slots_hbm, zeros_ref, sem, zero_sem,
                     *, tm, n_tok, tb):
    base = pl.program_id(0) * tm

    @pl.when(pl.program_id(0) == 0)
    def _():
        zeros_ref[...] = jnp.zeros(zeros_ref.shape, zeros_ref.dtype)
        n_blocks = slots_hbm.shape[0] // tb

        def block_fill(start):
            return pltpu.make_async_copy(zeros_ref, slots_hbm.at[pl.ds(start, tb)], zero_sem)

        def for_each_fill(act):
            for e in range(N_EXPERTS):
                seg_start = pend_ref[e - 1] if e else 0

                @pl.when(pend_ref[e] > seg_start)
                def _():
                    act(pl.multiple_of(pend_ref[e] - tb, tb))

            def tail(n, carry):
                act(pl.multiple_of(n * tb, tb))
                return carry

            lax.fori_loop(na_ref[0], n_blocks, tail, 0)

        for_each_fill(lambda start: block_fill(start).start())
        for_each_fill(lambda start: block_fill(start).wait())

    def issue(r, carry):
        tok = base + r
        _row_copy(x_ref, r, slots_hbm, dest_ref[tok], sem).start()
        _row_copy(x_ref, r, slots_hbm, dest_ref[n_tok + tok], sem).start()
        return carry

    def drain(r, carry):
        _row_copy(x_ref, 0, slots_hbm, 0, sem).wait()
        _row_copy(x_ref, 0, slots_hbm, 0, sem).wait()
        return carry

    lax.fori_loop(0, tm, issue, 0, unroll=DMA_UNROLL)
    lax.fori_loop(0, tm, drain, 0, unroll=DMA_UNROLL)


def _dispatch(dest_flat, pend, n_active, x, n_slots):
    t, d = x.shape
    tm = min(DISPATCH_TM, t)
    tb = EXPERT_TB
    return pl.pallas_call(
        functools.partial(_dispatch_kernel, tm=tm, n_tok=t, tb=tb),
        grid_spec=pltpu.PrefetchScalarGridSpec(
            num_scalar_prefetch=3,
            grid=(t // tm,),
            in_specs=[pl.BlockSpec((tm, d), lambda i, dest, pend, na: (i, 0))],
            out_specs=pl.BlockSpec(memory_space=pl.ANY),
            scratch_shapes=[pltpu.VMEM((tb, d), x.dtype), pltpu.SemaphoreType.DMA(()),
                            pltpu.SemaphoreType.DMA(())],
        ),
        out_shape=jax.ShapeDtypeStruct((n_slots, d), x.dtype),
        compiler_params=_params(("arbitrary",)),
    )(dest_flat, pend, n_active, x)


def _last_active(n, na_ref):
    return jnp.maximum(jnp.minimum(n, na_ref[0] - 1), 0)


def _expert_stage(be_ref, na_ref, w_refs, w_bf_refs, compute, out_ref):
    n = pl.program_id(0)
    active = n < na_ref[0]
    new_expert = (n == 0) | (be_ref[n] != be_ref[jnp.maximum(n - 1, 0)])

    @pl.when(active & new_expert)
    def _():
        for w_ref, w_bf_ref in zip(w_refs, w_bf_refs):
            w_bf_ref[...] = w_ref[...].astype(BF16)

    @pl.when(active)
    def _():
        rows_per_part = out_ref.shape[0] // EXPERT_PARTS
        for part in range(EXPERT_PARTS):
            rows = slice(part * rows_per_part, (part + 1) * rows_per_part)
            out_ref[rows, :] = compute(rows).astype(out_ref.dtype)

    @pl.when(jnp.logical_not(active))
    def _():
        out_ref[...] = jnp.zeros(out_ref.shape, out_ref.dtype)


def _expert_up_kernel(be_ref, na_ref, xs_ref, wg_ref, wu_ref, h_ref, wg_bf, wu_bf):
    def compute(rows):
        xb = xs_ref[rows, :].astype(BF16)
        gate = jnp.dot(xb, wg_bf[...], preferred_element_type=F32)
        up = jnp.dot(xb, wu_bf[...], preferred_element_type=F32)
        return gate * (1.0 / (1.0 + jnp.exp(-gate))) * up

    _expert_stage(be_ref, na_ref, (wg_ref, wu_ref), (wg_bf, wu_bf), compute, h_ref)


def _expert_down_kernel(be_ref, na_ref, h_ref, wd_ref, ys_ref, wd_bf):
    def compute(rows):
        return jnp.dot(h_ref[rows, :], wd_bf[...], preferred_element_type=F32)

    _expert_stage(be_ref, na_ref, (wd_ref,), (wd_bf,), compute, ys_ref)


def _experts(block_expert, n_active, xs, wg, wu, wd, layer):
    n_slots, d = xs.shape
    tb = EXPERT_TB
    f = wg.shape[3]

    def row_map(n, be, na):
        return (_last_active(n, na), 0)

    def w_map(n, be, na):
        return (layer, be[_last_active(n, na)], 0, 0)

    def out_map(n, be, na):
        return (n, 0)

    hidden = pl.pallas_call(
        _expert_up_kernel,
        grid_spec=pltpu.PrefetchScalarGridSpec(
            num_scalar_prefetch=2,
            grid=(n_slots // tb,),
            in_specs=[pl.BlockSpec((tb, d), row_map),
                      pl.BlockSpec((None, None, d, f), w_map),
                      pl.BlockSpec((None, None, d, f), w_map)],
            out_specs=pl.BlockSpec((tb, f), out_map),
            scratch_shapes=[pltpu.VMEM((d, f), BF16), pltpu.VMEM((d, f), BF16)],
        ),
        out_shape=jax.ShapeDtypeStruct((n_slots, f), BF16),
        compiler_params=_params(("arbitrary",)),
    )(block_expert, n_active, xs, wg, wu)
    return pl.pallas_call(
        _expert_down_kernel,
        grid_spec=pltpu.PrefetchScalarGridSpec(
            num_scalar_prefetch=2,
            grid=(n_slots // tb,),
            in_specs=[pl.BlockSpec((tb, f), row_map),
                      pl.BlockSpec((None, None, f, d), w_map)],
            out_specs=pl.BlockSpec((tb, d), out_map),
            scratch_shapes=[pltpu.VMEM((f, d), BF16)],
        ),
        out_shape=jax.ShapeDtypeStruct((n_slots, d), F32),
        compiler_params=_params(("arbitrary",)),
    )(block_expert, n_active, hidden, wd)


def _combine_ln_kernel(dest_ref, x_ref, gate_ref, g_ref, b_ref, ys_hbm, xo_ref, xb_ref, buf_ref, sem,
                       *, n_tok):
    tm = x_ref.shape[0]
    i = pl.program_id(0)
    slot = i % 2

    def request(tile, to_slot):
        def issue(r, carry):
            tok = tile * tm + r
            _row_copy(ys_hbm, dest_ref[tok], buf_ref.at[to_slot, 0], r, sem.at[to_slot]).start()
            _row_copy(ys_hbm, dest_ref[n_tok + tok], buf_ref.at[to_slot, 1], r, sem.at[to_slot]).start()
            return carry
        lax.fori_loop(0, tm, issue, 0, unroll=DMA_UNROLL)

    @pl.when(i == 0)
    def _():
        request(0, 0)

    @pl.when(i + 1 < pl.num_programs(0))
    def _():
        request(i + 1, 1 - slot)

    def drain(r, carry):
        _row_copy(ys_hbm, 0, buf_ref.at[slot, 0], 0, sem.at[slot]).wait()
        _row_copy(ys_hbm, 0, buf_ref.at[slot, 1], 0, sem.at[slot]).wait()
        return carry

    lax.fori_loop(0, tm, drain, 0, unroll=DMA_UNROLL)
    gate = gate_ref[...]
    moe = buf_ref[slot, 0] * gate[:, 0:1] + buf_ref[slot, 1] * gate[:, 1:2]
    out = _layer_norm(ALPHA * x_ref[...] + moe, g_ref[...], b_ref[...])
    xo_ref[...] = out
    xb_ref[...] = out.astype(BF16)


def _combine_ln(dest_flat, x, gate_t, g, b, ys):
    t, d = x.shape
    tm = min(COMBINE_TM, t)
    row_spec = pl.BlockSpec((tm, d), lambda i, dest: (i, 0))
    vec_spec = pl.BlockSpec((1, d), lambda i, dest: (0, 0))
    return pl.pallas_call(
        functools.partial(_combine_ln_kernel, n_tok=t),
        grid_spec=pltpu.PrefetchScalarGridSpec(
            num_scalar_prefetch=1,
            grid=(t // tm,),
            in_specs=[row_spec, pl.BlockSpec((tm, 2), lambda i, dest: (i, 0)), vec_spec, vec_spec,
                      pl.BlockSpec(memory_space=pl.ANY)],
            out_specs=[row_spec, row_spec],
            scratch_shapes=[pltpu.VMEM((2, 2, tm, d), F32), pltpu.SemaphoreType.DMA((2,))],
        ),
        out_shape=[jax.ShapeDtypeStruct((t, d), F32), jax.ShapeDtypeStruct((t, d), BF16)],
        compiler_params=_params(("arbitrary",)),
    )(dest_flat, x, gate_t, g, b, ys)


def _rope_tables(positions):
    pos = positions.reshape(-1).astype(F32)[:, None]

    def cos_sin(d):
        inv = jnp.power(ROPE_THETA, -jnp.arange(0, d, 2, dtype=F32) / d)
        ang = pos * inv
        return jnp.cos(ang), jnp.sin(ang)

    c, s = cos_sin(HEAD_DIM)
    t128 = (jnp.concatenate([c, c], axis=1), jnp.concatenate([-s, s], axis=1))
    c, s = cos_sin(IDX_DIM)
    z = jnp.zeros_like(s)
    t64 = (jnp.concatenate([c, c, c, c], axis=1),
           jnp.concatenate([-s, z, -s, z], axis=1),
           jnp.concatenate([z, s, z, s], axis=1))
    return t128, t64


def _col_scales():
    q128 = HEAD_DIM ** -0.5 * LOG2E
    s128 = np.ones((1, 5 * MIX_W + HEAD_DIM), np.float32)
    s128[:, 0:2 * MIX_W] = q128
    s128[:, 3 * MIX_W:4 * MIX_W] = q128
    s64 = np.ones((1, 2304), np.float32)
    s64[:, 1024:1024 + MIX_W] = DIFF_DIM ** -0.5 * LOG2E
    return jnp.asarray(s128), jnp.asarray(s64)


def _split_w_in(w):
    def cols(name):
        lo, hi = _OFF[name]
        return w[:, lo:hi]

    d = w.shape[0]
    w128 = jnp.concatenate([cols(n) for n in ("a_q", "c_q", "c_k", "d_q", "d_k", "a_k")], axis=1)
    w64 = jnp.concatenate([cols("i_q"), cols("b_q"), cols("b_k"), cols("i_k"), cols("i_k"),
                           jnp.zeros((d, LANES), w.dtype)], axis=1)
    wt = jnp.concatenate([cols("b_v"), cols("c_v"), cols("d_v"), cols("a_v"), cols("i_w")], axis=1).T
    return w128.astype(BF16), w64.astype(BF16), wt.astype(BF16)


def _moe_layout(e, rank, counts, tb):
    t = e.shape[1]
    n_slots = 2 * t + N_EXPERTS * tb
    padded = (counts + tb - 1) // tb * tb
    pend = jnp.cumsum(padded)
    pstart = pend - padded
    experts = jnp.arange(N_EXPERTS, dtype=I32)[:, None, None]
    seg_start = jnp.sum(jnp.where(e[None] == experts, pstart[:, None, None], 0), axis=0)
    dest = (seg_start + rank).astype(I32).reshape(-1)
    blocks = jnp.arange(n_slots // tb, dtype=I32) * tb
    block_expert = jnp.minimum(jnp.sum(blocks[:, None] >= pend[None, :], axis=1), N_EXPERTS - 1).astype(I32)
    n_active = (pend[-1:] // tb).astype(I32)
    return dest, pend.astype(I32), block_expert, n_active, n_slots


def kernel(x, positions, w_in, w_out, diff_lambda, diff_norm_g, ln_mix_g, ln_mix_b, router_w, router_bias,
           w_gate, w_up, w_down, ln_ffn_g, ln_ffn_b):
    b, s, d = x.shape
    t = b * s
    n_keep = min(DSA_TOPK_MAX, s // 4)
    t128, t64 = _rope_tables(positions)
    s128, s64 = _col_scales()
    rw_t = router_w.T
    bias = router_bias.reshape(N_EXPERTS, 1).astype(F32)
    xf = x.reshape(t, d)
    xb = xf
    nc = s // min(ATT_T, s)
    for layer in range(DEPTH):
        w128, w64, wt = _split_w_in(w_in[layer])
        p128 = _proj(xb, w128, s128, t128, "rope128", w128.shape[1]).reshape(b, s, -1)
        p64 = _proj(xb, w64, s64, t64, "rope64", 768).reshape(b, s, -1)
        vt, iwt = _proj_t(xb, wt)
        vt = vt.reshape(b, nc, V_ROWS, -1)
        iwt = iwt.reshape(b, nc, IDX_HEADS, -1)

        lam_init = 0.8 - 0.6 * math.exp(-0.3 * layer)
        o_a = _dsa(p128, p64, vt, iwt, n_keep)
        o_b = _diff(p64, vt, diff_lambda[layer], diff_norm_g[layer].reshape(HEAD_DIM, 1), lam_init)
        o_c = _moba(p128, vt, _kmean(p128))
        o_d = _dilated(p128, vt)
        xf, xb = _outproj_ln(o_a.reshape(t, MIX_W), o_b.reshape(t, MIX_W), o_c.reshape(t, MIX_W),
                             o_d.reshape(t, MIX_W), w_out[layer].astype(BF16), xf,
                             ln_mix_g[layer].reshape(1, d), ln_mix_b[layer].reshape(1, d))

        e, gate, rank, counts = _router(xf, rw_t, bias)
        dest, pend, block_expert, n_active, n_slots = _moe_layout(e, rank, counts[:, 0], EXPERT_TB)
        xs = _dispatch(dest, pend, n_active, xf, n_slots)
        ys = _experts(block_expert, n_active, xs, w_gate, w_up, w_down, layer)
        xf, xb = _combine_ln(dest, xf, gate.T, ln_ffn_g[layer].reshape(1, d), ln_ffn_b[layer].reshape(1, d), ys)
    return xf.reshape(b, s, d)
```

```python
import functools
import math

import numpy as np
import jax
import jax.numpy as jnp
from jax import lax
from jax.experimental import pallas as pl
from jax.experimental.pallas import tpu as pltpu

F32 = jnp.float32
BF16 = jnp.bfloat16
I32 = jnp.int32

D_MODEL = 2048
HEAD_DIM = 128
HEADS = 4
MIX_W = HEADS * HEAD_DIM
IDX_HEADS = 16
IDX_DIM = 64
DSA_TOPK_MAX = 256
DIFF_DIM = 64
MOBA_BLOCK = 256
MOBA_TOPK = 3
DILATED_PATTERNS = ((128, 1), (512, 4), (2048, 16))
ROPE_THETA = 10000.0
N_EXPERTS = 16
N_GROUPS = 4
EXPERTS_PER_GROUP = 4
LN_EPS = 1e-5
DEPTH = 2
ALPHA = (2 * DEPTH) ** 0.25
LOG2E = math.log2(math.e)

LANES = 128
SUBLANES = 8
NEG = -1e30
M_FLOOR = -1e29
INT_MIN = -(2 ** 31)
NEG_INF_ORDER = 0x007FFFFF
VMEM_LIMIT = 56 * 1024 * 1024

PROJ_TM = 512
PROJ_PARTS = 4
ATT_T = 256
DIFF_TQ = 512
DSA_TQ = 512
MOBA_TQ = 512
DIL_TQ = 512
OUT_TM = 512
OUT_PARTS = 2
ROUTER_TM = 256
DISPATCH_TM = 256
EXPERT_TB = 256
COMBINE_TM = 256
DMA_UNROLL = 8
V_ROWS = 3 * MIX_W + HEAD_DIM

_OFF = {}
_o = 0
for _name, _n in (("a_q", 512), ("a_k", 128), ("a_v", 128), ("i_q", 1024), ("i_k", 64), ("i_w", 16),
                  ("b_q", 512), ("b_k", 512), ("b_v", 512), ("c_q", 512), ("c_k", 512), ("c_v", 512),
                  ("d_q", 512), ("d_k", 512), ("d_v", 512)):
    _OFF[_name] = (_o, _o + _n)
    _o += _n


def _params(sem=None):
    return pltpu.CompilerParams(dimension_semantics=sem, vmem_limit_bytes=VMEM_LIMIT)


def _dot_nt(a, b):
    return lax.dot_general(a, b, (((1,), (1,)), ((), ())), preferred_element_type=F32)


def _proj_kernel(x_ref, w_ref, scale_ref, *rest, mode):
    o_ref = rest[-1]
    tm = x_ref.shape[0]
    rows_per_part = tm // PROJ_PARTS
    n_groups = w_ref.shape[1] // LANES
    for part in range(PROJ_PARTS):
        rows = slice(part * rows_per_part, (part + 1) * rows_per_part)
        acc = jnp.dot(x_ref[rows, :].astype(BF16), w_ref[...], preferred_element_type=F32)
        for g in range(n_groups):
            cols = slice(g * LANES, (g + 1) * LANES)
            y = acc[:, cols]
            if mode == "rope128":
                cos_ref, sin_ref = rest[0], rest[1]
                r = y * cos_ref[rows, :] + pltpu.roll(y, 64, 1) * sin_ref[rows, :]
            else:
                cos_ref, sina_ref, sinb_ref = rest[0], rest[1], rest[2]
                r = (y * cos_ref[rows, :] + pltpu.roll(y, 96, 1) * sina_ref[rows, :]
                     + pltpu.roll(y, 32, 1) * sinb_ref[rows, :])
            r = r * scale_ref[:, cols]
            o_ref[rows, cols] = r.astype(o_ref.dtype)


def _proj(x_bf, w_bf, col_scale, tables, mode, tn):
    t, d = x_bf.shape
    n = w_bf.shape[1]
    tm = min(PROJ_TM, t)
    in_specs = [pl.BlockSpec((tm, d), lambda j, i: (i, 0)),
                pl.BlockSpec((d, tn), lambda j, i: (0, j)),
                pl.BlockSpec((1, tn), lambda j, i: (0, j))]
    in_specs += [pl.BlockSpec((tm, LANES), lambda j, i: (i, 0)) for _ in tables]
    return pl.pallas_call(
        functools.partial(_proj_kernel, mode=mode),
        grid=(n // tn, t // tm),
        in_specs=in_specs,
        out_specs=pl.BlockSpec((tm, tn), lambda j, i: (i, j)),
        out_shape=jax.ShapeDtypeStruct((t, n), BF16),
        compiler_params=_params(("arbitrary", "arbitrary")),
    )(x_bf, w_bf, col_scale, *tables)


def _proj_t_kernel(x_ref, wt_ref, vt_ref, iwt_ref):
    acc = _dot_nt(wt_ref[...], x_ref[...].astype(BF16))
    vt_ref[...] = acc[:V_ROWS].astype(vt_ref.dtype)
    iwt_ref[...] = acc[V_ROWS:]


def _proj_t(x_bf, wt_bf):
    t, d = x_bf.shape
    tm = min(ATT_T, t)
    rows = wt_bf.shape[0]
    return pl.pallas_call(
        _proj_t_kernel,
        grid=(t // tm,),
        in_specs=[pl.BlockSpec((tm, d), lambda i: (i, 0)),
                  pl.BlockSpec((rows, d), lambda i: (0, 0))],
        out_specs=[pl.BlockSpec((None, V_ROWS, tm), lambda i: (i, 0, 0)),
                   pl.BlockSpec((None, IDX_HEADS, tm), lambda i: (i, 0, 0))],
        out_shape=[jax.ShapeDtypeStruct((t // tm, V_ROWS, tm), BF16),
                   jax.ShapeDtypeStruct((t // tm, IDX_HEADS, tm), F32)],
        compiler_params=_params(("arbitrary",)),
    )(x_bf, wt_bf)


def _online_updates(scores, mask, values, m_ref, l_ref, acc_ref, weight=None):
    probs, alphas = [], []
    masks = mask if isinstance(mask, list) else [mask] * len(scores)
    for idx, (s_t, mask) in enumerate(zip(scores, masks)):
        if mask is not None:
            s_t = jnp.where(mask, s_t, NEG)
        m_prev = m_ref[idx]
        m_new = jnp.maximum(m_prev, jnp.max(s_t, axis=0, keepdims=True))
        p = jnp.exp2(s_t - m_new)
        if weight is not None:
            p = p * weight
        alpha = jnp.exp2(m_prev - m_new)
        l_ref[idx] = alpha * l_ref[idx] + jnp.sum(p, axis=0, keepdims=True)
        m_ref[idx] = m_new
        probs.append(p.astype(BF16))
        alphas.append(alpha)
    for idx, (p, alpha) in enumerate(zip(probs, alphas)):
        acc_ref[idx] = alpha * acc_ref[idx] + jnp.dot(values[idx], p, preferred_element_type=F32)


def _init_softmax(m_ref, l_ref, acc_ref):
    m_ref[...] = jnp.full(m_ref.shape, M_FLOOR, F32)
    l_ref[...] = jnp.zeros(l_ref.shape, F32)
    acc_ref[...] = jnp.zeros(acc_ref.shape, F32)


def _softmax_scratch(n, t):
    return [pltpu.VMEM((n, 1, t), F32), pltpu.VMEM((n, 1, t), F32), pltpu.VMEM((n, HEAD_DIM, t), F32)]


def _causal_mask(tk, tq, k_start):
    return k_start + lax.broadcasted_iota(I32, (tk, tq), 0) <= lax.broadcasted_iota(I32, (tk, tq), 1)


def _head(ref_or_val, h):
    return ref_or_val[:, h * HEAD_DIM:(h + 1) * HEAD_DIM]


def _order_to_f32(u):
    key = u ^ jnp.int32(INT_MIN)
    bits = jnp.where(key < 0, key ^ jnp.int32(0x7FFFFFFF), key)
    return lax.bitcast_convert_type(bits, F32)


def _dsa_kernel(aq_ref, ak_ref, avt_ref, iq_ref, ik_ref, iwt_ref, o_ref,
                score_ref, m_ref, l_ref, acc_ref, jlim_ref, *, n_keep):
    tq = aq_ref.shape[0]
    t = avt_ref.shape[2]
    i = pl.program_id(1)
    n_chunks = (i + 1) * (tq // t)
    lane = lax.broadcasted_iota(I32, (1, LANES), 1)
    k_off = lax.broadcasted_iota(I32, (t, 1), 0)
    q_pos = i * tq + lax.broadcasted_iota(I32, (1, tq), 1)
    iw_t = jnp.concatenate([iwt_ref[n] for n in range(tq // t)], axis=1)

    def score_chunk(c, carry):
        ik_c = ik_ref[pl.ds(pl.multiple_of(c * t, t), t), :]
        ik_lo = jnp.where(lane < IDX_DIM, ik_c, jnp.zeros_like(ik_c))
        ik_hi = jnp.where(lane >= IDX_DIM, ik_c, jnp.zeros_like(ik_c))
        score = jnp.zeros((t, tq), F32)
        for j in range(IDX_HEADS // 2):
            q_pair = iq_ref[:, j * LANES:(j + 1) * LANES]
            s_even = jnp.maximum(_dot_nt(ik_lo, q_pair), 0.0)
            s_odd = jnp.maximum(_dot_nt(ik_hi, q_pair), 0.0)
            score = score + s_even * iw_t[2 * j:2 * j + 1, :] + s_odd * iw_t[2 * j + 1:2 * j + 2, :]
        score_ref[c] = jnp.where(c * t + k_off <= q_pos, score, -jnp.inf)
        return carry

    lax.fori_loop(0, n_chunks, score_chunk, 0)

    def count(pred):
        def body(c, part):
            hit = jnp.where(pred(score_ref[c], c), 1, 0)
            return part + jnp.sum(hit.reshape(t // SUBLANES, SUBLANES, tq), axis=0)
        part = lax.fori_loop(0, n_chunks, body, jnp.zeros((SUBLANES, tq), I32))
        return jnp.sum(part, axis=0, keepdims=True)

    def bisect(it, thr_u):
        cand_u = thr_u | jnp.left_shift(jnp.int32(1), 31 - it)
        cand = _order_to_f32(cand_u)
        cnt = count(lambda sc, c: sc >= cand)
        return jnp.where(cnt >= n_keep, cand_u, thr_u)

    thr_u = lax.fori_loop(0, 32, bisect, jnp.zeros((1, tq), I32))
    below = (thr_u >= 0) & (thr_u < NEG_INF_ORDER)
    thr = jnp.where(below, -jnp.inf, _order_to_f32(thr_u))

    n_gt = count(lambda sc, c: sc > thr)
    n_eq = count(lambda sc, c: sc == thr)
    need = n_keep - n_gt
    s_total = score_ref.shape[0] * t
    jlim_ref[...] = jnp.full((1, tq), s_total, I32)
    excess = jnp.max(jnp.where((n_eq > need) & (thr > -jnp.inf), 1, 0))

    @pl.when(excess > 0)
    def _():
        n_bits = max(1, (s_total - 1).bit_length())

        def bisect_idx(it, j):
            cand = j | jnp.left_shift(jnp.int32(1), n_bits - 1 - it)
            cnt = count(lambda sc, c: (sc == thr) & (c * t + k_off < cand))
            return jnp.where(cnt < need, cand, j)

        jlim_ref[...] = lax.fori_loop(0, n_bits, bisect_idx, jnp.zeros((1, tq), I32))

    _init_softmax(m_ref, l_ref, acc_ref)
    jlim = jnp.where(thr > -jnp.inf, jlim_ref[...], -1)
    thr_all = jnp.where(thr > -jnp.inf, thr, jnp.finfo(F32).min)

    def attend(select):
        def body(c, carry):
            start = pl.multiple_of(c * t, t)
            k_c = ak_ref[pl.ds(start, t), :]
            scores = [_dot_nt(k_c, _head(aq_ref, h)) for h in range(HEADS)]
            sel = select(score_ref[c], c * t + k_off)
            _online_updates(scores, sel, [avt_ref[c]] * HEADS, m_ref, l_ref, acc_ref)
            return carry
        lax.fori_loop(0, n_chunks, body, 0)

    @pl.when(excess > 0)
    def _():
        attend(lambda sc, k_pos: (sc > thr) | ((sc == thr) & (k_pos <= jlim)))

    @pl.when(excess <= 0)
    def _():
        attend(lambda sc, k_pos: sc >= thr_all)
    for h in range(HEADS):
        o_ref[:, h * HEAD_DIM:(h + 1) * HEAD_DIM] = (acc_ref[h] / l_ref[h]).T.astype(o_ref.dtype)


def _dsa(p128, p64, vt, iwt, n_keep):
    b, s, _ = p128.shape
    t = min(ATT_T, s)
    tq = min(DSA_TQ, s)
    nc = s // t
    return pl.pallas_call(
        functools.partial(_dsa_kernel, n_keep=n_keep),
        grid=(b, s // tq),
        in_specs=[
            pl.BlockSpec((None, tq, MIX_W), lambda bb, i: (bb, i, 0)),
            pl.BlockSpec((None, s, HEAD_DIM), lambda bb, i: (bb, 0, 20)),
            pl.BlockSpec((None, nc, HEAD_DIM, t), lambda bb, i: (bb, 0, 12, 0)),
            pl.BlockSpec((None, tq, 1024), lambda bb, i: (bb, i, 0)),
            pl.BlockSpec((None, s, LANES), lambda bb, i: (bb, 0, 16)),
            pl.BlockSpec((None, tq // t, IDX_HEADS, t), lambda bb, i: (bb, i, 0, 0)),
        ],
        out_specs=pl.BlockSpec((None, tq, MIX_W), lambda bb, i: (bb, i, 0)),
        out_shape=jax.ShapeDtypeStruct((b, s, MIX_W), BF16),
        scratch_shapes=[pltpu.VMEM((nc, t, tq), F32)] + _softmax_scratch(HEADS, tq)
        + [pltpu.VMEM((1, tq), I32)],
        compiler_params=_params(("arbitrary", "arbitrary")),
    )(p128, p128, vt, p64, p64, iwt)


def _diff_kernel(q_ref, k_ref, vt_ref, lam_ref, g_ref, o_ref, m_ref, l_ref, acc_ref, *, lam_init):
    tq = q_ref.shape[0]
    t = vt_ref.shape[2]
    per_tile = tq // t
    i = pl.program_id(1)
    lane = lax.broadcasted_iota(I32, (1, LANES), 1)
    _init_softmax(m_ref, l_ref, acc_ref)

    def step(c, mask):
        k_c = k_ref[pl.ds(pl.multiple_of(c * t, t), t), :]
        v_c = vt_ref[c]
        scores, values = [], []
        for h in range(HEADS):
            q_h = _head(q_ref, h)
            k_h = _head(k_c, h)
            q_1 = jnp.where(lane < DIFF_DIM, q_h, jnp.zeros_like(q_h))
            q_2 = jnp.where(lane >= DIFF_DIM, q_h, jnp.zeros_like(q_h))
            scores += [_dot_nt(k_h, q_1), _dot_nt(k_h, q_2)]
            values += [v_c[h * HEAD_DIM:(h + 1) * HEAD_DIM, :]] * 2
        _online_updates(scores, mask, values, m_ref, l_ref, acc_ref)

    def full_step(c, carry):
        step(c, None)
        return carry

    lax.fori_loop(0, i * per_tile, full_step, 0)
    for d in range(per_tile):
        step(i * per_tile + d, _causal_mask(t, tq, d * t))

    lam_p = lam_ref[...]
    lam = (jnp.exp(jnp.sum(lam_p[0:1] * lam_p[1:2], axis=1, keepdims=True))
           - jnp.exp(jnp.sum(lam_p[2:3] * lam_p[3:4], axis=1, keepdims=True)) + lam_init)
    for h in range(HEADS):
        o = acc_ref[2 * h] / l_ref[2 * h] - lam * (acc_ref[2 * h + 1] / l_ref[2 * h + 1])
        o = o * lax.rsqrt(jnp.mean(o * o, axis=0, keepdims=True) + LN_EPS) * g_ref[...]
        o_ref[:, h * HEAD_DIM:(h + 1) * HEAD_DIM] = (o * (1.0 - lam_init)).T.astype(o_ref.dtype)


def _diff(p64, vt, lam_p, g_col, lam_init):
    b, s, _ = p64.shape
    t = min(ATT_T, s)
    tq = min(DIFF_TQ, s)
    nc = s // t
    return pl.pallas_call(
        functools.partial(_diff_kernel, lam_init=lam_init),
        grid=(b, s // tq),
        in_specs=[
            pl.BlockSpec((None, tq, MIX_W), lambda bb, i: (bb, i, 2)),
            pl.BlockSpec((None, s, MIX_W), lambda bb, i: (bb, 0, 3)),
            pl.BlockSpec((None, nc, MIX_W, t), lambda bb, i: (bb, 0, 0, 0)),
            pl.BlockSpec((4, DIFF_DIM), lambda bb, i: (0, 0)),
            pl.BlockSpec((HEAD_DIM, 1), lambda bb, i: (0, 0)),
        ],
        out_specs=pl.BlockSpec((None, tq, MIX_W), lambda bb, i: (bb, i, 0)),
        out_shape=jax.ShapeDtypeStruct((b, s, MIX_W), BF16),
        scratch_shapes=_softmax_scratch(2 * HEADS, tq),
        compiler_params=_params(("arbitrary", "arbitrary")),
    )(p64, p64, vt, lam_p, g_col)


def _kmean_kernel(k_ref, o_ref):
    n_blk = k_ref.shape[0] // MOBA_BLOCK
    o_ref[...] = jnp.zeros(o_ref.shape, o_ref.dtype)
    for n in range(n_blk):
        blk = k_ref[n * MOBA_BLOCK:(n + 1) * MOBA_BLOCK, :].astype(F32)
        o_ref[n:n + 1, :] = jnp.mean(blk, axis=0, keepdims=True)


def _kmean(p128):
    b, s, _ = p128.shape
    rows = max(SUBLANES, s // MOBA_BLOCK)
    return pl.pallas_call(
        _kmean_kernel,
        grid=(b,),
        in_specs=[pl.BlockSpec((None, s, MIX_W), lambda bb: (bb, 0, 2))],
        out_specs=pl.BlockSpec((None, rows, MIX_W), lambda bb: (bb, 0, 0)),
        out_shape=jax.ShapeDtypeStruct((b, rows, MIX_W), F32),
        compiler_params=_params(("arbitrary",)),
    )(p128)


def _moba_kernel(q_ref, k_ref, vt_ref, km_ref, o_ref, sel_ref, m_ref, l_ref, acc_ref):
    tq = q_ref.shape[0]
    t = vt_ref.shape[2]
    per_tile = tq // t
    n_blk = km_ref.shape[0]
    i = pl.program_id(1)
    blk = lax.broadcasted_iota(I32, (n_blk, 1), 0)
    own_rel = lax.broadcasted_iota(I32, (1, tq), 1) // t
    own = i * per_tile + own_rel
    _init_softmax(m_ref, l_ref, acc_ref)

    for h in range(HEADS):
        gate = _dot_nt(_head(km_ref, h).astype(BF16), _head(q_ref, h))
        gate = jnp.where(blk < own, gate, -jnp.inf)
        sel = jnp.zeros(gate.shape, F32)
        for _ in range(MOBA_TOPK):
            best = jnp.max(gate, axis=0, keepdims=True)
            first = jnp.min(jnp.where(gate == best, blk, n_blk), axis=0, keepdims=True)
            pick = (blk == first) & (best > -jnp.inf)
            sel = jnp.where(pick, 1.0, sel)
            gate = jnp.where(pick, -jnp.inf, gate)
        sel_ref[h] = sel

    def step(c, tile_block):
        k_c = k_ref[pl.ds(pl.multiple_of(c * t, t), t), :]
        v_c = vt_ref[c]
        scores = [_dot_nt(_head(k_c, h), _head(q_ref, h)) for h in range(HEADS)]
        values = [v_c[h * HEAD_DIM:(h + 1) * HEAD_DIM, :] for h in range(HEADS)]
        picked = [sel_ref[h, pl.ds(c, 1), :] > 0.0 for h in range(HEADS)]
        if tile_block is None:
            masks = picked
        else:
            causal = (own_rel == tile_block) & _causal_mask(t, tq, tile_block * t)
            masks = [causal | ((own_rel > tile_block) & p) for p in picked]
        _online_updates(scores, masks, values, m_ref, l_ref, acc_ref)

    def past_step(c, carry):
        step(c, None)
        return carry

    lax.fori_loop(0, i * per_tile, past_step, 0)
    for d in range(per_tile):
        step(i * per_tile + d, d)
    for h in range(HEADS):
        o_ref[:, h * HEAD_DIM:(h + 1) * HEAD_DIM] = (acc_ref[h] / l_ref[h]).T.astype(o_ref.dtype)


def _moba(p128, vt, kmean):
    b, s, _ = p128.shape
    t = MOBA_BLOCK
    tq = min(MOBA_TQ, s)
    nc = s // t
    rows = kmean.shape[1]
    return pl.pallas_call(
        _moba_kernel,
        grid=(b, s // tq),
        in_specs=[
            pl.BlockSpec((None, tq, MIX_W), lambda bb, i: (bb, i, 1)),
            pl.BlockSpec((None, s, MIX_W), lambda bb, i: (bb, 0, 2)),
            pl.BlockSpec((None, nc, MIX_W, t), lambda bb, i: (bb, 0, 1, 0)),
            pl.BlockSpec((None, rows, MIX_W), lambda bb, i: (bb, 0, 0)),
        ],
        out_specs=pl.BlockSpec((None, tq, MIX_W), lambda bb, i: (bb, i, 0)),
        out_shape=jax.ShapeDtypeStruct((b, s, MIX_W), BF16),
        scratch_shapes=[pltpu.VMEM((HEADS, rows, tq), F32)] + _softmax_scratch(HEADS, tq),
        compiler_params=_params(("arbitrary", "arbitrary")),
    )(p128, p128, vt, kmean)


def _dilated_kernel(q_ref, k_ref, vt_ref, o_ref, m_ref, l_ref, acc_ref):
    tq = q_ref.shape[0]
    t = vt_ref.shape[2]
    i = pl.program_id(1)
    _init_softmax(m_ref, l_ref, acc_ref)
    rel = lax.broadcasted_iota(I32, (t, tq), 1) - lax.broadcasted_iota(I32, (t, tq), 0)

    def first_chunk_within(window):
        return jnp.maximum(i * tq - window, 0) // t

    def step(c, patterns):
        k_c = k_ref[pl.ds(pl.multiple_of(c * t, t), t), :]
        v_c = vt_ref[c]
        dist = rel + (i * tq - c * t)
        mult = jnp.zeros((t, tq), F32)
        for window, dilation in patterns:
            hit = (dist >= 0) & (dist <= window) & ((dist & (dilation - 1)) == 0)
            mult = mult + jnp.where(hit, 1.0, 0.0)
        mask = mult > 0.0
        scores = [_dot_nt(_head(k_c, h), _head(q_ref, h)) for h in range(HEADS)]
        values = [v_c[h * HEAD_DIM:(h + 1) * HEAD_DIM, :] for h in range(HEADS)]
        _online_updates(scores, mask, values, m_ref, l_ref, acc_ref, weight=mult)

    widest = max(DILATED_PATTERNS)
    others = [p for p in DILATED_PATTERNS if p != widest]
    first = first_chunk_within(widest[0])
    near_first = first_chunk_within(max(w for w, _ in others))

    def far_step(c, carry):
        step(c, [widest])
        return carry

    def near_step(c, carry):
        step(c, DILATED_PATTERNS)
        return carry

    lax.fori_loop(first, near_first, far_step, 0)
    lax.fori_loop(near_first, (i + 1) * (tq // t), near_step, 0)
    for h in range(HEADS):
        o_ref[:, h * HEAD_DIM:(h + 1) * HEAD_DIM] = (acc_ref[h] / l_ref[h]).T.astype(o_ref.dtype)


def _dilated(p128, vt):
    b, s, _ = p128.shape
    t = min(ATT_T, s)
    tq = min(DIL_TQ, s)
    nc = s // t
    return pl.pallas_call(
        _dilated_kernel,
        grid=(b, s // tq),
        in_specs=[
            pl.BlockSpec((None, tq, MIX_W), lambda bb, i: (bb, i, 3)),
            pl.BlockSpec((None, s, MIX_W), lambda bb, i: (bb, 0, 4)),
            pl.BlockSpec((None, nc, MIX_W, t), lambda bb, i: (bb, 0, 2, 0)),
        ],
        out_specs=pl.BlockSpec((None, tq, MIX_W), lambda bb, i: (bb, i, 0)),
        out_shape=jax.ShapeDtypeStruct((b, s, MIX_W), BF16),
        scratch_shapes=_softmax_scratch(HEADS, tq),
        compiler_params=_params(("arbitrary", "arbitrary")),
    )(p128, p128, vt)


def _layer_norm(z, g, b):
    mu = jnp.mean(z, axis=1, keepdims=True)
    zc = z - mu
    var = jnp.mean(zc * zc, axis=1, keepdims=True)
    return zc * lax.rsqrt(var + LN_EPS) * g + b


def _outproj_ln_kernel(oa_ref, ob_ref, oc_ref, od_ref, w_ref, x_ref, g_ref, b_ref, xo_ref, xb_ref):
    rows_per_part = x_ref.shape[0] // OUT_PARTS
    for part in range(OUT_PARTS):
        rows = slice(part * rows_per_part, (part + 1) * rows_per_part)
        y = jnp.dot(oa_ref[rows, :], w_ref[0:MIX_W, :], preferred_element_type=F32)
        y = y + jnp.dot(ob_ref[rows, :], w_ref[MIX_W:2 * MIX_W, :], preferred_element_type=F32)
        y = y + jnp.dot(oc_ref[rows, :], w_ref[2 * MIX_W:3 * MIX_W, :], preferred_element_type=F32)
        y = y + jnp.dot(od_ref[rows, :], w_ref[3 * MIX_W:4 * MIX_W, :], preferred_element_type=F32)
        out = _layer_norm(ALPHA * x_ref[rows, :] + y, g_ref[...], b_ref[...])
        xo_ref[rows, :] = out
        xb_ref[rows, :] = out.astype(BF16)


def _outproj_ln(oa, ob, oc, od, w_bf, x, g, b):
    t, d = x.shape
    tm = min(OUT_TM, t)
    mix_spec = pl.BlockSpec((tm, MIX_W), lambda i: (i, 0))
    row_spec = pl.BlockSpec((tm, d), lambda i: (i, 0))
    vec_spec = pl.BlockSpec((1, d), lambda i: (0, 0))
    return pl.pallas_call(
        _outproj_ln_kernel,
        grid=(t // tm,),
        in_specs=[mix_spec, mix_spec, mix_spec, mix_spec,
                  pl.BlockSpec((d, d), lambda i: (0, 0)), row_spec, vec_spec, vec_spec],
        out_specs=[row_spec, row_spec],
        out_shape=[jax.ShapeDtypeStruct((t, d), F32), jax.ShapeDtypeStruct((t, d), BF16)],
        compiler_params=_params(("arbitrary",)),
    )(oa, ob, oc, od, w_bf, x, g, b)


def _first_argmax(vals):
    best, arg = vals[0], jnp.zeros(vals[0].shape, I32)
    for j in range(1, len(vals)):
        better = vals[j] > best
        arg = jnp.where(better, j, arg)
        best = jnp.where(better, vals[j], best)
    return arg, best


def _pick(rows, idx):
    out = rows[0]
    for j in range(1, len(rows)):
        out = jnp.where(idx == j, rows[j], out)
    return out


def _router_kernel(x_ref, rwt_ref, bias_ref, e_ref, gate_ref, rank_ref, cnt_ref, carry_ref):
    tm = x_ref.shape[0]

    @pl.when(pl.program_id(0) == 0)
    def _():
        carry_ref[...] = jnp.zeros(carry_ref.shape, F32)

    logits = lax.dot_general(rwt_ref[...], x_ref[...], (((1,), (1,)), ((), ())),
                             precision=lax.Precision.HIGHEST, preferred_element_type=F32)
    aff = 1.0 / (1.0 + jnp.exp(-logits))
    biased = aff + bias_ref[...]
    sel_rows = [biased[r:r + 1, :] for r in range(N_EXPERTS)]
    aff_rows = [aff[r:r + 1, :] for r in range(N_EXPERTS)]

    group_scores = []
    for g in range(N_GROUPS):
        r = sel_rows[g * EXPERTS_PER_GROUP:(g + 1) * EXPERTS_PER_GROUP]
        best_pair = r[0] + r[1]
        for a in range(EXPERTS_PER_GROUP):
            for c in range(a + 1, EXPERTS_PER_GROUP):
                if (a, c) != (0, 1):
                    best_pair = jnp.maximum(best_pair, r[a] + r[c])
        group_scores.append(best_pair)
    grp, _ = _first_argmax(group_scores)

    in_sel = [_pick([sel_rows[g * EXPERTS_PER_GROUP + j] for g in range(N_GROUPS)], grp)
              for j in range(EXPERTS_PER_GROUP)]
    in_aff = [_pick([aff_rows[g * EXPERTS_PER_GROUP + j] for g in range(N_GROUPS)], grp)
              for j in range(EXPERTS_PER_GROUP)]
    first, _ = _first_argmax(in_sel)
    second, _ = _first_argmax([jnp.where(first == j, -jnp.inf, in_sel[j]) for j in range(EXPERTS_PER_GROUP)])
    a0 = _pick(in_aff, first)
    a1 = _pick(in_aff, second)
    e0 = grp * EXPERTS_PER_GROUP + first
    e1 = grp * EXPERTS_PER_GROUP + second
    e_ref[0:1, :] = e0
    e_ref[1:2, :] = e1
    gate_ref[0:1, :] = a0 / (a0 + a1)
    gate_ref[1:2, :] = a1 / (a0 + a1)

    e_iota = lax.broadcasted_iota(I32, (N_EXPERTS, tm), 0)
    one_hot = ((e_iota == e0) | (e_iota == e1)).astype(BF16)
    before = (lax.broadcasted_iota(I32, (tm, tm), 0) < lax.broadcasted_iota(I32, (tm, tm), 1)).astype(BF16)
    prior = jnp.dot(one_hot, before, preferred_element_type=F32) + carry_ref[...]
    rank_ref[0:1, :] = jnp.sum(jnp.where(e_iota == e0, prior, 0.0), axis=0, keepdims=True).astype(I32)
    rank_ref[1:2, :] = jnp.sum(jnp.where(e_iota == e1, prior, 0.0), axis=0, keepdims=True).astype(I32)
    carry_ref[...] = carry_ref[...] + jnp.sum(one_hot.astype(F32), axis=1, keepdims=True)
    cnt_ref[...] = jnp.broadcast_to(carry_ref[...], cnt_ref.shape).astype(I32)


def _router(x, rw_t, bias):
    t, d = x.shape
    tm = min(ROUTER_TM, t)
    tok_spec = pl.BlockSpec((2, tm), lambda i: (0, i))
    return pl.pallas_call(
        _router_kernel,
        grid=(t // tm,),
        in_specs=[pl.BlockSpec((tm, d), lambda i: (i, 0)),
                  pl.BlockSpec((N_EXPERTS, d), lambda i: (0, 0)),
                  pl.BlockSpec((N_EXPERTS, 1), lambda i: (0, 0))],
        out_specs=[tok_spec, tok_spec, tok_spec, pl.BlockSpec((N_EXPERTS, LANES), lambda i: (0, 0))],
        out_shape=[jax.ShapeDtypeStruct((2, t), I32), jax.ShapeDtypeStruct((2, t), F32),
                   jax.ShapeDtypeStruct((2, t), I32), jax.ShapeDtypeStruct((N_EXPERTS, LANES), I32)],
        scratch_shapes=[pltpu.VMEM((N_EXPERTS, 1), F32)],
        compiler_params=_params(("arbitrary",)),
    )(x, rw_t, bias)


def _row_copy(src, src_row, dst, dst_row, sem):
    return pltpu.make_async_copy(src.at[pl.ds(src_row, 1)], dst.at[pl.ds(dst_row, 1)], sem)


def _dispatch_kernel(dest_ref, pend_ref, na_ref, x_ref, slots_hbm, zeros_ref, sem, zero_sem,
                     *, tm, n_tok, tb):
    base = pl.program_id(0) * tm

    @pl.when(pl.program_id(0) == 0)
    def _():
        zeros_ref[...] = jnp.zeros(zeros_ref.shape, zeros_ref.dtype)
        n_blocks = slots_hbm.shape[0] // tb

        def block_fill(start):
            return pltpu.make_async_copy(zeros_ref, slots_hbm.at[pl.ds(start, tb)], zero_sem)

        def for_each_fill(act):
            for e in range(N_EXPERTS):
                seg_start = pend_ref[e - 1] if e else 0

                @pl.when(pend_ref[e] > seg_start)
                def _():
                    act(pl.multiple_of(pend_ref[e] - tb, tb))

            def tail(n, carry):
                act(pl.multiple_of(n * tb, tb))
                return carry

            lax.fori_loop(na_ref[0], n_blocks, tail, 0)

        for_each_fill(lambda start: block_fill(start).start())
        for_each_fill(lambda start: block_fill(start).wait())

    def issue(r, carry):
        tok = base + r
        _row_copy(x_ref, r, slots_hbm, dest_ref[tok], sem).start()
        _row_copy(x_ref, r, slots_hbm, dest_ref[n_tok + tok], sem).start()
        return carry

    def drain(r, carry):
        _row_copy(x_ref, 0, slots_hbm, 0, sem).wait()
        _row_copy(x_ref, 0, slots_hbm, 0, sem).wait()
        return carry

    lax.fori_loop(0, tm, issue, 0, unroll=DMA_UNROLL)
    lax.fori_loop(0, tm, drain, 0, unroll=DMA_UNROLL)


def _dispatch(dest_flat, pend, n_active, x, n_slots):
    t, d = x.shape
    tm = min(DISPATCH_TM, t)
    tb = EXPERT_TB
    return pl.pallas_call(
        functools.partial(_dispatch_kernel, tm=tm, n_tok=t, tb=tb),
        grid_spec=pltpu.PrefetchScalarGridSpec(
            num_scalar_prefetch=3,
            grid=(t // tm,),
            in_specs=[pl.BlockSpec((tm, d), lambda i, dest, pend, na: (i, 0))],
            out_specs=pl.BlockSpec(memory_space=pl.ANY),
            scratch_shapes=[pltpu.VMEM((tb, d), x.dtype), pltpu.SemaphoreType.DMA(()),
                            pltpu.SemaphoreType.DMA(())],
        ),
        out_shape=jax.ShapeDtypeStruct((n_slots, d), x.dtype),
        compiler_params=_params(("arbitrary",)),
    )(dest_flat, pend, n_active, x)


def _last_active(n, na_ref):
    return jnp.maximum(jnp.minimum(n, na_ref[0] - 1), 0)


def _expert_stage(be_ref, nxt_ref, na_ref, w_hbm_refs, stage_refs, w_bf_refs, sem, compute, out_ref, *, layer):
    n = pl.program_id(0)
    active = n < na_ref[0]
    new_expert = (n == 0) | (be_ref[n] != be_ref[jnp.maximum(n - 1, 0)])

    def copies(expert):
        return [pltpu.make_async_copy(w_hbm.at[layer, expert], stage, sem.at[k])
                for k, (w_hbm, stage) in enumerate(zip(w_hbm_refs, stage_refs))]

    @pl.when(active & (n == 0))
    def _():
        for copy in copies(be_ref[0]):
            copy.start()

    @pl.when(active & new_expert)
    def _():
        for copy, stage, w_bf in zip(copies(be_ref[n]), stage_refs, w_bf_refs):
            copy.wait()
            w_bf[...] = stage[...].astype(BF16)

        @pl.when(nxt_ref[n] >= 0)
        def _():
            for copy in copies(nxt_ref[n]):
                copy.start()

    @pl.when(active)
    def _():
        out_ref[...] = compute().astype(out_ref.dtype)

    @pl.when(jnp.logical_not(active))
    def _():
        out_ref[...] = jnp.zeros(out_ref.shape, out_ref.dtype)


def _expert_up_kernel(be_ref, nxt_ref, na_ref, xs_ref, wg_hbm, wu_hbm, h_ref,
                      wg_stage, wu_stage, wg_bf, wu_bf, sem, *, layer):
    def compute():
        xb = xs_ref[...].astype(BF16)
        gate = jnp.dot(xb, wg_bf[...], preferred_element_type=F32)
        up = jnp.dot(xb, wu_bf[...], preferred_element_type=F32)
        return gate * (1.0 / (1.0 + jnp.exp(-gate))) * up

    _expert_stage(be_ref, nxt_ref, na_ref, (wg_hbm, wu_hbm), (wg_stage, wu_stage), (wg_bf, wu_bf), sem,
                  compute, h_ref, layer=layer)


def _expert_down_kernel(be_ref, nxt_ref, na_ref, h_ref, wd_hbm, ys_ref, wd_stage, wd_bf, sem, *, layer):
    def compute():
        return jnp.dot(h_ref[...], wd_bf[...], preferred_element_type=F32)

    _expert_stage(be_ref, nxt_ref, na_ref, (wd_hbm,), (wd_stage,), (wd_bf,), sem, compute, ys_ref,
                  layer=layer)


def _experts(block_expert, next_expert, n_active, xs, wg, wu, wd, layer):
    n_slots, d = xs.shape
    tb = EXPERT_TB
    f = wg.shape[3]

    def row_map(n, be, nxt, na):
        return (_last_active(n, na), 0)

    def out_map(n, be, nxt, na):
        return (n, 0)

    hbm = pl.BlockSpec(memory_space=pl.ANY)
    hidden = pl.pallas_call(
        functools.partial(_expert_up_kernel, layer=layer),
        grid_spec=pltpu.PrefetchScalarGridSpec(
            num_scalar_prefetch=3,
            grid=(n_slots // tb,),
            in_specs=[pl.BlockSpec((tb, d), row_map), hbm, hbm],
            out_specs=pl.BlockSpec((tb, f), out_map),
            scratch_shapes=[pltpu.VMEM((d, f), F32), pltpu.VMEM((d, f), F32),
                            pltpu.VMEM((d, f), BF16), pltpu.VMEM((d, f), BF16),
                            pltpu.SemaphoreType.DMA((2,))],
        ),
        out_shape=jax.ShapeDtypeStruct((n_slots, f), BF16),
        compiler_params=_params(("arbitrary",)),
    )(block_expert, next_expert, n_active, xs, wg, wu)
    return pl.pallas_call(
        functools.partial(_expert_down_kernel, layer=layer),
        grid_spec=pltpu.PrefetchScalarGridSpec(
            num_scalar_prefetch=3,
            grid=(n_slots // tb,),
            in_specs=[pl.BlockSpec((tb, f), row_map), hbm],
            out_specs=pl.BlockSpec((tb, d), out_map),
            scratch_shapes=[pltpu.VMEM((f, d), F32), pltpu.VMEM((f, d), BF16),
                            pltpu.SemaphoreType.DMA((1,))],
        ),
        out_shape=jax.ShapeDtypeStruct((n_slots, d), F32),
        compiler_params=_params(("arbitrary",)),
    )(block_expert, next_expert, n_active, hidden, wd)


def _combine_ln_kernel(dest_ref, x_ref, gate_ref, g_ref, b_ref, ys_hbm, xo_ref, xb_ref, buf_ref, sem,
                       *, n_tok):
    tm = x_ref.shape[0]
    i = pl.program_id(0)
    slot = i % 2

    def request(tile, to_slot):
        def issue(r, carry):
            tok = tile * tm + r
            _row_copy(ys_hbm, dest_ref[tok], buf_ref.at[to_slot, 0], r, sem.at[to_slot]).start()
            _row_copy(ys_hbm, dest_ref[n_tok + tok], buf_ref.at[to_slot, 1], r, sem.at[to_slot]).start()
            return carry
        lax.fori_loop(0, tm, issue, 0, unroll=DMA_UNROLL)

    @pl.when(i == 0)
    def _():
        request(0, 0)

    @pl.when(i + 1 < pl.num_programs(0))
    def _():
        request(i + 1, 1 - slot)

    def drain(r, carry):
        _row_copy(ys_hbm, 0, buf_ref.at[slot, 0], 0, sem.at[slot]).wait()
        _row_copy(ys_hbm, 0, buf_ref.at[slot, 1], 0, sem.at[slot]).wait()
        return carry

    lax.fori_loop(0, tm, drain, 0, unroll=DMA_UNROLL)
    gate = gate_ref[...]
    moe = buf_ref[slot, 0] * gate[:, 0:1] + buf_ref[slot, 1] * gate[:, 1:2]
    out = _layer_norm(ALPHA * x_ref[...] + moe, g_ref[...], b_ref[...])
    xo_ref[...] = out
    xb_ref[...] = out.astype(BF16)


def _combine_ln(dest_flat, x, gate_t, g, b, ys):
    t, d = x.shape
    tm = min(COMBINE_TM, t)
    row_spec = pl.BlockSpec((tm, d), lambda i, dest: (i, 0))
    vec_spec = pl.BlockSpec((1, d), lambda i, dest: (0, 0))
    return pl.pallas_call(
        functools.partial(_combine_ln_kernel, n_tok=t),
        grid_spec=pltpu.PrefetchScalarGridSpec(
            num_scalar_prefetch=1,
            grid=(t // tm,),
            in_specs=[row_spec, pl.BlockSpec((tm, 2), lambda i, dest: (i, 0)), vec_spec, vec_spec,
                      pl.BlockSpec(memory_space=pl.ANY)],
            out_specs=[row_spec, row_spec],
            scratch_shapes=[pltpu.VMEM((2, 2, tm, d), F32), pltpu.SemaphoreType.DMA((2,))],
        ),
        out_shape=[jax.ShapeDtypeStruct((t, d), F32), jax.ShapeDtypeStruct((t, d), BF16)],
        compiler_params=_params(("arbitrary",)),
    )(dest_flat, x, gate_t, g, b, ys)


def _rope_tables(positions):
    pos = positions.reshape(-1).astype(F32)[:, None]

    def cos_sin(d):
        inv = jnp.power(ROPE_THETA, -jnp.arange(0, d, 2, dtype=F32) / d)
        ang = pos * inv
        return jnp.cos(ang), jnp.sin(ang)

    c, s = cos_sin(HEAD_DIM)
    t128 = (jnp.concatenate([c, c], axis=1), jnp.concatenate([-s, s], axis=1))
    c, s = cos_sin(IDX_DIM)
    z = jnp.zeros_like(s)
    t64 = (jnp.concatenate([c, c, c, c], axis=1),
           jnp.concatenate([-s, z, -s, z], axis=1),
           jnp.concatenate([z, s, z, s], axis=1))
    return t128, t64


def _col_scales():
    q128 = HEAD_DIM ** -0.5 * LOG2E
    s128 = np.ones((1, 5 * MIX_W + HEAD_DIM), np.float32)
    s128[:, 0:2 * MIX_W] = q128
    s128[:, 3 * MIX_W:4 * MIX_W] = q128
    s64 = np.ones((1, 2304), np.float32)
    s64[:, 1024:1024 + MIX_W] = DIFF_DIM ** -0.5 * LOG2E
    return jnp.asarray(s128), jnp.asarray(s64)


def _split_w_in(w):
    def cols(name):
        lo, hi = _OFF[name]
        return w[:, lo:hi]

    d = w.shape[0]
    w128 = jnp.concatenate([cols(n) for n in ("a_q", "c_q", "c_k", "d_q", "d_k", "a_k")], axis=1)
    w64 = jnp.concatenate([cols("i_q"), cols("b_q"), cols("b_k"), cols("i_k"), cols("i_k"),
                           jnp.zeros((d, LANES), w.dtype)], axis=1)
    wt = jnp.concatenate([cols("b_v"), cols("c_v"), cols("d_v"), cols("a_v"), cols("i_w")], axis=1).T
    return w128.astype(BF16), w64.astype(BF16), wt.astype(BF16)


def _moe_layout(e, rank, counts, tb):
    t = e.shape[1]
    n_slots = 2 * t + N_EXPERTS * tb
    padded = (counts + tb - 1) // tb * tb
    pend = jnp.cumsum(padded)
    pstart = pend - padded
    experts = jnp.arange(N_EXPERTS, dtype=I32)[:, None, None]
    seg_start = jnp.sum(jnp.where(e[None] == experts, pstart[:, None, None], 0), axis=0)
    dest = (seg_start + rank).astype(I32).reshape(-1)
    blocks = jnp.arange(n_slots // tb, dtype=I32) * tb
    block_expert = jnp.minimum(jnp.sum(blocks[:, None] >= pend[None, :], axis=1), N_EXPERTS - 1).astype(I32)
    n_active = (pend[-1:] // tb).astype(I32)
    n_blocks = n_slots // tb
    blk = jnp.arange(n_blocks, dtype=I32)
    later = ((blk[None, :] > blk[:, None]) & (block_expert[None, :] != block_expert[:, None])
             & (blk[None, :] < n_active[0]))
    first_later = jnp.min(jnp.where(later, blk[None, :], n_blocks), axis=1)
    next_expert = jnp.sum(jnp.where(blk[None, :] == first_later[:, None], block_expert[None, :] + 1, 0),
                          axis=1).astype(I32) - 1
    return dest, pend.astype(I32), block_expert, next_expert, n_active, n_slots


def kernel(x, positions, w_in, w_out, diff_lambda, diff_norm_g, ln_mix_g, ln_mix_b, router_w, router_bias,
           w_gate, w_up, w_down, ln_ffn_g, ln_ffn_b):
    b, s, d = x.shape
    t = b * s
    n_keep = min(DSA_TOPK_MAX, s // 4)
    t128, t64 = _rope_tables(positions)
    s128, s64 = _col_scales()
    rw_t = router_w.T
    bias = router_bias.reshape(N_EXPERTS, 1).astype(F32)
    xf = x.reshape(t, d)
    xb = xf
    nc = s // min(ATT_T, s)
    for layer in range(DEPTH):
        w128, w64, wt = _split_w_in(w_in[layer])
        p128 = _proj(xb, w128, s128, t128, "rope128", w128.shape[1]).reshape(b, s, -1)
        p64 = _proj(xb, w64, s64, t64, "rope64", 768).reshape(b, s, -1)
        vt, iwt = _proj_t(xb, wt)
        vt = vt.reshape(b, nc, V_ROWS, -1)
        iwt = iwt.reshape(b, nc, IDX_HEADS, -1)

        lam_init = 0.8 - 0.6 * math.exp(-0.3 * layer)
        o_a = _dsa(p128, p64, vt, iwt, n_keep)
        o_b = _diff(p64, vt, diff_lambda[layer], diff_norm_g[layer].reshape(HEAD_DIM, 1), lam_init)
        o_c = _moba(p128, vt, _kmean(p128))
        o_d = _dilated(p128, vt)
        xf, xb = _outproj_ln(o_a.reshape(t, MIX_W), o_b.reshape(t, MIX_W), o_c.reshape(t, MIX_W),
                             o_d.reshape(t, MIX_W), w_out[layer].astype(BF16), xf,
                             ln_mix_g[layer].reshape(1, d), ln_mix_b[layer].reshape(1, d))

        e, gate, rank, counts = _router(xf, rw_t, bias)
        dest, pend, block_expert, next_expert, n_active, n_slots = _moe_layout(e, rank, counts[:, 0], EXPERT_TB)
        xs = _dispatch(dest, pend, n_active, xf, n_slots)
        ys = _experts(block_expert, next_expert, n_active, xs, w_gate, w_up, w_down, layer)
        xf, xb = _combine_ln(dest, xf, gate.T, ln_ffn_g[layer].reshape(1, d), ln_ffn_b[layer].reshape(1, d), ys)
    return xf.reshape(b, s, d)
```

```python
import functools
import math

import numpy as np
import jax
import jax.numpy as jnp
from jax import lax
from jax.experimental import pallas as pl
from jax.experimental.pallas import tpu as pltpu

F32 = jnp.float32
BF16 = jnp.bfloat16
I32 = jnp.int32

D_MODEL = 2048
HEAD_DIM = 128
HEADS = 4
MIX_W = HEADS * HEAD_DIM
IDX_HEADS = 16
IDX_DIM = 64
DSA_TOPK_MAX = 256
DIFF_DIM = 64
MOBA_BLOCK = 256
MOBA_TOPK = 3
DILATED_PATTERNS = ((128, 1), (512, 4), (2048, 16))
ROPE_THETA = 10000.0
N_EXPERTS = 16
N_GROUPS = 4
EXPERTS_PER_GROUP = 4
LN_EPS = 1e-5
DEPTH = 2
ALPHA = (2 * DEPTH) ** 0.25
LOG2E = math.log2(math.e)

LANES = 128
SUBLANES = 8
NEG = -1e30
M_FLOOR = -1e29
INT_MIN = -(2 ** 31)
NEG_INF_ORDER = 0x007FFFFF
VMEM_LIMIT = 56 * 1024 * 1024

PROJ_TM = 512
PROJ_PARTS = 4
ATT_T = 256
DIFF_TQ = 512
DSA_TQ = 512
MOBA_TQ = 512
DIL_TQ = 512
OUT_TM = 512
OUT_PARTS = 2
ROUTER_TM = 256
DISPATCH_TM = 256
EXPERT_TB = 256
COMBINE_TM = 256
DMA_UNROLL = 8
V_ROWS = 3 * MIX_W + HEAD_DIM

_OFF = {}
_o = 0
for _name, _n in (("a_q", 512), ("a_k", 128), ("a_v", 128), ("i_q", 1024), ("i_k", 64), ("i_w", 16),
                  ("b_q", 512), ("b_k", 512), ("b_v", 512), ("c_q", 512), ("c_k", 512), ("c_v", 512),
                  ("d_q", 512), ("d_k", 512), ("d_v", 512)):
    _OFF[_name] = (_o, _o + _n)
    _o += _n


def _params(sem=None):
    return pltpu.CompilerParams(dimension_semantics=sem, vmem_limit_bytes=VMEM_LIMIT)


def _dot_nt(a, b):
    return lax.dot_general(a, b, (((1,), (1,)), ((), ())), preferred_element_type=F32)


def _proj_kernel(x_ref, w_ref, scale_ref, *rest, mode):
    o_ref = rest[-1]
    tm = x_ref.shape[0]
    rows_per_part = tm // PROJ_PARTS
    n_groups = w_ref.shape[1] // LANES
    for part in range(PROJ_PARTS):
        rows = slice(part * rows_per_part, (part + 1) * rows_per_part)
        acc = jnp.dot(x_ref[rows, :].astype(BF16), w_ref[...], preferred_element_type=F32)
        for g in range(n_groups):
            cols = slice(g * LANES, (g + 1) * LANES)
            y = acc[:, cols]
            if mode == "rope128":
                cos_ref, sin_ref = rest[0], rest[1]
                r = y * cos_ref[rows, :] + pltpu.roll(y, 64, 1) * sin_ref[rows, :]
            else:
                cos_ref, sina_ref, sinb_ref = rest[0], rest[1], rest[2]
                r = (y * cos_ref[rows, :] + pltpu.roll(y, 96, 1) * sina_ref[rows, :]
                     + pltpu.roll(y, 32, 1) * sinb_ref[rows, :])
            r = r * scale_ref[:, cols]
            o_ref[rows, cols] = r.astype(o_ref.dtype)


def _proj(x_bf, w_bf, col_scale, tables, mode, tn):
    t, d = x_bf.shape
    n = w_bf.shape[1]
    tm = min(PROJ_TM, t)
    in_specs = [pl.BlockSpec((tm, d), lambda j, i: (i, 0)),
                pl.BlockSpec((d, tn), lambda j, i: (0, j)),
                pl.BlockSpec((1, tn), lambda j, i: (0, j))]
    in_specs += [pl.BlockSpec((tm, LANES), lambda j, i: (i, 0)) for _ in tables]
    return pl.pallas_call(
        functools.partial(_proj_kernel, mode=mode),
        grid=(n // tn, t // tm),
        in_specs=in_specs,
        out_specs=pl.BlockSpec((tm, tn), lambda j, i: (i, j)),
        out_shape=jax.ShapeDtypeStruct((t, n), BF16),
        compiler_params=_params(("arbitrary", "arbitrary")),
    )(x_bf, w_bf, col_scale, *tables)


def _proj_t_kernel(x_ref, wt_ref, vt_ref, iwt_ref):
    acc = _dot_nt(wt_ref[...], x_ref[...].astype(BF16))
    vt_ref[...] = acc[:V_ROWS].astype(vt_ref.dtype)
    iwt_ref[...] = acc[V_ROWS:]


def _proj_t(x_bf, wt_bf):
    t, d = x_bf.shape
    tm = min(ATT_T, t)
    rows = wt_bf.shape[0]
    return pl.pallas_call(
        _proj_t_kernel,
        grid=(t // tm,),
        in_specs=[pl.BlockSpec((tm, d), lambda i: (i, 0)),
                  pl.BlockSpec((rows, d), lambda i: (0, 0))],
        out_specs=[pl.BlockSpec((None, V_ROWS, tm), lambda i: (i, 0, 0)),
                   pl.BlockSpec((None, IDX_HEADS, tm), lambda i: (i, 0, 0))],
        out_shape=[jax.ShapeDtypeStruct((t // tm, V_ROWS, tm), BF16),
                   jax.ShapeDtypeStruct((t // tm, IDX_HEADS, tm), F32)],
        compiler_params=_params(("arbitrary",)),
    )(x_bf, wt_bf)


def _online_updates(scores, mask, values, m_ref, l_ref, acc_ref, weight=None):
    probs, alphas = [], []
    masks = mask if isinstance(mask, list) else [mask] * len(scores)
    for idx, (s_t, mask) in enumerate(zip(scores, masks)):
        if mask is not None:
            s_t = jnp.where(mask, s_t, NEG)
        m_prev = m_ref[idx]
        m_new = jnp.maximum(m_prev, jnp.max(s_t, axis=0, keepdims=True))
        p = jnp.exp2(s_t - m_new)
        if weight is not None:
            p = p * weight
        alpha = jnp.exp2(m_prev - m_new)
        l_ref[idx] = alpha * l_ref[idx] + jnp.sum(p, axis=0, keepdims=True)
        m_ref[idx] = m_new
        probs.append(p.astype(BF16))
        alphas.append(alpha)
    for idx, (p, alpha) in enumerate(zip(probs, alphas)):
        acc_ref[idx] = alpha * acc_ref[idx] + jnp.dot(values[idx], p, preferred_element_type=F32)


def _init_softmax(m_ref, l_ref, acc_ref):
    m_ref[...] = jnp.full(m_ref.shape, M_FLOOR, F32)
    l_ref[...] = jnp.zeros(l_ref.shape, F32)
    acc_ref[...] = jnp.zeros(acc_ref.shape, F32)


def _softmax_scratch(n, t):
    return [pltpu.VMEM((n, 1, t), F32), pltpu.VMEM((n, 1, t), F32), pltpu.VMEM((n, HEAD_DIM, t), F32)]


def _causal_mask(tk, tq, k_start):
    return k_start + lax.broadcasted_iota(I32, (tk, tq), 0) <= lax.broadcasted_iota(I32, (tk, tq), 1)


def _head(ref_or_val, h):
    return ref_or_val[:, h * HEAD_DIM:(h + 1) * HEAD_DIM]


def _order_to_f32(u):
    key = u ^ jnp.int32(INT_MIN)
    bits = jnp.where(key < 0, key ^ jnp.int32(0x7FFFFFFF), key)
    return lax.bitcast_convert_type(bits, F32)


def _dsa_kernel(aq_ref, ak_ref, avt_ref, iq_ref, ik_ref, iwt_ref, o_ref,
                score_ref, m_ref, l_ref, acc_ref, jlim_ref, *, n_keep):
    tq = aq_ref.shape[0]
    t = avt_ref.shape[2]
    i = pl.program_id(1)
    n_chunks = (i + 1) * (tq // t)
    lane = lax.broadcasted_iota(I32, (1, LANES), 1)
    k_off = lax.broadcasted_iota(I32, (t, 1), 0)
    q_pos = i * tq + lax.broadcasted_iota(I32, (1, tq), 1)
    iw_t = jnp.concatenate([iwt_ref[n] for n in range(tq // t)], axis=1)

    def score_chunk(c, carry):
        ik_c = ik_ref[pl.ds(pl.multiple_of(c * t, t), t), :]
        ik_lo = jnp.where(lane < IDX_DIM, ik_c, jnp.zeros_like(ik_c))
        ik_hi = jnp.where(lane >= IDX_DIM, ik_c, jnp.zeros_like(ik_c))
        score = jnp.zeros((t, tq), F32)
        for j in range(IDX_HEADS // 2):
            q_pair = iq_ref[:, j * LANES:(j + 1) * LANES]
            s_even = jnp.maximum(_dot_nt(ik_lo, q_pair), 0.0)
            s_odd = jnp.maximum(_dot_nt(ik_hi, q_pair), 0.0)
            score = score + s_even * iw_t[2 * j:2 * j + 1, :] + s_odd * iw_t[2 * j + 1:2 * j + 2, :]
        score_ref[c] = jnp.where(c * t + k_off <= q_pos, score, -jnp.inf)
        return carry

    lax.fori_loop(0, n_chunks, score_chunk, 0)

    def count(pred):
        def body(c, part):
            hit = jnp.where(pred(score_ref[c], c), 1, 0)
            return part + jnp.sum(hit.reshape(t // SUBLANES, SUBLANES, tq), axis=0)
        part = lax.fori_loop(0, n_chunks, body, jnp.zeros((SUBLANES, tq), I32))
        return jnp.sum(part, axis=0, keepdims=True)

    def bisect(it, carry):
        thr_u, n_ge = carry
        cand_u = thr_u | jnp.left_shift(jnp.int32(1), 31 - it)
        cand = _order_to_f32(cand_u)
        cnt = count(lambda sc, c: sc >= cand)
        keep = cnt >= n_keep
        return jnp.where(keep, cand_u, thr_u), jnp.where(keep, cnt, n_ge)

    zeros = jnp.zeros((1, tq), I32)
    thr_u, n_ge = lax.fori_loop(0, 32, bisect, (zeros, zeros))
    below = (thr_u >= 0) & (thr_u < NEG_INF_ORDER)
    thr = jnp.where(below, -jnp.inf, _order_to_f32(thr_u))

    s_total = score_ref.shape[0] * t
    jlim_ref[...] = jnp.full((1, tq), s_total, I32)
    excess = jnp.max(jnp.where((n_ge > n_keep) & (thr > -jnp.inf), 1, 0))

    @pl.when(excess > 0)
    def _():
        n_bits = max(1, (s_total - 1).bit_length())
        need = n_keep - count(lambda sc, c: sc > thr)

        def bisect_idx(it, j):
            cand = j | jnp.left_shift(jnp.int32(1), n_bits - 1 - it)
            cnt = count(lambda sc, c: (sc == thr) & (c * t + k_off < cand))
            return jnp.where(cnt < need, cand, j)

        jlim_ref[...] = lax.fori_loop(0, n_bits, bisect_idx, jnp.zeros((1, tq), I32))

    _init_softmax(m_ref, l_ref, acc_ref)
    jlim = jnp.where(thr > -jnp.inf, jlim_ref[...], -1)
    thr_all = jnp.where(thr > -jnp.inf, thr, jnp.finfo(F32).min)

    def attend(select):
        def body(c, carry):
            start = pl.multiple_of(c * t, t)
            k_c = ak_ref[pl.ds(start, t), :]
            scores = [_dot_nt(k_c, _head(aq_ref, h)) for h in range(HEADS)]
            sel = select(score_ref[c], c * t + k_off)
            _online_updates(scores, sel, [avt_ref[c]] * HEADS, m_ref, l_ref, acc_ref)
            return carry
        lax.fori_loop(0, n_chunks, body, 0)

    @pl.when(excess > 0)
    def _():
        attend(lambda sc, k_pos: (sc > thr) | ((sc == thr) & (k_pos <= jlim)))

    @pl.when(excess <= 0)
    def _():
        attend(lambda sc, k_pos: sc >= thr_all)
    for h in range(HEADS):
        o_ref[:, h * HEAD_DIM:(h + 1) * HEAD_DIM] = (acc_ref[h] / l_ref[h]).T.astype(o_ref.dtype)


def _dsa(p128, p64, vt, iwt, n_keep):
    b, s, _ = p128.shape
    t = min(ATT_T, s)
    tq = min(DSA_TQ, s)
    nc = s // t
    return pl.pallas_call(
        functools.partial(_dsa_kernel, n_keep=n_keep),
        grid=(b, s // tq),
        in_specs=[
            pl.BlockSpec((None, tq, MIX_W), lambda bb, i: (bb, i, 0)),
            pl.BlockSpec((None, s, HEAD_DIM), lambda bb, i: (bb, 0, 20)),
            pl.BlockSpec((None, nc, HEAD_DIM, t), lambda bb, i: (bb, 0, 12, 0)),
            pl.BlockSpec((None, tq, 1024), lambda bb, i: (bb, i, 0)),
            pl.BlockSpec((None, s, LANES), lambda bb, i: (bb, 0, 16)),
            pl.BlockSpec((None, tq // t, IDX_HEADS, t), lambda bb, i: (bb, i, 0, 0)),
        ],
        out_specs=pl.BlockSpec((None, tq, MIX_W), lambda bb, i: (bb, i, 0)),
        out_shape=jax.ShapeDtypeStruct((b, s, MIX_W), BF16),
        scratch_shapes=[pltpu.VMEM((nc, t, tq), F32)] + _softmax_scratch(HEADS, tq)
        + [pltpu.VMEM((1, tq), I32)],
        compiler_params=_params(("arbitrary", "arbitrary")),
    )(p128, p128, vt, p64, p64, iwt)


def _diff_kernel(q_ref, k_ref, vt_ref, lam_ref, g_ref, o_ref, m_ref, l_ref, acc_ref, *, lam_init):
    tq = q_ref.shape[0]
    t = vt_ref.shape[2]
    per_tile = tq // t
    i = pl.program_id(1)
    lane = lax.broadcasted_iota(I32, (1, LANES), 1)
    _init_softmax(m_ref, l_ref, acc_ref)

    def step(c, mask):
        k_c = k_ref[pl.ds(pl.multiple_of(c * t, t), t), :]
        v_c = vt_ref[c]
        scores, values = [], []
        for h in range(HEADS):
            q_h = _head(q_ref, h)
            k_h = _head(k_c, h)
            q_1 = jnp.where(lane < DIFF_DIM, q_h, jnp.zeros_like(q_h))
            q_2 = jnp.where(lane >= DIFF_DIM, q_h, jnp.zeros_like(q_h))
            scores += [_dot_nt(k_h, q_1), _dot_nt(k_h, q_2)]
            values += [v_c[h * HEAD_DIM:(h + 1) * HEAD_DIM, :]] * 2
        _online_updates(scores, mask, values, m_ref, l_ref, acc_ref)

    def full_step(c, carry):
        step(c, None)
        return carry

    lax.fori_loop(0, i * per_tile, full_step, 0)
    for d in range(per_tile):
        step(i * per_tile + d, _causal_mask(t, tq, d * t))

    lam_p = lam_ref[...]
    lam = (jnp.exp(jnp.sum(lam_p[0:1] * lam_p[1:2], axis=1, keepdims=True))
           - jnp.exp(jnp.sum(lam_p[2:3] * lam_p[3:4], axis=1, keepdims=True)) + lam_init)
    for h in range(HEADS):
        o = acc_ref[2 * h] / l_ref[2 * h] - lam * (acc_ref[2 * h + 1] / l_ref[2 * h + 1])
        o = o * lax.rsqrt(jnp.mean(o * o, axis=0, keepdims=True) + LN_EPS) * g_ref[...]
        o_ref[:, h * HEAD_DIM:(h + 1) * HEAD_DIM] = (o * (1.0 - lam_init)).T.astype(o_ref.dtype)


def _diff(p64, vt, lam_p, g_col, lam_init):
    b, s, _ = p64.shape
    t = min(ATT_T, s)
    tq = min(DIFF_TQ, s)
    nc = s // t
    return pl.pallas_call(
        functools.partial(_diff_kernel, lam_init=lam_init),
        grid=(b, s // tq),
        in_specs=[
            pl.BlockSpec((None, tq, MIX_W), lambda bb, i: (bb, i, 2)),
            pl.BlockSpec((None, s, MIX_W), lambda bb, i: (bb, 0, 3)),
            pl.BlockSpec((None, nc, MIX_W, t), lambda bb, i: (bb, 0, 0, 0)),
            pl.BlockSpec((4, DIFF_DIM), lambda bb, i: (0, 0)),
            pl.BlockSpec((HEAD_DIM, 1), lambda bb, i: (0, 0)),
        ],
        out_specs=pl.BlockSpec((None, tq, MIX_W), lambda bb, i: (bb, i, 0)),
        out_shape=jax.ShapeDtypeStruct((b, s, MIX_W), BF16),
        scratch_shapes=_softmax_scratch(2 * HEADS, tq),
        compiler_params=_params(("arbitrary", "arbitrary")),
    )(p64, p64, vt, lam_p, g_col)


def _kmean_kernel(k_ref, o_ref):
    n_blk = k_ref.shape[0] // MOBA_BLOCK
    o_ref[...] = jnp.zeros(o_ref.shape, o_ref.dtype)
    for n in range(n_blk):
        blk = k_ref[n * MOBA_BLOCK:(n + 1) * MOBA_BLOCK, :].astype(F32)
        o_ref[n:n + 1, :] = jnp.mean(blk, axis=0, keepdims=True)


def _kmean(p128):
    b, s, _ = p128.shape
    rows = max(SUBLANES, s // MOBA_BLOCK)
    return pl.pallas_call(
        _kmean_kernel,
        grid=(b,),
        in_specs=[pl.BlockSpec((None, s, MIX_W), lambda bb: (bb, 0, 2))],
        out_specs=pl.BlockSpec((None, rows, MIX_W), lambda bb: (bb, 0, 0)),
        out_shape=jax.ShapeDtypeStruct((b, rows, MIX_W), F32),
        compiler_params=_params(("arbitrary",)),
    )(p128)


def _moba_kernel(q_ref, k_ref, vt_ref, km_ref, o_ref, sel_ref, m_ref, l_ref, acc_ref):
    tq = q_ref.shape[0]
    t = vt_ref.shape[2]
    per_tile = tq // t
    n_blk = km_ref.shape[0]
    i = pl.program_id(1)
    blk = lax.broadcasted_iota(I32, (n_blk, 1), 0)
    own_rel = lax.broadcasted_iota(I32, (1, tq), 1) // t
    own = i * per_tile + own_rel
    _init_softmax(m_ref, l_ref, acc_ref)

    for h in range(HEADS):
        gate = _dot_nt(_head(km_ref, h).astype(BF16), _head(q_ref, h))
        gate = jnp.where(blk < own, gate, -jnp.inf)
        sel = jnp.zeros(gate.shape, F32)
        for _ in range(MOBA_TOPK):
            best = jnp.max(gate, axis=0, keepdims=True)
            first = jnp.min(jnp.where(gate == best, blk, n_blk), axis=0, keepdims=True)
            pick = (blk == first) & (best > -jnp.inf)
            sel = jnp.where(pick, 1.0, sel)
            gate = jnp.where(pick, -jnp.inf, gate)
        sel_ref[h] = sel

    def step(c, tile_block):
        k_c = k_ref[pl.ds(pl.multiple_of(c * t, t), t), :]
        v_c = vt_ref[c]
        scores = [_dot_nt(_head(k_c, h), _head(q_ref, h)) for h in range(HEADS)]
        values = [v_c[h * HEAD_DIM:(h + 1) * HEAD_DIM, :] for h in range(HEADS)]
        picked = [sel_ref[h, pl.ds(c, 1), :] > 0.0 for h in range(HEADS)]
        if tile_block is None:
            masks = picked
        else:
            causal = (own_rel == tile_block) & _causal_mask(t, tq, tile_block * t)
            masks = [causal | ((own_rel > tile_block) & p) for p in picked]
        _online_updates(scores, masks, values, m_ref, l_ref, acc_ref)

    def past_step(c, carry):
        step(c, None)
        return carry

    lax.fori_loop(0, i * per_tile, past_step, 0)
    for d in range(per_tile):
        step(i * per_tile + d, d)
    for h in range(HEADS):
        o_ref[:, h * HEAD_DIM:(h + 1) * HEAD_DIM] = (acc_ref[h] / l_ref[h]).T.astype(o_ref.dtype)


def _moba(p128, vt, kmean):
    b, s, _ = p128.shape
    t = MOBA_BLOCK
    tq = min(MOBA_TQ, s)
    nc = s // t
    rows = kmean.shape[1]
    return pl.pallas_call(
        _moba_kernel,
        grid=(b, s // tq),
        in_specs=[
            pl.BlockSpec((None, tq, MIX_W), lambda bb, i: (bb, i, 1)),
            pl.BlockSpec((None, s, MIX_W), lambda bb, i: (bb, 0, 2)),
            pl.BlockSpec((None, nc, MIX_W, t), lambda bb, i: (bb, 0, 1, 0)),
            pl.BlockSpec((None, rows, MIX_W), lambda bb, i: (bb, 0, 0)),
        ],
        out_specs=pl.BlockSpec((None, tq, MIX_W), lambda bb, i: (bb, i, 0)),
        out_shape=jax.ShapeDtypeStruct((b, s, MIX_W), BF16),
        scratch_shapes=[pltpu.VMEM((HEADS, rows, tq), F32)] + _softmax_scratch(HEADS, tq),
        compiler_params=_params(("arbitrary", "arbitrary")),
    )(p128, p128, vt, kmean)


def _dilated_kernel(q_ref, k_ref, vt_ref, o_ref, m_ref, l_ref, acc_ref):
    tq = q_ref.shape[0]
    t = vt_ref.shape[2]
    i = pl.program_id(1)
    _init_softmax(m_ref, l_ref, acc_ref)
    rel = lax.broadcasted_iota(I32, (t, tq), 1) - lax.broadcasted_iota(I32, (t, tq), 0)

    def first_chunk_within(window):
        return jnp.maximum(i * tq - window, 0) // t

    def step(c, patterns):
        k_c = k_ref[pl.ds(pl.multiple_of(c * t, t), t), :]
        v_c = vt_ref[c]
        dist = rel + (i * tq - c * t)
        mult = jnp.zeros((t, tq), F32)
        for window, dilation in patterns:
            hit = (dist >= 0) & (dist <= window) & ((dist & (dilation - 1)) == 0)
            mult = mult + jnp.where(hit, 1.0, 0.0)
        mask = mult > 0.0
        scores = [_dot_nt(_head(k_c, h), _head(q_ref, h)) for h in range(HEADS)]
        values = [v_c[h * HEAD_DIM:(h + 1) * HEAD_DIM, :] for h in range(HEADS)]
        _online_updates(scores, mask, values, m_ref, l_ref, acc_ref, weight=mult)

    widest = max(DILATED_PATTERNS)
    others = [p for p in DILATED_PATTERNS if p != widest]
    first = first_chunk_within(widest[0])
    near_first = first_chunk_within(max(w for w, _ in others))

    def far_step(c, carry):
        step(c, [widest])
        return carry

    def near_step(c, carry):
        step(c, DILATED_PATTERNS)
        return carry

    lax.fori_loop(first, near_first, far_step, 0)
    lax.fori_loop(near_first, (i + 1) * (tq // t), near_step, 0)
    for h in range(HEADS):
        o_ref[:, h * HEAD_DIM:(h + 1) * HEAD_DIM] = (acc_ref[h] / l_ref[h]).T.astype(o_ref.dtype)


def _dilated(p128, vt):
    b, s, _ = p128.shape
    t = min(ATT_T, s)
    tq = min(DIL_TQ, s)
    nc = s // t
    return pl.pallas_call(
        _dilated_kernel,
        grid=(b, s // tq),
        in_specs=[
            pl.BlockSpec((None, tq, MIX_W), lambda bb, i: (bb, i, 3)),
            pl.BlockSpec((None, s, MIX_W), lambda bb, i: (bb, 0, 4)),
            pl.BlockSpec((None, nc, MIX_W, t), lambda bb, i: (bb, 0, 2, 0)),
        ],
        out_specs=pl.BlockSpec((None, tq, MIX_W), lambda bb, i: (bb, i, 0)),
        out_shape=jax.ShapeDtypeStruct((b, s, MIX_W), BF16),
        scratch_shapes=_softmax_scratch(HEADS, tq),
        compiler_params=_params(("arbitrary", "arbitrary")),
    )(p128, p128, vt)


def _layer_norm(z, g, b):
    mu = jnp.mean(z, axis=1, keepdims=True)
    zc = z - mu
    var = jnp.mean(zc * zc, axis=1, keepdims=True)
    return zc * lax.rsqrt(var + LN_EPS) * g + b


def _outproj_ln_kernel(oa_ref, ob_ref, oc_ref, od_ref, w_ref, x_ref, g_ref, b_ref, xo_ref, xb_ref):
    rows_per_part = x_ref.shape[0] // OUT_PARTS
    for part in range(OUT_PARTS):
        rows = slice(part * rows_per_part, (part + 1) * rows_per_part)
        y = jnp.dot(oa_ref[rows, :], w_ref[0:MIX_W, :], preferred_element_type=F32)
        y = y + jnp.dot(ob_ref[rows, :], w_ref[MIX_W:2 * MIX_W, :], preferred_element_type=F32)
        y = y + jnp.dot(oc_ref[rows, :], w_ref[2 * MIX_W:3 * MIX_W, :], preferred_element_type=F32)
        y = y + jnp.dot(od_ref[rows, :], w_ref[3 * MIX_W:4 * MIX_W, :], preferred_element_type=F32)
        out = _layer_norm(ALPHA * x_ref[rows, :] + y, g_ref[...], b_ref[...])
        xo_ref[rows, :] = out
        xb_ref[rows, :] = out.astype(BF16)


def _outproj_ln(oa, ob, oc, od, w_bf, x, g, b):
    t, d = x.shape
    tm = min(OUT_TM, t)
    mix_spec = pl.BlockSpec((tm, MIX_W), lambda i: (i, 0))
    row_spec = pl.BlockSpec((tm, d), lambda i: (i, 0))
    vec_spec = pl.BlockSpec((1, d), lambda i: (0, 0))
    return pl.pallas_call(
        _outproj_ln_kernel,
        grid=(t // tm,),
        in_specs=[mix_spec, mix_spec, mix_spec, mix_spec,
                  pl.BlockSpec((d, d), lambda i: (0, 0)), row_spec, vec_spec, vec_spec],
        out_specs=[row_spec, row_spec],
        out_shape=[jax.ShapeDtypeStruct((t, d), F32), jax.ShapeDtypeStruct((t, d), BF16)],
        compiler_params=_params(("arbitrary",)),
    )(oa, ob, oc, od, w_bf, x, g, b)


def _first_argmax(vals):
    best, arg = vals[0], jnp.zeros(vals[0].shape, I32)
    for j in range(1, len(vals)):
        better = vals[j] > best
        arg = jnp.where(better, j, arg)
        best = jnp.where(better, vals[j], best)
    return arg, best


def _pick(rows, idx):
    out = rows[0]
    for j in range(1, len(rows)):
        out = jnp.where(idx == j, rows[j], out)
    return out


def _router_kernel(x_ref, rwt_ref, bias_ref, e_ref, gate_ref, rank_ref, cnt_ref, carry_ref):
    tm = x_ref.shape[0]

    @pl.when(pl.program_id(0) == 0)
    def _():
        carry_ref[...] = jnp.zeros(carry_ref.shape, F32)

    logits = lax.dot_general(rwt_ref[...], x_ref[...], (((1,), (1,)), ((), ())),
                             precision=lax.Precision.HIGHEST, preferred_element_type=F32)
    aff = 1.0 / (1.0 + jnp.exp(-logits))
    biased = aff + bias_ref[...]
    sel_rows = [biased[r:r + 1, :] for r in range(N_EXPERTS)]
    aff_rows = [aff[r:r + 1, :] for r in range(N_EXPERTS)]

    group_scores = []
    for g in range(N_GROUPS):
        r = sel_rows[g * EXPERTS_PER_GROUP:(g + 1) * EXPERTS_PER_GROUP]
        best_pair = r[0] + r[1]
        for a in range(EXPERTS_PER_GROUP):
            for c in range(a + 1, EXPERTS_PER_GROUP):
                if (a, c) != (0, 1):
                    best_pair = jnp.maximum(best_pair, r[a] + r[c])
        group_scores.append(best_pair)
    grp, _ = _first_argmax(group_scores)

    in_sel = [_pick([sel_rows[g * EXPERTS_PER_GROUP + j] for g in range(N_GROUPS)], grp)
              for j in range(EXPERTS_PER_GROUP)]
    in_aff = [_pick([aff_rows[g * EXPERTS_PER_GROUP + j] for g in range(N_GROUPS)], grp)
              for j in range(EXPERTS_PER_GROUP)]
    first, _ = _first_argmax(in_sel)
    second, _ = _first_argmax([jnp.where(first == j, -jnp.inf, in_sel[j]) for j in range(EXPERTS_PER_GROUP)])
    a0 = _pick(in_aff, first)
    a1 = _pick(in_aff, second)
    e0 = grp * EXPERTS_PER_GROUP + first
    e1 = grp * EXPERTS_PER_GROUP + second
    e_ref[0:1, :] = e0
    e_ref[1:2, :] = e1
    gate_ref[0:1, :] = a0 / (a0 + a1)
    gate_ref[1:2, :] = a1 / (a0 + a1)

    e_iota = lax.broadcasted_iota(I32, (N_EXPERTS, tm), 0)
    one_hot = ((e_iota == e0) | (e_iota == e1)).astype(BF16)
    before = (lax.broadcasted_iota(I32, (tm, tm), 0) < lax.broadcasted_iota(I32, (tm, tm), 1)).astype(BF16)
    prior = jnp.dot(one_hot, before, preferred_element_type=F32) + carry_ref[...]
    rank_ref[0:1, :] = jnp.sum(jnp.where(e_iota == e0, prior, 0.0), axis=0, keepdims=True).astype(I32)
    rank_ref[1:2, :] = jnp.sum(jnp.where(e_iota == e1, prior, 0.0), axis=0, keepdims=True).astype(I32)
    carry_ref[...] = carry_ref[...] + jnp.sum(one_hot.astype(F32), axis=1, keepdims=True)
    cnt_ref[...] = jnp.broadcast_to(carry_ref[...], cnt_ref.shape).astype(I32)


def _router(x, rw_t, bias):
    t, d = x.shape
    tm = min(ROUTER_TM, t)
    tok_spec = pl.BlockSpec((2, tm), lambda i: (0, i))
    return pl.pallas_call(
        _router_kernel,
        grid=(t // tm,),
        in_specs=[pl.BlockSpec((tm, d), lambda i: (i, 0)),
                  pl.BlockSpec((N_EXPERTS, d), lambda i: (0, 0)),
                  pl.BlockSpec((N_EXPERTS, 1), lambda i: (0, 0))],
        out_specs=[tok_spec, tok_spec, tok_spec, pl.BlockSpec((N_EXPERTS, LANES), lambda i: (0, 0))],
        out_shape=[jax.ShapeDtypeStruct((2, t), I32), jax.ShapeDtypeStruct((2, t), F32),
                   jax.ShapeDtypeStruct((2, t), I32), jax.ShapeDtypeStruct((N_EXPERTS, LANES), I32)],
        scratch_shapes=[pltpu.VMEM((N_EXPERTS, 1), F32)],
        compiler_params=_params(("arbitrary",)),
    )(x, rw_t, bias)


def _row_copy(src, src_row, dst, dst_row, sem):
    return pltpu.make_async_copy(src.at[pl.ds(src_row, 1)], dst.at[pl.ds(dst_row, 1)], sem)


def _dispatch_kernel(dest_ref, pend_ref, na_ref, x_ref, slots_hbm, zeros_ref, sem, zero_sem,
                     *, tm, n_tok, tb):
    base = pl.program_id(0) * tm

    @pl.when(pl.program_id(0) == 0)
    def _():
        zeros_ref[...] = jnp.zeros(zeros_ref.shape, zeros_ref.dtype)
        n_blocks = slots_hbm.shape[0] // tb

        def block_fill(start):
            return pltpu.make_async_copy(zeros_ref, slots_hbm.at[pl.ds(start, tb)], zero_sem)

        def for_each_fill(act):
            for e in range(N_EXPERTS):
                seg_start = pend_ref[e - 1] if e else 0

                @pl.when(pend_ref[e] > seg_start)
                def _():
                    act(pl.multiple_of(pend_ref[e] - tb, tb))

            def tail(n, carry):
                act(pl.multiple_of(n * tb, tb))
                return carry

            lax.fori_loop(na_ref[0], n_blocks, tail, 0)

        for_each_fill(lambda start: block_fill(start).start())
        for_each_fill(lambda start: block_fill(start).wait())

    def issue(r, carry):
        tok = base + r
        _row_copy(x_ref, r, slots_hbm, dest_ref[tok], sem).start()
        _row_copy(x_ref, r, slots_hbm, dest_ref[n_tok + tok], sem).start()
        return carry

    def drain(r, carry):
        _row_copy(x_ref, 0, slots_hbm, 0, sem).wait()
        _row_copy(x_ref, 0, slots_hbm, 0, sem).wait()
        return carry

    lax.fori_loop(0, tm, issue, 0, unroll=DMA_UNROLL)
    lax.fori_loop(0, tm, drain, 0, unroll=DMA_UNROLL)


def _dispatch(dest_flat, pend, n_active, x, n_slots):
    t, d = x.shape
    tm = min(DISPATCH_TM, t)
    tb = EXPERT_TB
    return pl.pallas_call(
        functools.partial(_dispatch_kernel, tm=tm, n_tok=t, tb=tb),
        grid_spec=pltpu.PrefetchScalarGridSpec(
            num_scalar_prefetch=3,
            grid=(t // tm,),
            in_specs=[pl.BlockSpec((tm, d), lambda i, dest, pend, na: (i, 0))],
            out_specs=pl.BlockSpec(memory_space=pl.ANY),
            scratch_shapes=[pltpu.VMEM((tb, d), x.dtype), pltpu.SemaphoreType.DMA(()),
                            pltpu.SemaphoreType.DMA(())],
        ),
        out_shape=jax.ShapeDtypeStruct((n_slots, d), x.dtype),
        compiler_params=_params(("arbitrary",)),
    )(dest_flat, pend, n_active, x)


def _last_active(n, na_ref):
    return jnp.maximum(jnp.minimum(n, na_ref[0] - 1), 0)


def _expert_stage(be_ref, nxt_ref, na_ref, w_hbm_refs, stage_refs, w_bf_refs, sem, compute, out_ref, *, layer):
    n = pl.program_id(0)
    active = n < na_ref[0]
    new_expert = (n == 0) | (be_ref[n] != be_ref[jnp.maximum(n - 1, 0)])

    def copies(expert):
        return [pltpu.make_async_copy(w_hbm.at[layer, expert], stage, sem.at[k])
                for k, (w_hbm, stage) in enumerate(zip(w_hbm_refs, stage_refs))]

    @pl.when(active & (n == 0))
    def _():
        for copy in copies(be_ref[0]):
            copy.start()

    @pl.when(active & new_expert)
    def _():
        for copy, stage, w_bf in zip(copies(be_ref[n]), stage_refs, w_bf_refs):
            copy.wait()
            w_bf[...] = stage[...].astype(BF16)

        @pl.when(nxt_ref[n] >= 0)
        def _():
            for copy in copies(nxt_ref[n]):
                copy.start()

    @pl.when(active)
    def _():
        out_ref[...] = compute().astype(out_ref.dtype)

    @pl.when(jnp.logical_not(active))
    def _():
        out_ref[...] = jnp.zeros(out_ref.shape, out_ref.dtype)


def _expert_up_kernel(be_ref, nxt_ref, na_ref, xs_ref, wg_hbm, wu_hbm, h_ref,
                      wg_stage, wu_stage, wg_bf, wu_bf, sem, *, layer):
    def compute():
        xb = xs_ref[...].astype(BF16)
        gate = jnp.dot(xb, wg_bf[...], preferred_element_type=F32)
        up = jnp.dot(xb, wu_bf[...], preferred_element_type=F32)
        return gate * (1.0 / (1.0 + jnp.exp(-gate))) * up

    _expert_stage(be_ref, nxt_ref, na_ref, (wg_hbm, wu_hbm), (wg_stage, wu_stage), (wg_bf, wu_bf), sem,
                  compute, h_ref, layer=layer)


def _expert_down_kernel(be_ref, nxt_ref, na_ref, h_ref, wd_hbm, ys_ref, wd_stage, wd_bf, sem, *, layer):
    def compute():
        return jnp.dot(h_ref[...], wd_bf[...], preferred_element_type=F32)

    _expert_stage(be_ref, nxt_ref, na_ref, (wd_hbm,), (wd_stage,), (wd_bf,), sem, compute, ys_ref,
                  layer=layer)


def _experts(block_expert, next_expert, n_active, xs, wg, wu, wd, layer):
    n_slots, d = xs.shape
    tb = EXPERT_TB
    f = wg.shape[3]

    def row_map(n, be, nxt, na):
        return (_last_active(n, na), 0)

    def out_map(n, be, nxt, na):
        return (n, 0)

    hbm = pl.BlockSpec(memory_space=pl.ANY)
    hidden = pl.pallas_call(
        functools.partial(_expert_up_kernel, layer=layer),
        grid_spec=pltpu.PrefetchScalarGridSpec(
            num_scalar_prefetch=3,
            grid=(n_slots // tb,),
            in_specs=[pl.BlockSpec((tb, d), row_map), hbm, hbm],
            out_specs=pl.BlockSpec((tb, f), out_map),
            scratch_shapes=[pltpu.VMEM((d, f), F32), pltpu.VMEM((d, f), F32),
                            pltpu.VMEM((d, f), BF16), pltpu.VMEM((d, f), BF16),
                            pltpu.SemaphoreType.DMA((2,))],
        ),
        out_shape=jax.ShapeDtypeStruct((n_slots, f), BF16),
        compiler_params=_params(("arbitrary",)),
    )(block_expert, next_expert, n_active, xs, wg, wu)
    return pl.pallas_call(
        functools.partial(_expert_down_kernel, layer=layer),
        grid_spec=pltpu.PrefetchScalarGridSpec(
            num_scalar_prefetch=3,
            grid=(n_slots // tb,),
            in_specs=[pl.BlockSpec((tb, f), row_map), hbm],
            out_specs=pl.BlockSpec((tb, d), out_map),
            scratch_shapes=[pltpu.VMEM((f, d), F32), pltpu.VMEM((f, d), BF16),
                            pltpu.SemaphoreType.DMA((1,))],
        ),
        out_shape=jax.ShapeDtypeStruct((n_slots, d), F32),
        compiler_params=_params(("arbitrary",)),
    )(block_expert, next_expert, n_active, hidden, wd)


def _combine_ln_kernel(dest_ref, x_ref, gate_ref, g_ref, b_ref, ys_hbm, xo_ref, xb_ref, buf_ref, sem,
                       *, n_tok):
    tm = x_ref.shape[0]
    i = pl.program_id(0)
    slot = i % 2

    def request(tile, to_slot):
        def issue(r, carry):
            tok = tile * tm + r
            _row_copy(ys_hbm, dest_ref[tok], buf_ref.at[to_slot, 0], r, sem.at[to_slot]).start()
            _row_copy(ys_hbm, dest_ref[n_tok + tok], buf_ref.at[to_slot, 1], r, sem.at[to_slot]).start()
            return carry
        lax.fori_loop(0, tm, issue, 0, unroll=DMA_UNROLL)

    @pl.when(i == 0)
    def _():
        request(0, 0)

    @pl.when(i + 1 < pl.num_programs(0))
    def _():
        request(i + 1, 1 - slot)

    def drain(r, carry):
        _row_copy(ys_hbm, 0, buf_ref.at[slot, 0], 0, sem.at[slot]).wait()
        _row_copy(ys_hbm, 0, buf_ref.at[slot, 1], 0, sem.at[slot]).wait()
        return carry

    lax.fori_loop(0, tm, drain, 0, unroll=DMA_UNROLL)
    gate = gate_ref[...]
    moe = buf_ref[slot, 0] * gate[:, 0:1] + buf_ref[slot, 1] * gate[:, 1:2]
    out = _layer_norm(ALPHA * x_ref[...] + moe, g_ref[...], b_ref[...])
    xo_ref[...] = out
    xb_ref[...] = out.astype(BF16)


def _combine_ln(dest_flat, x, gate_t, g, b, ys):
    t, d = x.shape
    tm = min(COMBINE_TM, t)
    row_spec = pl.BlockSpec((tm, d), lambda i, dest: (i, 0))
    vec_spec = pl.BlockSpec((1, d), lambda i, dest: (0, 0))
    return pl.pallas_call(
        functools.partial(_combine_ln_kernel, n_tok=t),
        grid_spec=pltpu.PrefetchScalarGridSpec(
            num_scalar_prefetch=1,
            grid=(t // tm,),
            in_specs=[row_spec, pl.BlockSpec((tm, 2), lambda i, dest: (i, 0)), vec_spec, vec_spec,
                      pl.BlockSpec(memory_space=pl.ANY)],
            out_specs=[row_spec, row_spec],
            scratch_shapes=[pltpu.VMEM((2, 2, tm, d), F32), pltpu.SemaphoreType.DMA((2,))],
        ),
        out_shape=[jax.ShapeDtypeStruct((t, d), F32), jax.ShapeDtypeStruct((t, d), BF16)],
        compiler_params=_params(("arbitrary",)),
    )(dest_flat, x, gate_t, g, b, ys)


def _rope_tables(positions):
    pos = positions.reshape(-1).astype(F32)[:, None]

    def cos_sin(d):
        inv = jnp.power(ROPE_THETA, -jnp.arange(0, d, 2, dtype=F32) / d)
        ang = pos * inv
        return jnp.cos(ang), jnp.sin(ang)

    c, s = cos_sin(HEAD_DIM)
    t128 = (jnp.concatenate([c, c], axis=1), jnp.concatenate([-s, s], axis=1))
    c, s = cos_sin(IDX_DIM)
    z = jnp.zeros_like(s)
    t64 = (jnp.concatenate([c, c, c, c], axis=1),
           jnp.concatenate([-s, z, -s, z], axis=1),
           jnp.concatenate([z, s, z, s], axis=1))
    return t128, t64


def _col_scales():
    q128 = HEAD_DIM ** -0.5 * LOG2E
    s128 = np.ones((1, 5 * MIX_W + HEAD_DIM), np.float32)
    s128[:, 0:2 * MIX_W] = q128
    s128[:, 3 * MIX_W:4 * MIX_W] = q128
    s64 = np.ones((1, 2304), np.float32)
    s64[:, 1024:1024 + MIX_W] = DIFF_DIM ** -0.5 * LOG2E
    return jnp.asarray(s128), jnp.asarray(s64)


def _split_w_in(w):
    def cols(name):
        lo, hi = _OFF[name]
        return w[:, lo:hi]

    d = w.shape[0]
    w128 = jnp.concatenate([cols(n) for n in ("a_q", "c_q", "c_k", "d_q", "d_k", "a_k")], axis=1)
    w64 = jnp.concatenate([cols("i_q"), cols("b_q"), cols("b_k"), cols("i_k"), cols("i_k"),
                           jnp.zeros((d, LANES), w.dtype)], axis=1)
    wt = jnp.concatenate([cols("b_v"), cols("c_v"), cols("d_v"), cols("a_v"), cols("i_w")], axis=1).T
    return w128.astype(BF16), w64.astype(BF16), wt.astype(BF16)


def _moe_layout(e, rank, counts, tb):
    t = e.shape[1]
    n_slots = 2 * t + N_EXPERTS * tb
    padded = (counts + tb - 1) // tb * tb
    pend = jnp.cumsum(padded)
    pstart = pend - padded
    experts = jnp.arange(N_EXPERTS, dtype=I32)[:, None, None]
    seg_start = jnp.sum(jnp.where(e[None] == experts, pstart[:, None, None], 0), axis=0)
    dest = (seg_start + rank).astype(I32).reshape(-1)
    blocks = jnp.arange(n_slots // tb, dtype=I32) * tb
    block_expert = jnp.minimum(jnp.sum(blocks[:, None] >= pend[None, :], axis=1), N_EXPERTS - 1).astype(I32)
    n_active = (pend[-1:] // tb).astype(I32)
    n_blocks = n_slots // tb
    blk = jnp.arange(n_blocks, dtype=I32)
    later = ((blk[None, :] > blk[:, None]) & (block_expert[None, :] != block_expert[:, None])
             & (blk[None, :] < n_active[0]))
    first_later = jnp.min(jnp.where(later, blk[None, :], n_blocks), axis=1)
    next_expert = jnp.sum(jnp.where(blk[None, :] == first_later[:, None], block_expert[None, :] + 1, 0),
                          axis=1).astype(I32) - 1
    return dest, pend.astype(I32), block_expert, next_expert, n_active, n_slots


def kernel(x, positions, w_in, w_out, diff_lambda, diff_norm_g, ln_mix_g, ln_mix_b, router_w, router_bias,
           w_gate, w_up, w_down, ln_ffn_g, ln_ffn_b):
    b, s, d = x.shape
    t = b * s
    n_keep = min(DSA_TOPK_MAX, s // 4)
    t128, t64 = _rope_tables(positions)
    s128, s64 = _col_scales()
    rw_t = router_w.T
    bias = router_bias.reshape(N_EXPERTS, 1).astype(F32)
    xf = x.reshape(t, d)
    xb = xf
    nc = s // min(ATT_T, s)
    for layer in range(DEPTH):
        w128, w64, wt = _split_w_in(w_in[layer])
        p128 = _proj(xb, w128, s128, t128, "rope128", w128.shape[1]).reshape(b, s, -1)
        p64 = _proj(xb, w64, s64, t64, "rope64", 768).reshape(b, s, -1)
        vt, iwt = _proj_t(xb, wt)
        vt = vt.reshape(b, nc, V_ROWS, -1)
        iwt = iwt.reshape(b, nc, IDX_HEADS, -1)

        lam_init = 0.8 - 0.6 * math.exp(-0.3 * layer)
        o_a = _dsa(p128, p64, vt, iwt, n_keep)
        o_b = _diff(p64, vt, diff_lambda[layer], diff_norm_g[layer].reshape(HEAD_DIM, 1), lam_init)
        o_c = _moba(p128, vt, _kmean(p128))
        o_d = _dilated(p128, vt)
        xf, xb = _outproj_ln(o_a.reshape(t, MIX_W), o_b.reshape(t, MIX_W), o_c.reshape(t, MIX_W),
                             o_d.reshape(t, MIX_W), w_out[layer].astype(BF16), xf,
                             ln_mix_g[layer].reshape(1, d), ln_mix_b[layer].reshape(1, d))

        e, gate, rank, counts = _router(xf, rw_t, bias)
        dest, pend, block_expert, next_expert, n_active, n_slots = _moe_layout(e, rank, counts[:, 0], EXPERT_TB)
        xs = _dispatch(dest, pend, n_active, xf, n_slots)
        ys = _experts(block_expert, next_expert, n_active, xs, w_gate, w_up, w_down, layer)
        xf, xb = _combine_ln(dest, xf, gate.T, ln_ffn_g[layer].reshape(1, d), ln_ffn_b[layer].reshape(1, d), ys)
    return xf.reshape(b, s, d)
```

```python
import functools
import math

import numpy as np
import jax
import jax.numpy as jnp
from jax import lax
from jax.experimental import pallas as pl
from jax.experimental.pallas import tpu as pltpu

F32 = jnp.float32
BF16 = jnp.bfloat16
I32 = jnp.int32

D_MODEL = 2048
HEAD_DIM = 128
HEADS = 4
MIX_W = HEADS * HEAD_DIM
IDX_HEADS = 16
IDX_DIM = 64
DSA_TOPK_MAX = 256
DIFF_DIM = 64
MOBA_BLOCK = 256
MOBA_TOPK = 3
DILATED_PATTERNS = ((128, 1), (512, 4), (2048, 16))
ROPE_THETA = 10000.0
N_EXPERTS = 16
N_GROUPS = 4
EXPERTS_PER_GROUP = 4
LN_EPS = 1e-5
DEPTH = 2
ALPHA = (2 * DEPTH) ** 0.25
LOG2E = math.log2(math.e)

LANES = 128
SUBLANES = 8
NEG = -1e30
M_FLOOR = -1e29
INT_MIN = -(2 ** 31)
NEG_INF_ORDER = 0x007FFFFF
VMEM_LIMIT = 56 * 1024 * 1024

PROJ_TM = 512
PROJ_PARTS = 4
ATT_T = 256
DIFF_TQ = 512
DSA_TQ = 512
MOBA_TQ = 512
DIL_TQ = 512
OUT_TM = 512
OUT_PARTS = 2
ROUTER_TM = 256
DISPATCH_TM = 256
EXPERT_TB = 256
COMBINE_TM = 256
DMA_UNROLL = 8
V_ROWS = 3 * MIX_W + HEAD_DIM

_OFF = {}
_o = 0
for _name, _n in (("a_q", 512), ("a_k", 128), ("a_v", 128), ("i_q", 1024), ("i_k", 64), ("i_w", 16),
                  ("b_q", 512), ("b_k", 512), ("b_v", 512), ("c_q", 512), ("c_k", 512), ("c_v", 512),
                  ("d_q", 512), ("d_k", 512), ("d_v", 512)):
    _OFF[_name] = (_o, _o + _n)
    _o += _n

A_Q_BLK, C_Q_BLK, C_K_BLK, D_Q_BLK, D_K_BLK = range(5)
A_K_BLK = 5 * MIX_W // HEAD_DIM
P128_W = 5 * MIX_W + HEAD_DIM
IQ_W = IDX_HEADS * IDX_DIM
B_Q_BLK = IQ_W // MIX_W
B_K_BLK = B_Q_BLK + 1
I_K_BLK = (IQ_W + 2 * MIX_W) // LANES
P64_W = IQ_W + 2 * MIX_W + 2 * LANES
B_V_BLK, C_V_BLK, D_V_BLK = range(3)
A_V_BLK = 3 * MIX_W // HEAD_DIM


def _params(sem=None):
    return pltpu.CompilerParams(dimension_semantics=sem, vmem_limit_bytes=VMEM_LIMIT)


def _dot_nt(a, b):
    return lax.dot_general(a, b, (((1,), (1,)), ((), ())), preferred_element_type=F32)


def _proj_kernel(x_ref, w_ref, scale_ref, *rest, mode):
    o_ref = rest[-1]
    tm = x_ref.shape[0]
    rows_per_part = tm // PROJ_PARTS
    n_groups = w_ref.shape[1] // LANES
    for part in range(PROJ_PARTS):
        rows = slice(part * rows_per_part, (part + 1) * rows_per_part)
        acc = jnp.dot(x_ref[rows, :].astype(BF16), w_ref[...], preferred_element_type=F32)
        for g in range(n_groups):
            cols = slice(g * LANES, (g + 1) * LANES)
            y = acc[:, cols]
            if mode == "rope128":
                cos_ref, sin_ref = rest[0], rest[1]
                r = y * cos_ref[rows, :] + pltpu.roll(y, 64, 1) * sin_ref[rows, :]
            else:
                cos_ref, sina_ref, sinb_ref = rest[0], rest[1], rest[2]
                r = (y * cos_ref[rows, :] + pltpu.roll(y, 96, 1) * sina_ref[rows, :]
                     + pltpu.roll(y, 32, 1) * sinb_ref[rows, :])
            r = r * scale_ref[:, cols]
            o_ref[rows, cols] = r.astype(o_ref.dtype)


def _proj(x_bf, w_bf, col_scale, tables, mode, tn):
    t, d = x_bf.shape
    n = w_bf.shape[1]
    tm = min(PROJ_TM, t)
    in_specs = [pl.BlockSpec((tm, d), lambda j, i: (i, 0)),
                pl.BlockSpec((d, tn), lambda j, i: (0, j)),
                pl.BlockSpec((1, tn), lambda j, i: (0, j))]
    in_specs += [pl.BlockSpec((tm, LANES), lambda j, i: (i, 0)) for _ in tables]
    return pl.pallas_call(
        functools.partial(_proj_kernel, mode=mode),
        grid=(n // tn, t // tm),
        in_specs=in_specs,
        out_specs=pl.BlockSpec((tm, tn), lambda j, i: (i, j)),
        out_shape=jax.ShapeDtypeStruct((t, n), BF16),
        compiler_params=_params(("arbitrary", "arbitrary")),
    )(x_bf, w_bf, col_scale, *tables)


def _proj_t_kernel(x_ref, wt_ref, vt_ref, iwt_ref):
    acc = _dot_nt(wt_ref[...], x_ref[...].astype(BF16))
    vt_ref[...] = acc[:V_ROWS].astype(vt_ref.dtype)
    iwt_ref[...] = acc[V_ROWS:]


def _proj_t(x_bf, wt_bf):
    t, d = x_bf.shape
    tm = min(ATT_T, t)
    rows = wt_bf.shape[0]
    return pl.pallas_call(
        _proj_t_kernel,
        grid=(t // tm,),
        in_specs=[pl.BlockSpec((tm, d), lambda i: (i, 0)),
                  pl.BlockSpec((rows, d), lambda i: (0, 0))],
        out_specs=[pl.BlockSpec((None, V_ROWS, tm), lambda i: (i, 0, 0)),
                   pl.BlockSpec((None, IDX_HEADS, tm), lambda i: (i, 0, 0))],
        out_shape=[jax.ShapeDtypeStruct((t // tm, V_ROWS, tm), BF16),
                   jax.ShapeDtypeStruct((t // tm, IDX_HEADS, tm), F32)],
        compiler_params=_params(("arbitrary",)),
    )(x_bf, wt_bf)


def _online_updates(scores, mask, values, m_ref, l_ref, acc_ref, weight=None):
    probs, alphas = [], []
    masks = mask if isinstance(mask, list) else [mask] * len(scores)
    for idx, (s_t, mask) in enumerate(zip(scores, masks)):
        if mask is not None:
            s_t = jnp.where(mask, s_t, NEG)
        m_prev = m_ref[idx]
        m_new = jnp.maximum(m_prev, jnp.max(s_t, axis=0, keepdims=True))
        p = jnp.exp2(s_t - m_new)
        if weight is not None:
            p = p * weight
        alpha = jnp.exp2(m_prev - m_new)
        l_ref[idx] = alpha * l_ref[idx] + jnp.sum(p, axis=0, keepdims=True)
        m_ref[idx] = m_new
        probs.append(p.astype(BF16))
        alphas.append(alpha)
    for idx, (p, alpha) in enumerate(zip(probs, alphas)):
        acc_ref[idx] = alpha * acc_ref[idx] + jnp.dot(values[idx], p, preferred_element_type=F32)


def _init_softmax(m_ref, l_ref, acc_ref):
    m_ref[...] = jnp.full(m_ref.shape, M_FLOOR, F32)
    l_ref[...] = jnp.zeros(l_ref.shape, F32)
    acc_ref[...] = jnp.zeros(acc_ref.shape, F32)


def _softmax_scratch(n, t):
    return [pltpu.VMEM((n, 1, t), F32), pltpu.VMEM((n, 1, t), F32), pltpu.VMEM((n, HEAD_DIM, t), F32)]


def _causal_mask(tk, tq, k_start):
    return k_start + lax.broadcasted_iota(I32, (tk, tq), 0) <= lax.broadcasted_iota(I32, (tk, tq), 1)


def _head(ref_or_val, h):
    return ref_or_val[:, h * HEAD_DIM:(h + 1) * HEAD_DIM]


def _order_to_f32(u):
    key = u ^ jnp.int32(INT_MIN)
    bits = jnp.where(key < 0, key ^ jnp.int32(0x7FFFFFFF), key)
    return lax.bitcast_convert_type(bits, F32)


def _dsa_kernel(aq_ref, ak_ref, avt_ref, iq_ref, ik_ref, iwt_ref, o_ref,
                score_ref, m_ref, l_ref, acc_ref, jlim_ref, *, n_keep):
    tq = aq_ref.shape[0]
    t = avt_ref.shape[2]
    i = pl.program_id(1)
    n_chunks = (i + 1) * (tq // t)
    lane = lax.broadcasted_iota(I32, (1, LANES), 1)
    k_off = lax.broadcasted_iota(I32, (t, 1), 0)
    q_pos = i * tq + lax.broadcasted_iota(I32, (1, tq), 1)
    iw_t = jnp.concatenate([iwt_ref[n] for n in range(tq // t)], axis=1)

    def score_chunk(c, carry):
        ik_c = ik_ref[pl.ds(pl.multiple_of(c * t, t), t), :]
        ik_lo = jnp.where(lane < IDX_DIM, ik_c, jnp.zeros_like(ik_c))
        ik_hi = jnp.where(lane >= IDX_DIM, ik_c, jnp.zeros_like(ik_c))
        score = jnp.zeros((t, tq), F32)
        for j in range(IDX_HEADS // 2):
            q_pair = iq_ref[:, j * LANES:(j + 1) * LANES]
            s_even = jnp.maximum(_dot_nt(ik_lo, q_pair), 0.0)
            s_odd = jnp.maximum(_dot_nt(ik_hi, q_pair), 0.0)
            score = score + s_even * iw_t[2 * j:2 * j + 1, :] + s_odd * iw_t[2 * j + 1:2 * j + 2, :]
        score_ref[c] = jnp.where(c * t + k_off <= q_pos, score, -jnp.inf)
        return carry

    lax.fori_loop(0, n_chunks, score_chunk, 0)

    def count(pred):
        def body(c, part):
            hit = jnp.where(pred(score_ref[c], c), 1, 0)
            return part + jnp.sum(hit.reshape(t // SUBLANES, SUBLANES, tq), axis=0)
        part = lax.fori_loop(0, n_chunks, body, jnp.zeros((SUBLANES, tq), I32))
        return jnp.sum(part, axis=0, keepdims=True)

    def bisect(it, carry):
        thr_u, n_ge = carry
        cand_u = thr_u | jnp.left_shift(jnp.int32(1), 31 - it)
        cand = _order_to_f32(cand_u)
        cnt = count(lambda sc, c: sc >= cand)
        keep = cnt >= n_keep
        return jnp.where(keep, cand_u, thr_u), jnp.where(keep, cnt, n_ge)

    zeros = jnp.zeros((1, tq), I32)
    thr_u, n_ge = lax.fori_loop(0, 32, bisect, (zeros, zeros))
    below = (thr_u >= 0) & (thr_u < NEG_INF_ORDER)
    thr = jnp.where(below, -jnp.inf, _order_to_f32(thr_u))

    s_total = score_ref.shape[0] * t
    jlim_ref[...] = jnp.full((1, tq), s_total, I32)
    excess = jnp.max(jnp.where((n_ge > n_keep) & (thr > -jnp.inf), 1, 0))

    @pl.when(excess > 0)
    def _():
        n_bits = max(1, (s_total - 1).bit_length())
        need = n_keep - count(lambda sc, c: sc > thr)

        def bisect_idx(it, j):
            cand = j | jnp.left_shift(jnp.int32(1), n_bits - 1 - it)
            cnt = count(lambda sc, c: (sc == thr) & (c * t + k_off < cand))
            return jnp.where(cnt < need, cand, j)

        jlim_ref[...] = lax.fori_loop(0, n_bits, bisect_idx, jnp.zeros((1, tq), I32))

    _init_softmax(m_ref, l_ref, acc_ref)
    jlim = jnp.where(thr > -jnp.inf, jlim_ref[...], -1)
    thr_all = jnp.where(thr > -jnp.inf, thr, jnp.finfo(F32).min)

    def attend(select):
        def body(c, carry):
            start = pl.multiple_of(c * t, t)
            k_c = ak_ref[pl.ds(start, t), :]
            scores = [_dot_nt(k_c, _head(aq_ref, h)) for h in range(HEADS)]
            sel = select(score_ref[c], c * t + k_off)
            _online_updates(scores, sel, [avt_ref[c]] * HEADS, m_ref, l_ref, acc_ref)
            return carry
        lax.fori_loop(0, n_chunks, body, 0)

    @pl.when(excess > 0)
    def _():
        attend(lambda sc, k_pos: (sc > thr) | ((sc == thr) & (k_pos <= jlim)))

    @pl.when(excess <= 0)
    def _():
        attend(lambda sc, k_pos: sc >= thr_all)
    for h in range(HEADS):
        o_ref[:, h * HEAD_DIM:(h + 1) * HEAD_DIM] = (acc_ref[h] / l_ref[h]).T.astype(o_ref.dtype)


def _dsa(p128, p64, vt, iwt, n_keep):
    b, s, _ = p128.shape
    t = min(ATT_T, s)
    tq = min(DSA_TQ, s)
    nc = s // t
    return pl.pallas_call(
        functools.partial(_dsa_kernel, n_keep=n_keep),
        grid=(b, s // tq),
        in_specs=[
            pl.BlockSpec((None, tq, MIX_W), lambda bb, i: (bb, i, A_Q_BLK)),
            pl.BlockSpec((None, s, HEAD_DIM), lambda bb, i: (bb, 0, A_K_BLK)),
            pl.BlockSpec((None, nc, HEAD_DIM, t), lambda bb, i: (bb, 0, A_V_BLK, 0)),
            pl.BlockSpec((None, tq, IQ_W), lambda bb, i: (bb, i, 0)),
            pl.BlockSpec((None, s, LANES), lambda bb, i: (bb, 0, I_K_BLK)),
            pl.BlockSpec((None, tq // t, IDX_HEADS, t), lambda bb, i: (bb, i, 0, 0)),
        ],
        out_specs=pl.BlockSpec((None, tq, MIX_W), lambda bb, i: (bb, i, 0)),
        out_shape=jax.ShapeDtypeStruct((b, s, MIX_W), BF16),
        scratch_shapes=[pltpu.VMEM((nc, t, tq), F32)] + _softmax_scratch(HEADS, tq)
        + [pltpu.VMEM((1, tq), I32)],
        compiler_params=_params(("arbitrary", "arbitrary")),
    )(p128, p128, vt, p64, p64, iwt)


def _diff_kernel(q_ref, k_ref, vt_ref, lam_ref, g_ref, o_ref, m_ref, l_ref, acc_ref, *, lam_init):
    tq = q_ref.shape[0]
    t = vt_ref.shape[2]
    per_tile = tq // t
    i = pl.program_id(1)
    lane = lax.broadcasted_iota(I32, (1, LANES), 1)
    _init_softmax(m_ref, l_ref, acc_ref)

    def step(c, mask):
        k_c = k_ref[pl.ds(pl.multiple_of(c * t, t), t), :]
        v_c = vt_ref[c]
        scores, values = [], []
        for h in range(HEADS):
            q_h = _head(q_ref, h)
            k_h = _head(k_c, h)
            q_1 = jnp.where(lane < DIFF_DIM, q_h, jnp.zeros_like(q_h))
            q_2 = jnp.where(lane >= DIFF_DIM, q_h, jnp.zeros_like(q_h))
            scores += [_dot_nt(k_h, q_1), _dot_nt(k_h, q_2)]
            values += [v_c[h * HEAD_DIM:(h + 1) * HEAD_DIM, :]] * 2
        _online_updates(scores, mask, values, m_ref, l_ref, acc_ref)

    def full_step(c, carry):
        step(c, None)
        return carry

    lax.fori_loop(0, i * per_tile, full_step, 0)
    for d in range(per_tile):
        step(i * per_tile + d, _causal_mask(t, tq, d * t))

    lam_p = lam_ref[...]
    lam = (jnp.exp(jnp.sum(lam_p[0:1] * lam_p[1:2], axis=1, keepdims=True))
           - jnp.exp(jnp.sum(lam_p[2:3] * lam_p[3:4], axis=1, keepdims=True)) + lam_init)
    for h in range(HEADS):
        o = acc_ref[2 * h] / l_ref[2 * h] - lam * (acc_ref[2 * h + 1] / l_ref[2 * h + 1])
        o = o * lax.rsqrt(jnp.mean(o * o, axis=0, keepdims=True) + LN_EPS) * g_ref[...]
        o_ref[:, h * HEAD_DIM:(h + 1) * HEAD_DIM] = (o * (1.0 - lam_init)).T.astype(o_ref.dtype)


def _diff(p64, vt, lam_p, g_col, lam_init):
    b, s, _ = p64.shape
    t = min(ATT_T, s)
    tq = min(DIFF_TQ, s)
    nc = s // t
    return pl.pallas_call(
        functools.partial(_diff_kernel, lam_init=lam_init),
        grid=(b, s // tq),
        in_specs=[
            pl.BlockSpec((None, tq, MIX_W), lambda bb, i: (bb, i, B_Q_BLK)),
            pl.BlockSpec((None, s, MIX_W), lambda bb, i: (bb, 0, B_K_BLK)),
            pl.BlockSpec((None, nc, MIX_W, t), lambda bb, i: (bb, 0, B_V_BLK, 0)),
            pl.BlockSpec((4, DIFF_DIM), lambda bb, i: (0, 0)),
            pl.BlockSpec((HEAD_DIM, 1), lambda bb, i: (0, 0)),
        ],
        out_specs=pl.BlockSpec((None, tq, MIX_W), lambda bb, i: (bb, i, 0)),
        out_shape=jax.ShapeDtypeStruct((b, s, MIX_W), BF16),
        scratch_shapes=_softmax_scratch(2 * HEADS, tq),
        compiler_params=_params(("arbitrary", "arbitrary")),
    )(p64, p64, vt, lam_p, g_col)


def _kmean_kernel(k_ref, o_ref):
    n_blk = k_ref.shape[0] // MOBA_BLOCK
    o_ref[...] = jnp.zeros(o_ref.shape, o_ref.dtype)
    for n in range(n_blk):
        blk = k_ref[n * MOBA_BLOCK:(n + 1) * MOBA_BLOCK, :].astype(F32)
        o_ref[n:n + 1, :] = jnp.mean(blk, axis=0, keepdims=True)


def _kmean(p128):
    b, s, _ = p128.shape
    rows = max(SUBLANES, s // MOBA_BLOCK)
    return pl.pallas_call(
        _kmean_kernel,
        grid=(b,),
        in_specs=[pl.BlockSpec((None, s, MIX_W), lambda bb: (bb, 0, C_K_BLK))],
        out_specs=pl.BlockSpec((None, rows, MIX_W), lambda bb: (bb, 0, 0)),
        out_shape=jax.ShapeDtypeStruct((b, rows, MIX_W), F32),
        compiler_params=_params(("arbitrary",)),
    )(p128)


def _moba_kernel(q_ref, k_ref, vt_ref, km_ref, o_ref, sel_ref, m_ref, l_ref, acc_ref):
    tq = q_ref.shape[0]
    t = vt_ref.shape[2]
    per_tile = tq // t
    n_blk = km_ref.shape[0]
    i = pl.program_id(1)
    blk = lax.broadcasted_iota(I32, (n_blk, 1), 0)
    own_rel = lax.broadcasted_iota(I32, (1, tq), 1) // t
    own = i * per_tile + own_rel
    _init_softmax(m_ref, l_ref, acc_ref)

    for h in range(HEADS):
        gate = _dot_nt(_head(km_ref, h).astype(BF16), _head(q_ref, h))
        gate = jnp.where(blk < own, gate, -jnp.inf)
        sel = jnp.zeros(gate.shape, F32)
        for _ in range(MOBA_TOPK):
            best = jnp.max(gate, axis=0, keepdims=True)
            first = jnp.min(jnp.where(gate == best, blk, n_blk), axis=0, keepdims=True)
            pick = (blk == first) & (best > -jnp.inf)
            sel = jnp.where(pick, 1.0, sel)
            gate = jnp.where(pick, -jnp.inf, gate)
        sel_ref[h] = sel

    def step(c, tile_block):
        k_c = k_ref[pl.ds(pl.multiple_of(c * t, t), t), :]
        v_c = vt_ref[c]
        scores = [_dot_nt(_head(k_c, h), _head(q_ref, h)) for h in range(HEADS)]
        values = [v_c[h * HEAD_DIM:(h + 1) * HEAD_DIM, :] for h in range(HEADS)]
        picked = [sel_ref[h, pl.ds(c, 1), :] > 0.0 for h in range(HEADS)]
        if tile_block is None:
            masks = picked
        else:
            causal = (own_rel == tile_block) & _causal_mask(t, tq, tile_block * t)
            masks = [causal | ((own_rel > tile_block) & p) for p in picked]
        _online_updates(scores, masks, values, m_ref, l_ref, acc_ref)

    def past_step(c, carry):
        step(c, None)
        return carry

    lax.fori_loop(0, i * per_tile, past_step, 0)
    for d in range(per_tile):
        step(i * per_tile + d, d)
    for h in range(HEADS):
        o_ref[:, h * HEAD_DIM:(h + 1) * HEAD_DIM] = (acc_ref[h] / l_ref[h]).T.astype(o_ref.dtype)


def _moba(p128, vt, kmean):
    b, s, _ = p128.shape
    t = MOBA_BLOCK
    tq = min(MOBA_TQ, s)
    nc = s // t
    rows = kmean.shape[1]
    return pl.pallas_call(
        _moba_kernel,
        grid=(b, s // tq),
        in_specs=[
            pl.BlockSpec((None, tq, MIX_W), lambda bb, i: (bb, i, C_Q_BLK)),
            pl.BlockSpec((None, s, MIX_W), lambda bb, i: (bb, 0, C_K_BLK)),
            pl.BlockSpec((None, nc, MIX_W, t), lambda bb, i: (bb, 0, C_V_BLK, 0)),
            pl.BlockSpec((None, rows, MIX_W), lambda bb, i: (bb, 0, 0)),
        ],
        out_specs=pl.BlockSpec((None, tq, MIX_W), lambda bb, i: (bb, i, 0)),
        out_shape=jax.ShapeDtypeStruct((b, s, MIX_W), BF16),
        scratch_shapes=[pltpu.VMEM((HEADS, rows, tq), F32)] + _softmax_scratch(HEADS, tq),
        compiler_params=_params(("arbitrary", "arbitrary")),
    )(p128, p128, vt, kmean)


def _dilated_kernel(q_ref, k_ref, vt_ref, o_ref, m_ref, l_ref, acc_ref):
    tq = q_ref.shape[0]
    t = vt_ref.shape[2]
    i = pl.program_id(1)
    _init_softmax(m_ref, l_ref, acc_ref)
    rel = lax.broadcasted_iota(I32, (t, tq), 1) - lax.broadcasted_iota(I32, (t, tq), 0)

    def first_chunk_within(window):
        return jnp.maximum(i * tq - window, 0) // t

    def step(c, patterns):
        k_c = k_ref[pl.ds(pl.multiple_of(c * t, t), t), :]
        v_c = vt_ref[c]
        dist = rel + (i * tq - c * t)
        mult = jnp.zeros((t, tq), F32)
        for window, dilation in patterns:
            hit = (dist >= 0) & (dist <= window) & ((dist & (dilation - 1)) == 0)
            mult = mult + jnp.where(hit, 1.0, 0.0)
        mask = mult > 0.0
        scores = [_dot_nt(_head(k_c, h), _head(q_ref, h)) for h in range(HEADS)]
        values = [v_c[h * HEAD_DIM:(h + 1) * HEAD_DIM, :] for h in range(HEADS)]
        _online_updates(scores, mask, values, m_ref, l_ref, acc_ref, weight=mult)

    widest = max(DILATED_PATTERNS)
    others = [p for p in DILATED_PATTERNS if p != widest]
    first = first_chunk_within(widest[0])
    near_first = first_chunk_within(max(w for w, _ in others))

    def far_step(c, carry):
        step(c, [widest])
        return carry

    def near_step(c, carry):
        step(c, DILATED_PATTERNS)
        return carry

    lax.fori_loop(first, near_first, far_step, 0)
    lax.fori_loop(near_first, (i + 1) * (tq // t), near_step, 0)
    for h in range(HEADS):
        o_ref[:, h * HEAD_DIM:(h + 1) * HEAD_DIM] = (acc_ref[h] / l_ref[h]).T.astype(o_ref.dtype)


def _dilated(p128, vt):
    b, s, _ = p128.shape
    t = min(ATT_T, s)
    tq = min(DIL_TQ, s)
    nc = s // t
    return pl.pallas_call(
        _dilated_kernel,
        grid=(b, s // tq),
        in_specs=[
            pl.BlockSpec((None, tq, MIX_W), lambda bb, i: (bb, i, D_Q_BLK)),
            pl.BlockSpec((None, s, MIX_W), lambda bb, i: (bb, 0, D_K_BLK)),
            pl.BlockSpec((None, nc, MIX_W, t), lambda bb, i: (bb, 0, D_V_BLK, 0)),
        ],
        out_specs=pl.BlockSpec((None, tq, MIX_W), lambda bb, i: (bb, i, 0)),
        out_shape=jax.ShapeDtypeStruct((b, s, MIX_W), BF16),
        scratch_shapes=_softmax_scratch(HEADS, tq),
        compiler_params=_params(("arbitrary", "arbitrary")),
    )(p128, p128, vt)


def _layer_norm(z, g, b):
    mu = jnp.mean(z, axis=1, keepdims=True)
    zc = z - mu
    var = jnp.mean(zc * zc, axis=1, keepdims=True)
    return zc * lax.rsqrt(var + LN_EPS) * g + b


def _outproj_ln_kernel(oa_ref, ob_ref, oc_ref, od_ref, w_ref, x_ref, g_ref, b_ref, xo_ref, xb_ref):
    rows_per_part = x_ref.shape[0] // OUT_PARTS
    for part in range(OUT_PARTS):
        rows = slice(part * rows_per_part, (part + 1) * rows_per_part)
        y = jnp.dot(oa_ref[rows, :], w_ref[0:MIX_W, :], preferred_element_type=F32)
        y = y + jnp.dot(ob_ref[rows, :], w_ref[MIX_W:2 * MIX_W, :], preferred_element_type=F32)
        y = y + jnp.dot(oc_ref[rows, :], w_ref[2 * MIX_W:3 * MIX_W, :], preferred_element_type=F32)
        y = y + jnp.dot(od_ref[rows, :], w_ref[3 * MIX_W:4 * MIX_W, :], preferred_element_type=F32)
        out = _layer_norm(ALPHA * x_ref[rows, :] + y, g_ref[...], b_ref[...])
        xo_ref[rows, :] = out
        xb_ref[rows, :] = out.astype(BF16)


def _outproj_ln(oa, ob, oc, od, w_bf, x, g, b):
    t, d = x.shape
    tm = min(OUT_TM, t)
    mix_spec = pl.BlockSpec((tm, MIX_W), lambda i: (i, 0))
    row_spec = pl.BlockSpec((tm, d), lambda i: (i, 0))
    vec_spec = pl.BlockSpec((1, d), lambda i: (0, 0))
    return pl.pallas_call(
        _outproj_ln_kernel,
        grid=(t // tm,),
        in_specs=[mix_spec, mix_spec, mix_spec, mix_spec,
                  pl.BlockSpec((d, d), lambda i: (0, 0)), row_spec, vec_spec, vec_spec],
        out_specs=[row_spec, row_spec],
        out_shape=[jax.ShapeDtypeStruct((t, d), F32), jax.ShapeDtypeStruct((t, d), BF16)],
        compiler_params=_params(("arbitrary",)),
    )(oa, ob, oc, od, w_bf, x, g, b)


def _first_argmax(vals):
    best, arg = vals[0], jnp.zeros(vals[0].shape, I32)
    for j in range(1, len(vals)):
        better = vals[j] > best
        arg = jnp.where(better, j, arg)
        best = jnp.where(better, vals[j], best)
    return arg, best


def _pick(rows, idx):
    out = rows[0]
    for j in range(1, len(rows)):
        out = jnp.where(idx == j, rows[j], out)
    return out


def _router_kernel(x_ref, rwt_ref, bias_ref, e_ref, gate_ref, rank_ref, cnt_ref, carry_ref):
    tm = x_ref.shape[0]

    @pl.when(pl.program_id(0) == 0)
    def _():
        carry_ref[...] = jnp.zeros(carry_ref.shape, F32)

    logits = lax.dot_general(rwt_ref[...], x_ref[...], (((1,), (1,)), ((), ())),
                             precision=lax.Precision.HIGHEST, preferred_element_type=F32)
    aff = 1.0 / (1.0 + jnp.exp(-logits))
    biased = aff + bias_ref[...]
    sel_rows = [biased[r:r + 1, :] for r in range(N_EXPERTS)]
    aff_rows = [aff[r:r + 1, :] for r in range(N_EXPERTS)]

    group_scores = []
    for g in range(N_GROUPS):
        r = sel_rows[g * EXPERTS_PER_GROUP:(g + 1) * EXPERTS_PER_GROUP]
        best_pair = r[0] + r[1]
        for a in range(EXPERTS_PER_GROUP):
            for c in range(a + 1, EXPERTS_PER_GROUP):
                if (a, c) != (0, 1):
                    best_pair = jnp.maximum(best_pair, r[a] + r[c])
        group_scores.append(best_pair)
    grp, _ = _first_argmax(group_scores)

    in_sel = [_pick([sel_rows[g * EXPERTS_PER_GROUP + j] for g in range(N_GROUPS)], grp)
              for j in range(EXPERTS_PER_GROUP)]
    in_aff = [_pick([aff_rows[g * EXPERTS_PER_GROUP + j] for g in range(N_GROUPS)], grp)
              for j in range(EXPERTS_PER_GROUP)]
    first, _ = _first_argmax(in_sel)
    second, _ = _first_argmax([jnp.where(first == j, -jnp.inf, in_sel[j]) for j in range(EXPERTS_PER_GROUP)])
    a0 = _pick(in_aff, first)
    a1 = _pick(in_aff, second)
    e0 = grp * EXPERTS_PER_GROUP + first
    e1 = grp * EXPERTS_PER_GROUP + second
    e_ref[0:1, :] = e0
    e_ref[1:2, :] = e1
    gate_ref[0:1, :] = a0 / (a0 + a1)
    gate_ref[1:2, :] = a1 / (a0 + a1)

    e_iota = lax.broadcasted_iota(I32, (N_EXPERTS, tm), 0)
    one_hot = ((e_iota == e0) | (e_iota == e1)).astype(BF16)
    before = (lax.broadcasted_iota(I32, (tm, tm), 0) < lax.broadcasted_iota(I32, (tm, tm), 1)).astype(BF16)
    prior = jnp.dot(one_hot, before, preferred_element_type=F32) + carry_ref[...]
    rank_ref[0:1, :] = jnp.sum(jnp.where(e_iota == e0, prior, 0.0), axis=0, keepdims=True).astype(I32)
    rank_ref[1:2, :] = jnp.sum(jnp.where(e_iota == e1, prior, 0.0), axis=0, keepdims=True).astype(I32)
    carry_ref[...] = carry_ref[...] + jnp.sum(one_hot.astype(F32), axis=1, keepdims=True)
    cnt_ref[...] = jnp.broadcast_to(carry_ref[...], cnt_ref.shape).astype(I32)


def _router(x, rw_t, bias):
    t, d = x.shape
    tm = min(ROUTER_TM, t)
    tok_spec = pl.BlockSpec((2, tm), lambda i: (0, i))
    return pl.pallas_call(
        _router_kernel,
        grid=(t // tm,),
        in_specs=[pl.BlockSpec((tm, d), lambda i: (i, 0)),
                  pl.BlockSpec((N_EXPERTS, d), lambda i: (0, 0)),
                  pl.BlockSpec((N_EXPERTS, 1), lambda i: (0, 0))],
        out_specs=[tok_spec, tok_spec, tok_spec, pl.BlockSpec((N_EXPERTS, LANES), lambda i: (0, 0))],
        out_shape=[jax.ShapeDtypeStruct((2, t), I32), jax.ShapeDtypeStruct((2, t), F32),
                   jax.ShapeDtypeStruct((2, t), I32), jax.ShapeDtypeStruct((N_EXPERTS, LANES), I32)],
        scratch_shapes=[pltpu.VMEM((N_EXPERTS, 1), F32)],
        compiler_params=_params(("arbitrary",)),
    )(x, rw_t, bias)


def _row_copy(src, src_row, dst, dst_row, sem):
    return pltpu.make_async_copy(src.at[pl.ds(src_row, 1)], dst.at[pl.ds(dst_row, 1)], sem)


def _dispatch_kernel(dest_ref, pend_ref, na_ref, x_ref, slots_hbm, zeros_ref, sem, zero_sem,
                     *, tm, n_tok, tb):
    base = pl.program_id(0) * tm

    @pl.when(pl.program_id(0) == 0)
    def _():
        zeros_ref[...] = jnp.zeros(zeros_ref.shape, zeros_ref.dtype)
        n_blocks = slots_hbm.shape[0] // tb

        def block_fill(start):
            return pltpu.make_async_copy(zeros_ref, slots_hbm.at[pl.ds(start, tb)], zero_sem)

        def for_each_fill(act):
            for e in range(N_EXPERTS):
                seg_start = pend_ref[e - 1] if e else 0

                @pl.when(pend_ref[e] > seg_start)
                def _():
                    act(pl.multiple_of(pend_ref[e] - tb, tb))

            def tail(n, carry):
                act(pl.multiple_of(n * tb, tb))
                return carry

            lax.fori_loop(na_ref[0], n_blocks, tail, 0)

        for_each_fill(lambda start: block_fill(start).start())
        for_each_fill(lambda start: block_fill(start).wait())

    def issue(r, carry):
        tok = base + r
        _row_copy(x_ref, r, slots_hbm, dest_ref[tok], sem).start()
        _row_copy(x_ref, r, slots_hbm, dest_ref[n_tok + tok], sem).start()
        return carry

    def drain(r, carry):
        _row_copy(x_ref, 0, slots_hbm, 0, sem).wait()
        _row_copy(x_ref, 0, slots_hbm, 0, sem).wait()
        return carry

    lax.fori_loop(0, tm, issue, 0, unroll=DMA_UNROLL)
    lax.fori_loop(0, tm, drain, 0, unroll=DMA_UNROLL)


def _dispatch(dest_flat, pend, n_active, x, n_slots):
    t, d = x.shape
    tm = min(DISPATCH_TM, t)
    tb = EXPERT_TB
    return pl.pallas_call(
        functools.partial(_dispatch_kernel, tm=tm, n_tok=t, tb=tb),
        grid_spec=pltpu.PrefetchScalarGridSpec(
            num_scalar_prefetch=3,
            grid=(t // tm,),
            in_specs=[pl.BlockSpec((tm, d), lambda i, dest, pend, na: (i, 0))],
            out_specs=pl.BlockSpec(memory_space=pl.ANY),
            scratch_shapes=[pltpu.VMEM((tb, d), x.dtype), pltpu.SemaphoreType.DMA(()),
                            pltpu.SemaphoreType.DMA(())],
        ),
        out_shape=jax.ShapeDtypeStruct((n_slots, d), x.dtype),
        compiler_params=_params(("arbitrary",)),
    )(dest_flat, pend, n_active, x)


def _last_active(n, na_ref):
    return jnp.maximum(jnp.minimum(n, na_ref[0] - 1), 0)


def _expert_stage(be_ref, nxt_ref, na_ref, w_hbm_refs, stage_refs, w_bf_refs, sem, compute, out_ref, *, layer):
    n = pl.program_id(0)
    active = n < na_ref[0]
    new_expert = (n == 0) | (be_ref[n] != be_ref[jnp.maximum(n - 1, 0)])

    def copies(expert):
        return [pltpu.make_async_copy(w_hbm.at[layer, expert], stage, sem.at[k])
                for k, (w_hbm, stage) in enumerate(zip(w_hbm_refs, stage_refs))]

    @pl.when(active & (n == 0))
    def _():
        for copy in copies(be_ref[0]):
            copy.start()

    @pl.when(active & new_expert)
    def _():
        for copy, stage, w_bf in zip(copies(be_ref[n]), stage_refs, w_bf_refs):
            copy.wait()
            w_bf[...] = stage[...].astype(BF16)

        @pl.when(nxt_ref[n] >= 0)
        def _():
            for copy in copies(nxt_ref[n]):
                copy.start()

    @pl.when(active)
    def _():
        out_ref[...] = compute().astype(out_ref.dtype)

    @pl.when(jnp.logical_not(active))
    def _():
        out_ref[...] = jnp.zeros(out_ref.shape, out_ref.dtype)


def _expert_up_kernel(be_ref, nxt_ref, na_ref, xs_ref, wg_hbm, wu_hbm, h_ref,
                      wg_stage, wu_stage, wg_bf, wu_bf, sem, *, layer):
    def compute():
        xb = xs_ref[...].astype(BF16)
        gate = jnp.dot(xb, wg_bf[...], preferred_element_type=F32)
        up = jnp.dot(xb, wu_bf[...], preferred_element_type=F32)
        return gate * (1.0 / (1.0 + jnp.exp(-gate))) * up

    _expert_stage(be_ref, nxt_ref, na_ref, (wg_hbm, wu_hbm), (wg_stage, wu_stage), (wg_bf, wu_bf), sem,
                  compute, h_ref, layer=layer)


def _expert_down_kernel(be_ref, nxt_ref, na_ref, h_ref, wd_hbm, ys_ref, wd_stage, wd_bf, sem, *, layer):
    def compute():
        return jnp.dot(h_ref[...], wd_bf[...], preferred_element_type=F32)

    _expert_stage(be_ref, nxt_ref, na_ref, (wd_hbm,), (wd_stage,), (wd_bf,), sem, compute, ys_ref,
                  layer=layer)


def _experts(block_expert, next_expert, n_active, xs, wg, wu, wd, layer):
    n_slots, d = xs.shape
    tb = EXPERT_TB
    f = wg.shape[3]

    def row_map(n, be, nxt, na):
        return (_last_active(n, na), 0)

    def out_map(n, be, nxt, na):
        return (n, 0)

    hbm = pl.BlockSpec(memory_space=pl.ANY)
    hidden = pl.pallas_call(
        functools.partial(_expert_up_kernel, layer=layer),
        grid_spec=pltpu.PrefetchScalarGridSpec(
            num_scalar_prefetch=3,
            grid=(n_slots // tb,),
            in_specs=[pl.BlockSpec((tb, d), row_map), hbm, hbm],
            out_specs=pl.BlockSpec((tb, f), out_map),
            scratch_shapes=[pltpu.VMEM((d, f), F32), pltpu.VMEM((d, f), F32),
                            pltpu.VMEM((d, f), BF16), pltpu.VMEM((d, f), BF16),
                            pltpu.SemaphoreType.DMA((2,))],
        ),
        out_shape=jax.ShapeDtypeStruct((n_slots, f), BF16),
        compiler_params=_params(("arbitrary",)),
    )(block_expert, next_expert, n_active, xs, wg, wu)
    return pl.pallas_call(
        functools.partial(_expert_down_kernel, layer=layer),
        grid_spec=pltpu.PrefetchScalarGridSpec(
            num_scalar_prefetch=3,
            grid=(n_slots // tb,),
            in_specs=[pl.BlockSpec((tb, f), row_map), hbm],
            out_specs=pl.BlockSpec((tb, d), out_map),
            scratch_shapes=[pltpu.VMEM((f, d), F32), pltpu.VMEM((f, d), BF16),
                            pltpu.SemaphoreType.DMA((1,))],
        ),
        out_shape=jax.ShapeDtypeStruct((n_slots, d), F32),
        compiler_params=_params(("arbitrary",)),
    )(block_expert, next_expert, n_active, hidden, wd)


def _combine_ln_kernel(dest_ref, x_ref, gate_ref, g_ref, b_ref, ys_hbm, xo_ref, xb_ref, buf_ref, sem,
                       *, n_tok):
    tm = x_ref.shape[0]
    i = pl.program_id(0)
    slot = i % 2

    def request(tile, to_slot):
        def issue(r, carry):
            tok = tile * tm + r
            _row_copy(ys_hbm, dest_ref[tok], buf_ref.at[to_slot, 0], r, sem.at[to_slot]).start()
            _row_copy(ys_hbm, dest_ref[n_tok + tok], buf_ref.at[to_slot, 1], r, sem.at[to_slot]).start()
            return carry
        lax.fori_loop(0, tm, issue, 0, unroll=DMA_UNROLL)

    @pl.when(i == 0)
    def _():
        request(0, 0)

    @pl.when(i + 1 < pl.num_programs(0))
    def _():
        request(i + 1, 1 - slot)

    def drain(r, carry):
        _row_copy(ys_hbm, 0, buf_ref.at[slot, 0], 0, sem.at[slot]).wait()
        _row_copy(ys_hbm, 0, buf_ref.at[slot, 1], 0, sem.at[slot]).wait()
        return carry

    lax.fori_loop(0, tm, drain, 0, unroll=DMA_UNROLL)
    gate = gate_ref[...]
    moe = buf_ref[slot, 0] * gate[:, 0:1] + buf_ref[slot, 1] * gate[:, 1:2]
    out = _layer_norm(ALPHA * x_ref[...] + moe, g_ref[...], b_ref[...])
    xo_ref[...] = out
    xb_ref[...] = out.astype(BF16)


def _combine_ln(dest_flat, x, gate_t, g, b, ys):
    t, d = x.shape
    tm = min(COMBINE_TM, t)
    row_spec = pl.BlockSpec((tm, d), lambda i, dest: (i, 0))
    vec_spec = pl.BlockSpec((1, d), lambda i, dest: (0, 0))
    return pl.pallas_call(
        functools.partial(_combine_ln_kernel, n_tok=t),
        grid_spec=pltpu.PrefetchScalarGridSpec(
            num_scalar_prefetch=1,
            grid=(t // tm,),
            in_specs=[row_spec, pl.BlockSpec((tm, 2), lambda i, dest: (i, 0)), vec_spec, vec_spec,
                      pl.BlockSpec(memory_space=pl.ANY)],
            out_specs=[row_spec, row_spec],
            scratch_shapes=[pltpu.VMEM((2, 2, tm, d), F32), pltpu.SemaphoreType.DMA((2,))],
        ),
        out_shape=[jax.ShapeDtypeStruct((t, d), F32), jax.ShapeDtypeStruct((t, d), BF16)],
        compiler_params=_params(("arbitrary",)),
    )(dest_flat, x, gate_t, g, b, ys)


def _rope_tables(positions):
    pos = positions.reshape(-1).astype(F32)[:, None]

    def cos_sin(d):
        inv = jnp.power(ROPE_THETA, -jnp.arange(0, d, 2, dtype=F32) / d)
        ang = pos * inv
        return jnp.cos(ang), jnp.sin(ang)

    c, s = cos_sin(HEAD_DIM)
    t128 = (jnp.concatenate([c, c], axis=1), jnp.concatenate([-s, s], axis=1))
    c, s = cos_sin(IDX_DIM)
    z = jnp.zeros_like(s)
    t64 = (jnp.concatenate([c, c, c, c], axis=1),
           jnp.concatenate([-s, z, -s, z], axis=1),
           jnp.concatenate([z, s, z, s], axis=1))
    return t128, t64


def _col_scales():
    s128 = np.ones((1, P128_W), np.float32)
    for blk in (A_Q_BLK, C_Q_BLK, D_Q_BLK):
        s128[:, blk * MIX_W:(blk + 1) * MIX_W] = HEAD_DIM ** -0.5 * LOG2E
    s64 = np.ones((1, P64_W), np.float32)
    s64[:, B_Q_BLK * MIX_W:(B_Q_BLK + 1) * MIX_W] = DIFF_DIM ** -0.5 * LOG2E
    return jnp.asarray(s128), jnp.asarray(s64)


def _split_w_in(w):
    def cols(name):
        lo, hi = _OFF[name]
        return w[:, lo:hi]

    d = w.shape[0]
    w128 = jnp.concatenate([cols(n) for n in ("a_q", "c_q", "c_k", "d_q", "d_k", "a_k")], axis=1)
    w64 = jnp.concatenate([cols("i_q"), cols("b_q"), cols("b_k"), cols("i_k"), cols("i_k"),
                           jnp.zeros((d, LANES), w.dtype)], axis=1)
    wt = jnp.concatenate([cols("b_v"), cols("c_v"), cols("d_v"), cols("a_v"), cols("i_w")], axis=1).T
    return w128.astype(BF16), w64.astype(BF16), wt.astype(BF16)


def _moe_layout(e, rank, counts, tb):
    t = e.shape[1]
    n_slots = 2 * t + N_EXPERTS * tb
    padded = (counts + tb - 1) // tb * tb
    pend = jnp.cumsum(padded)
    pstart = pend - padded
    experts = jnp.arange(N_EXPERTS, dtype=I32)[:, None, None]
    seg_start = jnp.sum(jnp.where(e[None] == experts, pstart[:, None, None], 0), axis=0)
    dest = (seg_start + rank).astype(I32).reshape(-1)
    blocks = jnp.arange(n_slots // tb, dtype=I32) * tb
    block_expert = jnp.minimum(jnp.sum(blocks[:, None] >= pend[None, :], axis=1), N_EXPERTS - 1).astype(I32)
    n_active = (pend[-1:] // tb).astype(I32)
    n_blocks = n_slots // tb
    blk = jnp.arange(n_blocks, dtype=I32)
    later = ((blk[None, :] > blk[:, None]) & (block_expert[None, :] != block_expert[:, None])
             & (blk[None, :] < n_active[0]))
    first_later = jnp.min(jnp.where(later, blk[None, :], n_blocks), axis=1)
    next_expert = jnp.sum(jnp.where(blk[None, :] == first_later[:, None], block_expert[None, :] + 1, 0),
                          axis=1).astype(I32) - 1
    return dest, pend.astype(I32), block_expert, next_expert, n_active, n_slots


def kernel(x, positions, w_in, w_out, diff_lambda, diff_norm_g, ln_mix_g, ln_mix_b, router_w, router_bias,
           w_gate, w_up, w_down, ln_ffn_g, ln_ffn_b):
    b, s, d = x.shape
    t = b * s
    n_keep = min(DSA_TOPK_MAX, s // 4)
    t128, t64 = _rope_tables(positions)
    s128, s64 = _col_scales()
    rw_t = router_w.T
    bias = router_bias.reshape(N_EXPERTS, 1).astype(F32)
    xf = x.reshape(t, d)
    xb = xf
    nc = s // min(ATT_T, s)
    for layer in range(DEPTH):
        w128, w64, wt = _split_w_in(w_in[layer])
        p128 = _proj(xb, w128, s128, t128, "rope128", P128_W).reshape(b, s, -1)
        p64 = _proj(xb, w64, s64, t64, "rope64", P64_W).reshape(b, s, -1)
        vt, iwt = _proj_t(xb, wt)
        vt = vt.reshape(b, nc, V_ROWS, -1)
        iwt = iwt.reshape(b, nc, IDX_HEADS, -1)

        lam_init = 0.8 - 0.6 * math.exp(-0.3 * layer)
        o_a = _dsa(p128, p64, vt, iwt, n_keep)
        o_b = _diff(p64, vt, diff_lambda[layer], diff_norm_g[layer].reshape(HEAD_DIM, 1), lam_init)
        o_c = _moba(p128, vt, _kmean(p128))
        o_d = _dilated(p128, vt)
        xf, xb = _outproj_ln(o_a.reshape(t, MIX_W), o_b.reshape(t, MIX_W), o_c.reshape(t, MIX_W),
                             o_d.reshape(t, MIX_W), w_out[layer].astype(BF16), xf,
                             ln_mix_g[layer].reshape(1, d), ln_mix_b[layer].reshape(1, d))

        e, gate, rank, counts = _router(xf, rw_t, bias)
        dest, pend, block_expert, next_expert, n_active, n_slots = _moe_layout(e, rank, counts[:, 0], EXPERT_TB)
        xs = _dispatch(dest, pend, n_active, xf, n_slots)
        ys = _experts(block_expert, next_expert, n_active, xs, w_gate, w_up, w_down, layer)
        xf, xb = _combine_ln(dest, xf, gate.T, ln_ffn_g[layer].reshape(1, d), ln_ffn_b[layer].reshape(1, d), ys)
    return xf.reshape(b, s, d)
```

```python
import functools
import math

import numpy as np
import jax
import jax.numpy as jnp
from jax import lax
from jax.experimental import pallas as pl
from jax.experimental.pallas import tpu as pltpu

F32 = jnp.float32
BF16 = jnp.bfloat16
I32 = jnp.int32

D_MODEL = 2048
HEAD_DIM = 128
HEADS = 4
MIX_W = HEADS * HEAD_DIM
IDX_HEADS = 16
IDX_DIM = 64
DSA_TOPK_MAX = 256
DIFF_DIM = 64
MOBA_BLOCK = 256
MOBA_TOPK = 3
DILATED_PATTERNS = ((128, 1), (512, 4), (2048, 16))
ROPE_THETA = 10000.0
N_EXPERTS = 16
N_GROUPS = 4
EXPERTS_PER_GROUP = 4
LN_EPS = 1e-5
DEPTH = 2
ALPHA = (2 * DEPTH) ** 0.25
LOG2E = math.log2(math.e)

LANES = 128
SUBLANES = 8
NEG = -1e30
M_FLOOR = -1e29
INT_MIN = -(2 ** 31)
NEG_INF_ORDER = 0x007FFFFF
VMEM_LIMIT = 56 * 1024 * 1024

PROJ_TM = 512
PROJ_PARTS = 4
ATT_T = 256
DIFF_TQ = 512
DSA_TQ = 512
MOBA_TQ = 512
DIL_TQ = 512
OUT_TM = 512
OUT_PARTS = 2
ROUTER_TM = 256
DISPATCH_TM = 256
EXPERT_TB = 256
COMBINE_TM = 256
DMA_UNROLL = 8
V_ROWS = 3 * MIX_W + HEAD_DIM

_OFF = {}
_o = 0
for _name, _n in (("a_q", 512), ("a_k", 128), ("a_v", 128), ("i_q", 1024), ("i_k", 64), ("i_w", 16),
                  ("b_q", 512), ("b_k", 512), ("b_v", 512), ("c_q", 512), ("c_k", 512), ("c_v", 512),
                  ("d_q", 512), ("d_k", 512), ("d_v", 512)):
    _OFF[_name] = (_o, _o + _n)
    _o += _n

A_Q_BLK, C_Q_BLK, C_K_BLK, D_Q_BLK, D_K_BLK = range(5)
A_K_BLK = 5 * MIX_W // HEAD_DIM
P128_W = 5 * MIX_W + HEAD_DIM
IQ_W = IDX_HEADS * IDX_DIM
B_Q_BLK = IQ_W // MIX_W
B_K_BLK = B_Q_BLK + 1
I_K_BLK = (IQ_W + 2 * MIX_W) // LANES
P64_W = IQ_W + 2 * MIX_W + 2 * LANES
B_V_BLK, C_V_BLK, D_V_BLK = range(3)
A_V_BLK = 3 * MIX_W // HEAD_DIM


def _params(sem=None):
    return pltpu.CompilerParams(dimension_semantics=sem, vmem_limit_bytes=VMEM_LIMIT)


def _dot_nt(a, b):
    return lax.dot_general(a, b, (((1,), (1,)), ((), ())), preferred_element_type=F32)


def _proj_kernel(x_ref, w_ref, scale_ref, *rest, mode):
    o_ref = rest[-1]
    tm = x_ref.shape[0]
    rows_per_part = tm // PROJ_PARTS
    n_groups = w_ref.shape[1] // LANES
    for part in range(PROJ_PARTS):
        rows = slice(part * rows_per_part, (part + 1) * rows_per_part)
        acc = jnp.dot(x_ref[rows, :].astype(BF16), w_ref[...], preferred_element_type=F32)
        for g in range(n_groups):
            cols = slice(g * LANES, (g + 1) * LANES)
            y = acc[:, cols]
            if mode == "rope128":
                cos_ref, sin_ref = rest[0], rest[1]
                r = y * cos_ref[rows, :] + pltpu.roll(y, 64, 1) * sin_ref[rows, :]
            else:
                cos_ref, sina_ref, sinb_ref = rest[0], rest[1], rest[2]
                r = (y * cos_ref[rows, :] + pltpu.roll(y, 96, 1) * sina_ref[rows, :]
                     + pltpu.roll(y, 32, 1) * sinb_ref[rows, :])
            r = r * scale_ref[:, cols]
            o_ref[rows, cols] = r.astype(o_ref.dtype)


def _proj(x_bf, w_bf, col_scale, tables, mode, tn):
    t, d = x_bf.shape
    n = w_bf.shape[1]
    tm = min(PROJ_TM, t)
    in_specs = [pl.BlockSpec((tm, d), lambda j, i: (i, 0)),
                pl.BlockSpec((d, tn), lambda j, i: (0, j)),
                pl.BlockSpec((1, tn), lambda j, i: (0, j))]
    in_specs += [pl.BlockSpec((tm, LANES), lambda j, i: (i, 0)) for _ in tables]
    return pl.pallas_call(
        functools.partial(_proj_kernel, mode=mode),
        grid=(n // tn, t // tm),
        in_specs=in_specs,
        out_specs=pl.BlockSpec((tm, tn), lambda j, i: (i, j)),
        out_shape=jax.ShapeDtypeStruct((t, n), BF16),
        compiler_params=_params(("arbitrary", "arbitrary")),
    )(x_bf, w_bf, col_scale, *tables)


def _proj_t_kernel(x_ref, wt_ref, vt_ref, iwt_ref):
    acc = _dot_nt(wt_ref[...], x_ref[...].astype(BF16))
    vt_ref[...] = acc[:V_ROWS].astype(vt_ref.dtype)
    iwt_ref[...] = acc[V_ROWS:]


def _proj_t(x_bf, wt_bf):
    t, d = x_bf.shape
    tm = min(ATT_T, t)
    rows = wt_bf.shape[0]
    return pl.pallas_call(
        _proj_t_kernel,
        grid=(t // tm,),
        in_specs=[pl.BlockSpec((tm, d), lambda i: (i, 0)),
                  pl.BlockSpec((rows, d), lambda i: (0, 0))],
        out_specs=[pl.BlockSpec((None, V_ROWS, tm), lambda i: (i, 0, 0)),
                   pl.BlockSpec((None, IDX_HEADS, tm), lambda i: (i, 0, 0))],
        out_shape=[jax.ShapeDtypeStruct((t // tm, V_ROWS, tm), BF16),
                   jax.ShapeDtypeStruct((t // tm, IDX_HEADS, tm), F32)],
        compiler_params=_params(("arbitrary",)),
    )(x_bf, wt_bf)


def _online_updates(scores, mask, values, m_ref, l_ref, acc_ref, weight=None):
    probs, alphas = [], []
    masks = mask if isinstance(mask, list) else [mask] * len(scores)
    for idx, (s_t, mask) in enumerate(zip(scores, masks)):
        if mask is not None:
            s_t = jnp.where(mask, s_t, NEG)
        m_prev = m_ref[idx]
        m_new = jnp.maximum(m_prev, jnp.max(s_t, axis=0, keepdims=True))
        p = jnp.exp2(s_t - m_new)
        if weight is not None:
            p = p * weight
        alpha = jnp.exp2(m_prev - m_new)
        l_ref[idx] = alpha * l_ref[idx] + jnp.sum(p, axis=0, keepdims=True)
        m_ref[idx] = m_new
        probs.append(p.astype(BF16))
        alphas.append(alpha)
    for idx, (p, alpha) in enumerate(zip(probs, alphas)):
        acc_ref[idx] = alpha * acc_ref[idx] + jnp.dot(values[idx], p, preferred_element_type=F32)


def _init_softmax(m_ref, l_ref, acc_ref):
    m_ref[...] = jnp.full(m_ref.shape, M_FLOOR, F32)
    l_ref[...] = jnp.zeros(l_ref.shape, F32)
    acc_ref[...] = jnp.zeros(acc_ref.shape, F32)


def _softmax_scratch(n, t):
    return [pltpu.VMEM((n, 1, t), F32), pltpu.VMEM((n, 1, t), F32), pltpu.VMEM((n, HEAD_DIM, t), F32)]


def _causal_mask(tk, tq, k_start):
    return k_start + lax.broadcasted_iota(I32, (tk, tq), 0) <= lax.broadcasted_iota(I32, (tk, tq), 1)


def _head(ref_or_val, h):
    return ref_or_val[:, h * HEAD_DIM:(h + 1) * HEAD_DIM]


def _order_to_f32(u):
    key = u ^ jnp.int32(INT_MIN)
    bits = jnp.where(key < 0, key ^ jnp.int32(0x7FFFFFFF), key)
    return lax.bitcast_convert_type(bits, F32)


def _dsa_kernel(aq_ref, ak_ref, avt_ref, iq_ref, ik_ref, iwt_ref, o_ref,
                score_ref, m_ref, l_ref, acc_ref, jlim_ref, *, n_keep):
    tq = aq_ref.shape[0]
    t = avt_ref.shape[2]
    i = pl.program_id(1)
    n_chunks = (i + 1) * (tq // t)
    lane = lax.broadcasted_iota(I32, (1, LANES), 1)
    k_off = lax.broadcasted_iota(I32, (t, 1), 0)
    q_pos = i * tq + lax.broadcasted_iota(I32, (1, tq), 1)
    iw_t = jnp.concatenate([iwt_ref[n] for n in range(tq // t)], axis=1)

    def score_chunk(c, carry):
        ik_c = ik_ref[pl.ds(pl.multiple_of(c * t, t), t), :]
        ik_lo = jnp.where(lane < IDX_DIM, ik_c, jnp.zeros_like(ik_c))
        ik_hi = jnp.where(lane >= IDX_DIM, ik_c, jnp.zeros_like(ik_c))
        score = jnp.zeros((t, tq), F32)
        for j in range(IDX_HEADS // 2):
            q_pair = iq_ref[:, j * LANES:(j + 1) * LANES]
            s_even = jnp.maximum(_dot_nt(ik_lo, q_pair), 0.0)
            s_odd = jnp.maximum(_dot_nt(ik_hi, q_pair), 0.0)
            score = score + s_even * iw_t[2 * j:2 * j + 1, :] + s_odd * iw_t[2 * j + 1:2 * j + 2, :]
        score_ref[c] = jnp.where(c * t + k_off <= q_pos, score, -jnp.inf)
        return carry

    lax.fori_loop(0, n_chunks, score_chunk, 0)

    def count(pred):
        def body(c, part):
            hit = jnp.where(pred(score_ref[c], c), 1, 0)
            return part + jnp.sum(hit.reshape(t // SUBLANES, SUBLANES, tq), axis=0)
        part = lax.fori_loop(0, n_chunks, body, jnp.zeros((SUBLANES, tq), I32))
        return jnp.sum(part, axis=0, keepdims=True)

    def bisect(it, carry):
        thr_u, n_ge = carry
        cand_u = thr_u | jnp.left_shift(jnp.int32(1), 31 - it)
        cand = _order_to_f32(cand_u)
        cnt = count(lambda sc, c: sc >= cand)
        keep = cnt >= n_keep
        return jnp.where(keep, cand_u, thr_u), jnp.where(keep, cnt, n_ge)

    zeros = jnp.zeros((1, tq), I32)
    thr_u, n_ge = lax.fori_loop(0, 32, bisect, (zeros, zeros))
    below = (thr_u >= 0) & (thr_u < NEG_INF_ORDER)
    thr = jnp.where(below, -jnp.inf, _order_to_f32(thr_u))

    s_total = score_ref.shape[0] * t
    jlim_ref[...] = jnp.full((1, tq), s_total, I32)
    excess = jnp.max(jnp.where((n_ge > n_keep) & (thr > -jnp.inf), 1, 0))

    @pl.when(excess > 0)
    def _():
        n_bits = max(1, (s_total - 1).bit_length())
        need = n_keep - count(lambda sc, c: sc > thr)

        def bisect_idx(it, j):
            cand = j | jnp.left_shift(jnp.int32(1), n_bits - 1 - it)
            cnt = count(lambda sc, c: (sc == thr) & (c * t + k_off < cand))
            return jnp.where(cnt < need, cand, j)

        jlim_ref[...] = lax.fori_loop(0, n_bits, bisect_idx, jnp.zeros((1, tq), I32))

    _init_softmax(m_ref, l_ref, acc_ref)
    jlim = jnp.where(thr > -jnp.inf, jlim_ref[...], -1)
    thr_all = jnp.where(thr > -jnp.inf, thr, jnp.finfo(F32).min)

    def attend(select):
        def body(c, carry):
            start = pl.multiple_of(c * t, t)
            k_c = ak_ref[pl.ds(start, t), :]
            scores = [_dot_nt(k_c, _head(aq_ref, h)) for h in range(HEADS)]
            sel = select(score_ref[c], c * t + k_off)
            _online_updates(scores, sel, [avt_ref[c]] * HEADS, m_ref, l_ref, acc_ref)
            return carry
        lax.fori_loop(0, n_chunks, body, 0)

    @pl.when(excess > 0)
    def _():
        attend(lambda sc, k_pos: (sc > thr) | ((sc == thr) & (k_pos <= jlim)))

    @pl.when(excess <= 0)
    def _():
        attend(lambda sc, k_pos: sc >= thr_all)
    for h in range(HEADS):
        o_ref[:, h * HEAD_DIM:(h + 1) * HEAD_DIM] = (acc_ref[h] / l_ref[h]).T.astype(o_ref.dtype)


def _dsa(p128, p64, vt, iwt, n_keep):
    b, s, _ = p128.shape
    t = min(ATT_T, s)
    tq = min(DSA_TQ, s)
    nc = s // t
    return pl.pallas_call(
        functools.partial(_dsa_kernel, n_keep=n_keep),
        grid=(b, s // tq),
        in_specs=[
            pl.BlockSpec((None, tq, MIX_W), lambda bb, i: (bb, i, A_Q_BLK)),
            pl.BlockSpec((None, s, HEAD_DIM), lambda bb, i: (bb, 0, A_K_BLK)),
            pl.BlockSpec((None, nc, HEAD_DIM, t), lambda bb, i: (bb, 0, A_V_BLK, 0)),
            pl.BlockSpec((None, tq, IQ_W), lambda bb, i: (bb, i, 0)),
            pl.BlockSpec((None, s, LANES), lambda bb, i: (bb, 0, I_K_BLK)),
            pl.BlockSpec((None, tq // t, IDX_HEADS, t), lambda bb, i: (bb, i, 0, 0)),
        ],
        out_specs=pl.BlockSpec((None, tq, MIX_W), lambda bb, i: (bb, i, 0)),
        out_shape=jax.ShapeDtypeStruct((b, s, MIX_W), BF16),
        scratch_shapes=[pltpu.VMEM((nc, t, tq), F32)] + _softmax_scratch(HEADS, tq)
        + [pltpu.VMEM((1, tq), I32)],
        compiler_params=_params(("arbitrary", "arbitrary")),
    )(p128, p128, vt, p64, p64, iwt)


def _diff_kernel(q_ref, k_ref, vt_ref, lam_ref, g_ref, o_ref, m_ref, l_ref, acc_ref, *, lam_init):
    tq = q_ref.shape[0]
    t = vt_ref.shape[2]
    per_tile = tq // t
    i = pl.program_id(1)
    lane = lax.broadcasted_iota(I32, (1, LANES), 1)
    _init_softmax(m_ref, l_ref, acc_ref)

    def step(c, mask):
        k_c = k_ref[pl.ds(pl.multiple_of(c * t, t), t), :]
        v_c = vt_ref[c]
        scores, values = [], []
        for h in range(HEADS):
            q_h = _head(q_ref, h)
            k_h = _head(k_c, h)
            q_1 = jnp.where(lane < DIFF_DIM, q_h, jnp.zeros_like(q_h))
            q_2 = jnp.where(lane >= DIFF_DIM, q_h, jnp.zeros_like(q_h))
            scores += [_dot_nt(k_h, q_1), _dot_nt(k_h, q_2)]
            values += [v_c[h * HEAD_DIM:(h + 1) * HEAD_DIM, :]] * 2
        _online_updates(scores, mask, values, m_ref, l_ref, acc_ref)

    def full_step(c, carry):
        step(c, None)
        return carry

    lax.fori_loop(0, i * per_tile, full_step, 0)
    for d in range(per_tile):
        step(i * per_tile + d, _causal_mask(t, tq, d * t))

    lam_p = lam_ref[...]
    lam = (jnp.exp(jnp.sum(lam_p[0:1] * lam_p[1:2], axis=1, keepdims=True))
           - jnp.exp(jnp.sum(lam_p[2:3] * lam_p[3:4], axis=1, keepdims=True)) + lam_init)
    for h in range(HEADS):
        o = acc_ref[2 * h] / l_ref[2 * h] - lam * (acc_ref[2 * h + 1] / l_ref[2 * h + 1])
        o = o * lax.rsqrt(jnp.mean(o * o, axis=0, keepdims=True) + LN_EPS) * g_ref[...]
        o_ref[:, h * HEAD_DIM:(h + 1) * HEAD_DIM] = (o * (1.0 - lam_init)).T.astype(o_ref.dtype)


def _diff(p64, vt, lam_p, g_col, lam_init):
    b, s, _ = p64.shape
    t = min(ATT_T, s)
    tq = min(DIFF_TQ, s)
    nc = s // t
    return pl.pallas_call(
        functools.partial(_diff_kernel, lam_init=lam_init),
        grid=(b, s // tq),
        in_specs=[
            pl.BlockSpec((None, tq, MIX_W), lambda bb, i: (bb, i, B_Q_BLK)),
            pl.BlockSpec((None, s, MIX_W), lambda bb, i: (bb, 0, B_K_BLK)),
            pl.BlockSpec((None, nc, MIX_W, t), lambda bb, i: (bb, 0, B_V_BLK, 0)),
            pl.BlockSpec((4, DIFF_DIM), lambda bb, i: (0, 0)),
            pl.BlockSpec((HEAD_DIM, 1), lambda bb, i: (0, 0)),
        ],
        out_specs=pl.BlockSpec((None, tq, MIX_W), lambda bb, i: (bb, i, 0)),
        out_shape=jax.ShapeDtypeStruct((b, s, MIX_W), BF16),
        scratch_shapes=_softmax_scratch(2 * HEADS, tq),
        compiler_params=_params(("arbitrary", "arbitrary")),
    )(p64, p64, vt, lam_p, g_col)


def _kmean_kernel(k_ref, o_ref):
    n_blk = k_ref.shape[0] // MOBA_BLOCK
    o_ref[...] = jnp.zeros(o_ref.shape, o_ref.dtype)
    for n in range(n_blk):
        blk = k_ref[n * MOBA_BLOCK:(n + 1) * MOBA_BLOCK, :].astype(F32)
        o_ref[n:n + 1, :] = jnp.mean(blk, axis=0, keepdims=True)


def _kmean(p128):
    b, s, _ = p128.shape
    rows = max(SUBLANES, s // MOBA_BLOCK)
    return pl.pallas_call(
        _kmean_kernel,
        grid=(b,),
        in_specs=[pl.BlockSpec((None, s, MIX_W), lambda bb: (bb, 0, C_K_BLK))],
        out_specs=pl.BlockSpec((None, rows, MIX_W), lambda bb: (bb, 0, 0)),
        out_shape=jax.ShapeDtypeStruct((b, rows, MIX_W), F32),
        compiler_params=_params(("arbitrary",)),
    )(p128)


def _moba_kernel(q_ref, k_ref, vt_ref, km_ref, o_ref, sel_ref, m_ref, l_ref, acc_ref):
    tq = q_ref.shape[0]
    t = vt_ref.shape[2]
    per_tile = tq // t
    n_blk = km_ref.shape[0]
    i = pl.program_id(1)
    blk = lax.broadcasted_iota(I32, (n_blk, 1), 0)
    own_rel = lax.broadcasted_iota(I32, (1, tq), 1) // t
    own = i * per_tile + own_rel
    _init_softmax(m_ref, l_ref, acc_ref)

    for h in range(HEADS):
        gate = _dot_nt(_head(km_ref, h).astype(BF16), _head(q_ref, h))
        gate = jnp.where(blk < own, gate, -jnp.inf)
        sel = jnp.zeros(gate.shape, F32)
        for _ in range(MOBA_TOPK):
            best = jnp.max(gate, axis=0, keepdims=True)
            first = jnp.min(jnp.where(gate == best, blk, n_blk), axis=0, keepdims=True)
            pick = (blk == first) & (best > -jnp.inf)
            sel = jnp.where(pick, 1.0, sel)
            gate = jnp.where(pick, -jnp.inf, gate)
        sel_ref[h] = sel

    def step(c, tile_block):
        k_c = k_ref[pl.ds(pl.multiple_of(c * t, t), t), :]
        v_c = vt_ref[c]
        scores = [_dot_nt(_head(k_c, h), _head(q_ref, h)) for h in range(HEADS)]
        values = [v_c[h * HEAD_DIM:(h + 1) * HEAD_DIM, :] for h in range(HEADS)]
        picked = [sel_ref[h, pl.ds(c, 1), :] > 0.0 for h in range(HEADS)]
        if tile_block is None:
            masks = picked
        else:
            causal = (own_rel == tile_block) & _causal_mask(t, tq, tile_block * t)
            masks = [causal | ((own_rel > tile_block) & p) for p in picked]
        _online_updates(scores, masks, values, m_ref, l_ref, acc_ref)

    def past_step(c, carry):
        step(c, None)
        return carry

    lax.fori_loop(0, i * per_tile, past_step, 0)
    for d in range(per_tile):
        step(i * per_tile + d, d)
    for h in range(HEADS):
        o_ref[:, h * HEAD_DIM:(h + 1) * HEAD_DIM] = (acc_ref[h] / l_ref[h]).T.astype(o_ref.dtype)


def _moba(p128, vt, kmean):
    b, s, _ = p128.shape
    t = MOBA_BLOCK
    tq = min(MOBA_TQ, s)
    nc = s // t
    rows = kmean.shape[1]
    return pl.pallas_call(
        _moba_kernel,
        grid=(b, s // tq),
        in_specs=[
            pl.BlockSpec((None, tq, MIX_W), lambda bb, i: (bb, i, C_Q_BLK)),
            pl.BlockSpec((None, s, MIX_W), lambda bb, i: (bb, 0, C_K_BLK)),
            pl.BlockSpec((None, nc, MIX_W, t), lambda bb, i: (bb, 0, C_V_BLK, 0)),
            pl.BlockSpec((None, rows, MIX_W), lambda bb, i: (bb, 0, 0)),
        ],
        out_specs=pl.BlockSpec((None, tq, MIX_W), lambda bb, i: (bb, i, 0)),
        out_shape=jax.ShapeDtypeStruct((b, s, MIX_W), BF16),
        scratch_shapes=[pltpu.VMEM((HEADS, rows, tq), F32)] + _softmax_scratch(HEADS, tq),
        compiler_params=_params(("arbitrary", "arbitrary")),
    )(p128, p128, vt, kmean)


def _dilated_kernel(q_ref, k_ref, vt_ref, o_ref, m_ref, l_ref, acc_ref):
    tq = q_ref.shape[0]
    t = vt_ref.shape[2]
    i = pl.program_id(1)
    _init_softmax(m_ref, l_ref, acc_ref)
    rel = lax.broadcasted_iota(I32, (t, tq), 1) - lax.broadcasted_iota(I32, (t, tq), 0)

    def first_chunk_within(window):
        return jnp.maximum(i * tq - window, 0) // t

    def step(c, patterns):
        k_c = k_ref[pl.ds(pl.multiple_of(c * t, t), t), :]
        v_c = vt_ref[c]
        dist = rel + (i * tq - c * t)
        mult = jnp.zeros((t, tq), F32)
        for window, dilation in patterns:
            hit = (dist >= 0) & (dist <= window) & ((dist & (dilation - 1)) == 0)
            mult = mult + jnp.where(hit, 1.0, 0.0)
        mask = mult > 0.0
        scores = [_dot_nt(_head(k_c, h), _head(q_ref, h)) for h in range(HEADS)]
        values = [v_c[h * HEAD_DIM:(h + 1) * HEAD_DIM, :] for h in range(HEADS)]
        _online_updates(scores, mask, values, m_ref, l_ref, acc_ref, weight=mult)

    widest = max(DILATED_PATTERNS)
    others = [p for p in DILATED_PATTERNS if p != widest]
    first = first_chunk_within(widest[0])
    near_first = first_chunk_within(max(w for w, _ in others))

    def far_step(c, carry):
        step(c, [widest])
        return carry

    def near_step(c, carry):
        step(c, DILATED_PATTERNS)
        return carry

    lax.fori_loop(first, near_first, far_step, 0)
    lax.fori_loop(near_first, (i + 1) * (tq // t), near_step, 0)
    for h in range(HEADS):
        o_ref[:, h * HEAD_DIM:(h + 1) * HEAD_DIM] = (acc_ref[h] / l_ref[h]).T.astype(o_ref.dtype)


def _dilated(p128, vt):
    b, s, _ = p128.shape
    t = min(ATT_T, s)
    tq = min(DIL_TQ, s)
    nc = s // t
    return pl.pallas_call(
        _dilated_kernel,
        grid=(b, s // tq),
        in_specs=[
            pl.BlockSpec((None, tq, MIX_W), lambda bb, i: (bb, i, D_Q_BLK)),
            pl.BlockSpec((None, s, MIX_W), lambda bb, i: (bb, 0, D_K_BLK)),
            pl.BlockSpec((None, nc, MIX_W, t), lambda bb, i: (bb, 0, D_V_BLK, 0)),
        ],
        out_specs=pl.BlockSpec((None, tq, MIX_W), lambda bb, i: (bb, i, 0)),
        out_shape=jax.ShapeDtypeStruct((b, s, MIX_W), BF16),
        scratch_shapes=_softmax_scratch(HEADS, tq),
        compiler_params=_params(("arbitrary", "arbitrary")),
    )(p128, p128, vt)


def _layer_norm(z, g, b):
    mu = jnp.mean(z, axis=1, keepdims=True)
    zc = z - mu
    var = jnp.mean(zc * zc, axis=1, keepdims=True)
    return zc * lax.rsqrt(var + LN_EPS) * g + b


def _outproj_ln_kernel(oa_ref, ob_ref, oc_ref, od_ref, w_ref, x_ref, g_ref, b_ref, xo_ref, xb_ref):
    rows_per_part = x_ref.shape[0] // OUT_PARTS
    for part in range(OUT_PARTS):
        rows = slice(part * rows_per_part, (part + 1) * rows_per_part)
        y = jnp.dot(oa_ref[rows, :], w_ref[0:MIX_W, :], preferred_element_type=F32)
        y = y + jnp.dot(ob_ref[rows, :], w_ref[MIX_W:2 * MIX_W, :], preferred_element_type=F32)
        y = y + jnp.dot(oc_ref[rows, :], w_ref[2 * MIX_W:3 * MIX_W, :], preferred_element_type=F32)
        y = y + jnp.dot(od_ref[rows, :], w_ref[3 * MIX_W:4 * MIX_W, :], preferred_element_type=F32)
        out = _layer_norm(ALPHA * x_ref[rows, :] + y, g_ref[...], b_ref[...])
        xo_ref[rows, :] = out
        xb_ref[rows, :] = out.astype(BF16)


def _outproj_ln(oa, ob, oc, od, w_bf, x, g, b):
    t, d = x.shape
    tm = min(OUT_TM, t)
    mix_spec = pl.BlockSpec((tm, MIX_W), lambda i: (i, 0))
    row_spec = pl.BlockSpec((tm, d), lambda i: (i, 0))
    vec_spec = pl.BlockSpec((1, d), lambda i: (0, 0))
    return pl.pallas_call(
        _outproj_ln_kernel,
        grid=(t // tm,),
        in_specs=[mix_spec, mix_spec, mix_spec, mix_spec,
                  pl.BlockSpec((d, d), lambda i: (0, 0)), row_spec, vec_spec, vec_spec],
        out_specs=[row_spec, row_spec],
        out_shape=[jax.ShapeDtypeStruct((t, d), F32), jax.ShapeDtypeStruct((t, d), BF16)],
        compiler_params=_params(("arbitrary",)),
    )(oa, ob, oc, od, w_bf, x, g, b)


def _first_argmax(vals):
    best, arg = vals[0], jnp.zeros(vals[0].shape, I32)
    for j in range(1, len(vals)):
        better = vals[j] > best
        arg = jnp.where(better, j, arg)
        best = jnp.where(better, vals[j], best)
    return arg, best


def _pick(rows, idx):
    out = rows[0]
    for j in range(1, len(rows)):
        out = jnp.where(idx == j, rows[j], out)
    return out


def _router_kernel(x_ref, rwt_ref, bias_ref, e_ref, gate_ref, rank_ref, cnt_ref, carry_ref):
    tm = x_ref.shape[0]

    @pl.when(pl.program_id(0) == 0)
    def _():
        carry_ref[...] = jnp.zeros(carry_ref.shape, F32)

    logits = lax.dot_general(rwt_ref[...], x_ref[...], (((1,), (1,)), ((), ())),
                             precision=lax.Precision.HIGHEST, preferred_element_type=F32)
    aff = 1.0 / (1.0 + jnp.exp(-logits))
    biased = aff + bias_ref[...]
    sel_rows = [biased[r:r + 1, :] for r in range(N_EXPERTS)]
    aff_rows = [aff[r:r + 1, :] for r in range(N_EXPERTS)]

    group_scores = []
    for g in range(N_GROUPS):
        r = sel_rows[g * EXPERTS_PER_GROUP:(g + 1) * EXPERTS_PER_GROUP]
        best_pair = r[0] + r[1]
        for a in range(EXPERTS_PER_GROUP):
            for c in range(a + 1, EXPERTS_PER_GROUP):
                if (a, c) != (0, 1):
                    best_pair = jnp.maximum(best_pair, r[a] + r[c])
        group_scores.append(best_pair)
    grp, _ = _first_argmax(group_scores)

    in_sel = [_pick([sel_rows[g * EXPERTS_PER_GROUP + j] for g in range(N_GROUPS)], grp)
              for j in range(EXPERTS_PER_GROUP)]
    in_aff = [_pick([aff_rows[g * EXPERTS_PER_GROUP + j] for g in range(N_GROUPS)], grp)
              for j in range(EXPERTS_PER_GROUP)]
    first, _ = _first_argmax(in_sel)
    second, _ = _first_argmax([jnp.where(first == j, -jnp.inf, in_sel[j]) for j in range(EXPERTS_PER_GROUP)])
    a0 = _pick(in_aff, first)
    a1 = _pick(in_aff, second)
    e0 = grp * EXPERTS_PER_GROUP + first
    e1 = grp * EXPERTS_PER_GROUP + second
    e_ref[0:1, :] = e0
    e_ref[1:2, :] = e1
    gate_ref[0:1, :] = a0 / (a0 + a1)
    gate_ref[1:2, :] = a1 / (a0 + a1)

    e_iota = lax.broadcasted_iota(I32, (N_EXPERTS, tm), 0)
    one_hot = ((e_iota == e0) | (e_iota == e1)).astype(BF16)
    before = (lax.broadcasted_iota(I32, (tm, tm), 0) < lax.broadcasted_iota(I32, (tm, tm), 1)).astype(BF16)
    prior = jnp.dot(one_hot, before, preferred_element_type=F32) + carry_ref[...]
    rank_ref[0:1, :] = jnp.sum(jnp.where(e_iota == e0, prior, 0.0), axis=0, keepdims=True).astype(I32)
    rank_ref[1:2, :] = jnp.sum(jnp.where(e_iota == e1, prior, 0.0), axis=0, keepdims=True).astype(I32)
    carry_ref[...] = carry_ref[...] + jnp.sum(one_hot.astype(F32), axis=1, keepdims=True)
    cnt_ref[...] = jnp.broadcast_to(carry_ref[...], cnt_ref.shape).astype(I32)


def _router(x, rw_t, bias):
    t, d = x.shape
    tm = min(ROUTER_TM, t)
    tok_spec = pl.BlockSpec((2, tm), lambda i: (0, i))
    return pl.pallas_call(
        _router_kernel,
        grid=(t // tm,),
        in_specs=[pl.BlockSpec((tm, d), lambda i: (i, 0)),
                  pl.BlockSpec((N_EXPERTS, d), lambda i: (0, 0)),
                  pl.BlockSpec((N_EXPERTS, 1), lambda i: (0, 0))],
        out_specs=[tok_spec, tok_spec, tok_spec, pl.BlockSpec((N_EXPERTS, LANES), lambda i: (0, 0))],
        out_shape=[jax.ShapeDtypeStruct((2, t), I32), jax.ShapeDtypeStruct((2, t), F32),
                   jax.ShapeDtypeStruct((2, t), I32), jax.ShapeDtypeStruct((N_EXPERTS, LANES), I32)],
        scratch_shapes=[pltpu.VMEM((N_EXPERTS, 1), F32)],
        compiler_params=_params(("arbitrary",)),
    )(x, rw_t, bias)


def _row_copy(src, src_row, dst, dst_row, sem):
    return pltpu.make_async_copy(src.at[pl.ds(src_row, 1)], dst.at[pl.ds(dst_row, 1)], sem)


def _dispatch_kernel(dest_ref, pend_ref, na_ref, x_ref, slots_hbm, zeros_ref, sem, zero_sem,
                     *, tm, n_tok, tb):
    base = pl.program_id(0) * tm

    @pl.when(pl.program_id(0) == 0)
    def _():
        zeros_ref[...] = jnp.zeros(zeros_ref.shape, zeros_ref.dtype)
        n_blocks = slots_hbm.shape[0] // tb

        def block_fill(start):
            return pltpu.make_async_copy(zeros_ref, slots_hbm.at[pl.ds(start, tb)], zero_sem)

        def for_each_fill(act):
            for e in range(N_EXPERTS):
                seg_start = pend_ref[e - 1] if e else 0

                @pl.when(pend_ref[e] > seg_start)
                def _():
                    act(pl.multiple_of(pend_ref[e] - tb, tb))

            def tail(n, carry):
                act(pl.multiple_of(n * tb, tb))
                return carry

            lax.fori_loop(na_ref[0], n_blocks, tail, 0)

        for_each_fill(lambda start: block_fill(start).start())
        for_each_fill(lambda start: block_fill(start).wait())

    def issue(r, carry):
        tok = base + r
        _row_copy(x_ref, r, slots_hbm, dest_ref[tok], sem).start(priority=0)
        _row_copy(x_ref, r, slots_hbm, dest_ref[n_tok + tok], sem).start(priority=1)
        return carry

    def drain(r, carry):
        _row_copy(x_ref, 0, slots_hbm, 0, sem).wait()
        _row_copy(x_ref, 0, slots_hbm, 0, sem).wait()
        return carry

    lax.fori_loop(0, tm, issue, 0, unroll=DMA_UNROLL)
    lax.fori_loop(0, tm, drain, 0, unroll=DMA_UNROLL)


def _dispatch(dest_flat, pend, n_active, x, n_slots):
    t, d = x.shape
    tm = min(DISPATCH_TM, t)
    tb = EXPERT_TB
    return pl.pallas_call(
        functools.partial(_dispatch_kernel, tm=tm, n_tok=t, tb=tb),
        grid_spec=pltpu.PrefetchScalarGridSpec(
            num_scalar_prefetch=3,
            grid=(t // tm,),
            in_specs=[pl.BlockSpec((tm, d), lambda i, dest, pend, na: (i, 0))],
            out_specs=pl.BlockSpec(memory_space=pl.ANY),
            scratch_shapes=[pltpu.VMEM((tb, d), x.dtype), pltpu.SemaphoreType.DMA(()),
                            pltpu.SemaphoreType.DMA(())],
        ),
        out_shape=jax.ShapeDtypeStruct((n_slots, d), x.dtype),
        compiler_params=_params(("arbitrary",)),
    )(dest_flat, pend, n_active, x)


def _last_active(n, na_ref):
    return jnp.maximum(jnp.minimum(n, na_ref[0] - 1), 0)


def _expert_stage(be_ref, nxt_ref, na_ref, w_hbm_refs, stage_refs, w_bf_refs, sem, compute, out_ref, *, layer):
    n = pl.program_id(0)
    active = n < na_ref[0]
    new_expert = (n == 0) | (be_ref[n] != be_ref[jnp.maximum(n - 1, 0)])

    def copies(expert):
        return [pltpu.make_async_copy(w_hbm.at[layer, expert], stage, sem.at[k])
                for k, (w_hbm, stage) in enumerate(zip(w_hbm_refs, stage_refs))]

    @pl.when(active & (n == 0))
    def _():
        for copy in copies(be_ref[0]):
            copy.start()

    @pl.when(active & new_expert)
    def _():
        for copy, stage, w_bf in zip(copies(be_ref[n]), stage_refs, w_bf_refs):
            copy.wait()
            w_bf[...] = stage[...].astype(BF16)

        @pl.when(nxt_ref[n] >= 0)
        def _():
            for copy in copies(nxt_ref[n]):
                copy.start()

    @pl.when(active)
    def _():
        out_ref[...] = compute().astype(out_ref.dtype)

    @pl.when(jnp.logical_not(active))
    def _():
        out_ref[...] = jnp.zeros(out_ref.shape, out_ref.dtype)


def _expert_up_kernel(be_ref, nxt_ref, na_ref, xs_ref, wg_hbm, wu_hbm, h_ref,
                      wg_stage, wu_stage, wg_bf, wu_bf, sem, *, layer):
    def compute():
        xb = xs_ref[...].astype(BF16)
        gate = jnp.dot(xb, wg_bf[...], preferred_element_type=F32)
        up = jnp.dot(xb, wu_bf[...], preferred_element_type=F32)
        return gate * (1.0 / (1.0 + jnp.exp(-gate))) * up

    _expert_stage(be_ref, nxt_ref, na_ref, (wg_hbm, wu_hbm), (wg_stage, wu_stage), (wg_bf, wu_bf), sem,
                  compute, h_ref, layer=layer)


def _expert_down_kernel(be_ref, nxt_ref, na_ref, h_ref, wd_hbm, ys_ref, wd_stage, wd_bf, sem, *, layer):
    def compute():
        return jnp.dot(h_ref[...], wd_bf[...], preferred_element_type=F32)

    _expert_stage(be_ref, nxt_ref, na_ref, (wd_hbm,), (wd_stage,), (wd_bf,), sem, compute, ys_ref,
                  layer=layer)


def _experts(block_expert, next_expert, n_active, xs, wg, wu, wd, layer):
    n_slots, d = xs.shape
    tb = EXPERT_TB
    f = wg.shape[3]

    def row_map(n, be, nxt, na):
        return (_last_active(n, na), 0)

    def out_map(n, be, nxt, na):
        return (n, 0)

    hbm = pl.BlockSpec(memory_space=pl.ANY)
    hidden = pl.pallas_call(
        functools.partial(_expert_up_kernel, layer=layer),
        grid_spec=pltpu.PrefetchScalarGridSpec(
            num_scalar_prefetch=3,
            grid=(n_slots // tb,),
            in_specs=[pl.BlockSpec((tb, d), row_map), hbm, hbm],
            out_specs=pl.BlockSpec((tb, f), out_map),
            scratch_shapes=[pltpu.VMEM((d, f), F32), pltpu.VMEM((d, f), F32),
                            pltpu.VMEM((d, f), BF16), pltpu.VMEM((d, f), BF16),
                            pltpu.SemaphoreType.DMA((2,))],
        ),
        out_shape=jax.ShapeDtypeStruct((n_slots, f), BF16),
        compiler_params=_params(("arbitrary",)),
    )(block_expert, next_expert, n_active, xs, wg, wu)
    return pl.pallas_call(
        functools.partial(_expert_down_kernel, layer=layer),
        grid_spec=pltpu.PrefetchScalarGridSpec(
            num_scalar_prefetch=3,
            grid=(n_slots // tb,),
            in_specs=[pl.BlockSpec((tb, f), row_map), hbm],
            out_specs=pl.BlockSpec((tb, d), out_map),
            scratch_shapes=[pltpu.VMEM((f, d), F32), pltpu.VMEM((f, d), BF16),
                            pltpu.SemaphoreType.DMA((1,))],
        ),
        out_shape=jax.ShapeDtypeStruct((n_slots, d), F32),
        compiler_params=_params(("arbitrary",)),
    )(block_expert, next_expert, n_active, hidden, wd)


def _combine_ln_kernel(dest_ref, x_ref, gate_ref, g_ref, b_ref, ys_hbm, xo_ref, xb_ref, buf_ref, sem,
                       *, n_tok):
    tm = x_ref.shape[0]
    i = pl.program_id(0)
    slot = i % 2

    def request(tile, to_slot):
        def issue(r, carry):
            tok = tile * tm + r
            _row_copy(ys_hbm, dest_ref[tok], buf_ref.at[to_slot, 0], r, sem.at[to_slot]).start(priority=0)
            _row_copy(ys_hbm, dest_ref[n_tok + tok], buf_ref.at[to_slot, 1], r,
                      sem.at[to_slot]).start(priority=1)
            return carry
        lax.fori_loop(0, tm, issue, 0, unroll=DMA_UNROLL)

    @pl.when(i == 0)
    def _():
        request(0, 0)

    @pl.when(i + 1 < pl.num_programs(0))
    def _():
        request(i + 1, 1 - slot)

    def drain(r, carry):
        _row_copy(ys_hbm, 0, buf_ref.at[slot, 0], 0, sem.at[slot]).wait()
        _row_copy(ys_hbm, 0, buf_ref.at[slot, 1], 0, sem.at[slot]).wait()
        return carry

    lax.fori_loop(0, tm, drain, 0, unroll=DMA_UNROLL)
    gate = gate_ref[...]
    moe = buf_ref[slot, 0] * gate[:, 0:1] + buf_ref[slot, 1] * gate[:, 1:2]
    out = _layer_norm(ALPHA * x_ref[...] + moe, g_ref[...], b_ref[...])
    xo_ref[...] = out
    xb_ref[...] = out.astype(BF16)


def _combine_ln(dest_flat, x, gate_t, g, b, ys):
    t, d = x.shape
    tm = min(COMBINE_TM, t)
    row_spec = pl.BlockSpec((tm, d), lambda i, dest: (i, 0))
    vec_spec = pl.BlockSpec((1, d), lambda i, dest: (0, 0))
    return pl.pallas_call(
        functools.partial(_combine_ln_kernel, n_tok=t),
        grid_spec=pltpu.PrefetchScalarGridSpec(
            num_scalar_prefetch=1,
            grid=(t // tm,),
            in_specs=[row_spec, pl.BlockSpec((tm, 2), lambda i, dest: (i, 0)), vec_spec, vec_spec,
                      pl.BlockSpec(memory_space=pl.ANY)],
            out_specs=[row_spec, row_spec],
            scratch_shapes=[pltpu.VMEM((2, 2, tm, d), F32), pltpu.SemaphoreType.DMA((2,))],
        ),
        out_shape=[jax.ShapeDtypeStruct((t, d), F32), jax.ShapeDtypeStruct((t, d), BF16)],
        compiler_params=_params(("arbitrary",)),
    )(dest_flat, x, gate_t, g, b, ys)


def _rope_tables(positions):
    pos = positions.reshape(-1).astype(F32)[:, None]

    def cos_sin(d):
        inv = jnp.power(ROPE_THETA, -jnp.arange(0, d, 2, dtype=F32) / d)
        ang = pos * inv
        return jnp.cos(ang), jnp.sin(ang)

    c, s = cos_sin(HEAD_DIM)
    t128 = (jnp.concatenate([c, c], axis=1), jnp.concatenate([-s, s], axis=1))
    c, s = cos_sin(IDX_DIM)
    z = jnp.zeros_like(s)
    t64 = (jnp.concatenate([c, c, c, c], axis=1),
           jnp.concatenate([-s, z, -s, z], axis=1),
           jnp.concatenate([z, s, z, s], axis=1))
    return t128, t64


def _col_scales():
    s128 = np.ones((1, P128_W), np.float32)
    for blk in (A_Q_BLK, C_Q_BLK, D_Q_BLK):
        s128[:, blk * MIX_W:(blk + 1) * MIX_W] = HEAD_DIM ** -0.5 * LOG2E
    s64 = np.ones((1, P64_W), np.float32)
    s64[:, B_Q_BLK * MIX_W:(B_Q_BLK + 1) * MIX_W] = DIFF_DIM ** -0.5 * LOG2E
    return jnp.asarray(s128), jnp.asarray(s64)


def _split_w_in(w):
    def cols(name):
        lo, hi = _OFF[name]
        return w[:, lo:hi]

    d = w.shape[0]
    w128 = jnp.concatenate([cols(n) for n in ("a_q", "c_q", "c_k", "d_q", "d_k", "a_k")], axis=1)
    w64 = jnp.concatenate([cols("i_q"), cols("b_q"), cols("b_k"), cols("i_k"), cols("i_k"),
                           jnp.zeros((d, LANES), w.dtype)], axis=1)
    wt = jnp.concatenate([cols("b_v"), cols("c_v"), cols("d_v"), cols("a_v"), cols("i_w")], axis=1).T
    return w128.astype(BF16), w64.astype(BF16), wt.astype(BF16)


def _moe_layout(e, rank, counts, tb):
    t = e.shape[1]
    n_slots = 2 * t + N_EXPERTS * tb
    padded = (counts + tb - 1) // tb * tb
    pend = jnp.cumsum(padded)
    pstart = pend - padded
    experts = jnp.arange(N_EXPERTS, dtype=I32)[:, None, None]
    seg_start = jnp.sum(jnp.where(e[None] == experts, pstart[:, None, None], 0), axis=0)
    dest = (seg_start + rank).astype(I32).reshape(-1)
    blocks = jnp.arange(n_slots // tb, dtype=I32) * tb
    block_expert = jnp.minimum(jnp.sum(blocks[:, None] >= pend[None, :], axis=1), N_EXPERTS - 1).astype(I32)
    n_active = (pend[-1:] // tb).astype(I32)
    n_blocks = n_slots // tb
    blk = jnp.arange(n_blocks, dtype=I32)
    later = ((blk[None, :] > blk[:, None]) & (block_expert[None, :] != block_expert[:, None])
             & (blk[None, :] < n_active[0]))
    first_later = jnp.min(jnp.where(later, blk[None, :], n_blocks), axis=1)
    next_expert = jnp.sum(jnp.where(blk[None, :] == first_later[:, None], block_expert[None, :] + 1, 0),
                          axis=1).astype(I32) - 1
    return dest, pend.astype(I32), block_expert, next_expert, n_active, n_slots


def kernel(x, positions, w_in, w_out, diff_lambda, diff_norm_g, ln_mix_g, ln_mix_b, router_w, router_bias,
           w_gate, w_up, w_down, ln_ffn_g, ln_ffn_b):
    b, s, d = x.shape
    t = b * s
    n_keep = min(DSA_TOPK_MAX, s // 4)
    t128, t64 = _rope_tables(positions)
    s128, s64 = _col_scales()
    rw_t = router_w.T
    bias = router_bias.reshape(N_EXPERTS, 1).astype(F32)
    xf = x.reshape(t, d)
    xb = xf
    nc = s // min(ATT_T, s)
    for layer in range(DEPTH):
        w128, w64, wt = _split_w_in(w_in[layer])
        p128 = _proj(xb, w128, s128, t128, "rope128", P128_W).reshape(b, s, -1)
        p64 = _proj(xb, w64, s64, t64, "rope64", P64_W).reshape(b, s, -1)
        vt, iwt = _proj_t(xb, wt)
        vt = vt.reshape(b, nc, V_ROWS, -1)
        iwt = iwt.reshape(b, nc, IDX_HEADS, -1)

        lam_init = 0.8 - 0.6 * math.exp(-0.3 * layer)
        o_a = _dsa(p128, p64, vt, iwt, n_keep)
        o_b = _diff(p64, vt, diff_lambda[layer], diff_norm_g[layer].reshape(HEAD_DIM, 1), lam_init)
        o_c = _moba(p128, vt, _kmean(p128))
        o_d = _dilated(p128, vt)
        xf, xb = _outproj_ln(o_a.reshape(t, MIX_W), o_b.reshape(t, MIX_W), o_c.reshape(t, MIX_W),
                             o_d.reshape(t, MIX_W), w_out[layer].astype(BF16), xf,
                             ln_mix_g[layer].reshape(1, d), ln_mix_b[layer].reshape(1, d))

        e, gate, rank, counts = _router(xf, rw_t, bias)
        dest, pend, block_expert, next_expert, n_active, n_slots = _moe_layout(e, rank, counts[:, 0], EXPERT_TB)
        xs = _dispatch(dest, pend, n_active, xf, n_slots)
        ys = _experts(block_expert, next_expert, n_active, xs, w_gate, w_up, w_down, layer)
        xf, xb = _combine_ln(dest, xf, gate.T, ln_ffn_g[layer].reshape(1, d), ln_ffn_b[layer].reshape(1, d), ys)
    return xf.reshape(b, s, d)
```
